```python
import jax, jax.numpy as jnp
from jax import lax
import numpy as np

D_MODEL = 1024
BATCH = 16
SEQ = 4096
DEPTH = 1
DEC_BATCH = 2
DEC_SEQ = 8192
PAST_LEN = 128

GRID_W = 64
HA_DK = 128
HA_HEADS = D_MODEL // HA_DK
HA_DV = D_MODEL // HA_HEADS
HA_K = HA_HEADS * HA_DK
HA_V = HA_HEADS * HA_DV
HGRN_CHUNK = 32
NA_HEADS = 8
NA_DH = 64
NA_W = NA_HEADS * NA_DH
NA_MAX_KH = 8
NA_KW = 16
NA_QCB = 16
NA_SPAN = NA_QCB + NA_KW
N_EXPERTS = 32
TOP_K = 4
D_FF = 1024
SWIGLU_ALPHA = 1.702
SWIGLU_LIMIT = 7.0
MOE_BLOCK = 256
RMS_EPS = 1e-6
IN_SIZES = (HA_K, HA_K, HA_K, HA_V, HA_V, NA_W, NA_W, NA_W, D_MODEL, D_MODEL)
IN_COLS = sum(IN_SIZES)

kernel_name = "hybrid_hgrn2_natten_moe_encoder"


def _rms_norm(x, g):
    xf = x.astype(jnp.float32)
    y = xf * lax.rsqrt(jnp.mean(xf * xf, axis=-1, keepdims=True) + RMS_EPS)
    return (y * g.astype(jnp.float32)).astype(x.dtype)


def _hgrn2_chunk_scan(q, k, v, logf):
    B, T, H, dk = q.shape
    dv = v.shape[-1]
    n = T // HGRN_CHUNK

    def blk(a):
        return a.reshape(B, n, HGRN_CHUNK, H, a.shape[-1]).transpose(1, 0, 3, 2, 4)

    qc, kc, vc, gc = blk(q), blk(k), blk(v), blk(logf)
    b = jnp.cumsum(gc, axis=3)
    b_last = b[:, :, :, -1:, :]
    q_in = qc * jnp.exp(b)
    a = jnp.einsum('nbhcd,nbhsd->nbhcs', q_in, kc * jnp.exp(-b))
    tril = np.tril(np.ones((HGRN_CHUNK, HGRN_CHUNK), dtype=bool))
    a = jnp.where(tril, a, 0.0)
    intra = jnp.einsum('nbhcs,nbhsv->nbhcv', a, vc)
    k_st = kc * jnp.exp(b_last - b)
    dec = jnp.exp(b_last[:, :, :, 0, :])

    def step(S, inp):
        qi, ks, vv, d = inp
        inter = jnp.einsum('bhcd,bhdv->bhcv', qi, S)
        S = d[..., None] * S + jnp.einsum('bhsd,bhsv->bhdv', ks, vv)
        return S, inter

    S0 = jnp.zeros((B, H, dk, dv), jnp.float32)
    _, inter = lax.scan(step, S0, (q_in, k_st, vc, dec))
    o = intra + inter
    return o.transpose(1, 0, 3, 2, 4).reshape(B, T, H, dv)


def _hgrn2_bidir(q, f_fw, f_bw, i, g, lb, g_norm):
    B, T, _ = q.shape
    dt = q.dtype
    qh = q.astype(jnp.float32).reshape(B, T, HA_HEADS, HA_DK)
    vh = i.astype(jnp.float32).reshape(B, T, HA_HEADS, HA_DV)
    lb = lb.astype(jnp.float32)

    def gate(fpre, lbd):
        f = lbd + (1.0 - lbd) * jax.nn.sigmoid(fpre.astype(jnp.float32))
        f = f.reshape(B, T, HA_HEADS, HA_DK)
        return 1.0 - f, jnp.log(f)

    k_fw, lf_fw = gate(f_fw, lb[0])
    k_bw, lf_bw = gate(f_bw, lb[1])
    flip = lambda a: jnp.flip(a, axis=1)
    o = _hgrn2_chunk_scan(qh, k_fw, vh, lf_fw)
    o = o + flip(_hgrn2_chunk_scan(flip(qh), flip(k_bw), flip(vh), flip(lf_bw)))
    o = o * lax.rsqrt(jnp.mean(o * o, axis=-1, keepdims=True) + RMS_EPS)
    o = o.reshape(B, T, HA_V) * g_norm.astype(jnp.float32) * jax.nn.silu(g.astype(jnp.float32))
    return o.astype(dt)


def _col_blocks():
    ncb = GRID_W // NA_QCB
    cols = np.arange(GRID_W)
    c0 = np.clip(cols - NA_KW // 2, 0, GRID_W - NA_KW)
    span_start = np.clip(np.arange(ncb) * NA_QCB - NA_KW // 2, 0, GRID_W - NA_SPAN)
    span_idx = span_start[:, None] + np.arange(NA_SPAN)[None, :]
    kcol = span_idx[:, None, :]
    qcol = cols.reshape(ncb, NA_QCB)[:, :, None]
    c0b = c0.reshape(ncb, NA_QCB)[:, :, None]
    col_ok = (kcol >= c0b) & (kcol < c0b + NA_KW)
    dc = np.clip(kcol - qcol + (NA_KW - 1), 0, 2 * NA_KW - 2)
    return span_idx, col_ok, dc


def _neighborhood_attention(q, k, v, rpb):
    B, T, _ = q.shape
    rows = T // GRID_W
    kh = min(NA_MAX_KH, rows)
    ncb = GRID_W // NA_QCB
    span_idx, col_ok, dc = _col_blocks()
    qg = (q.reshape(B, rows, GRID_W, NA_HEADS, NA_DH) * (NA_DH ** -0.5)).transpose(1, 0, 2, 3, 4)
    kg = k.reshape(B, rows, GRID_W, NA_HEADS, NA_DH)
    vg = v.reshape(B, rows, GRID_W, NA_HEADS, NA_DH)
    rpb = rpb.astype(jnp.float32)

    def one_row(args):
        r, qr = args
        r0 = jnp.clip(r - kh // 2, 0, rows - kh)
        kr = lax.dynamic_slice_in_dim(kg, r0, kh, axis=1)[:, :, span_idx]
        vr = lax.dynamic_slice_in_dim(vg, r0, kh, axis=1)[:, :, span_idx]
        qb = qr.reshape(B, ncb, NA_QCB, NA_HEADS, NA_DH)
        s = jnp.einsum('bjqhd,bijshd->bhjqis', qb, kr).astype(jnp.float32)
        dr = r0 + jnp.arange(kh) - r + (NA_MAX_KH - 1)
        bias = rpb[:, dr][:, :, dc]
        s = s + bias.transpose(0, 2, 3, 1, 4)[None]
        s = jnp.where(col_ok[None, None, :, :, None, :], s, -1e30)
        p = jax.nn.softmax(s.reshape(B, NA_HEADS, ncb, NA_QCB, kh * NA_SPAN), axis=-1)
        p = p.reshape(s.shape).astype(v.dtype)
        o = jnp.einsum('bhjqis,bijshd->bjqhd', p, vr)
        return o.reshape(B, GRID_W, NA_HEADS, NA_DH)

    out = lax.map(one_row, (jnp.arange(rows), qg))
    return out.transpose(1, 0, 2, 3, 4).reshape(B, T, NA_W)


def _moe(x, router_w, router_b, w_gu, b_gu, w_down, b_down):
    N, D = x.shape
    nk = N * TOP_K
    logits = (x @ router_w).astype(jnp.float32) + router_b.astype(jnp.float32)
    top_v, top_i = lax.top_k(logits, TOP_K)
    gates = jax.nn.softmax(top_v, axis=-1)
    flat_e = top_i.reshape(-1)
    order = jnp.argsort(flat_e)
    sorted_e = flat_e[order]
    tok = order // TOP_K
    counts = jnp.bincount(flat_e, length=N_EXPERTS)
    padded = (counts + MOE_BLOCK - 1) // MOE_BLOCK * MOE_BLOCK
    start = jnp.cumsum(counts) - counts
    pend = jnp.cumsum(padded)
    pstart = pend - padded
    dest = pstart[sorted_e] + jnp.arange(nk) - start[sorted_e]
    n_blocks = (nk + MOE_BLOCK - 1) // MOE_BLOCK + N_EXPERTS
    slot_tok = jnp.full((n_blocks * MOE_BLOCK,), N, jnp.int32).at[dest].set(tok)
    block_e = jnp.clip(jnp.searchsorted(pend, jnp.arange(n_blocks) * MOE_BLOCK, side='right'), 0, N_EXPERTS - 1)
    x_pad = jnp.concatenate([x, jnp.zeros((1, D), x.dtype)], axis=0)
    xb = x_pad[slot_tok].reshape(n_blocks, MOE_BLOCK, D)

    def run(args):
        xe, e = args
        h = xe @ w_gu[e] + b_gu[e]
        x_glu = jnp.minimum(h[:, 0::2], SWIGLU_LIMIT)
        x_lin = jnp.clip(h[:, 1::2], -SWIGLU_LIMIT, SWIGLU_LIMIT)
        a = x_glu * jax.nn.sigmoid(SWIGLU_ALPHA * x_glu) * (x_lin + 1.0)
        return a @ w_down[e] + b_down[e]

    yb = lax.map(run, (xb, block_e)).reshape(-1, D)
    y = yb[dest] * gates.reshape(-1)[order][:, None].astype(yb.dtype)
    return jax.ops.segment_sum(y, tok, num_segments=N)


def _encoder_layer(x, c, lb, ada_w, ada_b, g_pre_mix, g_post_mix, g_pre_ffn, g_post_ffn,
                   w_in, hgrn_norm, na_rpb, w_branch_a, w_branch_b, w_out,
                   router_w, router_b, w_gate_up, b_gate_up, w_down, b_down):
    B, T, D = x.shape
    mod = jax.nn.silu(c) @ ada_w + ada_b
    sh1, sc1, gt1, sh2, sc2, gt2 = jnp.split(mod[:, None, :], 6, axis=-1)
    xn = _rms_norm(x, g_pre_mix) * (1.0 + sc1) + sh1
    proj = xn @ w_in
    offsets = [int(o) for o in np.cumsum(IN_SIZES)[:-1]]
    q_a, f_fw, f_bw, i_a, g_a, q_b, k_b, v_b, gate_a, gate_b = jnp.split(proj, offsets, axis=-1)
    y_a = _hgrn2_bidir(q_a, f_fw, f_bw, i_a, g_a, lb, hgrn_norm) @ w_branch_a
    y_b = _neighborhood_attention(q_b, k_b, v_b, na_rpb) @ w_branch_b
    mix = (jax.nn.sigmoid(gate_a) * y_a + jax.nn.sigmoid(gate_b) * y_b) @ w_out
    x = x + gt1 * _rms_norm(mix, g_post_mix)
    xn = _rms_norm(x, g_pre_ffn) * (1.0 + sc2) + sh2
    h = _moe(xn.reshape(B * T, D), router_w, router_b, w_gate_up, b_gate_up, w_down, b_down).reshape(B, T, D)
    x = x + gt2 * _rms_norm(h, g_post_ffn)
    return x


def setup_inputs(seed: int = 0) -> dict:
    key = jax.random.key(seed)
    ks = jax.random.split(key, 24)
    f32 = jnp.float32

    def nrm(k, shape, scale):
        return jax.random.normal(k, shape, f32) * scale

    def gain(k, shape):
        return 1.0 + 0.05 * jax.random.normal(k, shape, f32)

    return {
        'x_prompt': nrm(ks[0], (BATCH, SEQ, D_MODEL), 1.0),
        'x_sample': nrm(ks[1], (DEC_BATCH, DEC_SEQ, D_MODEL), 1.0),
        'c_prompt': nrm(ks[2], (BATCH, D_MODEL), 1.0),
        'c_sample': nrm(ks[3], (DEC_BATCH, D_MODEL), 1.0),
        'ada_w': nrm(ks[4], (DEPTH, D_MODEL, 6 * D_MODEL), D_MODEL ** -0.5),
        'ada_b': nrm(ks[5], (DEPTH, 6 * D_MODEL), 0.02),
        'g_pre_mix': gain(ks[6], (DEPTH, D_MODEL)),
        'g_post_mix': gain(ks[7], (DEPTH, D_MODEL)),
        'g_pre_ffn': gain(ks[8], (DEPTH, D_MODEL)),
        'g_post_ffn': gain(ks[9], (DEPTH, D_MODEL)),
        'w_in': nrm(ks[10], (DEPTH, D_MODEL, IN_COLS), D_MODEL ** -0.5),
        'hgrn_lb_logits': nrm(ks[11], (DEPTH + 1, 2, HA_K), 0.1),
        'hgrn_norm': gain(ks[12], (DEPTH, HA_V)),
        'na_rpb': nrm(ks[13], (DEPTH, NA_HEADS, 2 * NA_MAX_KH - 1, 2 * NA_KW - 1), 0.1),
        'w_branch_a': nrm(ks[14], (DEPTH, HA_V, D_MODEL), HA_V ** -0.5),
        'w_branch_b': nrm(ks[15], (DEPTH, NA_W, D_MODEL), NA_W ** -0.5),
        'w_out': nrm(ks[16], (DEPTH, D_MODEL, D_MODEL), D_MODEL ** -0.5),
        'router_w': nrm(ks[17], (DEPTH, D_MODEL, N_EXPERTS), D_MODEL ** -0.5),
        'router_b': nrm(ks[18], (DEPTH, N_EXPERTS), 0.01),
        'w_gate_up': nrm(ks[19], (DEPTH, N_EXPERTS, D_MODEL, 2 * D_FF), D_MODEL ** -0.5),
        'b_gate_up': nrm(ks[20], (DEPTH, N_EXPERTS, 2 * D_FF), 0.02),
        'w_down': nrm(ks[21], (DEPTH, N_EXPERTS, D_FF, D_MODEL), D_FF ** -0.5),
        'b_down': nrm(ks[22], (DEPTH, N_EXPERTS, D_MODEL), 0.02),
    }


def reference(x_prompt, x_sample, c_prompt, c_sample, ada_w, ada_b, g_pre_mix, g_post_mix,
              g_pre_ffn, g_post_ffn, w_in, hgrn_lb_logits, hgrn_norm, na_rpb, w_branch_a,
              w_branch_b, w_out, router_w, router_b, w_gate_up, b_gate_up, w_down, b_down):
    lbs = jnp.cumsum(jax.nn.softmax(hgrn_lb_logits.astype(jnp.float32), axis=0), axis=0)
    y_p = x_prompt
    y_s = x_sample
    for l in range(DEPTH):
        params = (lbs[l], ada_w[l], ada_b[l], g_pre_mix[l], g_post_mix[l], g_pre_ffn[l], g_post_ffn[l],
                  w_in[l], hgrn_norm[l], na_rpb[l], w_branch_a[l], w_branch_b[l], w_out[l],
                  router_w[l], router_b[l], w_gate_up[l], b_gate_up[l], w_down[l], b_down[l])
        y_p = _encoder_layer(y_p, c_prompt, *params)
        y_s = _encoder_layer(y_s, c_sample, *params)
    return (y_p, y_s)
```

```python
import functools

import numpy as np
import jax
import jax.numpy as jnp
from jax import lax
from jax.experimental import pallas as pl
from jax.experimental.pallas import tpu as pltpu

F32 = jnp.float32
BF16 = jnp.bfloat16

D_MODEL = 1024
GRID_W = 64
HA_HEADS = 8
HA_DK = 128
NA_HEADS = 8
NA_DH = 64
NA_W = NA_HEADS * NA_DH
NA_KH = 8
NA_KW = 16
N_EXPERTS = 32
TOP_K = 4
D_FF = 1024
SWIGLU_ALPHA = 1.702
SWIGLU_LIMIT = 7.0
RMS_EPS = 1e-6
NEG_BIG = -1e30

COL_A = 0
COL_GATES = 5 * 1024
COL_NA = 7 * 1024
PROJ_COLS = 7 * 1024 + 3 * NA_W

HGRN_CHUNK = 64
HGRN_TILE = 256
NA_ROWS_PER_STEP = 8
PROJ_TM = 1024
PROJ_TN = 512
MIX_TM = 512
MOE_BLK = 512
MOVE_TOK = 256
VMEM_LIMIT = 56 * 1024 * 1024


def _cparams(sem):
    return pltpu.CompilerParams(dimension_semantics=sem, vmem_limit_bytes=VMEM_LIMIT)


def _dot(a, b):
    return jnp.dot(a, b, preferred_element_type=F32)


def _dot_nt(a, b):
    return lax.dot_general(a, b, (((1,), (1,)), ((), ())), preferred_element_type=F32)


def _dot_tn(a, b):
    return lax.dot_general(a, b, (((0,), (0,)), ((), ())), preferred_element_type=F32)


def _split(a):
    hi = a.astype(BF16)
    lo = (a - hi.astype(F32)).astype(BF16)
    return hi, lo


def _dot3(a, w):
    ah, al = _split(a)
    wh, wl = _split(w)
    return _dot(ah, wh) + (_dot(al, wh) + _dot(ah, wl))


def _sigmoid(x):
    return 1.0 / (1.0 + jnp.exp(-x))


def _rms(x, g):
    return x * lax.rsqrt(jnp.mean(x * x, axis=-1, keepdims=True) + RMS_EPS) * g


def _ada_kernel(c_ref, w_ref, b_ref, o_ref):
    c = c_ref[...]
    o_ref[...] = _dot3(c * _sigmoid(c), w_ref[...]) + b_ref[...]


def _ada_mod(c, ada_w, ada_b):
    bp, d = c.shape
    n = ada_w.shape[1]
    tn = 1536
    return pl.pallas_call(
        _ada_kernel,
        grid=(n // tn,),
        in_specs=[pl.BlockSpec((bp, d), lambda j: (0, 0)),
                  pl.BlockSpec((d, tn), lambda j: (0, j)),
                  pl.BlockSpec((1, tn), lambda j: (0, j))],
        out_specs=pl.BlockSpec((bp, tn), lambda j: (0, j)),
        out_shape=jax.ShapeDtypeStruct((bp, n), F32),
        compiler_params=_cparams(("arbitrary",)),
        name="ada_mod",
    )(c, ada_w, ada_b.reshape(1, n))


def _in_proj_kernel(x_ref, mod_ref, g_ref, w_ref, o_ref, xn_ref):
    @pl.when(pl.program_id(2) == 0)
    def _():
        sh = mod_ref[0, 0:1, :]
        sc = mod_ref[0, 1:2, :]
        xn = _rms(x_ref[0], g_ref[...]) * (1.0 + sc) + sh
        xn_ref[...] = xn.astype(BF16)

    o_ref[0] = _dot(xn_ref[...], w_ref[...]).astype(BF16)


def _in_proj(x, mod, g_pre, w_in_bf):
    b, t, d = x.shape
    tm = min(PROJ_TM, t)
    n = w_in_bf.shape[1]
    return pl.pallas_call(
        _in_proj_kernel,
        grid=(b, t // tm, n // PROJ_TN),
        in_specs=[pl.BlockSpec((1, tm, d), lambda bi, i, j: (bi, i, 0)),
                  pl.BlockSpec((1, 6, d), lambda bi, i, j: (bi, 0, 0)),
                  pl.BlockSpec((1, d), lambda bi, i, j: (0, 0)),
                  pl.BlockSpec((d, PROJ_TN), lambda bi, i, j: (0, j))],
        out_specs=pl.BlockSpec((1, tm, PROJ_TN), lambda bi, i, j: (bi, i, j)),
        out_shape=jax.ShapeDtypeStruct((b, t, n), BF16),
        scratch_shapes=[pltpu.VMEM((tm, d), BF16)],
        compiler_params=_cparams(("arbitrary", "arbitrary", "arbitrary")),
        name="in_proj",
    )(x, mod, g_pre.reshape(1, d), w_in_bf)


def _hgrn_dir(q_ref, f_ref, i_ref, o_ref, s_ref, lbd, row0, reverse):
    c = HGRN_CHUNK
    rows = pl.ds(row0, c)
    q = q_ref[0, rows, :].astype(F32)
    fpre = f_ref[0, rows, :].astype(F32)
    v = i_ref[0, rows, :]
    f = lbd + (1.0 - lbd) * _sigmoid(fpre)
    logf = jnp.log(f)
    k = 1.0 - f
    ri = lax.broadcasted_iota(jnp.int32, (c, c), 0)
    ci = lax.broadcasted_iota(jnp.int32, (c, c), 1)
    keep = (ci >= ri) if reverse else (ri >= ci)
    tri = jnp.where(keep, 1.0, 0.0).astype(BF16)
    hi, lo = _split(logf)
    b = _dot(tri, hi) + _dot(tri, lo)
    if reverse:
        b_mid = b[c // 2:c // 2 + 1, :]
        b_end = b[0:1, :]
    else:
        b_mid = b[c // 2 - 1:c // 2, :]
        b_end = b[c - 1:c, :]
    qm = (q * jnp.exp(b - b_mid)).astype(BF16)
    km = (k * jnp.exp(b_mid - b)).astype(BF16)
    qi = (q * jnp.exp(b)).astype(BF16)
    ks = (k * jnp.exp(b_end - b)).astype(BF16)
    dec = jnp.exp(b_end)
    for h in range(HA_HEADS):
        sl = slice(h * HA_DK, (h + 1) * HA_DK)
        a = _dot_nt(qm[:, sl], km[:, sl])
        a = jnp.where(keep, a, 0.0).astype(BF16)
        st = s_ref[h]
        o = _dot(a, v[:, sl]) + _dot_nt(qi[:, sl], st.astype(BF16))
        s_ref[h] = st * dec[:, sl] + _dot_tn(v[:, sl], ks[:, sl])
        o_ref[0, rows, sl] = o.astype(BF16)


def _hgrn_kernel(lbl_ref, qf_ref, ff_ref, if_ref, qb_ref, fb_ref, ib_ref, of_ref, ob_ref, sf_ref, sb_ref):
    @pl.when(pl.program_id(1) == 0)
    def _():
        sf_ref[...] = jnp.zeros_like(sf_ref)
        sb_ref[...] = jnp.zeros_like(sb_ref)

    l0 = lbl_ref[0]
    l1 = lbl_ref[1]
    m = jnp.maximum(l0, l1)
    e0 = jnp.exp(l0 - m)
    e1 = jnp.exp(l1 - m)
    lb = e0 / (e0 + e1)
    n_chunks = HGRN_TILE // HGRN_CHUNK

    def body(ci, carry):
        rf = pl.multiple_of(ci * HGRN_CHUNK, HGRN_CHUNK)
        rb = pl.multiple_of((n_chunks - 1 - ci) * HGRN_CHUNK, HGRN_CHUNK)
        _hgrn_dir(qf_ref, ff_ref, if_ref, of_ref, sf_ref, lb[0:1, :], rf, False)
        _hgrn_dir(qb_ref, fb_ref, ib_ref, ob_ref, sb_ref, lb[1:2, :], rb, True)
        return carry

    lax.fori_loop(0, n_chunks, body, 0)


def _hgrn(proj, lb_logits):
    b, t, _ = proj.shape
    tt = HGRN_TILE
    nt = t // tt
    d = D_MODEL

    def fwd(col):
        return pl.BlockSpec((1, tt, d), lambda bi, ti: (bi, ti, col))

    def bwd(col):
        return pl.BlockSpec((1, tt, d), lambda bi, ti: (bi, nt - 1 - ti, col))

    return pl.pallas_call(
        _hgrn_kernel,
        grid=(b, nt),
        in_specs=[pl.BlockSpec((2, 2, d), lambda bi, ti: (0, 0, 0)),
                  fwd(0), fwd(1), fwd(3), bwd(0), bwd(2), bwd(3)],
        out_specs=[pl.BlockSpec((1, tt, d), lambda bi, ti: (bi, ti, 0)),
                   pl.BlockSpec((1, tt, d), lambda bi, ti: (bi, nt - 1 - ti, 0))],
        out_shape=[jax.ShapeDtypeStruct((b, t, d), BF16), jax.ShapeDtypeStruct((b, t, d), BF16)],
        scratch_shapes=[pltpu.VMEM((HA_HEADS, HA_DK, HA_DK), F32), pltpu.VMEM((HA_HEADS, HA_DK, HA_DK), F32)],
        compiler_params=_cparams(("arbitrary", "arbitrary")),
        name="hgrn",
    )(lb_logits, proj, proj, proj, proj, proj, proj)


def _na_bias_tables(rpb):
    q = np.arange(GRID_W)[:, None]
    kc = np.arange(GRID_W)[None, :]
    c0 = np.clip(q - NA_KW // 2, 0, GRID_W - NA_KW)
    ok = (kc >= c0) & (kc < c0 + NA_KW)
    dc = np.clip(kc - q + (NA_KW - 1), 0, 2 * NA_KW - 2)
    base = jnp.where(ok[None, None], jnp.take(rpb.astype(F32), dc, axis=2), NEG_BIG)
    tabs = []
    for dr0 in range(NA_KH):
        w = base[:, dr0:dr0 + NA_KH]
        w = w.transpose(0, 2, 1, 3).reshape(NA_HEADS // 2, 2 * GRID_W, NA_KH * GRID_W)
        tabs.append(w)
    return jnp.stack(tabs, axis=1)


def _na_kernel(q_ref, k_ref, v_ref, tab_ref, o_ref, *, grid_rows):
    g = pl.program_id(1)
    lane = lax.broadcasted_iota(jnp.int32, (GRID_W, 2 * NA_DH), 1)
    low = lane < NA_DH
    scale = NA_DH ** -0.5

    def body(rr, carry):
        r = g * NA_ROWS_PER_STEP + rr
        r0 = jnp.clip(r - NA_KH // 2, 0, grid_rows - NA_KH)
        dr0 = r0 - r + (NA_KH - 1)
        qrow = pl.ds(pl.multiple_of(rr * GRID_W, GRID_W), GRID_W)
        krow = pl.ds(pl.multiple_of(r0 * GRID_W, GRID_W), NA_KH * GRID_W)
        for j in range(NA_HEADS // 2):
            sl = slice(j * 2 * NA_DH, (j + 1) * 2 * NA_DH)
            qp = q_ref[0, qrow, sl]
            zero = jnp.zeros_like(qp)
            ql = jnp.concatenate([jnp.where(low, qp, zero), jnp.where(low, zero, qp)], axis=0)
            s = _dot_nt(ql, k_ref[0, krow, sl]) * scale + tab_ref[j, dr0]
            m = jnp.max(s, axis=-1, keepdims=True)
            p = jnp.exp(s - m)
            l = jnp.sum(p, axis=-1, keepdims=True)
            pv = _dot(p.astype(BF16), v_ref[0, krow, sl]) / l
            o_ref[0, qrow, sl] = jnp.where(low, pv[:GRID_W], pv[GRID_W:]).astype(BF16)
        return carry

    lax.fori_loop(0, NA_ROWS_PER_STEP, body, 0)


def _natten(proj, tabs):
    b, t, _ = proj.shape
    grid_rows = t // GRID_W
    tq = NA_ROWS_PER_STEP * GRID_W
    cq, ck, cv = (COL_NA // NA_W, COL_NA // NA_W + 1, COL_NA // NA_W + 2)
    return pl.pallas_call(
        functools.partial(_na_kernel, grid_rows=grid_rows),
        grid=(b, t // tq),
        in_specs=[pl.BlockSpec((1, tq, NA_W), lambda bi, gi: (bi, gi, cq)),
                  pl.BlockSpec((1, t, NA_W), lambda bi, gi: (bi, 0, ck)),
                  pl.BlockSpec((1, t, NA_W), lambda bi, gi: (bi, 0, cv)),
                  pl.BlockSpec(tabs.shape, lambda bi, gi: (0, 0, 0, 0))],
        out_specs=pl.BlockSpec((1, tq, NA_W), lambda bi, gi: (bi, gi, 0)),
        out_shape=jax.ShapeDtypeStruct((b, t, NA_W), BF16),
        compiler_params=_cparams(("arbitrary", "arbitrary")),
        name="natten",
    )(proj, proj, proj, tabs)


def _mix_kernel(x_ref, mod_ref, of_ref, ob_ref, g_ref, ga_ref, gb_ref, nb_ref,
                gn_ref, gpm_ref, gpf_ref, wa_ref, wb_ref, wo_ref, rw_ref, rb_ref,
                x1_ref, xn_ref, ri_ref, rg_ref, cnt_ref, tri_ref, carry_ref):
    tm = MIX_TM
    first = (pl.program_id(0) == 0) & (pl.program_id(1) == 0)

    @pl.when(first)
    def _():
        r = lax.broadcasted_iota(jnp.int32, (tm, tm), 0)
        c = lax.broadcasted_iota(jnp.int32, (tm, tm), 1)
        tri_ref[...] = jnp.where(r > c, 1.0, 0.0).astype(BF16)
        carry_ref[...] = jnp.zeros_like(carry_ref)

    sc2 = mod_ref[0, 4:5, :]
    sh2 = mod_ref[0, 3:4, :]
    gt1 = mod_ref[0, 2:3, :]

    o = of_ref[0].astype(F32) + ob_ref[0].astype(F32)
    parts = []
    for h in range(HA_HEADS):
        oh = o[:, h * HA_DK:(h + 1) * HA_DK]
        parts.append(oh * lax.rsqrt(jnp.mean(oh * oh, axis=-1, keepdims=True) + RMS_EPS))
    g = g_ref[0].astype(F32)
    oa = jnp.concatenate(parts, axis=-1) * gn_ref[...] * (g * _sigmoid(g))
    ya = _dot(oa.astype(BF16), wa_ref[...])
    yb = _dot(nb_ref[0], wb_ref[...])
    mix = _sigmoid(ga_ref[0].astype(F32)) * ya + _sigmoid(gb_ref[0].astype(F32)) * yb
    mo = _dot(mix.astype(BF16), wo_ref[...])
    x1 = x_ref[0] + gt1 * _rms(mo, gpm_ref[...])
    x1_ref[0] = x1

    xn = _rms(x1, gpf_ref[...]) * (1.0 + sc2) + sh2
    xn_ref[0] = xn
    logits = _dot(xn.astype(BF16), rw_ref[...]) + rb_ref[...]
    lane = lax.broadcasted_iota(jnp.int32, (tm, N_EXPERTS), 1)
    work = logits
    vals, idxs, hots = [], [], []
    for _ in range(TOP_K):
        mv = jnp.max(work, axis=-1, keepdims=True)
        mi = jnp.min(jnp.where(work == mv, lane, N_EXPERTS), axis=-1, keepdims=True)
        hot = lane == mi
        vals.append(mv)
        idxs.append(mi)
        hots.append(hot)
        work = jnp.where(hot, -jnp.inf, work)
    es = [jnp.exp(v - vals[0]) for v in vals]
    den = es[0] + es[1] + es[2] + es[3]

    cnt = sum(jnp.where(h, 1.0, 0.0) for h in hots)
    before = _dot(tri_ref[...], cnt.astype(BF16)) + carry_ref[...]
    carry_ref[...] = carry_ref[...] + jnp.sum(cnt, axis=0, keepdims=True)
    cnt_ref[...] = jnp.broadcast_to(carry_ref[...], cnt_ref.shape)

    lane_o = lax.broadcasted_iota(jnp.int32, (tm, 128), 1)
    ri = jnp.zeros((tm, 128), jnp.int32)
    rg = jnp.zeros((tm, 128), F32)
    for kk in range(TOP_K):
        rank = jnp.sum(jnp.where(hots[kk], before, 0.0), axis=-1, keepdims=True).astype(jnp.int32)
        ri = jnp.where(lane_o == kk, idxs[kk], ri)
        ri = jnp.where(lane_o == TOP_K + kk, rank, ri)
        rg = jnp.where(lane_o == kk, es[kk] / den, rg)
    ri_ref[0] = ri
    rg_ref[0] = rg


def _mix(x, mod, proj, o_fw, o_bw, o_b, hgrn_norm, g_post_mix, g_pre_ffn, wa, wb, wo, rw, rb):
    b, t, d = x.shape
    tm = min(MIX_TM, t)
    assert tm == MIX_TM
    row = lambda a: a.reshape(1, -1)
    tok = lambda w, col: pl.BlockSpec((1, tm, w), lambda bi, i: (bi, i, col))
    full = lambda a: pl.BlockSpec(a.shape, lambda bi, i: (0,) * a.ndim)
    gn, gpm, gpf, rbr = row(hgrn_norm), row(g_post_mix), row(g_pre_ffn), row(rb)
    outs = pl.pallas_call(
        _mix_kernel,
        grid=(b, t // tm),
        in_specs=[tok(d, 0),
                  pl.BlockSpec((1, 6, d), lambda bi, i: (bi, 0, 0)),
                  tok(d, 0), tok(d, 0),
                  tok(d, 4),
                  tok(d, COL_GATES // d), tok(d, COL_GATES // d + 1),
                  tok(NA_W, 0),
                  full(gn), full(gpm), full(gpf), full(wa), full(wb), full(wo), full(rw), full(rbr)],
        out_specs=[tok(d, 0), tok(d, 0), tok(128, 0), tok(128, 0),
                   pl.BlockSpec((8, N_EXPERTS), lambda bi, i: (0, 0))],
        out_shape=[jax.ShapeDtypeStruct((b, t, d), F32), jax.ShapeDtypeStruct((b, t, d), F32),
                   jax.ShapeDtypeStruct((b, t, 128), jnp.int32), jax.ShapeDtypeStruct((b, t, 128), F32),
                   jax.ShapeDtypeStruct((8, N_EXPERTS), F32)],
        scratch_shapes=[pltpu.VMEM((tm, tm), BF16), pltpu.VMEM((1, N_EXPERTS), F32)],
        compiler_params=_cparams(("arbitrary", "arbitrary")),
        name="mix",
    )(x, mod, o_fw, o_bw, proj, proj, proj, o_b, gn, gpm, gpf, wa, wb, wo, rw, rbr)
    return outs


def _dispatch_kernel(dest_hbm, x_hbm, xs_hbm, idx_smem, sem_idx, sem):
    i = pl.program_id(0)
    n_idx = MOVE_TOK * TOP_K
    cp = pltpu.make_async_copy(dest_hbm.at[pl.ds(i * n_idx, n_idx)], idx_smem, sem_idx)
    cp.start()
    cp.wait()

    def row_copy(tt, kk):
        return pltpu.make_async_copy(x_hbm.at[pl.ds(i * MOVE_TOK + tt, 1)],
                                     xs_hbm.at[pl.ds(idx_smem[tt * TOP_K + kk], 1)], sem)

    def start(tt, carry):
        for kk in range(TOP_K):
            row_copy(tt, kk).start()
        return carry

    def wait(tt, carry):
        for kk in range(TOP_K):
            row_copy(tt, kk).wait()
        return carry

    lax.fori_loop(0, MOVE_TOK, start, 0)
    lax.fori_loop(0, MOVE_TOK, wait, 0)


def _dispatch(xn, dest_flat, n_rows):
    n, d = xn.shape
    return pl.pallas_call(
        _dispatch_kernel,
        grid=(n // MOVE_TOK,),
        in_specs=[pl.BlockSpec(memory_space=pl.ANY), pl.BlockSpec(memory_space=pl.ANY)],
        out_specs=pl.BlockSpec(memory_space=pl.ANY),
        out_shape=jax.ShapeDtypeStruct((n_rows, d), F32),
        scratch_shapes=[pltpu.SMEM((MOVE_TOK * TOP_K,), jnp.int32),
                        pltpu.SemaphoreType.DMA, pltpu.SemaphoreType.DMA],
        compiler_params=pltpu.CompilerParams(dimension_semantics=("arbitrary",), has_side_effects=True),
        name="dispatch",
    )(dest_flat, xn)


def _expert_kernel(be_ref, bv_ref, x_ref, wg_ref, wu_ref, bg_ref, bu_ref, wd_ref, bd_ref, o_ref):
    i = pl.program_id(0)
    valid = bv_ref[i]

    @pl.when(valid > 0)
    def _():
        rowi = lax.broadcasted_iota(jnp.int32, (MOE_BLK, 1), 0)
        x = jnp.where(rowi < valid, x_ref[...], 0.0).astype(BF16)
        hg = _dot(x, wg_ref[0]) + bg_ref[0]
        hu = _dot(x, wu_ref[0]) + bu_ref[0]
        glu = jnp.minimum(hg, SWIGLU_LIMIT)
        lin = jnp.clip(hu, -SWIGLU_LIMIT, SWIGLU_LIMIT)
        a = glu * _sigmoid(SWIGLU_ALPHA * glu) * (lin + 1.0)
        o_ref[...] = _dot(a.astype(BF16), wd_ref[0]) + bd_ref[0]

    @pl.when(valid <= 0)
    def _():
        o_ref[...] = jnp.zeros_like(o_ref)


def _experts(xs, block_e, block_valid, wg, wu, bg, bu, wd, bd):
    n_rows, d = xs.shape
    n_blocks = n_rows // MOE_BLK
    wspec = lambda: pl.BlockSpec((1, d, D_FF), lambda i, be, bv: (be[i], 0, 0))
    bspec = lambda: pl.BlockSpec((1, 1, D_FF), lambda i, be, bv: (be[i], 0, 0))
    return pl.pallas_call(
        _expert_kernel,
        grid_spec=pltpu.PrefetchScalarGridSpec(
            num_scalar_prefetch=2,
            grid=(n_blocks,),
            in_specs=[pl.BlockSpec((MOE_BLK, d), lambda i, be, bv: (i, 0)),
                      wspec(), wspec(), bspec(), bspec(), wspec(), bspec()],
            out_specs=pl.BlockSpec((MOE_BLK, d), lambda i, be, bv: (i, 0)),
        ),
        out_shape=jax.ShapeDtypeStruct((n_rows, d), F32),
        compiler_params=_cparams(("arbitrary",)),
        name="experts",
    )(block_e, block_valid, xs, wg, wu, bg, bu, wd, bd)


def _combine_kernel(dest_hbm, y_hbm, x1_ref, mod_ref, rg_ref, gpo_ref, o_ref, idx_smem, buf, sem_idx, sem):
    i = pl.program_id(0)
    n_idx = MOVE_TOK * TOP_K
    cp = pltpu.make_async_copy(dest_hbm.at[pl.ds(i * n_idx, n_idx)], idx_smem, sem_idx)
    cp.start()
    cp.wait()

    def row_copy(tt, kk):
        return pltpu.make_async_copy(y_hbm.at[pl.ds(idx_smem[tt * TOP_K + kk], 1)],
                                     buf.at[kk, pl.ds(tt, 1)], sem)

    def start(tt, carry):
        for kk in range(TOP_K):
            row_copy(tt, kk).start()
        return carry

    def wait(tt, carry):
        for kk in range(TOP_K):
            row_copy(tt, kk).wait()
        return carry

    lax.fori_loop(0, MOVE_TOK, start, 0)
    lax.fori_loop(0, MOVE_TOK, wait, 0)

    rg = rg_ref[...]
    h = rg[:, 0:1] * buf[0]
    for kk in range(1, TOP_K):
        h = h + rg[:, kk:kk + 1] * buf[kk]
    gt2 = mod_ref[0, 5:6, :]
    o_ref[...] = x1_ref[...] + gt2 * _rms(h, gpo_ref[...])


def _combine(y, dest_flat, x1, mod, rg, g_post_ffn, tokens_per_batch):
    n, d = x1.shape
    per_b = tokens_per_batch // MOVE_TOK
    gpo = g_post_ffn.reshape(1, d)
    return pl.pallas_call(
        _combine_kernel,
        grid=(n // MOVE_TOK,),
        in_specs=[pl.BlockSpec(memory_space=pl.ANY), pl.BlockSpec(memory_space=pl.ANY),
                  pl.BlockSpec((MOVE_TOK, d), lambda i: (i, 0)),
                  pl.BlockSpec((1, 6, d), lambda i: (i // per_b, 0, 0)),
                  pl.BlockSpec((MOVE_TOK, 128), lambda i: (i, 0)),
                  pl.BlockSpec((1, d), lambda i: (0, 0))],
        out_specs=pl.BlockSpec((MOVE_TOK, d), lambda i: (i, 0)),
        out_shape=jax.ShapeDtypeStruct((n, d), F32),
        scratch_shapes=[pltpu.SMEM((MOVE_TOK * TOP_K,), jnp.int32),
                        pltpu.VMEM((TOP_K, MOVE_TOK, d), F32),
                        pltpu.SemaphoreType.DMA, pltpu.SemaphoreType.DMA],
        compiler_params=_cparams(("arbitrary",)),
        name="combine",
    )(dest_flat, y, x1, mod, rg, gpo)


def _moe_plan(ri, counts, n_tok):
    idx = ri[:, :TOP_K]
    rank = ri[:, TOP_K:2 * TOP_K]
    cnt = counts.astype(jnp.int32)
    padded = (cnt + MOE_BLK - 1) // MOE_BLK * MOE_BLK
    pend = jnp.cumsum(padded)
    pstart = pend - padded
    dest = pstart[idx] + rank
    n_blocks = (n_tok * TOP_K + MOE_BLK - 1) // MOE_BLK + N_EXPERTS
    bstart = jnp.arange(n_blocks, dtype=jnp.int32) * MOE_BLK
    block_e = jnp.clip(jnp.searchsorted(pend, bstart, side='right'), 0, N_EXPERTS - 1).astype(jnp.int32)
    block_valid = jnp.clip(pstart[block_e] + cnt[block_e] - bstart, 0, MOE_BLK).astype(jnp.int32)
    return dest.reshape(-1).astype(jnp.int32), block_e, block_valid, n_blocks


def _layer(x, mod, p):
    b, t, d = x.shape
    n = b * t
    proj = _in_proj(x, mod, p['g_pre_mix'], p['w_in'])
    o_fw, o_bw = _hgrn(proj, p['lb_logits'])
    o_b = _natten(proj, p['na_tabs'])
    x1, xn, ri, rg, counts = _mix(x, mod, proj, o_fw, o_bw, o_b, p['hgrn_norm'], p['g_post_mix'], p['g_pre_ffn'],
                                  p['w_branch_a'], p['w_branch_b'], p['w_out'], p['router_w'], p['router_b'])
    ri = ri.reshape(n, 128)
    rg = rg.reshape(n, 128)
    dest, block_e, block_valid, n_blocks = _moe_plan(ri, counts[0], n)
    xs = _dispatch(xn.reshape(n, d), dest, n_blocks * MOE_BLK)
    y = _experts(xs, block_e, block_valid, p['w_g'], p['w_u'], p['b_g'], p['b_u'], p['w_d'], p['b_d'])
    out = _combine(y, dest, x1.reshape(n, d), mod, rg, p['g_post_ffn'], t)
    return out.reshape(b, t, d)


def kernel(x_prompt, x_sample, c_prompt, c_sample, ada_w, ada_b, g_pre_mix, g_post_mix, g_pre_ffn, g_post_ffn,
           w_in, hgrn_lb_logits, hgrn_norm, na_rpb, w_branch_a, w_branch_b, w_out, router_w, router_b,
           w_gate_up, b_gate_up, w_down, b_down):
    d = D_MODEL
    bp, bs = c_prompt.shape[0], c_sample.shape[0]
    pad = (-(bp + bs)) % 8
    c_all = jnp.concatenate([c_prompt, c_sample, jnp.zeros((pad, d), F32)], axis=0)
    mod = _ada_mod(c_all, ada_w[0], ada_b[0]).reshape(-1, 6, d)

    w = w_in[0]
    w_perm = jnp.concatenate([w[:, :5 * d], w[:, 5 * d + 3 * NA_W:], w[:, 5 * d:5 * d + 3 * NA_W]], axis=1).astype(BF16)
    wgu = w_gate_up[0]
    bgu = b_gate_up[0]
    p = dict(
        g_pre_mix=g_pre_mix[0], g_post_mix=g_post_mix[0], g_pre_ffn=g_pre_ffn[0], g_post_ffn=g_post_ffn[0],
        w_in=w_perm, lb_logits=hgrn_lb_logits.astype(F32), hgrn_norm=hgrn_norm[0],
        na_tabs=_na_bias_tables(na_rpb[0]),
        w_branch_a=w_branch_a[0].astype(BF16), w_branch_b=w_branch_b[0].astype(BF16), w_out=w_out[0].astype(BF16),
        router_w=router_w[0].astype(BF16), router_b=router_b[0],
        w_g=wgu[:, :, 0::2].astype(BF16), w_u=wgu[:, :, 1::2].astype(BF16),
        b_g=bgu[:, None, 0::2], b_u=bgu[:, None, 1::2],
        w_d=w_down[0].astype(BF16), b_d=b_down[0][:, None, :],
    )
    y_p = _layer(x_prompt, mod[:bp], p)
    y_s = _layer(x_sample, mod[bp:bp + bs], p)
    return (y_p, y_s)
```

```python
import functools

import numpy as np
import jax
import jax.numpy as jnp
from jax import lax
from jax.experimental import pallas as pl
from jax.experimental.pallas import tpu as pltpu

F32 = jnp.float32
BF16 = jnp.bfloat16

D_MODEL = 1024
GRID_W = 64
HA_HEADS = 8
HA_DK = 128
NA_HEADS = 8
NA_DH = 64
NA_W = NA_HEADS * NA_DH
NA_KH = 8
NA_KW = 16
N_EXPERTS = 32
TOP_K = 4
D_FF = 1024
SWIGLU_ALPHA = 1.702
SWIGLU_LIMIT = 7.0
RMS_EPS = 1e-6
NEG_BIG = -1e30

COL_A = 0
COL_GATES = 5 * 1024
COL_NA = 7 * 1024
PROJ_COLS = 7 * 1024 + 3 * NA_W

HGRN_CHUNK = 64
HGRN_TILE = 256
NA_ROWS_PER_STEP = 8
PROJ_TM = 1024
PROJ_TN = 512
MIX_TM = 512
MOE_BLK = 512
MOVE_TOK = 256
VMEM_LIMIT = 56 * 1024 * 1024


def _cparams(sem):
    return pltpu.CompilerParams(dimension_semantics=sem, vmem_limit_bytes=VMEM_LIMIT)


def _dot(a, b):
    return jnp.dot(a, b, preferred_element_type=F32)


def _dot_nt(a, b):
    return lax.dot_general(a, b, (((1,), (1,)), ((), ())), preferred_element_type=F32)


def _dot_tn(a, b):
    return lax.dot_general(a, b, (((0,), (0,)), ((), ())), preferred_element_type=F32)


def _split(a):
    hi = a.astype(BF16)
    lo = (a - hi.astype(F32)).astype(BF16)
    return hi, lo


def _dot3(a, w):
    ah, al = _split(a)
    wh, wl = _split(w)
    return _dot(ah, wh) + (_dot(al, wh) + _dot(ah, wl))


def _sigmoid(x):
    return 1.0 / (1.0 + jnp.exp(-x))


def _rms(x, g):
    return x * lax.rsqrt(jnp.mean(x * x, axis=-1, keepdims=True) + RMS_EPS) * g


def _ada_kernel(c_ref, w_ref, b_ref, o_ref):
    c = c_ref[...]
    o_ref[...] = _dot3(c * _sigmoid(c), w_ref[...]) + b_ref[...]


def _ada_mod(c, ada_w, ada_b):
    bp, d = c.shape
    n = ada_w.shape[1]
    tn = 1536
    return pl.pallas_call(
        _ada_kernel,
        grid=(n // tn,),
        in_specs=[pl.BlockSpec((bp, d), lambda j: (0, 0)),
                  pl.BlockSpec((d, tn), lambda j: (0, j)),
                  pl.BlockSpec((1, tn), lambda j: (0, j))],
        out_specs=pl.BlockSpec((bp, tn), lambda j: (0, j)),
        out_shape=jax.ShapeDtypeStruct((bp, n), F32),
        compiler_params=_cparams(("arbitrary",)),
        name="ada_mod",
    )(c, ada_w, ada_b.reshape(1, n))


def _in_proj_kernel(x_ref, mod_ref, g_ref, w_ref, o_ref, xn_ref):
    @pl.when(pl.program_id(2) == 0)
    def _():
        sh = mod_ref[0, 0:1, :]
        sc = mod_ref[0, 1:2, :]
        xn = _rms(x_ref[0], g_ref[...]) * (1.0 + sc) + sh
        xn_ref[...] = xn.astype(BF16)

    o_ref[0] = _dot(xn_ref[...], w_ref[...]).astype(BF16)


def _in_proj(x, mod, g_pre, w_in_bf):
    b, t, d = x.shape
    tm = min(PROJ_TM, t)
    n = w_in_bf.shape[1]
    return pl.pallas_call(
        _in_proj_kernel,
        grid=(b, t // tm, n // PROJ_TN),
        in_specs=[pl.BlockSpec((1, tm, d), lambda bi, i, j: (bi, i, 0)),
                  pl.BlockSpec((1, 6, d), lambda bi, i, j: (bi, 0, 0)),
                  pl.BlockSpec((1, d), lambda bi, i, j: (0, 0)),
                  pl.BlockSpec((d, PROJ_TN), lambda bi, i, j: (0, j))],
        out_specs=pl.BlockSpec((1, tm, PROJ_TN), lambda bi, i, j: (bi, i, j)),
        out_shape=jax.ShapeDtypeStruct((b, t, n), BF16),
        scratch_shapes=[pltpu.VMEM((tm, d), BF16)],
        compiler_params=_cparams(("arbitrary", "arbitrary", "arbitrary")),
        name="in_proj",
    )(x, mod, g_pre.reshape(1, d), w_in_bf)


def _hgrn_dir(q_ref, f_ref, i_ref, o_ref, s_ref, lbd, row0, reverse):
    c = HGRN_CHUNK
    rows = pl.ds(row0, c)
    q = q_ref[0, rows, :].astype(F32)
    fpre = f_ref[0, rows, :].astype(F32)
    v = i_ref[0, rows, :]
    f = lbd + (1.0 - lbd) * _sigmoid(fpre)
    logf = jnp.log(f)
    k = 1.0 - f
    ri = lax.broadcasted_iota(jnp.int32, (c, c), 0)
    ci = lax.broadcasted_iota(jnp.int32, (c, c), 1)
    keep = (ci >= ri) if reverse else (ri >= ci)
    tri = jnp.where(keep, 1.0, 0.0).astype(BF16)
    hi, lo = _split(logf)
    b = _dot(tri, hi) + _dot(tri, lo)
    if reverse:
        b_mid = b[c // 2:c // 2 + 1, :]
        b_end = b[0:1, :]
    else:
        b_mid = b[c // 2 - 1:c // 2, :]
        b_end = b[c - 1:c, :]
    qm = (q * jnp.exp(b - b_mid)).astype(BF16)
    km = (k * jnp.exp(b_mid - b)).astype(BF16)
    qi = (q * jnp.exp(b)).astype(BF16)
    ks = (k * jnp.exp(b_end - b)).astype(BF16)
    dec = jnp.exp(b_end)
    for h in range(HA_HEADS):
        sl = slice(h * HA_DK, (h + 1) * HA_DK)
        a = _dot_nt(qm[:, sl], km[:, sl])
        a = jnp.where(keep, a, 0.0).astype(BF16)
        st = s_ref[h]
        o = _dot(a, v[:, sl]) + _dot_nt(qi[:, sl], st.astype(BF16))
        s_ref[h] = st * dec[:, sl] + _dot_tn(v[:, sl], ks[:, sl])
        o_ref[0, rows, sl] = o.astype(BF16)


def _hgrn_kernel(lbl_ref, qf_ref, ff_ref, if_ref, qb_ref, fb_ref, ib_ref, of_ref, ob_ref, sf_ref, sb_ref):
    @pl.when(pl.program_id(1) == 0)
    def _():
        sf_ref[...] = jnp.zeros_like(sf_ref)
        sb_ref[...] = jnp.zeros_like(sb_ref)

    l0 = lbl_ref[0]
    l1 = lbl_ref[1]
    m = jnp.maximum(l0, l1)
    e0 = jnp.exp(l0 - m)
    e1 = jnp.exp(l1 - m)
    lb = e0 / (e0 + e1)
    n_chunks = HGRN_TILE // HGRN_CHUNK

    def body(ci, carry):
        rf = pl.multiple_of(ci * HGRN_CHUNK, HGRN_CHUNK)
        rb = pl.multiple_of((n_chunks - 1 - ci) * HGRN_CHUNK, HGRN_CHUNK)
        _hgrn_dir(qf_ref, ff_ref, if_ref, of_ref, sf_ref, lb[0:1, :], rf, False)
        _hgrn_dir(qb_ref, fb_ref, ib_ref, ob_ref, sb_ref, lb[1:2, :], rb, True)
        return carry

    lax.fori_loop(0, n_chunks, body, 0)


def _hgrn(proj, lb_logits):
    b, t, _ = proj.shape
    tt = HGRN_TILE
    nt = t // tt
    d = D_MODEL

    def fwd(col):
        return pl.BlockSpec((1, tt, d), lambda bi, ti: (bi, ti, col))

    def bwd(col):
        return pl.BlockSpec((1, tt, d), lambda bi, ti: (bi, nt - 1 - ti, col))

    return pl.pallas_call(
        _hgrn_kernel,
        grid=(b, nt),
        in_specs=[pl.BlockSpec((2, 2, d), lambda bi, ti: (0, 0, 0)),
                  fwd(0), fwd(1), fwd(3), bwd(0), bwd(2), bwd(3)],
        out_specs=[pl.BlockSpec((1, tt, d), lambda bi, ti: (bi, ti, 0)),
                   pl.BlockSpec((1, tt, d), lambda bi, ti: (bi, nt - 1 - ti, 0))],
        out_shape=[jax.ShapeDtypeStruct((b, t, d), BF16), jax.ShapeDtypeStruct((b, t, d), BF16)],
        scratch_shapes=[pltpu.VMEM((HA_HEADS, HA_DK, HA_DK), F32), pltpu.VMEM((HA_HEADS, HA_DK, HA_DK), F32)],
        compiler_params=_cparams(("arbitrary", "arbitrary")),
        name="hgrn",
    )(lb_logits, proj, proj, proj, proj, proj, proj)


def _na_bias_tables(rpb):
    q = np.arange(GRID_W)[:, None]
    kc = np.arange(GRID_W)[None, :]
    c0 = np.clip(q - NA_KW // 2, 0, GRID_W - NA_KW)
    ok = (kc >= c0) & (kc < c0 + NA_KW)
    dc = np.clip(kc - q + (NA_KW - 1), 0, 2 * NA_KW - 2)
    base = jnp.where(ok[None, None], jnp.take(rpb.astype(F32), dc, axis=2), NEG_BIG)
    tabs = []
    for dr0 in range(NA_KH):
        w = base[:, dr0:dr0 + NA_KH]
        w = w.transpose(0, 2, 1, 3).reshape(NA_HEADS // 2, 2 * GRID_W, NA_KH * GRID_W)
        tabs.append(w)
    return jnp.stack(tabs, axis=1)


def _na_kernel(q_ref, k_ref, v_ref, tab_ref, o_ref, *, grid_rows):
    g = pl.program_id(1)
    lane = lax.broadcasted_iota(jnp.int32, (GRID_W, 2 * NA_DH), 1)
    low = lane < NA_DH
    scale = NA_DH ** -0.5

    def body(rr, carry):
        r = g * NA_ROWS_PER_STEP + rr
        r0 = jnp.clip(r - NA_KH // 2, 0, grid_rows - NA_KH)
        dr0 = r0 - r + (NA_KH - 1)
        qrow = pl.ds(pl.multiple_of(rr * GRID_W, GRID_W), GRID_W)
        krow = pl.ds(pl.multiple_of(r0 * GRID_W, GRID_W), NA_KH * GRID_W)
        for j in range(NA_HEADS // 2):
            sl = slice(j * 2 * NA_DH, (j + 1) * 2 * NA_DH)
            qp = q_ref[0, qrow, sl]
            zero = jnp.zeros_like(qp)
            ql = jnp.concatenate([jnp.where(low, qp, zero), jnp.where(low, zero, qp)], axis=0)
            s = _dot_nt(ql, k_ref[0, krow, sl]) * scale + tab_ref[j, dr0]
            m = jnp.max(s, axis=-1, keepdims=True)
            p = jnp.exp(s - m)
            l = jnp.sum(p, axis=-1, keepdims=True)
            pv = _dot(p.astype(BF16), v_ref[0, krow, sl]) / l
            o_ref[0, qrow, sl] = jnp.where(low, pv[:GRID_W], pv[GRID_W:]).astype(BF16)
        return carry

    lax.fori_loop(0, NA_ROWS_PER_STEP, body, 0)


def _natten(proj, tabs):
    b, t, _ = proj.shape
    grid_rows = t // GRID_W
    tq = NA_ROWS_PER_STEP * GRID_W
    cq, ck, cv = (COL_NA // NA_W, COL_NA // NA_W + 1, COL_NA // NA_W + 2)
    return pl.pallas_call(
        functools.partial(_na_kernel, grid_rows=grid_rows),
        grid=(b, t // tq),
        in_specs=[pl.BlockSpec((1, tq, NA_W), lambda bi, gi: (bi, gi, cq)),
                  pl.BlockSpec((1, t, NA_W), lambda bi, gi: (bi, 0, ck)),
                  pl.BlockSpec((1, t, NA_W), lambda bi, gi: (bi, 0, cv)),
                  pl.BlockSpec(tabs.shape, lambda bi, gi: (0, 0, 0, 0))],
        out_specs=pl.BlockSpec((1, tq, NA_W), lambda bi, gi: (bi, gi, 0)),
        out_shape=jax.ShapeDtypeStruct((b, t, NA_W), BF16),
        compiler_params=_cparams(("arbitrary", "arbitrary")),
        name="natten",
    )(proj, proj, proj, tabs)


def _mix_kernel(x_ref, mod_ref, of_ref, ob_ref, g_ref, ga_ref, gb_ref, nb_ref,
                gn_ref, gpm_ref, gpf_ref, wa_ref, wb_ref, wo_ref, rw_ref, rb_ref,
                x1_ref, xn_ref, ri_ref, rg_ref, cnt_ref, tri_ref, carry_ref):
    tm = MIX_TM
    first = (pl.program_id(0) == 0) & (pl.program_id(1) == 0)

    @pl.when(first)
    def _():
        r = lax.broadcasted_iota(jnp.int32, (tm, tm), 0)
        c = lax.broadcasted_iota(jnp.int32, (tm, tm), 1)
        tri_ref[...] = jnp.where(r > c, 1.0, 0.0).astype(BF16)
        carry_ref[...] = jnp.zeros_like(carry_ref)

    sc2 = mod_ref[0, 4:5, :]
    sh2 = mod_ref[0, 3:4, :]
    gt1 = mod_ref[0, 2:3, :]

    o = of_ref[0].astype(F32) + ob_ref[0].astype(F32)
    parts = []
    for h in range(HA_HEADS):
        oh = o[:, h * HA_DK:(h + 1) * HA_DK]
        parts.append(oh * lax.rsqrt(jnp.mean(oh * oh, axis=-1, keepdims=True) + RMS_EPS))
    g = g_ref[0].astype(F32)
    oa = jnp.concatenate(parts, axis=-1) * gn_ref[...] * (g * _sigmoid(g))
    ya = _dot(oa.astype(BF16), wa_ref[...])
    yb = _dot(nb_ref[0], wb_ref[...])
    mix = _sigmoid(ga_ref[0].astype(F32)) * ya + _sigmoid(gb_ref[0].astype(F32)) * yb
    mo = _dot(mix.astype(BF16), wo_ref[...])
    x1 = x_ref[0] + gt1 * _rms(mo, gpm_ref[...])
    x1_ref[0] = x1

    xn = _rms(x1, gpf_ref[...]) * (1.0 + sc2) + sh2
    xn_ref[0] = xn
    logits = _dot(xn.astype(BF16), rw_ref[...]) + rb_ref[...]
    lane = lax.broadcasted_iota(jnp.int32, (tm, N_EXPERTS), 1)
    work = logits
    vals, idxs, hots = [], [], []
    for _ in range(TOP_K):
        mv = jnp.max(work, axis=-1, keepdims=True)
        mi = jnp.min(jnp.where(work == mv, lane, N_EXPERTS), axis=-1, keepdims=True)
        hot = lane == mi
        vals.append(mv)
        idxs.append(mi)
        hots.append(hot)
        work = jnp.where(hot, -jnp.inf, work)
    es = [jnp.exp(v - vals[0]) for v in vals]
    den = es[0] + es[1] + es[2] + es[3]

    cnt = sum(jnp.where(h, 1.0, 0.0) for h in hots)
    before = _dot(tri_ref[...], cnt.astype(BF16)) + carry_ref[...]
    carry_ref[...] = carry_ref[...] + jnp.sum(cnt, axis=0, keepdims=True)
    cnt_ref[...] = jnp.broadcast_to(carry_ref[...], cnt_ref.shape)

    lane_o = lax.broadcasted_iota(jnp.int32, (tm, 128), 1)
    ri = jnp.zeros((tm, 128), jnp.int32)
    rg = jnp.zeros((tm, 128), F32)
    for kk in range(TOP_K):
        rank = jnp.sum(jnp.where(hots[kk], before, 0.0), axis=-1, keepdims=True).astype(jnp.int32)
        ri = jnp.where(lane_o == kk, idxs[kk], ri)
        ri = jnp.where(lane_o == TOP_K + kk, rank, ri)
        rg = jnp.where(lane_o == kk, es[kk] / den, rg)
    ri_ref[0] = ri
    rg_ref[0] = rg


def _mix(x, mod, proj, o_fw, o_bw, o_b, hgrn_norm, g_post_mix, g_pre_ffn, wa, wb, wo, rw, rb):
    b, t, d = x.shape
    tm = min(MIX_TM, t)
    assert tm == MIX_TM
    row = lambda a: a.reshape(1, -1)
    tok = lambda w, col: pl.BlockSpec((1, tm, w), lambda bi, i: (bi, i, col))
    full = lambda a: pl.BlockSpec(a.shape, lambda bi, i: (0,) * a.ndim)
    gn, gpm, gpf, rbr = row(hgrn_norm), row(g_post_mix), row(g_pre_ffn), row(rb)
    outs = pl.pallas_call(
        _mix_kernel,
        grid=(b, t // tm),
        in_specs=[tok(d, 0),
                  pl.BlockSpec((1, 6, d), lambda bi, i: (bi, 0, 0)),
                  tok(d, 0), tok(d, 0),
                  tok(d, 4),
                  tok(d, COL_GATES // d), tok(d, COL_GATES // d + 1),
                  tok(NA_W, 0),
                  full(gn), full(gpm), full(gpf), full(wa), full(wb), full(wo), full(rw), full(rbr)],
        out_specs=[tok(d, 0), tok(d, 0), tok(128, 0), tok(128, 0),
                   pl.BlockSpec((8, N_EXPERTS), lambda bi, i: (0, 0))],
        out_shape=[jax.ShapeDtypeStruct((b, t, d), F32), jax.ShapeDtypeStruct((b, t, d), F32),
                   jax.ShapeDtypeStruct((b, t, 128), jnp.int32), jax.ShapeDtypeStruct((b, t, 128), F32),
                   jax.ShapeDtypeStruct((8, N_EXPERTS), F32)],
        scratch_shapes=[pltpu.VMEM((tm, tm), BF16), pltpu.VMEM((1, N_EXPERTS), F32)],
        compiler_params=_cparams(("arbitrary", "arbitrary")),
        name="mix",
    )(x, mod, o_fw, o_bw, proj, proj, proj, o_b, gn, gpm, gpf, wa, wb, wo, rw, rbr)
    return outs


def _dispatch_kernel(dest_hbm, x_ref, xs_hbm, idx_smem, sem_idx, sem):
    i = pl.program_id(0)
    n_idx = MOVE_TOK * TOP_K
    cp = pltpu.make_async_copy(dest_hbm.at[pl.ds(i * n_idx, n_idx)], idx_smem, sem_idx)
    cp.start()
    cp.wait()

    def row_copy(tt, kk):
        return pltpu.make_async_copy(x_ref.at[pl.ds(tt, 1)],
                                     xs_hbm.at[pl.ds(idx_smem[tt * TOP_K + kk], 1)], sem)

    def start(tt, carry):
        for kk in range(TOP_K):
            row_copy(tt, kk).start()
        return carry

    def wait(tt, carry):
        for kk in range(TOP_K):
            row_copy(tt, kk).wait()
        return carry

    lax.fori_loop(0, MOVE_TOK, start, 0)
    lax.fori_loop(0, MOVE_TOK, wait, 0)


def _dispatch(xn, dest_flat, n_rows):
    n, d = xn.shape
    return pl.pallas_call(
        _dispatch_kernel,
        grid=(n // MOVE_TOK,),
        in_specs=[pl.BlockSpec(memory_space=pl.ANY), pl.BlockSpec((MOVE_TOK, d), lambda i: (i, 0))],
        out_specs=pl.BlockSpec(memory_space=pl.ANY),
        out_shape=jax.ShapeDtypeStruct((n_rows, d), F32),
        scratch_shapes=[pltpu.SMEM((MOVE_TOK * TOP_K,), jnp.int32),
                        pltpu.SemaphoreType.DMA, pltpu.SemaphoreType.DMA],
        compiler_params=_cparams(("arbitrary",)),
        name="dispatch",
    )(dest_flat, xn)


GU_SLAB = 256


def _gate_up_layout(w_gate_up, b_gate_up):
    e, k, n2 = w_gate_up.shape
    half = GU_SLAB // 2
    pm = np.zeros((GU_SLAB, GU_SLAB), np.float32)
    pm[2 * np.arange(half), np.arange(half)] = 1.0
    pm[2 * np.arange(half) + 1, half + np.arange(half)] = 1.0
    w4 = w_gate_up.astype(BF16).reshape(e, k, n2 // GU_SLAB, GU_SLAB)
    w = jnp.einsum('eksc,cd->eksd', w4, jnp.asarray(pm, BF16), preferred_element_type=BF16).reshape(e, k, n2)
    perm = np.concatenate([s * GU_SLAB + np.concatenate([2 * np.arange(half), 2 * np.arange(half) + 1])
                           for s in range(n2 // GU_SLAB)])
    return w, b_gate_up[:, None, perm]


def _expert_kernel(be_ref, bv_ref, x_ref, wgu_ref, bgu_ref, wd_ref, bd_ref, o_ref, a_ref):
    i = pl.program_id(0)
    valid = bv_ref[i]
    half = GU_SLAB // 2

    @pl.when(valid > 0)
    def _():
        rowi = lax.broadcasted_iota(jnp.int32, (MOE_BLK, 1), 0)
        x = jnp.where(rowi < valid, x_ref[...], 0.0).astype(BF16)
        for s in range(2 * D_FF // GU_SLAB):
            cols = slice(s * GU_SLAB, (s + 1) * GU_SLAB)
            h = _dot(x, wgu_ref[0, :, cols]) + bgu_ref[0, :, cols]
            glu = jnp.minimum(h[:, :half], SWIGLU_LIMIT)
            lin = jnp.clip(h[:, half:], -SWIGLU_LIMIT, SWIGLU_LIMIT)
            a = glu * _sigmoid(SWIGLU_ALPHA * glu) * (lin + 1.0)
            a_ref[:, s * half:(s + 1) * half] = a.astype(BF16)
        o_ref[...] = _dot(a_ref[...], wd_ref[0]) + bd_ref[0]

    @pl.when(valid <= 0)
    def _():
        o_ref[...] = jnp.zeros_like(o_ref)


def _experts(xs, block_e, block_valid, wgu, bgu, wd, bd):
    n_rows, d = xs.shape
    n_blocks = n_rows // MOE_BLK
    emap = lambda i, be, bv: (be[i], 0, 0)
    return pl.pallas_call(
        _expert_kernel,
        grid_spec=pltpu.PrefetchScalarGridSpec(
            num_scalar_prefetch=2,
            grid=(n_blocks,),
            in_specs=[pl.BlockSpec((MOE_BLK, d), lambda i, be, bv: (i, 0)),
                      pl.BlockSpec((1, d, 2 * D_FF), emap), pl.BlockSpec((1, 1, 2 * D_FF), emap),
                      pl.BlockSpec((1, D_FF, d), emap), pl.BlockSpec((1, 1, d), emap)],
            out_specs=pl.BlockSpec((MOE_BLK, d), lambda i, be, bv: (i, 0)),
            scratch_shapes=[pltpu.VMEM((MOE_BLK, D_FF), BF16)],
        ),
        out_shape=jax.ShapeDtypeStruct((n_rows, d), F32),
        compiler_params=_cparams(("arbitrary",)),
        name="experts",
    )(block_e, block_valid, xs, wgu, bgu, wd, bd)


def _combine_kernel(dest_hbm, y_hbm, x1_ref, mod_ref, rg_ref, gpo_ref, o_ref, idx_smem, buf, sem_idx, sem):
    i = pl.program_id(0)
    n_idx = MOVE_TOK * TOP_K
    cp = pltpu.make_async_copy(dest_hbm.at[pl.ds(i * n_idx, n_idx)], idx_smem, sem_idx)
    cp.start()
    cp.wait()

    def row_copy(tt, kk):
        return pltpu.make_async_copy(y_hbm.at[pl.ds(idx_smem[tt * TOP_K + kk], 1)],
                                     buf.at[kk, pl.ds(tt, 1)], sem)

    def start(tt, carry):
        for kk in range(TOP_K):
            row_copy(tt, kk).start()
        return carry

    def wait(tt, carry):
        for kk in range(TOP_K):
            row_copy(tt, kk).wait()
        return carry

    lax.fori_loop(0, MOVE_TOK, start, 0)
    lax.fori_loop(0, MOVE_TOK, wait, 0)

    rg = rg_ref[...]
    h = rg[:, 0:1] * buf[0]
    for kk in range(1, TOP_K):
        h = h + rg[:, kk:kk + 1] * buf[kk]
    gt2 = mod_ref[0, 5:6, :]
    o_ref[...] = x1_ref[...] + gt2 * _rms(h, gpo_ref[...])


def _combine(y, dest_flat, x1, mod, rg, g_post_ffn, tokens_per_batch):
    n, d = x1.shape
    per_b = tokens_per_batch // MOVE_TOK
    gpo = g_post_ffn.reshape(1, d)
    return pl.pallas_call(
        _combine_kernel,
        grid=(n // MOVE_TOK,),
        in_specs=[pl.BlockSpec(memory_space=pl.ANY), pl.BlockSpec(memory_space=pl.ANY),
                  pl.BlockSpec((MOVE_TOK, d), lambda i: (i, 0)),
                  pl.BlockSpec((1, 6, d), lambda i: (i // per_b, 0, 0)),
                  pl.BlockSpec((MOVE_TOK, 128), lambda i: (i, 0)),
                  pl.BlockSpec((1, d), lambda i: (0, 0))],
        out_specs=pl.BlockSpec((MOVE_TOK, d), lambda i: (i, 0)),
        out_shape=jax.ShapeDtypeStruct((n, d), F32),
        scratch_shapes=[pltpu.SMEM((MOVE_TOK * TOP_K,), jnp.int32),
                        pltpu.VMEM((TOP_K, MOVE_TOK, d), F32),
                        pltpu.SemaphoreType.DMA, pltpu.SemaphoreType.DMA],
        compiler_params=_cparams(("arbitrary",)),
        name="combine",
    )(dest_flat, y, x1, mod, rg, gpo)


def _moe_plan(ri, counts, n_tok):
    idx = ri[:, :TOP_K]
    rank = ri[:, TOP_K:2 * TOP_K]
    cnt = counts.astype(jnp.int32)
    padded = (cnt + MOE_BLK - 1) // MOE_BLK * MOE_BLK
    pend = jnp.cumsum(padded)
    pstart = pend - padded
    dest = pstart[idx] + rank
    n_blocks = (n_tok * TOP_K + MOE_BLK - 1) // MOE_BLK + N_EXPERTS
    bstart = jnp.arange(n_blocks, dtype=jnp.int32) * MOE_BLK
    block_e = jnp.minimum(jnp.sum((bstart[:, None] >= pend[None, :]).astype(jnp.int32), axis=1), N_EXPERTS - 1)
    block_valid = jnp.clip(pstart[block_e] + cnt[block_e] - bstart, 0, MOE_BLK).astype(jnp.int32)
    return dest.reshape(-1).astype(jnp.int32), block_e, block_valid, n_blocks


def _layer(x, mod, p):
    b, t, d = x.shape
    n = b * t
    proj = _in_proj(x, mod, p['g_pre_mix'], p['w_in'])
    o_fw, o_bw = _hgrn(proj, p['lb_logits'])
    o_b = _natten(proj, p['na_tabs'])
    x1, xn, ri, rg, counts = _mix(x, mod, proj, o_fw, o_bw, o_b, p['hgrn_norm'], p['g_post_mix'], p['g_pre_ffn'],
                                  p['w_branch_a'], p['w_branch_b'], p['w_out'], p['router_w'], p['router_b'])
    ri = ri.reshape(n, 128)
    rg = rg.reshape(n, 128)
    dest, block_e, block_valid, n_blocks = _moe_plan(ri, counts[0], n)
    xs = _dispatch(xn.reshape(n, d), dest, n_blocks * MOE_BLK)
    y = _experts(xs, block_e, block_valid, p['w_gu'], p['b_gu'], p['w_d'], p['b_d'])
    out = _combine(y, dest, x1.reshape(n, d), mod, rg, p['g_post_ffn'], t)
    return out.reshape(b, t, d)


def kernel(x_prompt, x_sample, c_prompt, c_sample, ada_w, ada_b, g_pre_mix, g_post_mix, g_pre_ffn, g_post_ffn,
           w_in, hgrn_lb_logits, hgrn_norm, na_rpb, w_branch_a, w_branch_b, w_out, router_w, router_b,
           w_gate_up, b_gate_up, w_down, b_down):
    d = D_MODEL
    bp, bs = c_prompt.shape[0], c_sample.shape[0]
    pad = (-(bp + bs)) % 8
    c_all = jnp.concatenate([c_prompt, c_sample, jnp.zeros((pad, d), F32)], axis=0)
    mod = _ada_mod(c_all, ada_w[0], ada_b[0]).reshape(-1, 6, d)

    w = w_in[0]
    w_perm = jnp.concatenate([w[:, :5 * d], w[:, 5 * d + 3 * NA_W:], w[:, 5 * d:5 * d + 3 * NA_W]], axis=1).astype(BF16)
    w_gu, b_gu = _gate_up_layout(w_gate_up[0], b_gate_up[0])
    p = dict(
        g_pre_mix=g_pre_mix[0], g_post_mix=g_post_mix[0], g_pre_ffn=g_pre_ffn[0], g_post_ffn=g_post_ffn[0],
        w_in=w_perm, lb_logits=hgrn_lb_logits.astype(F32), hgrn_norm=hgrn_norm[0],
        na_tabs=_na_bias_tables(na_rpb[0]),
        w_branch_a=w_branch_a[0].astype(BF16), w_branch_b=w_branch_b[0].astype(BF16), w_out=w_out[0].astype(BF16),
        router_w=router_w[0].astype(BF16), router_b=router_b[0],
        w_gu=w_gu, b_gu=b_gu,
        w_d=w_down[0].astype(BF16), b_d=b_down[0][:, None, :],
    )
    y_p = _layer(x_prompt, mod[:bp], p)
    y_s = _layer(x_sample, mod[bp:bp + bs], p)
    return (y_p, y_s)
```

```python
import functools

import numpy as np
import jax
import jax.numpy as jnp
from jax import lax
from jax.experimental import pallas as pl
from jax.experimental.pallas import tpu as pltpu

F32 = jnp.float32
BF16 = jnp.bfloat16

D_MODEL = 1024
GRID_W = 64
HA_HEADS = 8
HA_DK = 128
NA_HEADS = 8
NA_DH = 64
NA_W = NA_HEADS * NA_DH
NA_KH = 8
NA_KW = 16
N_EXPERTS = 32
TOP_K = 4
D_FF = 1024
SWIGLU_ALPHA = 1.702
SWIGLU_LIMIT = 7.0
RMS_EPS = 1e-6
NEG_BIG = -1e30

COL_A = 0
COL_GATES = 5 * 1024
COL_NA = 7 * 1024
PROJ_COLS = 7 * 1024 + 3 * NA_W

HGRN_CHUNK = 64
HGRN_TILE = 256
NA_ROWS_PER_STEP = 8
PROJ_TM = 1024
PROJ_TN = 2176
MIX_TM = 512
MOE_BLK = 512
DISP_TOK = 1024
COMB_TOK = 256
WAIT_GROUP = 64
VMEM_LIMIT = 56 * 1024 * 1024


def _cparams(sem):
    return pltpu.CompilerParams(dimension_semantics=sem, vmem_limit_bytes=VMEM_LIMIT)


def _dot(a, b):
    return jnp.dot(a, b, preferred_element_type=F32)


def _dot_nt(a, b):
    return lax.dot_general(a, b, (((1,), (1,)), ((), ())), preferred_element_type=F32)


def _dot_tn(a, b):
    return lax.dot_general(a, b, (((0,), (0,)), ((), ())), preferred_element_type=F32)


def _split(a):
    hi = a.astype(BF16)
    lo = (a - hi.astype(F32)).astype(BF16)
    return hi, lo


def _dot3(a, w):
    ah, al = _split(a)
    wh, wl = _split(w)
    return _dot(ah, wh) + (_dot(al, wh) + _dot(ah, wl))


def _sigmoid(x):
    return 1.0 / (1.0 + jnp.exp(-x))


def _rms(x, g):
    return x * lax.rsqrt(jnp.mean(x * x, axis=-1, keepdims=True) + RMS_EPS) * g


def _ada_kernel(c_ref, w_ref, b_ref, o_ref):
    c = c_ref[...]
    o_ref[...] = _dot3(c * _sigmoid(c), w_ref[...]) + b_ref[...]


def _ada_mod(c, ada_w, ada_b):
    bp, d = c.shape
    n = ada_w.shape[1]
    tn = 1536
    return pl.pallas_call(
        _ada_kernel,
        grid=(n // tn,),
        in_specs=[pl.BlockSpec((bp, d), lambda j: (0, 0)),
                  pl.BlockSpec((d, tn), lambda j: (0, j)),
                  pl.BlockSpec((1, tn), lambda j: (0, j))],
        out_specs=pl.BlockSpec((bp, tn), lambda j: (0, j)),
        out_shape=jax.ShapeDtypeStruct((bp, n), F32),
        compiler_params=_cparams(("arbitrary",)),
        name="ada_mod",
    )(c, ada_w, ada_b.reshape(1, n))


def _in_proj_kernel(x_ref, mod_ref, g_ref, w_ref, o_ref, xn_ref):
    @pl.when(pl.program_id(2) == 0)
    def _():
        sh = mod_ref[0, 0:1, :]
        sc = mod_ref[0, 1:2, :]
        xn = _rms(x_ref[0], g_ref[...]) * (1.0 + sc) + sh
        xn_ref[...] = xn.astype(BF16)

    o_ref[0] = _dot(xn_ref[...], w_ref[...]).astype(BF16)


def _in_proj(x, mod, g_pre, w_in_bf):
    b, t, d = x.shape
    tm = min(PROJ_TM, t)
    n = w_in_bf.shape[1]
    return pl.pallas_call(
        _in_proj_kernel,
        grid=(b, t // tm, n // PROJ_TN),
        in_specs=[pl.BlockSpec((1, tm, d), lambda bi, i, j: (bi, i, 0)),
                  pl.BlockSpec((1, 6, d), lambda bi, i, j: (bi, 0, 0)),
                  pl.BlockSpec((1, d), lambda bi, i, j: (0, 0)),
                  pl.BlockSpec((d, PROJ_TN), lambda bi, i, j: (0, j))],
        out_specs=pl.BlockSpec((1, tm, PROJ_TN), lambda bi, i, j: (bi, i, j)),
        out_shape=jax.ShapeDtypeStruct((b, t, n), BF16),
        scratch_shapes=[pltpu.VMEM((tm, d), BF16)],
        compiler_params=_cparams(("arbitrary", "arbitrary", "arbitrary")),
        name="in_proj",
    )(x, mod, g_pre.reshape(1, d), w_in_bf)


def _hgrn_dir(q_ref, f_ref, i_ref, o_ref, s_ref, lbd, row0, reverse):
    c = HGRN_CHUNK
    rows = pl.ds(row0, c)
    q = q_ref[0, rows, :].astype(F32)
    fpre = f_ref[0, rows, :].astype(F32)
    v = i_ref[0, rows, :]
    f = lbd + (1.0 - lbd) * _sigmoid(fpre)
    logf = jnp.log(f)
    k = 1.0 - f
    ri = lax.broadcasted_iota(jnp.int32, (c, c), 0)
    ci = lax.broadcasted_iota(jnp.int32, (c, c), 1)
    keep = (ci >= ri) if reverse else (ri >= ci)
    tri = jnp.where(keep, 1.0, 0.0).astype(BF16)
    hi, lo = _split(logf)
    b = _dot(tri, hi) + _dot(tri, lo)
    if reverse:
        b_mid = b[c // 2:c // 2 + 1, :]
        b_end = b[0:1, :]
    else:
        b_mid = b[c // 2 - 1:c // 2, :]
        b_end = b[c - 1:c, :]
    qm = (q * jnp.exp(b - b_mid)).astype(BF16)
    km = (k * jnp.exp(b_mid - b)).astype(BF16)
    qi = (q * jnp.exp(b)).astype(BF16)
    ks = (k * jnp.exp(b_end - b)).astype(BF16)
    dec = jnp.exp(b_end)
    for h in range(HA_HEADS):
        sl = slice(h * HA_DK, (h + 1) * HA_DK)
        a = _dot_nt(qm[:, sl], km[:, sl])
        a = jnp.where(keep, a, 0.0).astype(BF16)
        st = s_ref[h]
        o = _dot(a, v[:, sl]) + _dot_nt(qi[:, sl], st.astype(BF16))
        s_ref[h] = st * dec[:, sl] + _dot_tn(v[:, sl], ks[:, sl])
        o_ref[0, rows, sl] = o.astype(BF16)


def _hgrn_kernel(lbl_ref, qf_ref, ff_ref, if_ref, qb_ref, fb_ref, ib_ref, of_ref, ob_ref, sf_ref, sb_ref):
    @pl.when(pl.program_id(1) == 0)
    def _():
        sf_ref[...] = jnp.zeros_like(sf_ref)
        sb_ref[...] = jnp.zeros_like(sb_ref)

    l0 = lbl_ref[0]
    l1 = lbl_ref[1]
    m = jnp.maximum(l0, l1)
    e0 = jnp.exp(l0 - m)
    e1 = jnp.exp(l1 - m)
    lb = e0 / (e0 + e1)
    n_chunks = HGRN_TILE // HGRN_CHUNK

    for ci in range(n_chunks):
        _hgrn_dir(qf_ref, ff_ref, if_ref, of_ref, sf_ref, lb[0:1, :], ci * HGRN_CHUNK, False)
        _hgrn_dir(qb_ref, fb_ref, ib_ref, ob_ref, sb_ref, lb[1:2, :], (n_chunks - 1 - ci) * HGRN_CHUNK, True)


def _hgrn(proj, lb_logits):
    b, t, _ = proj.shape
    tt = HGRN_TILE
    nt = t // tt
    d = D_MODEL

    def fwd(col):
        return pl.BlockSpec((1, tt, d), lambda bi, ti: (bi, ti, col))

    def bwd(col):
        return pl.BlockSpec((1, tt, d), lambda bi, ti: (bi, nt - 1 - ti, col))

    return pl.pallas_call(
        _hgrn_kernel,
        grid=(b, nt),
        in_specs=[pl.BlockSpec((2, 2, d), lambda bi, ti: (0, 0, 0)),
                  fwd(0), fwd(1), fwd(3), bwd(0), bwd(2), bwd(3)],
        out_specs=[pl.BlockSpec((1, tt, d), lambda bi, ti: (bi, ti, 0)),
                   pl.BlockSpec((1, tt, d), lambda bi, ti: (bi, nt - 1 - ti, 0))],
        out_shape=[jax.ShapeDtypeStruct((b, t, d), BF16), jax.ShapeDtypeStruct((b, t, d), BF16)],
        scratch_shapes=[pltpu.VMEM((HA_HEADS, HA_DK, HA_DK), F32), pltpu.VMEM((HA_HEADS, HA_DK, HA_DK), F32)],
        compiler_params=_cparams(("arbitrary", "arbitrary")),
        name="hgrn",
    )(lb_logits, proj, proj, proj, proj, proj, proj)


def _na_bias_tables(rpb):
    q = np.arange(GRID_W)[:, None]
    kc = np.arange(GRID_W)[None, :]
    c0 = np.clip(q - NA_KW // 2, 0, GRID_W - NA_KW)
    ok = (kc >= c0) & (kc < c0 + NA_KW)
    dc = np.clip(kc - q + (NA_KW - 1), 0, 2 * NA_KW - 2)
    base = jnp.where(ok[None, None], jnp.take(rpb.astype(F32), dc, axis=2), NEG_BIG)
    tabs = []
    for dr0 in range(NA_KH):
        w = base[:, dr0:dr0 + NA_KH]
        w = w.transpose(0, 2, 1, 3).reshape(NA_HEADS // 2, 2 * GRID_W, NA_KH * GRID_W)
        tabs.append(w)
    return jnp.stack(tabs, axis=1)


def _na_kernel(q_ref, k_ref, v_ref, tab_ref, o_ref, *, grid_rows):
    g = pl.program_id(1)
    lane = lax.broadcasted_iota(jnp.int32, (GRID_W, 2 * NA_DH), 1)
    low = lane < NA_DH
    scale = NA_DH ** -0.5

    def body(rr, carry):
        r = g * NA_ROWS_PER_STEP + rr
        r0 = jnp.clip(r - NA_KH // 2, 0, grid_rows - NA_KH)
        dr0 = r0 - r + (NA_KH - 1)
        qrow = pl.ds(pl.multiple_of(rr * GRID_W, GRID_W), GRID_W)
        krow = pl.ds(pl.multiple_of(r0 * GRID_W, GRID_W), NA_KH * GRID_W)
        for j in range(NA_HEADS // 2):
            sl = slice(j * 2 * NA_DH, (j + 1) * 2 * NA_DH)
            qp = q_ref[0, qrow, sl]
            zero = jnp.zeros_like(qp)
            ql = jnp.concatenate([jnp.where(low, qp, zero), jnp.where(low, zero, qp)], axis=0)
            s = _dot_nt(ql, k_ref[0, krow, sl]) * scale + tab_ref[j, dr0]
            m = jnp.max(s, axis=-1, keepdims=True)
            p = jnp.exp(s - m)
            l = jnp.sum(p, axis=-1, keepdims=True)
            pv = _dot(p.astype(BF16), v_ref[0, krow, sl]) / l
            o_ref[0, qrow, sl] = jnp.where(low, pv[:GRID_W], pv[GRID_W:]).astype(BF16)
        return carry

    lax.fori_loop(0, NA_ROWS_PER_STEP, body, 0, unroll=2)


def _natten(proj, tabs):
    b, t, _ = proj.shape
    grid_rows = t // GRID_W
    tq = NA_ROWS_PER_STEP * GRID_W
    cq, ck, cv = (COL_NA // NA_W, COL_NA // NA_W + 1, COL_NA // NA_W + 2)
    return pl.pallas_call(
        functools.partial(_na_kernel, grid_rows=grid_rows),
        grid=(b, t // tq),
        in_specs=[pl.BlockSpec((1, tq, NA_W), lambda bi, gi: (bi, gi, cq)),
                  pl.BlockSpec((1, t, NA_W), lambda bi, gi: (bi, 0, ck)),
                  pl.BlockSpec((1, t, NA_W), lambda bi, gi: (bi, 0, cv)),
                  pl.BlockSpec(tabs.shape, lambda bi, gi: (0, 0, 0, 0))],
        out_specs=pl.BlockSpec((1, tq, NA_W), lambda bi, gi: (bi, gi, 0)),
        out_shape=jax.ShapeDtypeStruct((b, t, NA_W), BF16),
        compiler_params=_cparams(("arbitrary", "arbitrary")),
        name="natten",
    )(proj, proj, proj, tabs)


def _mix_kernel(x_ref, mod_ref, of_ref, ob_ref, g_ref, ga_ref, gb_ref, nb_ref,
                gn_ref, gpm_ref, gpf_ref, wa_ref, wb_ref, wo_ref, rw_ref, rb_ref,
                x1_ref, xn_ref, ri_ref, rg_ref, cnt_ref, tri_ref, carry_ref):
    tm = MIX_TM
    first = (pl.program_id(0) == 0) & (pl.program_id(1) == 0)

    @pl.when(first)
    def _():
        r = lax.broadcasted_iota(jnp.int32, (tm, tm), 0)
        c = lax.broadcasted_iota(jnp.int32, (tm, tm), 1)
        tri_ref[...] = jnp.where(r > c, 1.0, 0.0).astype(BF16)
        carry_ref[...] = jnp.zeros_like(carry_ref)

    sc2 = mod_ref[0, 4:5, :]
    sh2 = mod_ref[0, 3:4, :]
    gt1 = mod_ref[0, 2:3, :]

    o = of_ref[0].astype(F32) + ob_ref[0].astype(F32)
    parts = []
    for h in range(HA_HEADS):
        oh = o[:, h * HA_DK:(h + 1) * HA_DK]
        parts.append(oh * lax.rsqrt(jnp.mean(oh * oh, axis=-1, keepdims=True) + RMS_EPS))
    g = g_ref[0].astype(F32)
    oa = jnp.concatenate(parts, axis=-1) * gn_ref[...] * (g * _sigmoid(g))
    ya = _dot(oa.astype(BF16), wa_ref[...])
    yb = _dot(nb_ref[0], wb_ref[...])
    mix = _sigmoid(ga_ref[0].astype(F32)) * ya + _sigmoid(gb_ref[0].astype(F32)) * yb
    mo = _dot(mix.astype(BF16), wo_ref[...])
    x1 = x_ref[0] + gt1 * _rms(mo, gpm_ref[...])
    x1_ref[0] = x1

    xn = _rms(x1, gpf_ref[...]) * (1.0 + sc2) + sh2
    xn_ref[0] = xn
    logits = _dot(xn.astype(BF16), rw_ref[...]) + rb_ref[...]
    lane = lax.broadcasted_iota(jnp.int32, (tm, N_EXPERTS), 1)
    work = logits
    vals, idxs, hots = [], [], []
    for _ in range(TOP_K):
        mv = jnp.max(work, axis=-1, keepdims=True)
        mi = jnp.min(jnp.where(work == mv, lane, N_EXPERTS), axis=-1, keepdims=True)
        hot = lane == mi
        vals.append(mv)
        idxs.append(mi)
        hots.append(hot)
        work = jnp.where(hot, -jnp.inf, work)
    es = [jnp.exp(v - vals[0]) for v in vals]
    den = es[0] + es[1] + es[2] + es[3]

    cnt = sum(jnp.where(h, 1.0, 0.0) for h in hots)
    before = _dot(tri_ref[...], cnt.astype(BF16)) + carry_ref[...]
    carry_ref[...] = carry_ref[...] + jnp.sum(cnt, axis=0, keepdims=True)
    cnt_ref[...] = jnp.broadcast_to(carry_ref[...], cnt_ref.shape)

    lane_o = lax.broadcasted_iota(jnp.int32, (tm, 128), 1)
    ri = jnp.zeros((tm, 128), jnp.int32)
    rg = jnp.zeros((tm, 128), F32)
    for kk in range(TOP_K):
        rank = jnp.sum(jnp.where(hots[kk], before, 0.0), axis=-1, keepdims=True).astype(jnp.int32)
        ri = jnp.where(lane_o == kk, idxs[kk], ri)
        ri = jnp.where(lane_o == TOP_K + kk, rank, ri)
        rg = jnp.where(lane_o == kk, es[kk] / den, rg)
    ri_ref[0] = ri
    rg_ref[0] = rg


def _mix(x, mod, proj, o_fw, o_bw, o_b, hgrn_norm, g_post_mix, g_pre_ffn, wa, wb, wo, rw, rb):
    b, t, d = x.shape
    tm = min(MIX_TM, t)
    assert tm == MIX_TM
    row = lambda a: a.reshape(1, -1)
    tok = lambda w, col: pl.BlockSpec((1, tm, w), lambda bi, i: (bi, i, col))
    full = lambda a: pl.BlockSpec(a.shape, lambda bi, i: (0,) * a.ndim)
    gn, gpm, gpf, rbr = row(hgrn_norm), row(g_post_mix), row(g_pre_ffn), row(rb)
    outs = pl.pallas_call(
        _mix_kernel,
        grid=(b, t // tm),
        in_specs=[tok(d, 0),
                  pl.BlockSpec((1, 6, d), lambda bi, i: (bi, 0, 0)),
                  tok(d, 0), tok(d, 0),
                  tok(d, 4),
                  tok(d, COL_GATES // d), tok(d, COL_GATES // d + 1),
                  tok(NA_W, 0),
                  full(gn), full(gpm), full(gpf), full(wa), full(wb), full(wo), full(rw), full(rbr)],
        out_specs=[tok(d, 0), tok(d, 0), tok(128, 0), tok(128, 0),
                   pl.BlockSpec((8, N_EXPERTS), lambda bi, i: (0, 0))],
        out_shape=[jax.ShapeDtypeStruct((b, t, d), F32), jax.ShapeDtypeStruct((b, t, d), F32),
                   jax.ShapeDtypeStruct((b, t, 128), jnp.int32), jax.ShapeDtypeStruct((b, t, 128), F32),
                   jax.ShapeDtypeStruct((8, N_EXPERTS), F32)],
        scratch_shapes=[pltpu.VMEM((tm, tm), BF16), pltpu.VMEM((1, N_EXPERTS), F32)],
        compiler_params=_cparams(("arbitrary", "arbitrary")),
        name="mix",
    )(x, mod, o_fw, o_bw, proj, proj, proj, o_b, gn, gpm, gpf, wa, wb, wo, rw, rbr)
    return outs


def _wait_rows(row_copy, n):
    def body(_, carry):
        for _ in range(WAIT_GROUP):
            row_copy().wait()
        return carry

    lax.fori_loop(0, n // WAIT_GROUP, body, 0)


def _dispatch_kernel(dest_hbm, x_ref, xs_hbm, idx_smem, sem_idx, sem):
    i = pl.program_id(0)
    n_idx = DISP_TOK * TOP_K
    cp = pltpu.make_async_copy(dest_hbm.at[pl.ds(i * n_idx, n_idx)], idx_smem, sem_idx)
    cp.start()
    cp.wait()

    def start(tt, carry):
        for kk in range(TOP_K):
            pltpu.make_async_copy(x_ref.at[pl.ds(tt, 1)],
                                  xs_hbm.at[pl.ds(idx_smem[tt * TOP_K + kk], 1)], sem).start()
        return carry

    lax.fori_loop(0, DISP_TOK, start, 0, unroll=4)
    _wait_rows(lambda: pltpu.make_async_copy(x_ref.at[pl.ds(0, 1)], xs_hbm.at[pl.ds(0, 1)], sem), n_idx)


def _dispatch(xn, dest_flat, n_rows):
    n, d = xn.shape
    return pl.pallas_call(
        _dispatch_kernel,
        grid=(n // DISP_TOK,),
        in_specs=[pl.BlockSpec(memory_space=pl.ANY), pl.BlockSpec((DISP_TOK, d), lambda i: (i, 0))],
        out_specs=pl.BlockSpec(memory_space=pl.ANY),
        out_shape=jax.ShapeDtypeStruct((n_rows, d), F32),
        scratch_shapes=[pltpu.SMEM((DISP_TOK * TOP_K,), jnp.int32),
                        pltpu.SemaphoreType.DMA, pltpu.SemaphoreType.DMA],
        compiler_params=_cparams(("arbitrary",)),
        name="dispatch",
    )(dest_flat, xn)


GU_SLAB = 256


def _gate_up_layout(w_gate_up, b_gate_up):
    e, k, n2 = w_gate_up.shape
    half = GU_SLAB // 2
    pm = np.zeros((GU_SLAB, GU_SLAB), np.float32)
    pm[2 * np.arange(half), np.arange(half)] = 1.0
    pm[2 * np.arange(half) + 1, half + np.arange(half)] = 1.0
    w4 = w_gate_up.astype(BF16).reshape(e, k, n2 // GU_SLAB, GU_SLAB)
    w = jnp.einsum('eksc,cd->eksd', w4, jnp.asarray(pm, BF16), preferred_element_type=BF16).reshape(e, k, n2)
    perm = np.concatenate([s * GU_SLAB + np.concatenate([2 * np.arange(half), 2 * np.arange(half) + 1])
                           for s in range(n2 // GU_SLAB)])
    return w, b_gate_up[:, None, perm]


def _expert_kernel(be_ref, bv_ref, x_ref, wgu_ref, bgu_ref, wd_ref, bd_ref, o_ref, a_ref):
    i = pl.program_id(0)
    valid = bv_ref[i]
    half = GU_SLAB // 2

    @pl.when(valid > 0)
    def _():
        rowi = lax.broadcasted_iota(jnp.int32, (MOE_BLK, 1), 0)
        x = jnp.where(rowi < valid, x_ref[...], 0.0).astype(BF16)
        for s in range(2 * D_FF // GU_SLAB):
            cols = slice(s * GU_SLAB, (s + 1) * GU_SLAB)
            h = _dot(x, wgu_ref[0, :, cols]) + bgu_ref[0, :, cols]
            glu = jnp.minimum(h[:, :half], SWIGLU_LIMIT)
            lin = jnp.clip(h[:, half:], -SWIGLU_LIMIT, SWIGLU_LIMIT)
            a = glu * _sigmoid(SWIGLU_ALPHA * glu) * (lin + 1.0)
            a_ref[:, s * half:(s + 1) * half] = a.astype(BF16)
        o_ref[...] = _dot(a_ref[...], wd_ref[0]) + bd_ref[0]

    @pl.when(valid <= 0)
    def _():
        o_ref[...] = jnp.zeros_like(o_ref)


def _experts(xs, block_e, block_valid, wgu, bgu, wd, bd):
    n_rows, d = xs.shape
    n_blocks = n_rows // MOE_BLK
    emap = lambda i, be, bv: (be[i], 0, 0)
    return pl.pallas_call(
        _expert_kernel,
        grid_spec=pltpu.PrefetchScalarGridSpec(
            num_scalar_prefetch=2,
            grid=(n_blocks,),
            in_specs=[pl.BlockSpec((MOE_BLK, d), lambda i, be, bv: (i, 0)),
                      pl.BlockSpec((1, d, 2 * D_FF), emap), pl.BlockSpec((1, 1, 2 * D_FF), emap),
                      pl.BlockSpec((1, D_FF, d), emap), pl.BlockSpec((1, 1, d), emap)],
            out_specs=pl.BlockSpec((MOE_BLK, d), lambda i, be, bv: (i, 0)),
            scratch_shapes=[pltpu.VMEM((MOE_BLK, D_FF), BF16)],
        ),
        out_shape=jax.ShapeDtypeStruct((n_rows, d), F32),
        compiler_params=_cparams(("arbitrary",)),
        name="experts",
    )(block_e, block_valid, xs, wgu, bgu, wd, bd)


def _combine_kernel(dest_hbm, y_hbm, x1_ref, mod_ref, rg_ref, gpo_ref, o_ref, idx_smem, buf, sem_idx, sem):
    s = pl.program_id(0)
    ns = pl.num_programs(0)
    n_idx = COMB_TOK * TOP_K
    slot = s % 2

    def idx_copy(step, sl):
        return pltpu.make_async_copy(dest_hbm.at[pl.ds(step * n_idx, n_idx)], idx_smem.at[sl], sem_idx.at[sl])

    def issue(sl):
        def start(tt, carry):
            for kk in range(TOP_K):
                pltpu.make_async_copy(y_hbm.at[pl.ds(idx_smem[sl, tt * TOP_K + kk], 1)],
                                      buf.at[sl, kk, pl.ds(tt, 1)], sem.at[sl]).start()
            return carry

        lax.fori_loop(0, COMB_TOK, start, 0, unroll=4)

    @pl.when(s == 0)
    def _():
        idx_copy(0, 0).start()
        idx_copy(0, 0).wait()
        issue(0)

        @pl.when(ns > 1)
        def _():
            idx_copy(1, 1).start()

    @pl.when(s + 1 < ns)
    def _():
        idx_copy(s + 1, 1 - slot).wait()
        issue(1 - slot)

    @pl.when(s + 2 < ns)
    def _():
        idx_copy(s + 2, slot).start()

    _wait_rows(lambda: pltpu.make_async_copy(y_hbm.at[pl.ds(0, 1)], buf.at[slot, 0, pl.ds(0, 1)], sem.at[slot]), n_idx)

    rg = rg_ref[...]
    h = rg[:, 0:1] * buf[slot, 0]
    for kk in range(1, TOP_K):
        h = h + rg[:, kk:kk + 1] * buf[slot, kk]
    gt2 = mod_ref[0, 5:6, :]
    o_ref[...] = x1_ref[...] + gt2 * _rms(h, gpo_ref[...])


def _combine(y, dest_flat, x1, mod, rg, g_post_ffn, tokens_per_batch):
    n, d = x1.shape
    per_b = tokens_per_batch // COMB_TOK
    gpo = g_post_ffn.reshape(1, d)
    return pl.pallas_call(
        _combine_kernel,
        grid=(n // COMB_TOK,),
        in_specs=[pl.BlockSpec(memory_space=pl.ANY), pl.BlockSpec(memory_space=pl.ANY),
                  pl.BlockSpec((COMB_TOK, d), lambda i: (i, 0)),
                  pl.BlockSpec((1, 6, d), lambda i: (i // per_b, 0, 0)),
                  pl.BlockSpec((COMB_TOK, 128), lambda i: (i, 0)),
                  pl.BlockSpec((1, d), lambda i: (0, 0))],
        out_specs=pl.BlockSpec((COMB_TOK, d), lambda i: (i, 0)),
        out_shape=jax.ShapeDtypeStruct((n, d), F32),
        scratch_shapes=[pltpu.SMEM((2, COMB_TOK * TOP_K), jnp.int32),
                        pltpu.VMEM((2, TOP_K, COMB_TOK, d), F32),
                        pltpu.SemaphoreType.DMA((2,)), pltpu.SemaphoreType.DMA((2,))],
        compiler_params=_cparams(("arbitrary",)),
        name="combine",
    )(dest_flat, y, x1, mod, rg, gpo)


def _moe_plan(ri, counts, n_tok):
    idx = ri[:, :TOP_K]
    rank = ri[:, TOP_K:2 * TOP_K]
    cnt = counts.astype(jnp.int32)
    padded = (cnt + MOE_BLK - 1) // MOE_BLK * MOE_BLK
    pend = jnp.cumsum(padded)
    pstart = pend - padded
    dest = pstart[idx] + rank
    n_blocks = (n_tok * TOP_K + MOE_BLK - 1) // MOE_BLK + N_EXPERTS
    bstart = jnp.arange(n_blocks, dtype=jnp.int32) * MOE_BLK
    block_e = jnp.minimum(jnp.sum((bstart[:, None] >= pend[None, :]).astype(jnp.int32), axis=1), N_EXPERTS - 1)
    block_valid = jnp.clip(pstart[block_e] + cnt[block_e] - bstart, 0, MOE_BLK).astype(jnp.int32)
    return dest.reshape(-1).astype(jnp.int32), block_e, block_valid, n_blocks


def _layer(x, mod, p):
    b, t, d = x.shape
    n = b * t
    proj = _in_proj(x, mod, p['g_pre_mix'], p['w_in'])
    o_fw, o_bw = _hgrn(proj, p['lb_logits'])
    o_b = _natten(proj, p['na_tabs'])
    x1, xn, ri, rg, counts = _mix(x, mod, proj, o_fw, o_bw, o_b, p['hgrn_norm'], p['g_post_mix'], p['g_pre_ffn'],
                                  p['w_branch_a'], p['w_branch_b'], p['w_out'], p['router_w'], p['router_b'])
    ri = ri.reshape(n, 128)
    rg = rg.reshape(n, 128)
    dest, block_e, block_valid, n_blocks = _moe_plan(ri, counts[0], n)
    xs = _dispatch(xn.reshape(n, d), dest, n_blocks * MOE_BLK)
    y = _experts(xs, block_e, block_valid, p['w_gu'], p['b_gu'], p['w_d'], p['b_d'])
    out = _combine(y, dest, x1.reshape(n, d), mod, rg, p['g_post_ffn'], t)
    return out.reshape(b, t, d)


def kernel(x_prompt, x_sample, c_prompt, c_sample, ada_w, ada_b, g_pre_mix, g_post_mix, g_pre_ffn, g_post_ffn,
           w_in, hgrn_lb_logits, hgrn_norm, na_rpb, w_branch_a, w_branch_b, w_out, router_w, router_b,
           w_gate_up, b_gate_up, w_down, b_down):
    d = D_MODEL
    bp, bs = c_prompt.shape[0], c_sample.shape[0]
    pad = (-(bp + bs)) % 8
    c_all = jnp.concatenate([c_prompt, c_sample, jnp.zeros((pad, d), F32)], axis=0)
    mod = _ada_mod(c_all, ada_w[0], ada_b[0]).reshape(-1, 6, d)

    w = w_in[0]
    w_perm = jnp.concatenate([w[:, :5 * d], w[:, 5 * d + 3 * NA_W:], w[:, 5 * d:5 * d + 3 * NA_W]], axis=1).astype(BF16)
    w_gu, b_gu = _gate_up_layout(w_gate_up[0], b_gate_up[0])
    p = dict(
        g_pre_mix=g_pre_mix[0], g_post_mix=g_post_mix[0], g_pre_ffn=g_pre_ffn[0], g_post_ffn=g_post_ffn[0],
        w_in=w_perm, lb_logits=hgrn_lb_logits.astype(F32), hgrn_norm=hgrn_norm[0],
        na_tabs=_na_bias_tables(na_rpb[0]),
        w_branch_a=w_branch_a[0].astype(BF16), w_branch_b=w_branch_b[0].astype(BF16), w_out=w_out[0].astype(BF16),
        router_w=router_w[0].astype(BF16), router_b=router_b[0],
        w_gu=w_gu, b_gu=b_gu,
        w_d=w_down[0].astype(BF16), b_d=b_down[0][:, None, :],
    )
    y_p = _layer(x_prompt, mod[:bp], p)
    y_s = _layer(x_sample, mod[bp:bp + bs], p)
    return (y_p, y_s)
```

```python
import functools

import numpy as np
import jax
import jax.numpy as jnp
from jax import lax
from jax.experimental import pallas as pl
from jax.experimental.pallas import tpu as pltpu

F32 = jnp.float32
BF16 = jnp.bfloat16

D_MODEL = 1024
GRID_W = 64
HA_HEADS = 8
HA_DK = 128
NA_HEADS = 8
NA_DH = 64
NA_W = NA_HEADS * NA_DH
NA_KH = 8
NA_KW = 16
N_EXPERTS = 32
TOP_K = 4
D_FF = 1024
SWIGLU_ALPHA = 1.702
SWIGLU_LIMIT = 7.0
RMS_EPS = 1e-6
NEG_BIG = -1e30

COL_A = 0
COL_GATES = 5 * 1024
COL_NA = 7 * 1024
PROJ_COLS = 7 * 1024 + 3 * NA_W

HGRN_CHUNK = 64
HGRN_TILE = 256
NA_ROWS_PER_STEP = 8
PROJ_TM = 1024
PROJ_TN = 2176
MIX_TM = 512
MOE_BLK = 512
DISP_TOK = 1024
COMB_TOK = 256
WAIT_GROUP = 64
ROW_SUB, ROW_LANE = 8, 128
VMEM_LIMIT = 56 * 1024 * 1024


def _cparams(sem):
    return pltpu.CompilerParams(dimension_semantics=sem, vmem_limit_bytes=VMEM_LIMIT)


def _dot(a, b):
    return jnp.dot(a, b, preferred_element_type=F32)


def _dot_nt(a, b):
    return lax.dot_general(a, b, (((1,), (1,)), ((), ())), preferred_element_type=F32)


def _dot_tn(a, b):
    return lax.dot_general(a, b, (((0,), (0,)), ((), ())), preferred_element_type=F32)


def _split(a):
    hi = a.astype(BF16)
    lo = (a - hi.astype(F32)).astype(BF16)
    return hi, lo


def _dot3(a, w):
    ah, al = _split(a)
    wh, wl = _split(w)
    return _dot(ah, wh) + (_dot(al, wh) + _dot(ah, wl))


def _sigmoid(x):
    return 1.0 / (1.0 + jnp.exp(-x))


def _rms(x, g):
    return x * lax.rsqrt(jnp.mean(x * x, axis=-1, keepdims=True) + RMS_EPS) * g


def _ada_kernel(c_ref, w_ref, b_ref, o_ref):
    c = c_ref[...]
    o_ref[...] = _dot3(c * _sigmoid(c), w_ref[...]) + b_ref[...]


def _ada_mod(c, ada_w, ada_b):
    bp, d = c.shape
    n = ada_w.shape[1]
    tn = 1536
    return pl.pallas_call(
        _ada_kernel,
        grid=(n // tn,),
        in_specs=[pl.BlockSpec((bp, d), lambda j: (0, 0)),
                  pl.BlockSpec((d, tn), lambda j: (0, j)),
                  pl.BlockSpec((1, tn), lambda j: (0, j))],
        out_specs=pl.BlockSpec((bp, tn), lambda j: (0, j)),
        out_shape=jax.ShapeDtypeStruct((bp, n), F32),
        compiler_params=_cparams(("arbitrary",)),
        name="ada_mod",
    )(c, ada_w, ada_b.reshape(1, n))


def _in_proj_kernel(x_ref, mod_ref, g_ref, w_ref, o_ref, xn_ref):
    @pl.when(pl.program_id(2) == 0)
    def _():
        sh = mod_ref[0, 0:1, :]
        sc = mod_ref[0, 1:2, :]
        xn = _rms(x_ref[0], g_ref[...]) * (1.0 + sc) + sh
        xn_ref[...] = xn.astype(BF16)

    o_ref[0] = _dot(xn_ref[...], w_ref[...]).astype(BF16)


def _in_proj(x, mod, g_pre, w_in_bf):
    b, t, d = x.shape
    tm = min(PROJ_TM, t)
    n = w_in_bf.shape[1]
    return pl.pallas_call(
        _in_proj_kernel,
        grid=(b, t // tm, n // PROJ_TN),
        in_specs=[pl.BlockSpec((1, tm, d), lambda bi, i, j: (bi, i, 0)),
                  pl.BlockSpec((1, 6, d), lambda bi, i, j: (bi, 0, 0)),
                  pl.BlockSpec((1, d), lambda bi, i, j: (0, 0)),
                  pl.BlockSpec((d, PROJ_TN), lambda bi, i, j: (0, j))],
        out_specs=pl.BlockSpec((1, tm, PROJ_TN), lambda bi, i, j: (bi, i, j)),
        out_shape=jax.ShapeDtypeStruct((b, t, n), BF16),
        scratch_shapes=[pltpu.VMEM((tm, d), BF16)],
        compiler_params=_cparams(("arbitrary", "arbitrary", "arbitrary")),
        name="in_proj",
    )(x, mod, g_pre.reshape(1, d), w_in_bf)


def _hgrn_dir(q_ref, f_ref, i_ref, o_ref, s_ref, lbd, row0, reverse):
    c = HGRN_CHUNK
    rows = pl.ds(row0, c)
    q = q_ref[0, rows, :].astype(F32)
    fpre = f_ref[0, rows, :].astype(F32)
    v = i_ref[0, rows, :]
    f = lbd + (1.0 - lbd) * _sigmoid(fpre)
    logf = jnp.log(f)
    k = 1.0 - f
    ri = lax.broadcasted_iota(jnp.int32, (c, c), 0)
    ci = lax.broadcasted_iota(jnp.int32, (c, c), 1)
    keep = (ci >= ri) if reverse else (ri >= ci)
    tri = jnp.where(keep, 1.0, 0.0).astype(BF16)
    hi, lo = _split(logf)
    b = _dot(tri, hi) + _dot(tri, lo)
    if reverse:
        b_mid = b[c // 2:c // 2 + 1, :]
        b_end = b[0:1, :]
    else:
        b_mid = b[c // 2 - 1:c // 2, :]
        b_end = b[c - 1:c, :]
    qm = (q * jnp.exp(b - b_mid)).astype(BF16)
    km = (k * jnp.exp(b_mid - b)).astype(BF16)
    qi = (q * jnp.exp(b)).astype(BF16)
    ks = (k * jnp.exp(b_end - b)).astype(BF16)
    dec = jnp.exp(b_end)
    for h in range(HA_HEADS):
        sl = slice(h * HA_DK, (h + 1) * HA_DK)
        a = _dot_nt(qm[:, sl], km[:, sl])
        a = jnp.where(keep, a, 0.0).astype(BF16)
        st = s_ref[h]
        o = _dot(a, v[:, sl]) + _dot_nt(qi[:, sl], st.astype(BF16))
        s_ref[h] = st * dec[:, sl] + _dot_tn(v[:, sl], ks[:, sl])
        o_ref[0, rows, sl] = o.astype(BF16)


def _hgrn_kernel(lbl_ref, qf_ref, ff_ref, if_ref, qb_ref, fb_ref, ib_ref, of_ref, ob_ref, sf_ref, sb_ref):
    @pl.when(pl.program_id(1) == 0)
    def _():
        sf_ref[...] = jnp.zeros_like(sf_ref)
        sb_ref[...] = jnp.zeros_like(sb_ref)

    l0 = lbl_ref[0]
    l1 = lbl_ref[1]
    m = jnp.maximum(l0, l1)
    e0 = jnp.exp(l0 - m)
    e1 = jnp.exp(l1 - m)
    lb = e0 / (e0 + e1)
    n_chunks = HGRN_TILE // HGRN_CHUNK

    for ci in range(n_chunks):
        _hgrn_dir(qf_ref, ff_ref, if_ref, of_ref, sf_ref, lb[0:1, :], ci * HGRN_CHUNK, False)
        _hgrn_dir(qb_ref, fb_ref, ib_ref, ob_ref, sb_ref, lb[1:2, :], (n_chunks - 1 - ci) * HGRN_CHUNK, True)


def _hgrn(proj, lb_logits):
    b, t, _ = proj.shape
    tt = HGRN_TILE
    nt = t // tt
    d = D_MODEL

    def fwd(col):
        return pl.BlockSpec((1, tt, d), lambda bi, ti: (bi, ti, col))

    def bwd(col):
        return pl.BlockSpec((1, tt, d), lambda bi, ti: (bi, nt - 1 - ti, col))

    return pl.pallas_call(
        _hgrn_kernel,
        grid=(b, nt),
        in_specs=[pl.BlockSpec((2, 2, d), lambda bi, ti: (0, 0, 0)),
                  fwd(0), fwd(1), fwd(3), bwd(0), bwd(2), bwd(3)],
        out_specs=[pl.BlockSpec((1, tt, d), lambda bi, ti: (bi, ti, 0)),
                   pl.BlockSpec((1, tt, d), lambda bi, ti: (bi, nt - 1 - ti, 0))],
        out_shape=[jax.ShapeDtypeStruct((b, t, d), BF16), jax.ShapeDtypeStruct((b, t, d), BF16)],
        scratch_shapes=[pltpu.VMEM((HA_HEADS, HA_DK, HA_DK), F32), pltpu.VMEM((HA_HEADS, HA_DK, HA_DK), F32)],
        compiler_params=_cparams(("arbitrary", "arbitrary")),
        name="hgrn",
    )(lb_logits, proj, proj, proj, proj, proj, proj)


def _na_bias_tables(rpb):
    q = np.arange(GRID_W)[:, None]
    kc = np.arange(GRID_W)[None, :]
    c0 = np.clip(q - NA_KW // 2, 0, GRID_W - NA_KW)
    ok = (kc >= c0) & (kc < c0 + NA_KW)
    dc = np.clip(kc - q + (NA_KW - 1), 0, 2 * NA_KW - 2)
    base = jnp.where(ok[None, None], jnp.take(rpb.astype(F32), dc, axis=2), NEG_BIG)
    tabs = []
    for dr0 in range(NA_KH):
        w = base[:, dr0:dr0 + NA_KH]
        w = w.transpose(0, 2, 1, 3).reshape(NA_HEADS // 2, 2 * GRID_W, NA_KH * GRID_W)
        tabs.append(w)
    return jnp.stack(tabs, axis=1)


def _na_kernel(q_ref, k_ref, v_ref, tab_ref, o_ref, *, grid_rows):
    g = pl.program_id(1)
    lane = lax.broadcasted_iota(jnp.int32, (GRID_W, 2 * NA_DH), 1)
    low = lane < NA_DH
    scale = NA_DH ** -0.5

    def body(rr, carry):
        r = g * NA_ROWS_PER_STEP + rr
        r0 = jnp.clip(r - NA_KH // 2, 0, grid_rows - NA_KH)
        dr0 = r0 - r + (NA_KH - 1)
        qrow = pl.ds(pl.multiple_of(rr * GRID_W, GRID_W), GRID_W)
        krow = pl.ds(pl.multiple_of(r0 * GRID_W, GRID_W), NA_KH * GRID_W)
        for j in range(NA_HEADS // 2):
            sl = slice(j * 2 * NA_DH, (j + 1) * 2 * NA_DH)
            qp = q_ref[0, qrow, sl]
            zero = jnp.zeros_like(qp)
            ql = jnp.concatenate([jnp.where(low, qp, zero), jnp.where(low, zero, qp)], axis=0)
            s = _dot_nt(ql, k_ref[0, krow, sl]) * scale + tab_ref[j, dr0]
            m = jnp.max(s, axis=-1, keepdims=True)
            p = jnp.exp(s - m)
            l = jnp.sum(p, axis=-1, keepdims=True)
            pv = _dot(p.astype(BF16), v_ref[0, krow, sl]) / l
            o_ref[0, qrow, sl] = jnp.where(low, pv[:GRID_W], pv[GRID_W:]).astype(BF16)
        return carry

    lax.fori_loop(0, NA_ROWS_PER_STEP, body, 0, unroll=2)


def _natten(proj, tabs):
    b, t, _ = proj.shape
    grid_rows = t // GRID_W
    tq = NA_ROWS_PER_STEP * GRID_W
    cq, ck, cv = (COL_NA // NA_W, COL_NA // NA_W + 1, COL_NA // NA_W + 2)
    return pl.pallas_call(
        functools.partial(_na_kernel, grid_rows=grid_rows),
        grid=(b, t // tq),
        in_specs=[pl.BlockSpec((1, tq, NA_W), lambda bi, gi: (bi, gi, cq)),
                  pl.BlockSpec((1, t, NA_W), lambda bi, gi: (bi, 0, ck)),
                  pl.BlockSpec((1, t, NA_W), lambda bi, gi: (bi, 0, cv)),
                  pl.BlockSpec(tabs.shape, lambda bi, gi: (0, 0, 0, 0))],
        out_specs=pl.BlockSpec((1, tq, NA_W), lambda bi, gi: (bi, gi, 0)),
        out_shape=jax.ShapeDtypeStruct((b, t, NA_W), BF16),
        compiler_params=_cparams(("arbitrary", "arbitrary")),
        name="natten",
    )(proj, proj, proj, tabs)


def _mix_kernel(x_ref, mod_ref, of_ref, ob_ref, g_ref, ga_ref, gb_ref, nb_ref,
                gn_ref, gpm_ref, gpf_ref, wa_ref, wb_ref, wo_ref, rw_ref, rb_ref,
                x1_ref, xn_ref, ri_ref, rg_ref, cnt_ref, tri_ref, carry_ref):
    tm = MIX_TM
    first = (pl.program_id(0) == 0) & (pl.program_id(1) == 0)

    @pl.when(first)
    def _():
        r = lax.broadcasted_iota(jnp.int32, (tm, tm), 0)
        c = lax.broadcasted_iota(jnp.int32, (tm, tm), 1)
        tri_ref[...] = jnp.where(r > c, 1.0, 0.0).astype(BF16)
        carry_ref[...] = jnp.zeros_like(carry_ref)

    sc2 = mod_ref[0, 4:5, :]
    sh2 = mod_ref[0, 3:4, :]
    gt1 = mod_ref[0, 2:3, :]

    o = of_ref[0].astype(F32) + ob_ref[0].astype(F32)
    parts = []
    for h in range(HA_HEADS):
        oh = o[:, h * HA_DK:(h + 1) * HA_DK]
        parts.append(oh * lax.rsqrt(jnp.mean(oh * oh, axis=-1, keepdims=True) + RMS_EPS))
    g = g_ref[0].astype(F32)
    oa = jnp.concatenate(parts, axis=-1) * gn_ref[...] * (g * _sigmoid(g))
    ya = _dot(oa.astype(BF16), wa_ref[...])
    yb = _dot(nb_ref[0], wb_ref[...])
    mix = _sigmoid(ga_ref[0].astype(F32)) * ya + _sigmoid(gb_ref[0].astype(F32)) * yb
    mo = _dot(mix.astype(BF16), wo_ref[...])
    x1 = x_ref[0] + gt1 * _rms(mo, gpm_ref[...])
    x1_ref[0] = x1

    xn = _rms(x1, gpf_ref[...]) * (1.0 + sc2) + sh2
    xn_ref[0] = xn.reshape(tm, ROW_SUB, ROW_LANE)
    logits = _dot(xn.astype(BF16), rw_ref[...]) + rb_ref[...]
    lane = lax.broadcasted_iota(jnp.int32, (tm, N_EXPERTS), 1)
    work = logits
    vals, idxs, hots = [], [], []
    for _ in range(TOP_K):
        mv = jnp.max(work, axis=-1, keepdims=True)
        mi = jnp.min(jnp.where(work == mv, lane, N_EXPERTS), axis=-1, keepdims=True)
        hot = lane == mi
        vals.append(mv)
        idxs.append(mi)
        hots.append(hot)
        work = jnp.where(hot, -jnp.inf, work)
    es = [jnp.exp(v - vals[0]) for v in vals]
    den = es[0] + es[1] + es[2] + es[3]

    cnt = sum(jnp.where(h, 1.0, 0.0) for h in hots)
    before = _dot(tri_ref[...], cnt.astype(BF16)) + carry_ref[...]
    carry_ref[...] = carry_ref[...] + jnp.sum(cnt, axis=0, keepdims=True)
    cnt_ref[...] = jnp.broadcast_to(carry_ref[...], cnt_ref.shape)

    lane_o = lax.broadcasted_iota(jnp.int32, (tm, 128), 1)
    ri = jnp.zeros((tm, 128), jnp.int32)
    rg = jnp.zeros((tm, 128), F32)
    for kk in range(TOP_K):
        rank = jnp.sum(jnp.where(hots[kk], before, 0.0), axis=-1, keepdims=True).astype(jnp.int32)
        ri = jnp.where(lane_o == kk, idxs[kk], ri)
        ri = jnp.where(lane_o == TOP_K + kk, rank, ri)
        rg = jnp.where(lane_o == kk, es[kk] / den, rg)
    ri_ref[0] = ri
    rg_ref[0] = rg


def _mix(x, mod, proj, o_fw, o_bw, o_b, hgrn_norm, g_post_mix, g_pre_ffn, wa, wb, wo, rw, rb):
    b, t, d = x.shape
    tm = min(MIX_TM, t)
    assert tm == MIX_TM
    row = lambda a: a.reshape(1, -1)
    tok = lambda w, col: pl.BlockSpec((1, tm, w), lambda bi, i: (bi, i, col))
    full = lambda a: pl.BlockSpec(a.shape, lambda bi, i: (0,) * a.ndim)
    gn, gpm, gpf, rbr = row(hgrn_norm), row(g_post_mix), row(g_pre_ffn), row(rb)
    outs = pl.pallas_call(
        _mix_kernel,
        grid=(b, t // tm),
        in_specs=[tok(d, 0),
                  pl.BlockSpec((1, 6, d), lambda bi, i: (bi, 0, 0)),
                  tok(d, 0), tok(d, 0),
                  tok(d, 4),
                  tok(d, COL_GATES // d), tok(d, COL_GATES // d + 1),
                  tok(NA_W, 0),
                  full(gn), full(gpm), full(gpf), full(wa), full(wb), full(wo), full(rw), full(rbr)],
        out_specs=[tok(d, 0), pl.BlockSpec((1, tm, ROW_SUB, ROW_LANE), lambda bi, i: (bi, i, 0, 0)),
                   tok(128, 0), tok(128, 0),
                   pl.BlockSpec((8, N_EXPERTS), lambda bi, i: (0, 0))],
        out_shape=[jax.ShapeDtypeStruct((b, t, d), F32), jax.ShapeDtypeStruct((b, t, ROW_SUB, ROW_LANE), F32),
                   jax.ShapeDtypeStruct((b, t, 128), jnp.int32), jax.ShapeDtypeStruct((b, t, 128), F32),
                   jax.ShapeDtypeStruct((8, N_EXPERTS), F32)],
        scratch_shapes=[pltpu.VMEM((tm, tm), BF16), pltpu.VMEM((1, N_EXPERTS), F32)],
        compiler_params=_cparams(("arbitrary", "arbitrary")),
        name="mix",
    )(x, mod, o_fw, o_bw, proj, proj, proj, o_b, gn, gpm, gpf, wa, wb, wo, rw, rbr)
    return outs


def _wait_rows(row_copy, n):
    def body(_, carry):
        for _ in range(WAIT_GROUP):
            row_copy().wait()
        return carry

    lax.fori_loop(0, n // WAIT_GROUP, body, 0)


def _dispatch_kernel(dest_hbm, x_ref, xs_hbm, idx_smem, sem_idx, sem):
    i = pl.program_id(0)
    n_idx = DISP_TOK * TOP_K
    cp = pltpu.make_async_copy(dest_hbm.at[pl.ds(i * n_idx, n_idx)], idx_smem, sem_idx)
    cp.start()
    cp.wait()

    def start(tt, carry):
        for kk in range(TOP_K):
            pltpu.make_async_copy(x_ref.at[tt], xs_hbm.at[idx_smem[tt * TOP_K + kk]], sem).start(priority=kk % 2)
        return carry

    lax.fori_loop(0, DISP_TOK, start, 0, unroll=4)
    _wait_rows(lambda: pltpu.make_async_copy(x_ref.at[0], xs_hbm.at[0], sem), n_idx)


def _dispatch(xn, dest_flat, n_rows):
    n = xn.shape[0]
    return pl.pallas_call(
        _dispatch_kernel,
        grid=(n // DISP_TOK,),
        in_specs=[pl.BlockSpec(memory_space=pl.ANY),
                  pl.BlockSpec((DISP_TOK, ROW_SUB, ROW_LANE), lambda i: (i, 0, 0))],
        out_specs=pl.BlockSpec(memory_space=pl.ANY),
        out_shape=jax.ShapeDtypeStruct((n_rows, ROW_SUB, ROW_LANE), F32),
        scratch_shapes=[pltpu.SMEM((DISP_TOK * TOP_K,), jnp.int32),
                        pltpu.SemaphoreType.DMA, pltpu.SemaphoreType.DMA],
        compiler_params=_cparams(("arbitrary",)),
        name="dispatch",
    )(dest_flat, xn)


GU_SLAB = 256


def _gate_up_layout(w_gate_up, b_gate_up):
    e, k, n2 = w_gate_up.shape
    half = GU_SLAB // 2
    pm = np.zeros((GU_SLAB, GU_SLAB), np.float32)
    pm[2 * np.arange(half), np.arange(half)] = 1.0
    pm[2 * np.arange(half) + 1, half + np.arange(half)] = 1.0
    w4 = w_gate_up.astype(BF16).reshape(e, k, n2 // GU_SLAB, GU_SLAB)
    w = jnp.einsum('eksc,cd->eksd', w4, jnp.asarray(pm, BF16), preferred_element_type=BF16).reshape(e, k, n2)
    perm = np.concatenate([s * GU_SLAB + np.concatenate([2 * np.arange(half), 2 * np.arange(half) + 1])
                           for s in range(n2 // GU_SLAB)])
    return w, b_gate_up[:, None, perm]


def _expert_kernel(be_ref, bv_ref, x_ref, wgu_ref, bgu_ref, wd_ref, bd_ref, o_ref, a_ref):
    i = pl.program_id(0)
    valid = bv_ref[i]
    half = GU_SLAB // 2

    @pl.when(valid > 0)
    def _():
        rowi = lax.broadcasted_iota(jnp.int32, (MOE_BLK, 1), 0)
        x = x_ref[...].reshape(MOE_BLK, D_MODEL)
        x = jnp.where(rowi < valid, x, 0.0).astype(BF16)
        for s in range(2 * D_FF // GU_SLAB):
            cols = slice(s * GU_SLAB, (s + 1) * GU_SLAB)
            h = _dot(x, wgu_ref[0, :, cols]) + bgu_ref[0, :, cols]
            glu = jnp.minimum(h[:, :half], SWIGLU_LIMIT)
            lin = jnp.clip(h[:, half:], -SWIGLU_LIMIT, SWIGLU_LIMIT)
            a = glu * _sigmoid(SWIGLU_ALPHA * glu) * (lin + 1.0)
            a_ref[:, s * half:(s + 1) * half] = a.astype(BF16)
        o_ref[...] = (_dot(a_ref[...], wd_ref[0]) + bd_ref[0]).reshape(o_ref.shape)

    @pl.when(valid <= 0)
    def _():
        o_ref[...] = jnp.zeros_like(o_ref)


def _experts(xs, block_e, block_valid, wgu, bgu, wd, bd):
    n_rows = xs.shape[0]
    d = D_MODEL
    n_blocks = n_rows // MOE_BLK
    emap = lambda i, be, bv: (be[i], 0, 0)
    rows = lambda: pl.BlockSpec((MOE_BLK, ROW_SUB, ROW_LANE), lambda i, be, bv: (i, 0, 0))
    return pl.pallas_call(
        _expert_kernel,
        grid_spec=pltpu.PrefetchScalarGridSpec(
            num_scalar_prefetch=2,
            grid=(n_blocks,),
            in_specs=[rows(),
                      pl.BlockSpec((1, d, 2 * D_FF), emap), pl.BlockSpec((1, 1, 2 * D_FF), emap),
                      pl.BlockSpec((1, D_FF, d), emap), pl.BlockSpec((1, 1, d), emap)],
            out_specs=rows(),
            scratch_shapes=[pltpu.VMEM((MOE_BLK, D_FF), BF16)],
        ),
        out_shape=jax.ShapeDtypeStruct((n_rows, ROW_SUB, ROW_LANE), F32),
        compiler_params=_cparams(("arbitrary",)),
        name="experts",
    )(block_e, block_valid, xs, wgu, bgu, wd, bd)


def _combine_kernel(dest_hbm, y_hbm, x1_ref, mod_ref, rg_ref, gpo_ref, o_ref, idx_smem, buf, sem_idx, sem):
    s = pl.program_id(0)
    ns = pl.num_programs(0)
    n_idx = COMB_TOK * TOP_K
    slot = s % 2

    def idx_copy(step, sl):
        return pltpu.make_async_copy(dest_hbm.at[pl.ds(step * n_idx, n_idx)], idx_smem.at[sl], sem_idx.at[sl])

    def issue(sl):
        def start(tt, carry):
            for kk in range(TOP_K):
                pltpu.make_async_copy(y_hbm.at[idx_smem[sl, tt * TOP_K + kk]],
                                      buf.at[sl, kk, tt], sem.at[sl]).start(priority=kk % 2)
            return carry

        lax.fori_loop(0, COMB_TOK, start, 0, unroll=4)

    @pl.when(s == 0)
    def _():
        idx_copy(0, 0).start()
        idx_copy(0, 0).wait()
        issue(0)

        @pl.when(ns > 1)
        def _():
            idx_copy(1, 1).start()

    @pl.when(s + 1 < ns)
    def _():
        idx_copy(s + 1, 1 - slot).wait()
        issue(1 - slot)

    @pl.when(s + 2 < ns)
    def _():
        idx_copy(s + 2, slot).start()

    _wait_rows(lambda: pltpu.make_async_copy(y_hbm.at[0], buf.at[slot, 0, 0], sem.at[slot]), n_idx)

    rg = rg_ref[...]
    h = rg[:, 0:1] * buf[slot, 0].reshape(COMB_TOK, D_MODEL)
    for kk in range(1, TOP_K):
        h = h + rg[:, kk:kk + 1] * buf[slot, kk].reshape(COMB_TOK, D_MODEL)
    gt2 = mod_ref[0, 5:6, :]
    o_ref[...] = x1_ref[...] + gt2 * _rms(h, gpo_ref[...])


def _combine(y, dest_flat, x1, mod, rg, g_post_ffn, tokens_per_batch):
    n, d = x1.shape
    per_b = tokens_per_batch // COMB_TOK
    gpo = g_post_ffn.reshape(1, d)
    return pl.pallas_call(
        _combine_kernel,
        grid=(n // COMB_TOK,),
        in_specs=[pl.BlockSpec(memory_space=pl.ANY), pl.BlockSpec(memory_space=pl.ANY),
                  pl.BlockSpec((COMB_TOK, d), lambda i: (i, 0)),
                  pl.BlockSpec((1, 6, d), lambda i: (i // per_b, 0, 0)),
                  pl.BlockSpec((COMB_TOK, 128), lambda i: (i, 0)),
                  pl.BlockSpec((1, d), lambda i: (0, 0))],
        out_specs=pl.BlockSpec((COMB_TOK, d), lambda i: (i, 0)),
        out_shape=jax.ShapeDtypeStruct((n, d), F32),
        scratch_shapes=[pltpu.SMEM((2, COMB_TOK * TOP_K), jnp.int32),
                        pltpu.VMEM((2, TOP_K, COMB_TOK, ROW_SUB, ROW_LANE), F32),
                        pltpu.SemaphoreType.DMA((2,)), pltpu.SemaphoreType.DMA((2,))],
        compiler_params=_cparams(("arbitrary",)),
        name="combine",
    )(dest_flat, y, x1, mod, rg, gpo)


def _moe_plan(ri, counts, n_tok):
    idx = ri[:, :TOP_K]
    rank = ri[:, TOP_K:2 * TOP_K]
    cnt = counts.astype(jnp.int32)
    padded = (cnt + MOE_BLK - 1) // MOE_BLK * MOE_BLK
    pend = jnp.cumsum(padded)
    pstart = pend - padded
    dest = pstart[idx] + rank
    n_blocks = (n_tok * TOP_K + MOE_BLK - 1) // MOE_BLK + N_EXPERTS
    bstart = jnp.arange(n_blocks, dtype=jnp.int32) * MOE_BLK
    block_e = jnp.minimum(jnp.sum((bstart[:, None] >= pend[None, :]).astype(jnp.int32), axis=1), N_EXPERTS - 1)
    block_valid = jnp.clip(pstart[block_e] + cnt[block_e] - bstart, 0, MOE_BLK).astype(jnp.int32)
    return dest.reshape(-1).astype(jnp.int32), block_e, block_valid, n_blocks


def _layer(x, mod, p):
    b, t, d = x.shape
    n = b * t
    proj = _in_proj(x, mod, p['g_pre_mix'], p['w_in'])
    o_fw, o_bw = _hgrn(proj, p['lb_logits'])
    o_b = _natten(proj, p['na_tabs'])
    x1, xn, ri, rg, counts = _mix(x, mod, proj, o_fw, o_bw, o_b, p['hgrn_norm'], p['g_post_mix'], p['g_pre_ffn'],
                                  p['w_branch_a'], p['w_branch_b'], p['w_out'], p['router_w'], p['router_b'])
    ri = ri.reshape(n, 128)
    rg = rg.reshape(n, 128)
    dest, block_e, block_valid, n_blocks = _moe_plan(ri, counts[0], n)
    xs = _dispatch(xn.reshape(n, ROW_SUB, ROW_LANE), dest, n_blocks * MOE_BLK)
    y = _experts(xs, block_e, block_valid, p['w_gu'], p['b_gu'], p['w_d'], p['b_d'])
    out = _combine(y, dest, x1.reshape(n, d), mod, rg, p['g_post_ffn'], t)
    return out.reshape(b, t, d)


def kernel(x_prompt, x_sample, c_prompt, c_sample, ada_w, ada_b, g_pre_mix, g_post_mix, g_pre_ffn, g_post_ffn,
           w_in, hgrn_lb_logits, hgrn_norm, na_rpb, w_branch_a, w_branch_b, w_out, router_w, router_b,
           w_gate_up, b_gate_up, w_down, b_down):
    d = D_MODEL
    bp, bs = c_prompt.shape[0], c_sample.shape[0]
    pad = (-(bp + bs)) % 8
    c_all = jnp.concatenate([c_prompt, c_sample, jnp.zeros((pad, d), F32)], axis=0)
    mod = _ada_mod(c_all, ada_w[0], ada_b[0]).reshape(-1, 6, d)

    w = w_in[0]
    w_perm = jnp.concatenate([w[:, :5 * d], w[:, 5 * d + 3 * NA_W:], w[:, 5 * d:5 * d + 3 * NA_W]], axis=1).astype(BF16)
    w_gu, b_gu = _gate_up_layout(w_gate_up[0], b_gate_up[0])
    p = dict(
        g_pre_mix=g_pre_mix[0], g_post_mix=g_post_mix[0], g_pre_ffn=g_pre_ffn[0], g_post_ffn=g_post_ffn[0],
        w_in=w_perm, lb_logits=hgrn_lb_logits.astype(F32), hgrn_norm=hgrn_norm[0],
        na_tabs=_na_bias_tables(na_rpb[0]),
        w_branch_a=w_branch_a[0].astype(BF16), w_branch_b=w_branch_b[0].astype(BF16), w_out=w_out[0].astype(BF16),
        router_w=router_w[0].astype(BF16), router_b=router_b[0],
        w_gu=w_gu, b_gu=b_gu,
        w_d=w_down[0].astype(BF16), b_d=b_down[0][:, None, :],
    )
    y_p = _layer(x_prompt, mod[:bp], p)
    y_s = _layer(x_sample, mod[bp:bp + bs], p)
    return (y_p, y_s)
```

```python
import functools

import numpy as np
import jax
import jax.numpy as jnp
from jax import lax
from jax.experimental import pallas as pl
from jax.experimental.pallas import tpu as pltpu

F32 = jnp.float32
BF16 = jnp.bfloat16

D_MODEL = 1024
GRID_W = 64
HA_HEADS = 8
HA_DK = 128
NA_HEADS = 8
NA_DH = 64
NA_W = NA_HEADS * NA_DH
NA_KH = 8
NA_KW = 16
N_EXPERTS = 32
TOP_K = 4
D_FF = 1024
SWIGLU_ALPHA = 1.702
SWIGLU_LIMIT = 7.0
RMS_EPS = 1e-6
NEG_BIG = -1e30

COL_A = 0
COL_GATES = 5 * 1024
COL_NA = 7 * 1024
PROJ_COLS = 7 * 1024 + 3 * NA_W

HGRN_CHUNK = 64
HGRN_TILE = 256
NA_ROWS_PER_STEP = 8
PROJ_TM = 1024
PROJ_TN = 2176
MIX_TM = 512
MOE_BLK = 512
DISP_TOK = 1024
COMB_TOK = 256
WAIT_GROUP = 64
ROW_SUB, ROW_LANE = 8, 128
VMEM_LIMIT = 56 * 1024 * 1024


def _cparams(sem):
    return pltpu.CompilerParams(dimension_semantics=sem, vmem_limit_bytes=VMEM_LIMIT)


def _dot(a, b):
    return jnp.dot(a, b, preferred_element_type=F32)


def _dot_nt(a, b):
    return lax.dot_general(a, b, (((1,), (1,)), ((), ())), preferred_element_type=F32)


def _dot_tn(a, b):
    return lax.dot_general(a, b, (((0,), (0,)), ((), ())), preferred_element_type=F32)


def _split(a):
    hi = a.astype(BF16)
    lo = (a - hi.astype(F32)).astype(BF16)
    return hi, lo


def _dot3(a, w):
    ah, al = _split(a)
    wh, wl = _split(w)
    return _dot(ah, wh) + (_dot(al, wh) + _dot(ah, wl))


def _sigmoid(x):
    return 1.0 / (1.0 + jnp.exp(-x))


def _rms(x, g):
    return x * lax.rsqrt(jnp.mean(x * x, axis=-1, keepdims=True) + RMS_EPS) * g


def _ada_kernel(c_ref, w_ref, b_ref, o_ref):
    c = c_ref[...]
    o_ref[...] = _dot3(c * _sigmoid(c), w_ref[...]) + b_ref[...]


def _ada_mod(c, ada_w, ada_b):
    bp, d = c.shape
    n = ada_w.shape[1]
    tn = 1536
    return pl.pallas_call(
        _ada_kernel,
        grid=(n // tn,),
        in_specs=[pl.BlockSpec((bp, d), lambda j: (0, 0)),
                  pl.BlockSpec((d, tn), lambda j: (0, j)),
                  pl.BlockSpec((1, tn), lambda j: (0, j))],
        out_specs=pl.BlockSpec((bp, tn), lambda j: (0, j)),
        out_shape=jax.ShapeDtypeStruct((bp, n), F32),
        compiler_params=_cparams(("arbitrary",)),
        name="ada_mod",
    )(c, ada_w, ada_b.reshape(1, n))


def _in_proj_kernel(x_ref, mod_ref, g_ref, w_ref, o_ref, xn_ref):
    @pl.when(pl.program_id(2) == 0)
    def _():
        sh = mod_ref[0, 0:1, :]
        sc = mod_ref[0, 1:2, :]
        xn = _rms(x_ref[0], g_ref[...]) * (1.0 + sc) + sh
        xn_ref[...] = xn.astype(BF16)

    o_ref[0] = _dot(xn_ref[...], w_ref[...]).astype(BF16)


def _in_proj(x, mod, g_pre, w_in_bf):
    b, t, d = x.shape
    tm = min(PROJ_TM, t)
    n = w_in_bf.shape[1]
    return pl.pallas_call(
        _in_proj_kernel,
        grid=(b, t // tm, n // PROJ_TN),
        in_specs=[pl.BlockSpec((1, tm, d), lambda bi, i, j: (bi, i, 0)),
                  pl.BlockSpec((1, 6, d), lambda bi, i, j: (bi, 0, 0)),
                  pl.BlockSpec((1, d), lambda bi, i, j: (0, 0)),
                  pl.BlockSpec((d, PROJ_TN), lambda bi, i, j: (0, j))],
        out_specs=pl.BlockSpec((1, tm, PROJ_TN), lambda bi, i, j: (bi, i, j)),
        out_shape=jax.ShapeDtypeStruct((b, t, n), BF16),
        scratch_shapes=[pltpu.VMEM((tm, d), BF16)],
        compiler_params=_cparams(("arbitrary", "arbitrary", "arbitrary")),
        name="in_proj",
    )(x, mod, g_pre.reshape(1, d), w_in_bf)


def _cumsum_rows(x, reverse):
    n = x.shape[0]
    row = lax.broadcasted_iota(jnp.int32, x.shape, 0)
    step = 1
    while step < n:
        if reverse:
            x = x + jnp.where(row < n - step, pltpu.roll(x, n - step, axis=0), 0.0)
        else:
            x = x + jnp.where(row >= step, pltpu.roll(x, step, axis=0), 0.0)
        step *= 2
    return x


def _hgrn_dir(q_ref, f_ref, i_ref, o_ref, s_ref, sn_ref, lbd, row0, reverse):
    c = HGRN_CHUNK
    rows = pl.ds(row0, c)
    q = q_ref[0, rows, :].astype(F32)
    fpre = f_ref[0, rows, :].astype(F32)
    v = i_ref[0, rows, :]
    f = lbd + (1.0 - lbd) * _sigmoid(fpre)
    logf = jnp.log(f)
    k = 1.0 - f
    ri = lax.broadcasted_iota(jnp.int32, (c, c), 0)
    ci = lax.broadcasted_iota(jnp.int32, (c, c), 1)
    keep = (ci >= ri) if reverse else (ri >= ci)
    b = _cumsum_rows(logf, reverse)
    if reverse:
        b_mid = b[c // 2:c // 2 + 1, :]
        b_end = b[0:1, :]
    else:
        b_mid = b[c // 2 - 1:c // 2, :]
        b_end = b[c - 1:c, :]
    qm = (q * jnp.exp(b - b_mid)).astype(BF16)
    km = (k * jnp.exp(b_mid - b)).astype(BF16)
    qi = (q * jnp.exp(b)).astype(BF16)
    ks = (k * jnp.exp(b_end - b)).astype(BF16)
    dec = jnp.exp(b_end)
    for h in range(HA_HEADS):
        sl = slice(h * HA_DK, (h + 1) * HA_DK)
        a = _dot_nt(qm[:, sl], km[:, sl])
        a = jnp.where(keep, a, 0.0).astype(BF16)
        lhs = jnp.concatenate([qi[:, sl], a], axis=1)
        rhs = jnp.concatenate([sn_ref[h], v[:, sl]], axis=0)
        o_ref[0, rows, sl] = _dot(lhs, rhs).astype(BF16)
        st = s_ref[h] * dec[:, sl] + _dot_tn(v[:, sl], ks[:, sl])
        s_ref[h] = st
        sn_ref[h] = st.astype(BF16).T


def _hgrn_kernel(lbl_ref, qf_ref, ff_ref, if_ref, qb_ref, fb_ref, ib_ref, of_ref, ob_ref,
                 sf_ref, sb_ref, snf_ref, snb_ref):
    @pl.when(pl.program_id(1) == 0)
    def _():
        for ref in (sf_ref, sb_ref, snf_ref, snb_ref):
            ref[...] = jnp.zeros_like(ref)

    l0 = lbl_ref[0]
    l1 = lbl_ref[1]
    m = jnp.maximum(l0, l1)
    e0 = jnp.exp(l0 - m)
    e1 = jnp.exp(l1 - m)
    lb = e0 / (e0 + e1)
    n_chunks = HGRN_TILE // HGRN_CHUNK

    for ci in range(n_chunks):
        _hgrn_dir(qf_ref, ff_ref, if_ref, of_ref, sf_ref, snf_ref, lb[0:1, :], ci * HGRN_CHUNK, False)
        _hgrn_dir(qb_ref, fb_ref, ib_ref, ob_ref, sb_ref, snb_ref, lb[1:2, :],
                  (n_chunks - 1 - ci) * HGRN_CHUNK, True)


def _hgrn(proj, lb_logits):
    b, t, _ = proj.shape
    tt = HGRN_TILE
    nt = t // tt
    d = D_MODEL

    def fwd(col):
        return pl.BlockSpec((1, tt, d), lambda bi, ti: (bi, ti, col))

    def bwd(col):
        return pl.BlockSpec((1, tt, d), lambda bi, ti: (bi, nt - 1 - ti, col))

    return pl.pallas_call(
        _hgrn_kernel,
        grid=(b, nt),
        in_specs=[pl.BlockSpec((2, 2, d), lambda bi, ti: (0, 0, 0)),
                  fwd(0), fwd(1), fwd(3), bwd(0), bwd(2), bwd(3)],
        out_specs=[pl.BlockSpec((1, tt, d), lambda bi, ti: (bi, ti, 0)),
                   pl.BlockSpec((1, tt, d), lambda bi, ti: (bi, nt - 1 - ti, 0))],
        out_shape=[jax.ShapeDtypeStruct((b, t, d), BF16), jax.ShapeDtypeStruct((b, t, d), BF16)],
        scratch_shapes=[pltpu.VMEM((HA_HEADS, HA_DK, HA_DK), F32), pltpu.VMEM((HA_HEADS, HA_DK, HA_DK), F32),
                        pltpu.VMEM((HA_HEADS, HA_DK, HA_DK), BF16), pltpu.VMEM((HA_HEADS, HA_DK, HA_DK), BF16)],
        compiler_params=_cparams(("arbitrary", "arbitrary")),
        name="hgrn",
    )(lb_logits, proj, proj, proj, proj, proj, proj)


def _na_bias_tables(rpb):
    q = np.arange(GRID_W)[:, None]
    kc = np.arange(GRID_W)[None, :]
    c0 = np.clip(q - NA_KW // 2, 0, GRID_W - NA_KW)
    ok = (kc >= c0) & (kc < c0 + NA_KW)
    dc = np.clip(kc - q + (NA_KW - 1), 0, 2 * NA_KW - 2)
    base = jnp.where(ok[None, None], jnp.take(rpb.astype(F32), dc, axis=2), NEG_BIG)
    tabs = []
    for dr0 in range(NA_KH):
        w = base[:, dr0:dr0 + NA_KH]
        w = w.transpose(0, 2, 1, 3).reshape(NA_HEADS // 2, 2 * GRID_W, NA_KH * GRID_W)
        tabs.append(w)
    return jnp.stack(tabs, axis=1)


def _na_kernel(q_ref, k_ref, v_ref, tab_ref, o_ref, *, grid_rows):
    g = pl.program_id(1)
    lane = lax.broadcasted_iota(jnp.int32, (GRID_W, 2 * NA_DH), 1)
    low = lane < NA_DH
    scale = NA_DH ** -0.5

    def body(rr, carry):
        r = g * NA_ROWS_PER_STEP + rr
        r0 = jnp.clip(r - NA_KH // 2, 0, grid_rows - NA_KH)
        dr0 = r0 - r + (NA_KH - 1)
        qrow = pl.ds(pl.multiple_of(rr * GRID_W, GRID_W), GRID_W)
        krow = pl.ds(pl.multiple_of(r0 * GRID_W, GRID_W), NA_KH * GRID_W)
        for j in range(NA_HEADS // 2):
            sl = slice(j * 2 * NA_DH, (j + 1) * 2 * NA_DH)
            qp = q_ref[0, qrow, sl]
            zero = jnp.zeros_like(qp)
            ql = jnp.concatenate([jnp.where(low, qp, zero), jnp.where(low, zero, qp)], axis=0)
            s = _dot_nt(ql, k_ref[0, krow, sl]) * scale + tab_ref[j, dr0]
            m = jnp.max(s, axis=-1, keepdims=True)
            p = jnp.exp(s - m)
            l = jnp.sum(p, axis=-1, keepdims=True)
            pv = _dot(p.astype(BF16), v_ref[0, krow, sl]) / l
            o_ref[0, qrow, sl] = jnp.where(low, pv[:GRID_W], pv[GRID_W:]).astype(BF16)
        return carry

    lax.fori_loop(0, NA_ROWS_PER_STEP, body, 0, unroll=2)


def _natten(proj, tabs):
    b, t, _ = proj.shape
    grid_rows = t // GRID_W
    tq = NA_ROWS_PER_STEP * GRID_W
    cq, ck, cv = (COL_NA // NA_W, COL_NA // NA_W + 1, COL_NA // NA_W + 2)
    return pl.pallas_call(
        functools.partial(_na_kernel, grid_rows=grid_rows),
        grid=(b, t // tq),
        in_specs=[pl.BlockSpec((1, tq, NA_W), lambda bi, gi: (bi, gi, cq)),
                  pl.BlockSpec((1, t, NA_W), lambda bi, gi: (bi, 0, ck)),
                  pl.BlockSpec((1, t, NA_W), lambda bi, gi: (bi, 0, cv)),
                  pl.BlockSpec(tabs.shape, lambda bi, gi: (0, 0, 0, 0))],
        out_specs=pl.BlockSpec((1, tq, NA_W), lambda bi, gi: (bi, gi, 0)),
        out_shape=jax.ShapeDtypeStruct((b, t, NA_W), BF16),
        compiler_params=_cparams(("arbitrary", "arbitrary")),
        name="natten",
    )(proj, proj, proj, tabs)


def _mix_kernel(x_ref, mod_ref, of_ref, ob_ref, g_ref, ga_ref, gb_ref, nb_ref,
                gn_ref, gpm_ref, gpf_ref, wa_ref, wb_ref, wo_ref, rw_ref, rb_ref,
                x1_ref, xn_ref, ri_ref, rg_ref, cnt_ref, tri_ref, carry_ref):
    tm = MIX_TM
    first = (pl.program_id(0) == 0) & (pl.program_id(1) == 0)

    @pl.when(first)
    def _():
        r = lax.broadcasted_iota(jnp.int32, (tm, tm), 0)
        c = lax.broadcasted_iota(jnp.int32, (tm, tm), 1)
        tri_ref[...] = jnp.where(r > c, 1.0, 0.0).astype(BF16)
        carry_ref[...] = jnp.zeros_like(carry_ref)

    sc2 = mod_ref[0, 4:5, :]
    sh2 = mod_ref[0, 3:4, :]
    gt1 = mod_ref[0, 2:3, :]

    o = of_ref[0].astype(F32) + ob_ref[0].astype(F32)
    parts = []
    for h in range(HA_HEADS):
        oh = o[:, h * HA_DK:(h + 1) * HA_DK]
        parts.append(oh * lax.rsqrt(jnp.mean(oh * oh, axis=-1, keepdims=True) + RMS_EPS))
    g = g_ref[0].astype(F32)
    oa = jnp.concatenate(parts, axis=-1) * gn_ref[...] * (g * _sigmoid(g))
    ya = _dot(oa.astype(BF16), wa_ref[...])
    yb = _dot(nb_ref[0], wb_ref[...])
    mix = _sigmoid(ga_ref[0].astype(F32)) * ya + _sigmoid(gb_ref[0].astype(F32)) * yb
    mo = _dot(mix.astype(BF16), wo_ref[...])
    x1 = x_ref[0] + gt1 * _rms(mo, gpm_ref[...])
    x1_ref[0] = x1

    xn = _rms(x1, gpf_ref[...]) * (1.0 + sc2) + sh2
    xn_ref[0] = xn.reshape(tm, ROW_SUB, ROW_LANE)
    logits = _dot(xn.astype(BF16), rw_ref[...]) + rb_ref[...]
    lane = lax.broadcasted_iota(jnp.int32, (tm, N_EXPERTS), 1)
    work = logits
    vals, idxs, hots = [], [], []
    for _ in range(TOP_K):
        mv = jnp.max(work, axis=-1, keepdims=True)
        mi = jnp.min(jnp.where(work == mv, lane, N_EXPERTS), axis=-1, keepdims=True)
        hot = lane == mi
        vals.append(mv)
        idxs.append(mi)
        hots.append(hot)
        work = jnp.where(hot, -jnp.inf, work)
    es = [jnp.exp(v - vals[0]) for v in vals]
    den = es[0] + es[1] + es[2] + es[3]

    cnt = sum(jnp.where(h, 1.0, 0.0) for h in hots)
    before = _dot(tri_ref[...], cnt.astype(BF16)) + carry_ref[...]
    carry_ref[...] = carry_ref[...] + jnp.sum(cnt, axis=0, keepdims=True)
    cnt_ref[...] = jnp.broadcast_to(carry_ref[...], cnt_ref.shape)

    lane_o = lax.broadcasted_iota(jnp.int32, (tm, 128), 1)
    ri = jnp.zeros((tm, 128), jnp.int32)
    rg = jnp.zeros((tm, 128), F32)
    for kk in range(TOP_K):
        rank = jnp.sum(jnp.where(hots[kk], before, 0.0), axis=-1, keepdims=True).astype(jnp.int32)
        ri = jnp.where(lane_o == kk, idxs[kk], ri)
        ri = jnp.where(lane_o == TOP_K + kk, rank, ri)
        rg = jnp.where(lane_o == kk, es[kk] / den, rg)
    ri_ref[0] = ri
    rg_ref[0] = rg


def _mix(x, mod, proj, o_fw, o_bw, o_b, hgrn_norm, g_post_mix, g_pre_ffn, wa, wb, wo, rw, rb):
    b, t, d = x.shape
    tm = min(MIX_TM, t)
    assert tm == MIX_TM
    row = lambda a: a.reshape(1, -1)
    tok = lambda w, col: pl.BlockSpec((1, tm, w), lambda bi, i: (bi, i, col))
    full = lambda a: pl.BlockSpec(a.shape, lambda bi, i: (0,) * a.ndim)
    gn, gpm, gpf, rbr = row(hgrn_norm), row(g_post_mix), row(g_pre_ffn), row(rb)
    outs = pl.pallas_call(
        _mix_kernel,
        grid=(b, t // tm),
        in_specs=[tok(d, 0),
                  pl.BlockSpec((1, 6, d), lambda bi, i: (bi, 0, 0)),
                  tok(d, 0), tok(d, 0),
                  tok(d, 4),
                  tok(d, COL_GATES // d), tok(d, COL_GATES // d + 1),
                  tok(NA_W, 0),
                  full(gn), full(gpm), full(gpf), full(wa), full(wb), full(wo), full(rw), full(rbr)],
        out_specs=[tok(d, 0), pl.BlockSpec((1, tm, ROW_SUB, ROW_LANE), lambda bi, i: (bi, i, 0, 0)),
                   tok(128, 0), tok(128, 0),
                   pl.BlockSpec((8, N_EXPERTS), lambda bi, i: (0, 0))],
        out_shape=[jax.ShapeDtypeStruct((b, t, d), F32), jax.ShapeDtypeStruct((b, t, ROW_SUB, ROW_LANE), F32),
                   jax.ShapeDtypeStruct((b, t, 128), jnp.int32), jax.ShapeDtypeStruct((b, t, 128), F32),
                   jax.ShapeDtypeStruct((8, N_EXPERTS), F32)],
        scratch_shapes=[pltpu.VMEM((tm, tm), BF16), pltpu.VMEM((1, N_EXPERTS), F32)],
        compiler_params=_cparams(("arbitrary", "arbitrary")),
        name="mix",
    )(x, mod, o_fw, o_bw, proj, proj, proj, o_b, gn, gpm, gpf, wa, wb, wo, rw, rbr)
    return outs


def _wait_rows(row_copy, n):
    def body(_, carry):
        for _ in range(WAIT_GROUP):
            row_copy().wait()
        return carry

    lax.fori_loop(0, n // WAIT_GROUP, body, 0)


def _dispatch_kernel(dest_hbm, x_ref, xs_hbm, idx_smem, sem_idx, sem):
    i = pl.program_id(0)
    n_idx = DISP_TOK * TOP_K
    cp = pltpu.make_async_copy(dest_hbm.at[pl.ds(i * n_idx, n_idx)], idx_smem, sem_idx)
    cp.start()
    cp.wait()

    def start(tt, carry):
        for kk in range(TOP_K):
            pltpu.make_async_copy(x_ref.at[tt], xs_hbm.at[idx_smem[tt * TOP_K + kk]], sem).start(priority=kk % 2)
        return carry

    lax.fori_loop(0, DISP_TOK, start, 0, unroll=4)
    _wait_rows(lambda: pltpu.make_async_copy(x_ref.at[0], xs_hbm.at[0], sem), n_idx)


def _dispatch(xn, dest_flat, n_rows):
    n = xn.shape[0]
    return pl.pallas_call(
        _dispatch_kernel,
        grid=(n // DISP_TOK,),
        in_specs=[pl.BlockSpec(memory_space=pl.ANY),
                  pl.BlockSpec((DISP_TOK, ROW_SUB, ROW_LANE), lambda i: (i, 0, 0))],
        out_specs=pl.BlockSpec(memory_space=pl.ANY),
        out_shape=jax.ShapeDtypeStruct((n_rows, ROW_SUB, ROW_LANE), F32),
        scratch_shapes=[pltpu.SMEM((DISP_TOK * TOP_K,), jnp.int32),
                        pltpu.SemaphoreType.DMA, pltpu.SemaphoreType.DMA],
        compiler_params=_cparams(("arbitrary",)),
        name="dispatch",
    )(dest_flat, xn)


GU_SLAB = 256


def _gate_up_layout(w_gate_up, b_gate_up):
    e, k, n2 = w_gate_up.shape
    half = GU_SLAB // 2
    pm = np.zeros((GU_SLAB, GU_SLAB), np.float32)
    pm[2 * np.arange(half), np.arange(half)] = 1.0
    pm[2 * np.arange(half) + 1, half + np.arange(half)] = 1.0
    w4 = w_gate_up.astype(BF16).reshape(e, k, n2 // GU_SLAB, GU_SLAB)
    w = jnp.einsum('eksc,cd->eksd', w4, jnp.asarray(pm, BF16), preferred_element_type=BF16).reshape(e, k, n2)
    perm = np.concatenate([s * GU_SLAB + np.concatenate([2 * np.arange(half), 2 * np.arange(half) + 1])
                           for s in range(n2 // GU_SLAB)])
    return w, b_gate_up[:, None, perm]


def _expert_kernel(be_ref, bv_ref, x_ref, wgu_ref, bgu_ref, wd_ref, bd_ref, o_ref, a_ref):
    i = pl.program_id(0)
    valid = bv_ref[i]
    half = GU_SLAB // 2

    @pl.when(valid > 0)
    def _():
        rowi = lax.broadcasted_iota(jnp.int32, (MOE_BLK, 1), 0)
        x = x_ref[...].reshape(MOE_BLK, D_MODEL)
        x = jnp.where(rowi < valid, x, 0.0).astype(BF16)
        for s in range(2 * D_FF // GU_SLAB):
            cols = slice(s * GU_SLAB, (s + 1) * GU_SLAB)
            h = _dot(x, wgu_ref[0, :, cols]) + bgu_ref[0, :, cols]
            glu = jnp.minimum(h[:, :half], SWIGLU_LIMIT)
            lin = jnp.clip(h[:, half:], -SWIGLU_LIMIT, SWIGLU_LIMIT)
            a = glu * _sigmoid(SWIGLU_ALPHA * glu) * (lin + 1.0)
            a_ref[:, s * half:(s + 1) * half] = a.astype(BF16)
        o_ref[...] = (_dot(a_ref[...], wd_ref[0]) + bd_ref[0]).reshape(o_ref.shape)

    @pl.when(valid <= 0)
    def _():
        o_ref[...] = jnp.zeros_like(o_ref)


def _experts(xs, block_e, block_valid, wgu, bgu, wd, bd):
    n_rows = xs.shape[0]
    d = D_MODEL
    n_blocks = n_rows // MOE_BLK
    emap = lambda i, be, bv: (be[i], 0, 0)
    rows = lambda: pl.BlockSpec((MOE_BLK, ROW_SUB, ROW_LANE), lambda i, be, bv: (i, 0, 0))
    return pl.pallas_call(
        _expert_kernel,
        grid_spec=pltpu.PrefetchScalarGridSpec(
            num_scalar_prefetch=2,
            grid=(n_blocks,),
            in_specs=[rows(),
                      pl.BlockSpec((1, d, 2 * D_FF), emap), pl.BlockSpec((1, 1, 2 * D_FF), emap),
                      pl.BlockSpec((1, D_FF, d), emap), pl.BlockSpec((1, 1, d), emap)],
            out_specs=rows(),
            scratch_shapes=[pltpu.VMEM((MOE_BLK, D_FF), BF16)],
        ),
        out_shape=jax.ShapeDtypeStruct((n_rows, ROW_SUB, ROW_LANE), F32),
        compiler_params=_cparams(("arbitrary",)),
        name="experts",
    )(block_e, block_valid, xs, wgu, bgu, wd, bd)


def _combine_kernel(dest_hbm, y_hbm, x1_ref, mod_ref, rg_ref, gpo_ref, o_ref, idx_smem, buf, sem_idx, sem):
    s = pl.program_id(0)
    ns = pl.num_programs(0)
    n_half = COMB_TOK * TOP_K
    n_idx = 2 * n_half
    cur = s % 2

    def idx_copy(step, sl):
        return pltpu.make_async_copy(dest_hbm.at[pl.ds(step * n_idx, n_idx)], idx_smem.at[sl], sem_idx.at[sl])

    def issue(idx_slot, half):
        def start(tt, carry):
            for kk in range(TOP_K):
                pltpu.make_async_copy(y_hbm.at[idx_smem[idx_slot, half * n_half + tt * TOP_K + kk]],
                                      buf.at[half, kk, tt], sem.at[half]).start(priority=kk % 2)
            return carry

        lax.fori_loop(0, COMB_TOK, start, 0, unroll=4)

    def finish(half):
        _wait_rows(lambda: pltpu.make_async_copy(y_hbm.at[0], buf.at[half, 0, 0], sem.at[half]), n_half)
        rows = pl.ds(half * COMB_TOK, COMB_TOK)
        rg = rg_ref[rows, :]
        h = rg[:, 0:1] * buf[half, 0].reshape(COMB_TOK, D_MODEL)
        for kk in range(1, TOP_K):
            h = h + rg[:, kk:kk + 1] * buf[half, kk].reshape(COMB_TOK, D_MODEL)
        gt2 = mod_ref[0, 5:6, :]
        o_ref[rows, :] = x1_ref[rows, :] + gt2 * _rms(h, gpo_ref[...])

    @pl.when(s == 0)
    def _():
        idx_copy(0, 0).start()
        idx_copy(0, 0).wait()
        issue(0, 0)

    @pl.when(s + 1 < ns)
    def _():
        idx_copy(s + 1, 1 - cur).start()

    issue(cur, 1)
    finish(0)

    @pl.when(s + 1 < ns)
    def _():
        idx_copy(s + 1, 1 - cur).wait()
        issue(1 - cur, 0)

    finish(1)


def _combine(y, dest_flat, x1, mod, rg, g_post_ffn, tokens_per_batch):
    n, d = x1.shape
    tok = 2 * COMB_TOK
    per_b = tokens_per_batch // tok
    gpo = g_post_ffn.reshape(1, d)
    return pl.pallas_call(
        _combine_kernel,
        grid=(n // tok,),
        in_specs=[pl.BlockSpec(memory_space=pl.ANY), pl.BlockSpec(memory_space=pl.ANY),
                  pl.BlockSpec((tok, d), lambda i: (i, 0)),
                  pl.BlockSpec((1, 6, d), lambda i: (i // per_b, 0, 0)),
                  pl.BlockSpec((tok, 128), lambda i: (i, 0)),
                  pl.BlockSpec((1, d), lambda i: (0, 0))],
        out_specs=pl.BlockSpec((tok, d), lambda i: (i, 0)),
        out_shape=jax.ShapeDtypeStruct((n, d), F32),
        scratch_shapes=[pltpu.SMEM((2, tok * TOP_K), jnp.int32),
                        pltpu.VMEM((2, TOP_K, COMB_TOK, ROW_SUB, ROW_LANE), F32),
                        pltpu.SemaphoreType.DMA((2,)), pltpu.SemaphoreType.DMA((2,))],
        compiler_params=_cparams(("arbitrary",)),
        name="combine",
    )(dest_flat, y, x1, mod, rg, gpo)


def _moe_plan(ri, counts, n_tok):
    idx = ri[:, :TOP_K]
    rank = ri[:, TOP_K:2 * TOP_K]
    cnt = counts.astype(jnp.int32)
    padded = (cnt + MOE_BLK - 1) // MOE_BLK * MOE_BLK
    pend = jnp.cumsum(padded)
    pstart = pend - padded
    dest = pstart[idx] + rank
    n_blocks = (n_tok * TOP_K + MOE_BLK - 1) // MOE_BLK + N_EXPERTS
    bstart = jnp.arange(n_blocks, dtype=jnp.int32) * MOE_BLK
    block_e = jnp.minimum(jnp.sum((bstart[:, None] >= pend[None, :]).astype(jnp.int32), axis=1), N_EXPERTS - 1)
    block_valid = jnp.clip(pstart[block_e] + cnt[block_e] - bstart, 0, MOE_BLK).astype(jnp.int32)
    return dest.reshape(-1).astype(jnp.int32), block_e, block_valid, n_blocks


def _layer(x, mod, p):
    b, t, d = x.shape
    n = b * t
    proj = _in_proj(x, mod, p['g_pre_mix'], p['w_in'])
    o_fw, o_bw = _hgrn(proj, p['lb_logits'])
    o_b = _natten(proj, p['na_tabs'])
    x1, xn, ri, rg, counts = _mix(x, mod, proj, o_fw, o_bw, o_b, p['hgrn_norm'], p['g_post_mix'], p['g_pre_ffn'],
                                  p['w_branch_a'], p['w_branch_b'], p['w_out'], p['router_w'], p['router_b'])
    ri = ri.reshape(n, 128)
    rg = rg.reshape(n, 128)
    dest, block_e, block_valid, n_blocks = _moe_plan(ri, counts[0], n)
    xs = _dispatch(xn.reshape(n, ROW_SUB, ROW_LANE), dest, n_blocks * MOE_BLK)
    y = _experts(xs, block_e, block_valid, p['w_gu'], p['b_gu'], p['w_d'], p['b_d'])
    out = _combine(y, dest, x1.reshape(n, d), mod, rg, p['g_post_ffn'], t)
    return out.reshape(b, t, d)


def kernel(x_prompt, x_sample, c_prompt, c_sample, ada_w, ada_b, g_pre_mix, g_post_mix, g_pre_ffn, g_post_ffn,
           w_in, hgrn_lb_logits, hgrn_norm, na_rpb, w_branch_a, w_branch_b, w_out, router_w, router_b,
           w_gate_up, b_gate_up, w_down, b_down):
    d = D_MODEL
    bp, bs = c_prompt.shape[0], c_sample.shape[0]
    pad = (-(bp + bs)) % 8
    c_all = jnp.concatenate([c_prompt, c_sample, jnp.zeros((pad, d), F32)], axis=0)
    mod = _ada_mod(c_all, ada_w[0], ada_b[0]).reshape(-1, 6, d)

    w = w_in[0]
    w_perm = jnp.concatenate([w[:, :5 * d], w[:, 5 * d + 3 * NA_W:], w[:, 5 * d:5 * d + 3 * NA_W]], axis=1).astype(BF16)
    w_gu, b_gu = _gate_up_layout(w_gate_up[0], b_gate_up[0])
    p = dict(
        g_pre_mix=g_pre_mix[0], g_post_mix=g_post_mix[0], g_pre_ffn=g_pre_ffn[0], g_post_ffn=g_post_ffn[0],
        w_in=w_perm, lb_logits=hgrn_lb_logits.astype(F32), hgrn_norm=hgrn_norm[0],
        na_tabs=_na_bias_tables(na_rpb[0]),
        w_branch_a=w_branch_a[0].astype(BF16), w_branch_b=w_branch_b[0].astype(BF16), w_out=w_out[0].astype(BF16),
        router_w=router_w[0].astype(BF16), router_b=router_b[0],
        w_gu=w_gu, b_gu=b_gu,
        w_d=w_down[0].astype(BF16), b_d=b_down[0][:, None, :],
    )
    y_p = _layer(x_prompt, mod[:bp], p)
    y_s = _layer(x_sample, mod[bp:bp + bs], p)
    return (y_p, y_s)
```

```python
import functools

import numpy as np
import jax
import jax.numpy as jnp
from jax import lax
from jax.experimental import pallas as pl
from jax.experimental.pallas import tpu as pltpu
from jax.experimental.pallas import tpu_sc as plsc

F32 = jnp.float32
BF16 = jnp.bfloat16

D_MODEL = 1024
GRID_W = 64
HA_HEADS = 8
HA_DK = 128
NA_HEADS = 8
NA_DH = 64
NA_W = NA_HEADS * NA_DH
NA_KH = 8
NA_KW = 16
N_EXPERTS = 32
TOP_K = 4
D_FF = 1024
SWIGLU_ALPHA = 1.702
SWIGLU_LIMIT = 7.0
RMS_EPS = 1e-6
NEG_BIG = -1e30

COL_A = 0
COL_GATES = 5 * 1024
COL_NA = 7 * 1024
PROJ_COLS = 7 * 1024 + 3 * NA_W

HGRN_CHUNK = 64
HGRN_TILE = 256
NA_ROWS_PER_STEP = 8
PROJ_TM = 1024
PROJ_TN = 2176
MIX_TM = 512
MOE_BLK = 512
DISP_TOK = 1024
COMB_TOK = 512
WAIT_GROUP = 64
ROW_SUB, ROW_LANE = 8, 128
VMEM_LIMIT = 56 * 1024 * 1024


def _cparams(sem):
    return pltpu.CompilerParams(dimension_semantics=sem, vmem_limit_bytes=VMEM_LIMIT)


def _dot(a, b):
    return jnp.dot(a, b, preferred_element_type=F32)


def _dot_nt(a, b):
    return lax.dot_general(a, b, (((1,), (1,)), ((), ())), preferred_element_type=F32)


def _dot_tn(a, b):
    return lax.dot_general(a, b, (((0,), (0,)), ((), ())), preferred_element_type=F32)


def _split(a):
    hi = a.astype(BF16)
    lo = (a - hi.astype(F32)).astype(BF16)
    return hi, lo


def _dot3(a, w):
    ah, al = _split(a)
    wh, wl = _split(w)
    return _dot(ah, wh) + (_dot(al, wh) + _dot(ah, wl))


def _sigmoid(x):
    return 1.0 / (1.0 + jnp.exp(-x))


def _rms(x, g):
    return x * lax.rsqrt(jnp.mean(x * x, axis=-1, keepdims=True) + RMS_EPS) * g


def _ada_kernel(c_ref, w_ref, b_ref, o_ref):
    c = c_ref[...]
    o_ref[...] = _dot3(c * _sigmoid(c), w_ref[...]) + b_ref[...]


def _ada_mod(c, ada_w, ada_b):
    bp, d = c.shape
    n = ada_w.shape[1]
    tn = 1536
    return pl.pallas_call(
        _ada_kernel,
        grid=(n // tn,),
        in_specs=[pl.BlockSpec((bp, d), lambda j: (0, 0)),
                  pl.BlockSpec((d, tn), lambda j: (0, j)),
                  pl.BlockSpec((1, tn), lambda j: (0, j))],
        out_specs=pl.BlockSpec((bp, tn), lambda j: (0, j)),
        out_shape=jax.ShapeDtypeStruct((bp, n), F32),
        compiler_params=_cparams(("arbitrary",)),
        name="ada_mod",
    )(c, ada_w, ada_b.reshape(1, n))


def _in_proj_kernel(x_ref, mod_ref, g_ref, w_ref, o_ref, xn_ref):
    @pl.when(pl.program_id(2) == 0)
    def _():
        sh = mod_ref[0, 0:1, :]
        sc = mod_ref[0, 1:2, :]
        xn = _rms(x_ref[0], g_ref[...]) * (1.0 + sc) + sh
        xn_ref[...] = xn.astype(BF16)

    o_ref[0] = _dot(xn_ref[...], w_ref[...]).astype(BF16)


def _in_proj(x, mod, g_pre, w_in_bf):
    b, t, d = x.shape
    tm = min(PROJ_TM, t)
    n = w_in_bf.shape[1]
    return pl.pallas_call(
        _in_proj_kernel,
        grid=(b, t // tm, n // PROJ_TN),
        in_specs=[pl.BlockSpec((1, tm, d), lambda bi, i, j: (bi, i, 0)),
                  pl.BlockSpec((1, 6, d), lambda bi, i, j: (bi, 0, 0)),
                  pl.BlockSpec((1, d), lambda bi, i, j: (0, 0)),
                  pl.BlockSpec((d, PROJ_TN), lambda bi, i, j: (0, j))],
        out_specs=pl.BlockSpec((1, tm, PROJ_TN), lambda bi, i, j: (bi, i, j)),
        out_shape=jax.ShapeDtypeStruct((b, t, n), BF16),
        scratch_shapes=[pltpu.VMEM((tm, d), BF16)],
        compiler_params=_cparams(("arbitrary", "arbitrary", "arbitrary")),
        name="in_proj",
    )(x, mod, g_pre.reshape(1, d), w_in_bf)


def _cumsum_rows(x, reverse):
    n = x.shape[0]
    row = lax.broadcasted_iota(jnp.int32, x.shape, 0)
    step = 1
    while step < n:
        if reverse:
            x = x + jnp.where(row < n - step, pltpu.roll(x, n - step, axis=0), 0.0)
        else:
            x = x + jnp.where(row >= step, pltpu.roll(x, step, axis=0), 0.0)
        step *= 2
    return x


def _hgrn_dir(q_ref, f_ref, i_ref, o_ref, s_ref, sn_ref, lbd, row0, reverse):
    c = HGRN_CHUNK
    rows = pl.ds(row0, c)
    q = q_ref[0, rows, :].astype(F32)
    fpre = f_ref[0, rows, :].astype(F32)
    v = i_ref[0, rows, :]
    f = lbd + (1.0 - lbd) * _sigmoid(fpre)
    logf = jnp.log(f)
    k = 1.0 - f
    ri = lax.broadcasted_iota(jnp.int32, (c, c), 0)
    ci = lax.broadcasted_iota(jnp.int32, (c, c), 1)
    keep = (ci >= ri) if reverse else (ri >= ci)
    b = _cumsum_rows(logf, reverse)
    if reverse:
        b_mid = b[c // 2:c // 2 + 1, :]
        b_end = b[0:1, :]
    else:
        b_mid = b[c // 2 - 1:c // 2, :]
        b_end = b[c - 1:c, :]
    qm = (q * jnp.exp(b - b_mid)).astype(BF16)
    km = (k * jnp.exp(b_mid - b)).astype(BF16)
    qi = (q * jnp.exp(b)).astype(BF16)
    ks = (k * jnp.exp(b_end - b)).astype(BF16)
    dec = jnp.exp(b_end)
    for h in range(HA_HEADS):
        sl = slice(h * HA_DK, (h + 1) * HA_DK)
        a = _dot_nt(qm[:, sl], km[:, sl])
        a = jnp.where(keep, a, 0.0).astype(BF16)
        lhs = jnp.concatenate([qi[:, sl], a], axis=1)
        rhs = jnp.concatenate([sn_ref[h], v[:, sl]], axis=0)
        o_ref[0, rows, sl] = _dot(lhs, rhs).astype(BF16)
        st = s_ref[h] * dec[:, sl] + _dot_tn(v[:, sl], ks[:, sl])
        s_ref[h] = st
        sn_ref[h] = st.astype(BF16).T


def _hgrn_kernel(lbl_ref, qf_ref, ff_ref, if_ref, qb_ref, fb_ref, ib_ref, of_ref, ob_ref,
                 sf_ref, sb_ref, snf_ref, snb_ref):
    @pl.when(pl.program_id(1) == 0)
    def _():
        for ref in (sf_ref, sb_ref, snf_ref, snb_ref):
            ref[...] = jnp.zeros_like(ref)

    l0 = lbl_ref[0]
    l1 = lbl_ref[1]
    m = jnp.maximum(l0, l1)
    e0 = jnp.exp(l0 - m)
    e1 = jnp.exp(l1 - m)
    lb = e0 / (e0 + e1)
    n_chunks = HGRN_TILE // HGRN_CHUNK

    for ci in range(n_chunks):
        _hgrn_dir(qf_ref, ff_ref, if_ref, of_ref, sf_ref, snf_ref, lb[0:1, :], ci * HGRN_CHUNK, False)
        _hgrn_dir(qb_ref, fb_ref, ib_ref, ob_ref, sb_ref, snb_ref, lb[1:2, :],
                  (n_chunks - 1 - ci) * HGRN_CHUNK, True)


def _hgrn(proj, lb_logits):
    b, t, _ = proj.shape
    tt = HGRN_TILE
    nt = t // tt
    d = D_MODEL

    def fwd(col):
        return pl.BlockSpec((1, tt, d), lambda bi, ti: (bi, ti, col))

    def bwd(col):
        return pl.BlockSpec((1, tt, d), lambda bi, ti: (bi, nt - 1 - ti, col))

    return pl.pallas_call(
        _hgrn_kernel,
        grid=(b, nt),
        in_specs=[pl.BlockSpec((2, 2, d), lambda bi, ti: (0, 0, 0)),
                  fwd(0), fwd(1), fwd(3), bwd(0), bwd(2), bwd(3)],
        out_specs=[pl.BlockSpec((1, tt, d), lambda bi, ti: (bi, ti, 0)),
                   pl.BlockSpec((1, tt, d), lambda bi, ti: (bi, nt - 1 - ti, 0))],
        out_shape=[jax.ShapeDtypeStruct((b, t, d), BF16), jax.ShapeDtypeStruct((b, t, d), BF16)],
        scratch_shapes=[pltpu.VMEM((HA_HEADS, HA_DK, HA_DK), F32), pltpu.VMEM((HA_HEADS, HA_DK, HA_DK), F32),
                        pltpu.VMEM((HA_HEADS, HA_DK, HA_DK), BF16), pltpu.VMEM((HA_HEADS, HA_DK, HA_DK), BF16)],
        compiler_params=_cparams(("arbitrary", "arbitrary")),
        name="hgrn",
    )(lb_logits, proj, proj, proj, proj, proj, proj)


def _na_bias_tables(rpb):
    q = np.arange(GRID_W)[:, None]
    kc = np.arange(GRID_W)[None, :]
    c0 = np.clip(q - NA_KW // 2, 0, GRID_W - NA_KW)
    ok = (kc >= c0) & (kc < c0 + NA_KW)
    dc = np.clip(kc - q + (NA_KW - 1), 0, 2 * NA_KW - 2)
    base = jnp.where(ok[None, None], jnp.take(rpb.astype(F32), dc, axis=2), NEG_BIG)
    tabs = []
    for dr0 in range(NA_KH):
        w = base[:, dr0:dr0 + NA_KH]
        w = w.transpose(0, 2, 1, 3).reshape(NA_HEADS // 2, 2 * GRID_W, NA_KH * GRID_W)
        tabs.append(w)
    return jnp.stack(tabs, axis=1)


def _na_kernel(q_ref, k_ref, v_ref, tab_ref, o_ref, *, grid_rows):
    g = pl.program_id(1)
    lane = lax.broadcasted_iota(jnp.int32, (GRID_W, 2 * NA_DH), 1)
    low = lane < NA_DH
    scale = NA_DH ** -0.5

    def body(rr, carry):
        r = g * NA_ROWS_PER_STEP + rr
        r0 = jnp.clip(r - NA_KH // 2, 0, grid_rows - NA_KH)
        dr0 = r0 - r + (NA_KH - 1)
        qrow = pl.ds(pl.multiple_of(rr * GRID_W, GRID_W), GRID_W)
        krow = pl.ds(pl.multiple_of(r0 * GRID_W, GRID_W), NA_KH * GRID_W)
        for j in range(NA_HEADS // 2):
            sl = slice(j * 2 * NA_DH, (j + 1) * 2 * NA_DH)
            qp = q_ref[0, qrow, sl]
            zero = jnp.zeros_like(qp)
            ql = jnp.concatenate([jnp.where(low, qp, zero), jnp.where(low, zero, qp)], axis=0)
            s = _dot_nt(ql, k_ref[0, krow, sl]) * scale + tab_ref[j, dr0]
            m = jnp.max(s, axis=-1, keepdims=True)
            p = jnp.exp(s - m)
            l = jnp.sum(p, axis=-1, keepdims=True)
            pv = _dot(p.astype(BF16), v_ref[0, krow, sl]) / l
            o_ref[0, qrow, sl] = jnp.where(low, pv[:GRID_W], pv[GRID_W:]).astype(BF16)
        return carry

    lax.fori_loop(0, NA_ROWS_PER_STEP, body, 0, unroll=2)


def _natten(proj, tabs):
    b, t, _ = proj.shape
    grid_rows = t // GRID_W
    tq = NA_ROWS_PER_STEP * GRID_W
    cq, ck, cv = (COL_NA // NA_W, COL_NA // NA_W + 1, COL_NA // NA_W + 2)
    return pl.pallas_call(
        functools.partial(_na_kernel, grid_rows=grid_rows),
        grid=(b, t // tq),
        in_specs=[pl.BlockSpec((1, tq, NA_W), lambda bi, gi: (bi, gi, cq)),
                  pl.BlockSpec((1, t, NA_W), lambda bi, gi: (bi, 0, ck)),
                  pl.BlockSpec((1, t, NA_W), lambda bi, gi: (bi, 0, cv)),
                  pl.BlockSpec(tabs.shape, lambda bi, gi: (0, 0, 0, 0))],
        out_specs=pl.BlockSpec((1, tq, NA_W), lambda bi, gi: (bi, gi, 0)),
        out_shape=jax.ShapeDtypeStruct((b, t, NA_W), BF16),
        compiler_params=_cparams(("arbitrary", "arbitrary")),
        name="natten",
    )(proj, proj, proj, tabs)


def _mix_kernel(x_ref, mod_ref, of_ref, ob_ref, g_ref, ga_ref, gb_ref, nb_ref,
                gn_ref, gpm_ref, gpf_ref, wa_ref, wb_ref, wo_ref, rw_ref, rb_ref,
                x1_ref, xn_ref, ri_ref, rg_ref, cnt_ref, tri_ref, carry_ref):
    tm = MIX_TM
    first = (pl.program_id(0) == 0) & (pl.program_id(1) == 0)

    @pl.when(first)
    def _():
        r = lax.broadcasted_iota(jnp.int32, (tm, tm), 0)
        c = lax.broadcasted_iota(jnp.int32, (tm, tm), 1)
        tri_ref[...] = jnp.where(r > c, 1.0, 0.0).astype(BF16)
        carry_ref[...] = jnp.zeros_like(carry_ref)

    sc2 = mod_ref[0, 4:5, :]
    sh2 = mod_ref[0, 3:4, :]
    gt1 = mod_ref[0, 2:3, :]

    o = of_ref[0].astype(F32) + ob_ref[0].astype(F32)
    parts = []
    for h in range(HA_HEADS):
        oh = o[:, h * HA_DK:(h + 1) * HA_DK]
        parts.append(oh * lax.rsqrt(jnp.mean(oh * oh, axis=-1, keepdims=True) + RMS_EPS))
    g = g_ref[0].astype(F32)
    oa = jnp.concatenate(parts, axis=-1) * gn_ref[...] * (g * _sigmoid(g))
    ya = _dot(oa.astype(BF16), wa_ref[...])
    yb = _dot(nb_ref[0], wb_ref[...])
    mix = _sigmoid(ga_ref[0].astype(F32)) * ya + _sigmoid(gb_ref[0].astype(F32)) * yb
    mo = _dot(mix.astype(BF16), wo_ref[...])
    x1 = x_ref[0] + gt1 * _rms(mo, gpm_ref[...])
    x1_ref[0] = x1

    xn = _rms(x1, gpf_ref[...]) * (1.0 + sc2) + sh2
    xn_ref[0] = xn.reshape(tm, ROW_SUB, ROW_LANE)
    logits = _dot(xn.astype(BF16), rw_ref[...]) + rb_ref[...]
    lane = lax.broadcasted_iota(jnp.int32, (tm, N_EXPERTS), 1)
    work = logits
    vals, idxs, hots = [], [], []
    for _ in range(TOP_K):
        mv = jnp.max(work, axis=-1, keepdims=True)
        mi = jnp.min(jnp.where(work == mv, lane, N_EXPERTS), axis=-1, keepdims=True)
        hot = lane == mi
        vals.append(mv)
        idxs.append(mi)
        hots.append(hot)
        work = jnp.where(hot, -jnp.inf, work)
    es = [jnp.exp(v - vals[0]) for v in vals]
    den = es[0] + es[1] + es[2] + es[3]

    cnt = sum(jnp.where(h, 1.0, 0.0) for h in hots)
    before = _dot(tri_ref[...], cnt.astype(BF16)) + carry_ref[...]
    carry_ref[...] = carry_ref[...] + jnp.sum(cnt, axis=0, keepdims=True)
    cnt_ref[...] = jnp.broadcast_to(carry_ref[...], cnt_ref.shape)

    lane_o = lax.broadcasted_iota(jnp.int32, (tm, 128), 1)
    ri = jnp.zeros((tm, 128), jnp.int32)
    rg = jnp.zeros((tm, 128), F32)
    for kk in range(TOP_K):
        rank = jnp.sum(jnp.where(hots[kk], before, 0.0), axis=-1, keepdims=True).astype(jnp.int32)
        ri = jnp.where(lane_o == kk, idxs[kk], ri)
        ri = jnp.where(lane_o == TOP_K + kk, rank, ri)
        rg = jnp.where(lane_o == kk, es[kk] / den, rg)
    ri_ref[0] = ri
    rg_ref[0] = rg


def _mix(x, mod, proj, o_fw, o_bw, o_b, hgrn_norm, g_post_mix, g_pre_ffn, wa, wb, wo, rw, rb):
    b, t, d = x.shape
    tm = min(MIX_TM, t)
    assert tm == MIX_TM
    row = lambda a: a.reshape(1, -1)
    tok = lambda w, col: pl.BlockSpec((1, tm, w), lambda bi, i: (bi, i, col))
    full = lambda a: pl.BlockSpec(a.shape, lambda bi, i: (0,) * a.ndim)
    gn, gpm, gpf, rbr = row(hgrn_norm), row(g_post_mix), row(g_pre_ffn), row(rb)
    outs = pl.pallas_call(
        _mix_kernel,
        grid=(b, t // tm),
        in_specs=[tok(d, 0),
                  pl.BlockSpec((1, 6, d), lambda bi, i: (bi, 0, 0)),
                  tok(d, 0), tok(d, 0),
                  tok(d, 4),
                  tok(d, COL_GATES // d), tok(d, COL_GATES // d + 1),
                  tok(NA_W, 0),
                  full(gn), full(gpm), full(gpf), full(wa), full(wb), full(wo), full(rw), full(rbr)],
        out_specs=[tok(d, 0), pl.BlockSpec((1, tm, ROW_SUB, ROW_LANE), lambda bi, i: (bi, i, 0, 0)),
                   tok(128, 0), tok(128, 0),
                   pl.BlockSpec((8, N_EXPERTS), lambda bi, i: (0, 0))],
        out_shape=[jax.ShapeDtypeStruct((b, t, d), F32), jax.ShapeDtypeStruct((b, t, ROW_SUB, ROW_LANE), F32),
                   jax.ShapeDtypeStruct((b, t, 128), jnp.int32), jax.ShapeDtypeStruct((b, t, 128), F32),
                   jax.ShapeDtypeStruct((8, N_EXPERTS), F32)],
        scratch_shapes=[pltpu.VMEM((tm, tm), BF16), pltpu.VMEM((1, N_EXPERTS), F32)],
        compiler_params=_cparams(("arbitrary", "arbitrary")),
        name="mix",
    )(x, mod, o_fw, o_bw, proj, proj, proj, o_b, gn, gpm, gpf, wa, wb, wo, rw, rbr)
    return outs


def _wait_rows(row_copy, n):
    def body(_, carry):
        for _ in range(WAIT_GROUP):
            row_copy().wait()
        return carry

    lax.fori_loop(0, n // WAIT_GROUP, body, 0)


def _dispatch_kernel(dest_hbm, x_ref, xs_hbm, idx_smem, sem_idx, sem):
    i = pl.program_id(0)
    n_idx = DISP_TOK * TOP_K
    cp = pltpu.make_async_copy(dest_hbm.at[pl.ds(i * n_idx, n_idx)], idx_smem, sem_idx)
    cp.start()
    cp.wait()

    def start(tt, carry):
        for kk in range(TOP_K):
            pltpu.make_async_copy(x_ref.at[tt], xs_hbm.at[idx_smem[tt * TOP_K + kk]], sem).start(priority=kk % 2)
        return carry

    lax.fori_loop(0, DISP_TOK, start, 0, unroll=4)
    _wait_rows(lambda: pltpu.make_async_copy(x_ref.at[0], xs_hbm.at[0], sem), n_idx)


def _dispatch(xn, dest_flat, n_rows):
    n = xn.shape[0]
    return pl.pallas_call(
        _dispatch_kernel,
        grid=(n // DISP_TOK,),
        in_specs=[pl.BlockSpec(memory_space=pl.ANY),
                  pl.BlockSpec((DISP_TOK, ROW_SUB, ROW_LANE), lambda i: (i, 0, 0))],
        out_specs=pl.BlockSpec(memory_space=pl.ANY),
        out_shape=jax.ShapeDtypeStruct((n_rows, ROW_SUB, ROW_LANE), F32),
        scratch_shapes=[pltpu.SMEM((DISP_TOK * TOP_K,), jnp.int32),
                        pltpu.SemaphoreType.DMA, pltpu.SemaphoreType.DMA],
        compiler_params=_cparams(("arbitrary",)),
        name="dispatch",
    )(dest_flat, xn)


GU_SLAB = 256


def _gate_up_layout(w_gate_up, b_gate_up):
    e, k, n2 = w_gate_up.shape
    half = GU_SLAB // 2
    pm = np.zeros((GU_SLAB, GU_SLAB), np.float32)
    pm[2 * np.arange(half), np.arange(half)] = 1.0
    pm[2 * np.arange(half) + 1, half + np.arange(half)] = 1.0
    w4 = w_gate_up.astype(BF16).reshape(e, k, n2 // GU_SLAB, GU_SLAB)
    w = jnp.einsum('eksc,cd->eksd', w4, jnp.asarray(pm, BF16), preferred_element_type=BF16).reshape(e, k, n2)
    perm = np.concatenate([s * GU_SLAB + np.concatenate([2 * np.arange(half), 2 * np.arange(half) + 1])
                           for s in range(n2 // GU_SLAB)])
    return w, b_gate_up[:, None, perm]


def _expert_kernel(be_ref, bv_ref, x_ref, wgu_ref, bgu_ref, wd_ref, bd_ref, o_ref, a_ref):
    i = pl.program_id(0)
    valid = bv_ref[i]
    half = GU_SLAB // 2

    @pl.when(valid > 0)
    def _():
        rowi = lax.broadcasted_iota(jnp.int32, (MOE_BLK, 1), 0)
        x = x_ref[...].reshape(MOE_BLK, D_MODEL)
        x = jnp.where(rowi < valid, x, 0.0).astype(BF16)
        for s in range(2 * D_FF // GU_SLAB):
            cols = slice(s * GU_SLAB, (s + 1) * GU_SLAB)
            h = _dot(x, wgu_ref[0, :, cols]) + bgu_ref[0, :, cols]
            glu = jnp.minimum(h[:, :half], SWIGLU_LIMIT)
            lin = jnp.clip(h[:, half:], -SWIGLU_LIMIT, SWIGLU_LIMIT)
            a = glu * _sigmoid(SWIGLU_ALPHA * glu) * (lin + 1.0)
            a_ref[:, s * half:(s + 1) * half] = a.astype(BF16)
        o_ref[...] = (_dot(a_ref[...], wd_ref[0]) + bd_ref[0]).reshape(o_ref.shape)

    @pl.when(valid <= 0)
    def _():
        o_ref[...] = jnp.zeros_like(o_ref)


def _experts(xs, block_e, block_valid, wgu, bgu, wd, bd):
    n_rows = xs.shape[0]
    d = D_MODEL
    n_blocks = n_rows // MOE_BLK
    emap = lambda i, be, bv: (be[i], 0, 0)
    rows = lambda: pl.BlockSpec((MOE_BLK, ROW_SUB, ROW_LANE), lambda i, be, bv: (i, 0, 0))
    return pl.pallas_call(
        _expert_kernel,
        grid_spec=pltpu.PrefetchScalarGridSpec(
            num_scalar_prefetch=2,
            grid=(n_blocks,),
            in_specs=[rows(),
                      pl.BlockSpec((1, d, 2 * D_FF), emap), pl.BlockSpec((1, 1, 2 * D_FF), emap),
                      pl.BlockSpec((1, D_FF, d), emap), pl.BlockSpec((1, 1, d), emap)],
            out_specs=rows(),
            scratch_shapes=[pltpu.VMEM((MOE_BLK, D_FF), BF16)],
        ),
        out_shape=jax.ShapeDtypeStruct((n_rows, ROW_SUB, ROW_LANE), F32),
        compiler_params=_cparams(("arbitrary",)),
        name="experts",
    )(block_e, block_valid, xs, wgu, bgu, wd, bd)


SC_CORES, SC_SUBCORES = 2, 16
SC_ROWS = 32


def _gather_rows(y, dest_flat):
    n_out = dest_flat.shape[0]
    workers = SC_CORES * SC_SUBCORES
    per_w = n_out // workers
    n_it = per_w // SC_ROWS
    assert per_w * workers == n_out and n_it * SC_ROWS == per_w and n_it % 2 == 0
    mesh = plsc.VectorSubcoreMesh(core_axis_name="c", subcore_axis_name="s")

    @functools.partial(
        pl.kernel, mesh=mesh,
        out_type=jax.ShapeDtypeStruct((n_out, ROW_SUB, ROW_LANE), F32),
        scratch_types=[pltpu.VMEM((2, SC_ROWS), jnp.int32),
                       pltpu.VMEM((2, SC_ROWS, ROW_SUB, ROW_LANE), F32),
                       pltpu.SemaphoreType.DMA((2,))],
    )
    def gather_kernel(y_hbm, idx_hbm, out_hbm, idx_v, rows_v, sem):
        base = (lax.axis_index("s") * SC_CORES + lax.axis_index("c")) * per_w

        def span(g):
            return pl.ds(pl.multiple_of(base + g * SC_ROWS, SC_ROWS), SC_ROWS)

        def row_gather(slot):
            return pltpu.make_async_copy(y_hbm.at[idx_v.at[slot]], rows_v.at[slot], sem.at[slot])

        def start(g, slot):
            pltpu.sync_copy(idx_hbm.at[span(g)], idx_v.at[slot])
            row_gather(slot).start()

        start(0, 0)

        @pl.loop(0, n_it, step=2)
        def _(g):
            start(g + 1, 1)
            row_gather(0).wait()
            pltpu.sync_copy(rows_v.at[0], out_hbm.at[span(g)])

            @pl.when(g + 2 < n_it)
            def _():
                start(g + 2, 0)

            row_gather(1).wait()
            pltpu.sync_copy(rows_v.at[1], out_hbm.at[span(g + 1)])

    return gather_kernel(y, dest_flat)


def _combine_kernel(yg_ref, x1_ref, mod_ref, rg_ref, gpo_ref, o_ref):
    rg = rg_ref[...]
    h = rg[:, 0:1] * yg_ref[:, 0].reshape(COMB_TOK, D_MODEL)
    for kk in range(1, TOP_K):
        h = h + rg[:, kk:kk + 1] * yg_ref[:, kk].reshape(COMB_TOK, D_MODEL)
    gt2 = mod_ref[0, 5:6, :]
    o_ref[...] = x1_ref[...] + gt2 * _rms(h, gpo_ref[...])


def _combine(yg, x1, mod, rg, g_post_ffn, tokens_per_batch):
    n, d = x1.shape
    per_b = tokens_per_batch // COMB_TOK
    gpo = g_post_ffn.reshape(1, d)
    return pl.pallas_call(
        _combine_kernel,
        grid=(n // COMB_TOK,),
        in_specs=[pl.BlockSpec((COMB_TOK, TOP_K, ROW_SUB, ROW_LANE), lambda i: (i, 0, 0, 0)),
                  pl.BlockSpec((COMB_TOK, d), lambda i: (i, 0)),
                  pl.BlockSpec((1, 6, d), lambda i: (i // per_b, 0, 0)),
                  pl.BlockSpec((COMB_TOK, 128), lambda i: (i, 0)),
                  pl.BlockSpec((1, d), lambda i: (0, 0))],
        out_specs=pl.BlockSpec((COMB_TOK, d), lambda i: (i, 0)),
        out_shape=jax.ShapeDtypeStruct((n, d), F32),
        compiler_params=_cparams(("arbitrary",)),
        name="combine",
    )(yg, x1, mod, rg, gpo)


def _moe_plan(ri, counts, n_tok):
    idx = ri[:, :TOP_K]
    rank = ri[:, TOP_K:2 * TOP_K]
    cnt = counts.astype(jnp.int32)
    padded = (cnt + MOE_BLK - 1) // MOE_BLK * MOE_BLK
    pend = jnp.cumsum(padded)
    pstart = pend - padded
    dest = pstart[idx] + rank
    n_blocks = (n_tok * TOP_K + MOE_BLK - 1) // MOE_BLK + N_EXPERTS
    bstart = jnp.arange(n_blocks, dtype=jnp.int32) * MOE_BLK
    block_e = jnp.minimum(jnp.sum((bstart[:, None] >= pend[None, :]).astype(jnp.int32), axis=1), N_EXPERTS - 1)
    block_valid = jnp.clip(pstart[block_e] + cnt[block_e] - bstart, 0, MOE_BLK).astype(jnp.int32)
    return dest.reshape(-1).astype(jnp.int32), block_e, block_valid, n_blocks


def _layer(x, mod, p):
    b, t, d = x.shape
    n = b * t
    proj = _in_proj(x, mod, p['g_pre_mix'], p['w_in'])
    o_fw, o_bw = _hgrn(proj, p['lb_logits'])
    o_b = _natten(proj, p['na_tabs'])
    x1, xn, ri, rg, counts = _mix(x, mod, proj, o_fw, o_bw, o_b, p['hgrn_norm'], p['g_post_mix'], p['g_pre_ffn'],
                                  p['w_branch_a'], p['w_branch_b'], p['w_out'], p['router_w'], p['router_b'])
    ri = ri.reshape(n, 128)
    rg = rg.reshape(n, 128)
    dest, block_e, block_valid, n_blocks = _moe_plan(ri, counts[0], n)
    xs = _dispatch(xn.reshape(n, ROW_SUB, ROW_LANE), dest, n_blocks * MOE_BLK)
    y = _experts(xs, block_e, block_valid, p['w_gu'], p['b_gu'], p['w_d'], p['b_d'])
    yg = _gather_rows(y, dest).reshape(n, TOP_K, ROW_SUB, ROW_LANE)
    out = _combine(yg, x1.reshape(n, d), mod, rg, p['g_post_ffn'], t)
    return out.reshape(b, t, d)


def kernel(x_prompt, x_sample, c_prompt, c_sample, ada_w, ada_b, g_pre_mix, g_post_mix, g_pre_ffn, g_post_ffn,
           w_in, hgrn_lb_logits, hgrn_norm, na_rpb, w_branch_a, w_branch_b, w_out, router_w, router_b,
           w_gate_up, b_gate_up, w_down, b_down):
    d = D_MODEL
    bp, bs = c_prompt.shape[0], c_sample.shape[0]
    pad = (-(bp + bs)) % 8
    c_all = jnp.concatenate([c_prompt, c_sample, jnp.zeros((pad, d), F32)], axis=0)
    mod = _ada_mod(c_all, ada_w[0], ada_b[0]).reshape(-1, 6, d)

    w = w_in[0]
    w_perm = jnp.concatenate([w[:, :5 * d], w[:, 5 * d + 3 * NA_W:], w[:, 5 * d:5 * d + 3 * NA_W]], axis=1).astype(BF16)
    w_gu, b_gu = _gate_up_layout(w_gate_up[0], b_gate_up[0])
    p = dict(
        g_pre_mix=g_pre_mix[0], g_post_mix=g_post_mix[0], g_pre_ffn=g_pre_ffn[0], g_post_ffn=g_post_ffn[0],
        w_in=w_perm, lb_logits=hgrn_lb_logits.astype(F32), hgrn_norm=hgrn_norm[0],
        na_tabs=_na_bias_tables(na_rpb[0]),
        w_branch_a=w_branch_a[0].astype(BF16), w_branch_b=w_branch_b[0].astype(BF16), w_out=w_out[0].astype(BF16),
        router_w=router_w[0].astype(BF16), router_b=router_b[0],
        w_gu=w_gu, b_gu=b_gu,
        w_d=w_down[0].astype(BF16), b_d=b_down[0][:, None, :],
    )
    y_p = _layer(x_prompt, mod[:bp], p)
    y_s = _layer(x_sample, mod[bp:bp + bs], p)
    return (y_p, y_s)
```

```python
import functools

import numpy as np
import jax
import jax.numpy as jnp
from jax import lax
from jax.experimental import pallas as pl
from jax.experimental.pallas import tpu as pltpu
from jax.experimental.pallas import tpu_sc as plsc

F32 = jnp.float32
BF16 = jnp.bfloat16

D_MODEL = 1024
GRID_W = 64
HA_HEADS = 8
HA_DK = 128
NA_HEADS = 8
NA_DH = 64
NA_W = NA_HEADS * NA_DH
NA_KH = 8
NA_KW = 16
N_EXPERTS = 32
TOP_K = 4
D_FF = 1024
SWIGLU_ALPHA = 1.702
SWIGLU_LIMIT = 7.0
RMS_EPS = 1e-6
NEG_BIG = -1e30

COL_A = 0
COL_GATES = 5 * 1024
COL_NA = 7 * 1024
PROJ_COLS = 7 * 1024 + 3 * NA_W

HGRN_CHUNK = 64
HGRN_TILE = 256
NA_ROWS_PER_STEP = 8
PROJ_TM = 1024
PROJ_TN = 2176
MIX_TM = 512
MOE_BLK = 512
COMB_TOK = 512
ROW_SUB, ROW_LANE = 8, 128
VMEM_LIMIT = 56 * 1024 * 1024


def _cparams(sem):
    return pltpu.CompilerParams(dimension_semantics=sem, vmem_limit_bytes=VMEM_LIMIT)


def _dot(a, b):
    return jnp.dot(a, b, preferred_element_type=F32)


def _dot_nt(a, b):
    return lax.dot_general(a, b, (((1,), (1,)), ((), ())), preferred_element_type=F32)


def _dot_tn(a, b):
    return lax.dot_general(a, b, (((0,), (0,)), ((), ())), preferred_element_type=F32)


def _split(a):
    hi = a.astype(BF16)
    lo = (a - hi.astype(F32)).astype(BF16)
    return hi, lo


def _dot3(a, w):
    ah, al = _split(a)
    wh, wl = _split(w)
    return _dot(ah, wh) + (_dot(al, wh) + _dot(ah, wl))


def _sigmoid(x):
    return 1.0 / (1.0 + jnp.exp(-x))


def _rms(x, g):
    return x * lax.rsqrt(jnp.mean(x * x, axis=-1, keepdims=True) + RMS_EPS) * g


def _ada_kernel(c_ref, w_ref, b_ref, o_ref):
    c = c_ref[...]
    o_ref[...] = _dot3(c * _sigmoid(c), w_ref[...]) + b_ref[...]


def _ada_mod(c, ada_w, ada_b):
    bp, d = c.shape
    n = ada_w.shape[1]
    tn = 1536
    return pl.pallas_call(
        _ada_kernel,
        grid=(n // tn,),
        in_specs=[pl.BlockSpec((bp, d), lambda j: (0, 0)),
                  pl.BlockSpec((d, tn), lambda j: (0, j)),
                  pl.BlockSpec((1, tn), lambda j: (0, j))],
        out_specs=pl.BlockSpec((bp, tn), lambda j: (0, j)),
        out_shape=jax.ShapeDtypeStruct((bp, n), F32),
        compiler_params=_cparams(("arbitrary",)),
        name="ada_mod",
    )(c, ada_w, ada_b.reshape(1, n))


def _in_proj_kernel(x_ref, mod_ref, g_ref, w_ref, o_ref, xn_ref):
    @pl.when(pl.program_id(2) == 0)
    def _():
        sh = mod_ref[0, 0:1, :]
        sc = mod_ref[0, 1:2, :]
        xn = _rms(x_ref[0], g_ref[...]) * (1.0 + sc) + sh
        xn_ref[...] = xn.astype(BF16)

    o_ref[0] = _dot(xn_ref[...], w_ref[...]).astype(BF16)


def _in_proj(x, mod, g_pre, w_in_bf):
    b, t, d = x.shape
    tm = min(PROJ_TM, t)
    n = w_in_bf.shape[1]
    return pl.pallas_call(
        _in_proj_kernel,
        grid=(b, t // tm, n // PROJ_TN),
        in_specs=[pl.BlockSpec((1, tm, d), lambda bi, i, j: (bi, i, 0)),
                  pl.BlockSpec((1, 6, d), lambda bi, i, j: (bi, 0, 0)),
                  pl.BlockSpec((1, d), lambda bi, i, j: (0, 0)),
                  pl.BlockSpec((d, PROJ_TN), lambda bi, i, j: (0, j))],
        out_specs=pl.BlockSpec((1, tm, PROJ_TN), lambda bi, i, j: (bi, i, j)),
        out_shape=jax.ShapeDtypeStruct((b, t, n), BF16),
        scratch_shapes=[pltpu.VMEM((tm, d), BF16)],
        compiler_params=_cparams(("arbitrary", "arbitrary", "arbitrary")),
        name="in_proj",
    )(x, mod, g_pre.reshape(1, d), w_in_bf)


def _cumsum_rows(x, reverse):
    n = x.shape[0]
    row = lax.broadcasted_iota(jnp.int32, x.shape, 0)
    step = 1
    while step < n:
        if reverse:
            x = x + jnp.where(row < n - step, pltpu.roll(x, n - step, axis=0), 0.0)
        else:
            x = x + jnp.where(row >= step, pltpu.roll(x, step, axis=0), 0.0)
        step *= 2
    return x


def _hgrn_dir(q_ref, f_ref, i_ref, o_ref, s_ref, sn_ref, lbd, row0, reverse):
    c = HGRN_CHUNK
    rows = pl.ds(row0, c)
    q = q_ref[0, rows, :].astype(F32)
    fpre = f_ref[0, rows, :].astype(F32)
    v = i_ref[0, rows, :]
    f = lbd + (1.0 - lbd) * _sigmoid(fpre)
    logf = jnp.log(f)
    k = 1.0 - f
    ri = lax.broadcasted_iota(jnp.int32, (c, c), 0)
    ci = lax.broadcasted_iota(jnp.int32, (c, c), 1)
    keep = (ci >= ri) if reverse else (ri >= ci)
    b = _cumsum_rows(logf, reverse)
    if reverse:
        b_mid = b[c // 2:c // 2 + 1, :]
        b_end = b[0:1, :]
    else:
        b_mid = b[c // 2 - 1:c // 2, :]
        b_end = b[c - 1:c, :]
    qm = (q * jnp.exp(b - b_mid)).astype(BF16)
    km = (k * jnp.exp(b_mid - b)).astype(BF16)
    qi = (q * jnp.exp(b)).astype(BF16)
    ks = (k * jnp.exp(b_end - b)).astype(BF16)
    dec = jnp.exp(b_end)
    for h in range(HA_HEADS):
        sl = slice(h * HA_DK, (h + 1) * HA_DK)
        a = _dot_nt(qm[:, sl], km[:, sl])
        a = jnp.where(keep, a, 0.0).astype(BF16)
        lhs = jnp.concatenate([qi[:, sl], a], axis=1)
        rhs = jnp.concatenate([sn_ref[h], v[:, sl]], axis=0)
        o_ref[0, rows, sl] = _dot(lhs, rhs).astype(BF16)
        st = s_ref[h] * dec[:, sl] + _dot_tn(v[:, sl], ks[:, sl])
        s_ref[h] = st
        sn_ref[h] = st.astype(BF16).T


def _hgrn_kernel(lbl_ref, qf_ref, ff_ref, if_ref, qb_ref, fb_ref, ib_ref, of_ref, ob_ref,
                 sf_ref, sb_ref, snf_ref, snb_ref):
    @pl.when(pl.program_id(1) == 0)
    def _():
        for ref in (sf_ref, sb_ref, snf_ref, snb_ref):
            ref[...] = jnp.zeros_like(ref)

    l0 = lbl_ref[0]
    l1 = lbl_ref[1]
    m = jnp.maximum(l0, l1)
    e0 = jnp.exp(l0 - m)
    e1 = jnp.exp(l1 - m)
    lb = e0 / (e0 + e1)
    n_chunks = HGRN_TILE // HGRN_CHUNK

    for ci in range(n_chunks):
        _hgrn_dir(qf_ref, ff_ref, if_ref, of_ref, sf_ref, snf_ref, lb[0:1, :], ci * HGRN_CHUNK, False)
        _hgrn_dir(qb_ref, fb_ref, ib_ref, ob_ref, sb_ref, snb_ref, lb[1:2, :],
                  (n_chunks - 1 - ci) * HGRN_CHUNK, True)


def _hgrn(proj, lb_logits):
    b, t, _ = proj.shape
    tt = HGRN_TILE
    nt = t // tt
    d = D_MODEL

    def fwd(col):
        return pl.BlockSpec((1, tt, d), lambda bi, ti: (bi, ti, col))

    def bwd(col):
        return pl.BlockSpec((1, tt, d), lambda bi, ti: (bi, nt - 1 - ti, col))

    return pl.pallas_call(
        _hgrn_kernel,
        grid=(b, nt),
        in_specs=[pl.BlockSpec((2, 2, d), lambda bi, ti: (0, 0, 0)),
                  fwd(0), fwd(1), fwd(3), bwd(0), bwd(2), bwd(3)],
        out_specs=[pl.BlockSpec((1, tt, d), lambda bi, ti: (bi, ti, 0)),
                   pl.BlockSpec((1, tt, d), lambda bi, ti: (bi, nt - 1 - ti, 0))],
        out_shape=[jax.ShapeDtypeStruct((b, t, d), BF16), jax.ShapeDtypeStruct((b, t, d), BF16)],
        scratch_shapes=[pltpu.VMEM((HA_HEADS, HA_DK, HA_DK), F32), pltpu.VMEM((HA_HEADS, HA_DK, HA_DK), F32),
                        pltpu.VMEM((HA_HEADS, HA_DK, HA_DK), BF16), pltpu.VMEM((HA_HEADS, HA_DK, HA_DK), BF16)],
        compiler_params=_cparams(("arbitrary", "arbitrary")),
        name="hgrn",
    )(lb_logits, proj, proj, proj, proj, proj, proj)


def _na_bias_tables(rpb):
    q = np.arange(GRID_W)[:, None]
    kc = np.arange(GRID_W)[None, :]
    c0 = np.clip(q - NA_KW // 2, 0, GRID_W - NA_KW)
    ok = (kc >= c0) & (kc < c0 + NA_KW)
    dc = np.clip(kc - q + (NA_KW - 1), 0, 2 * NA_KW - 2)
    base = jnp.where(ok[None, None], jnp.take(rpb.astype(F32), dc, axis=2), NEG_BIG)
    tabs = []
    for dr0 in range(NA_KH):
        w = base[:, dr0:dr0 + NA_KH]
        w = w.transpose(0, 2, 1, 3).reshape(NA_HEADS // 2, 2 * GRID_W, NA_KH * GRID_W)
        tabs.append(w)
    return jnp.stack(tabs, axis=1)


def _na_kernel(q_ref, k_ref, v_ref, tab_ref, o_ref, *, grid_rows):
    g = pl.program_id(1)
    lane = lax.broadcasted_iota(jnp.int32, (GRID_W, 2 * NA_DH), 1)
    low = lane < NA_DH
    scale = NA_DH ** -0.5

    def body(rr, carry):
        r = g * NA_ROWS_PER_STEP + rr
        r0 = jnp.clip(r - NA_KH // 2, 0, grid_rows - NA_KH)
        dr0 = r0 - r + (NA_KH - 1)
        qrow = pl.ds(pl.multiple_of(rr * GRID_W, GRID_W), GRID_W)
        krow = pl.ds(pl.multiple_of(r0 * GRID_W, GRID_W), NA_KH * GRID_W)
        for j in range(NA_HEADS // 2):
            sl = slice(j * 2 * NA_DH, (j + 1) * 2 * NA_DH)
            qp = q_ref[0, qrow, sl]
            zero = jnp.zeros_like(qp)
            ql = jnp.concatenate([jnp.where(low, qp, zero), jnp.where(low, zero, qp)], axis=0)
            s = _dot_nt(ql, k_ref[0, krow, sl]) * scale + tab_ref[j, dr0]
            m = jnp.max(s, axis=-1, keepdims=True)
            p = jnp.exp(s - m)
            l = jnp.sum(p, axis=-1, keepdims=True)
            pv = _dot(p.astype(BF16), v_ref[0, krow, sl]) / l
            o_ref[0, qrow, sl] = jnp.where(low, pv[:GRID_W], pv[GRID_W:]).astype(BF16)
        return carry

    lax.fori_loop(0, NA_ROWS_PER_STEP, body, 0, unroll=2)


def _natten(proj, tabs):
    b, t, _ = proj.shape
    grid_rows = t // GRID_W
    tq = NA_ROWS_PER_STEP * GRID_W
    cq, ck, cv = (COL_NA // NA_W, COL_NA // NA_W + 1, COL_NA // NA_W + 2)
    return pl.pallas_call(
        functools.partial(_na_kernel, grid_rows=grid_rows),
        grid=(b, t // tq),
        in_specs=[pl.BlockSpec((1, tq, NA_W), lambda bi, gi: (bi, gi, cq)),
                  pl.BlockSpec((1, t, NA_W), lambda bi, gi: (bi, 0, ck)),
                  pl.BlockSpec((1, t, NA_W), lambda bi, gi: (bi, 0, cv)),
                  pl.BlockSpec(tabs.shape, lambda bi, gi: (0, 0, 0, 0))],
        out_specs=pl.BlockSpec((1, tq, NA_W), lambda bi, gi: (bi, gi, 0)),
        out_shape=jax.ShapeDtypeStruct((b, t, NA_W), BF16),
        compiler_params=_cparams(("arbitrary", "arbitrary")),
        name="natten",
    )(proj, proj, proj, tabs)


def _mix_kernel(x_ref, mod_ref, of_ref, ob_ref, g_ref, ga_ref, gb_ref, nb_ref,
                gn_ref, gpm_ref, gpf_ref, wa_ref, wb_ref, wo_ref, rw_ref, rb_ref,
                x1_ref, xn_ref, ri_ref, rg_ref, cnt_ref, tri_ref, carry_ref):
    tm = MIX_TM
    first = (pl.program_id(0) == 0) & (pl.program_id(1) == 0)

    @pl.when(first)
    def _():
        r = lax.broadcasted_iota(jnp.int32, (tm, tm), 0)
        c = lax.broadcasted_iota(jnp.int32, (tm, tm), 1)
        tri_ref[...] = jnp.where(r > c, 1.0, 0.0).astype(BF16)
        carry_ref[...] = jnp.zeros_like(carry_ref)

    sc2 = mod_ref[0, 4:5, :]
    sh2 = mod_ref[0, 3:4, :]
    gt1 = mod_ref[0, 2:3, :]

    o = of_ref[0].astype(F32) + ob_ref[0].astype(F32)
    parts = []
    for h in range(HA_HEADS):
        oh = o[:, h * HA_DK:(h + 1) * HA_DK]
        parts.append(oh * lax.rsqrt(jnp.mean(oh * oh, axis=-1, keepdims=True) + RMS_EPS))
    g = g_ref[0].astype(F32)
    oa = jnp.concatenate(parts, axis=-1) * gn_ref[...] * (g * _sigmoid(g))
    ya = _dot(oa.astype(BF16), wa_ref[...])
    yb = _dot(nb_ref[0], wb_ref[...])
    mix = _sigmoid(ga_ref[0].astype(F32)) * ya + _sigmoid(gb_ref[0].astype(F32)) * yb
    mo = _dot(mix.astype(BF16), wo_ref[...])
    x1 = x_ref[0] + gt1 * _rms(mo, gpm_ref[...])
    x1_ref[0] = x1

    xn = _rms(x1, gpf_ref[...]) * (1.0 + sc2) + sh2
    xn_ref[0] = xn.reshape(tm, ROW_SUB, ROW_LANE)
    logits = _dot(xn.astype(BF16), rw_ref[...]) + rb_ref[...]
    lane = lax.broadcasted_iota(jnp.int32, (tm, N_EXPERTS), 1)
    work = logits
    vals, idxs, hots = [], [], []
    for _ in range(TOP_K):
        mv = jnp.max(work, axis=-1, keepdims=True)
        mi = jnp.min(jnp.where(work == mv, lane, N_EXPERTS), axis=-1, keepdims=True)
        hot = lane == mi
        vals.append(mv)
        idxs.append(mi)
        hots.append(hot)
        work = jnp.where(hot, -jnp.inf, work)
    es = [jnp.exp(v - vals[0]) for v in vals]
    den = es[0] + es[1] + es[2] + es[3]

    cnt = sum(jnp.where(h, 1.0, 0.0) for h in hots)
    before = _dot(tri_ref[...], cnt.astype(BF16)) + carry_ref[...]
    carry_ref[...] = carry_ref[...] + jnp.sum(cnt, axis=0, keepdims=True)
    cnt_ref[...] = jnp.broadcast_to(carry_ref[...], cnt_ref.shape)

    lane_o = lax.broadcasted_iota(jnp.int32, (tm, 128), 1)
    ri = jnp.zeros((tm, 128), jnp.int32)
    rg = jnp.zeros((tm, 128), F32)
    for kk in range(TOP_K):
        rank = jnp.sum(jnp.where(hots[kk], before, 0.0), axis=-1, keepdims=True).astype(jnp.int32)
        ri = jnp.where(lane_o == kk, idxs[kk], ri)
        ri = jnp.where(lane_o == TOP_K + kk, rank, ri)
        rg = jnp.where(lane_o == kk, es[kk] / den, rg)
    ri_ref[0] = ri
    rg_ref[0] = rg


def _mix(x, mod, proj, o_fw, o_bw, o_b, hgrn_norm, g_post_mix, g_pre_ffn, wa, wb, wo, rw, rb):
    b, t, d = x.shape
    tm = min(MIX_TM, t)
    assert tm == MIX_TM
    row = lambda a: a.reshape(1, -1)
    tok = lambda w, col: pl.BlockSpec((1, tm, w), lambda bi, i: (bi, i, col))
    full = lambda a: pl.BlockSpec(a.shape, lambda bi, i: (0,) * a.ndim)
    gn, gpm, gpf, rbr = row(hgrn_norm), row(g_post_mix), row(g_pre_ffn), row(rb)
    outs = pl.pallas_call(
        _mix_kernel,
        grid=(b, t // tm),
        in_specs=[tok(d, 0),
                  pl.BlockSpec((1, 6, d), lambda bi, i: (bi, 0, 0)),
                  tok(d, 0), tok(d, 0),
                  tok(d, 4),
                  tok(d, COL_GATES // d), tok(d, COL_GATES // d + 1),
                  tok(NA_W, 0),
                  full(gn), full(gpm), full(gpf), full(wa), full(wb), full(wo), full(rw), full(rbr)],
        out_specs=[tok(d, 0), pl.BlockSpec((1, tm, ROW_SUB, ROW_LANE), lambda bi, i: (bi, i, 0, 0)),
                   tok(128, 0), tok(128, 0),
                   pl.BlockSpec((8, N_EXPERTS), lambda bi, i: (0, 0))],
        out_shape=[jax.ShapeDtypeStruct((b, t, d), F32), jax.ShapeDtypeStruct((b, t, ROW_SUB, ROW_LANE), F32),
                   jax.ShapeDtypeStruct((b, t, 128), jnp.int32), jax.ShapeDtypeStruct((b, t, 128), F32),
                   jax.ShapeDtypeStruct((8, N_EXPERTS), F32)],
        scratch_shapes=[pltpu.VMEM((tm, tm), BF16), pltpu.VMEM((1, N_EXPERTS), F32)],
        compiler_params=_cparams(("arbitrary", "arbitrary")),
        name="mix",
    )(x, mod, o_fw, o_bw, proj, proj, proj, o_b, gn, gpm, gpf, wa, wb, wo, rw, rbr)
    return outs


SC_CORES, SC_SUBCORES = 2, 16
SC_ROWS = 32


def _scatter_rows(xn, dest, n_rows):
    n_tok = xn.shape[0]
    workers = SC_CORES * SC_SUBCORES
    per_w = n_tok // workers
    n_it = per_w // SC_ROWS
    assert per_w * workers == n_tok and n_it * SC_ROWS == per_w and n_it % 2 == 0
    idx = dest.reshape(n_tok // SC_ROWS, SC_ROWS, TOP_K).transpose(0, 2, 1)
    mesh = plsc.VectorSubcoreMesh(core_axis_name="c", subcore_axis_name="s")

    @functools.partial(
        pl.kernel, mesh=mesh,
        out_type=jax.ShapeDtypeStruct((n_rows, ROW_SUB, ROW_LANE), F32),
        scratch_types=[pltpu.VMEM((2, TOP_K, SC_ROWS), jnp.int32),
                       pltpu.VMEM((2, SC_ROWS, ROW_SUB, ROW_LANE), F32),
                       pltpu.SemaphoreType.DMA((2,)), pltpu.SemaphoreType.DMA((2,))],
    )
    def scatter_kernel(x_hbm, idx_hbm, out_hbm, idx_v, rows_v, sem_load, sem_store):
        chunk0 = (lax.axis_index("s") * SC_CORES + lax.axis_index("c")) * n_it

        def load(g, slot):
            tok = pl.ds(pl.multiple_of((chunk0 + g) * SC_ROWS, SC_ROWS), SC_ROWS)
            return pltpu.make_async_copy(x_hbm.at[tok], rows_v.at[slot], sem_load.at[slot])

        def start_load(g, slot):
            pltpu.sync_copy(idx_hbm.at[chunk0 + g], idx_v.at[slot])
            load(g, slot).start()

        def store(slot, kk):
            return pltpu.make_async_copy(rows_v.at[slot], out_hbm.at[idx_v.at[slot, kk]], sem_store.at[slot])

        def scatter(g, slot):
            load(g, slot).wait()
            for kk in range(TOP_K):
                store(slot, kk).start()
            for kk in range(TOP_K):
                store(slot, kk).wait()

        start_load(0, 0)

        @pl.loop(0, n_it, step=2)
        def _(g):
            start_load(g + 1, 1)
            scatter(g, 0)

            @pl.when(g + 2 < n_it)
            def _():
                start_load(g + 2, 0)

            scatter(g + 1, 1)

    return scatter_kernel(xn, idx)


GU_SLAB = 256


def _gate_up_layout(w_gate_up, b_gate_up):
    e, k, n2 = w_gate_up.shape
    half = GU_SLAB // 2
    pm = np.zeros((GU_SLAB, GU_SLAB), np.float32)
    pm[2 * np.arange(half), np.arange(half)] = 1.0
    pm[2 * np.arange(half) + 1, half + np.arange(half)] = 1.0
    w4 = w_gate_up.astype(BF16).reshape(e, k, n2 // GU_SLAB, GU_SLAB)
    w = jnp.einsum('eksc,cd->eksd', w4, jnp.asarray(pm, BF16), preferred_element_type=BF16).reshape(e, k, n2)
    perm = np.concatenate([s * GU_SLAB + np.concatenate([2 * np.arange(half), 2 * np.arange(half) + 1])
                           for s in range(n2 // GU_SLAB)])
    return w, b_gate_up[:, None, perm]


def _expert_kernel(be_ref, bv_ref, x_ref, wgu_ref, bgu_ref, wd_ref, bd_ref, o_ref, a_ref):
    i = pl.program_id(0)
    valid = bv_ref[i]
    half = GU_SLAB // 2

    @pl.when(valid > 0)
    def _():
        rowi = lax.broadcasted_iota(jnp.int32, (MOE_BLK, 1), 0)
        x = x_ref[...].reshape(MOE_BLK, D_MODEL)
        x = jnp.where(rowi < valid, x, 0.0).astype(BF16)
        for s in range(2 * D_FF // GU_SLAB):
            cols = slice(s * GU_SLAB, (s + 1) * GU_SLAB)
            h = _dot(x, wgu_ref[0, :, cols]) + bgu_ref[0, :, cols]
            glu = jnp.minimum(h[:, :half], SWIGLU_LIMIT)
            lin = jnp.clip(h[:, half:], -SWIGLU_LIMIT, SWIGLU_LIMIT)
            a = glu * _sigmoid(SWIGLU_ALPHA * glu) * (lin + 1.0)
            a_ref[:, s * half:(s + 1) * half] = a.astype(BF16)
        o_ref[...] = (_dot(a_ref[...], wd_ref[0]) + bd_ref[0]).reshape(o_ref.shape)

    @pl.when(valid <= 0)
    def _():
        o_ref[...] = jnp.zeros_like(o_ref)


def _experts(xs, block_e, block_valid, wgu, bgu, wd, bd):
    n_rows = xs.shape[0]
    d = D_MODEL
    n_blocks = n_rows // MOE_BLK
    emap = lambda i, be, bv: (be[i], 0, 0)
    rows = lambda: pl.BlockSpec((MOE_BLK, ROW_SUB, ROW_LANE), lambda i, be, bv: (i, 0, 0))
    return pl.pallas_call(
        _expert_kernel,
        grid_spec=pltpu.PrefetchScalarGridSpec(
            num_scalar_prefetch=2,
            grid=(n_blocks,),
            in_specs=[rows(),
                      pl.BlockSpec((1, d, 2 * D_FF), emap), pl.BlockSpec((1, 1, 2 * D_FF), emap),
                      pl.BlockSpec((1, D_FF, d), emap), pl.BlockSpec((1, 1, d), emap)],
            out_specs=rows(),
            scratch_shapes=[pltpu.VMEM((MOE_BLK, D_FF), BF16)],
        ),
        out_shape=jax.ShapeDtypeStruct((n_rows, ROW_SUB, ROW_LANE), F32),
        compiler_params=_cparams(("arbitrary",)),
        name="experts",
    )(block_e, block_valid, xs, wgu, bgu, wd, bd)


def _gather_rows(y, dest_flat):
    n_out = dest_flat.shape[0]
    workers = SC_CORES * SC_SUBCORES
    per_w = n_out // workers
    n_it = per_w // SC_ROWS
    assert per_w * workers == n_out and n_it * SC_ROWS == per_w and n_it % 2 == 0
    mesh = plsc.VectorSubcoreMesh(core_axis_name="c", subcore_axis_name="s")

    @functools.partial(
        pl.kernel, mesh=mesh,
        out_type=jax.ShapeDtypeStruct((n_out, ROW_SUB, ROW_LANE), F32),
        scratch_types=[pltpu.VMEM((2, SC_ROWS), jnp.int32),
                       pltpu.VMEM((2, SC_ROWS, ROW_SUB, ROW_LANE), F32),
                       pltpu.SemaphoreType.DMA((2,))],
    )
    def gather_kernel(y_hbm, idx_hbm, out_hbm, idx_v, rows_v, sem):
        base = (lax.axis_index("s") * SC_CORES + lax.axis_index("c")) * per_w

        def span(g):
            return pl.ds(pl.multiple_of(base + g * SC_ROWS, SC_ROWS), SC_ROWS)

        def row_gather(slot):
            return pltpu.make_async_copy(y_hbm.at[idx_v.at[slot]], rows_v.at[slot], sem.at[slot])

        def start(g, slot):
            pltpu.sync_copy(idx_hbm.at[span(g)], idx_v.at[slot])
            row_gather(slot).start()

        start(0, 0)

        @pl.loop(0, n_it, step=2)
        def _(g):
            start(g + 1, 1)
            row_gather(0).wait()
            pltpu.sync_copy(rows_v.at[0], out_hbm.at[span(g)])

            @pl.when(g + 2 < n_it)
            def _():
                start(g + 2, 0)

            row_gather(1).wait()
            pltpu.sync_copy(rows_v.at[1], out_hbm.at[span(g + 1)])

    return gather_kernel(y, dest_flat)


def _combine_kernel(yg_ref, x1_ref, mod_ref, rg_ref, gpo_ref, o_ref):
    rg = rg_ref[...]
    h = rg[:, 0:1] * yg_ref[:, 0].reshape(COMB_TOK, D_MODEL)
    for kk in range(1, TOP_K):
        h = h + rg[:, kk:kk + 1] * yg_ref[:, kk].reshape(COMB_TOK, D_MODEL)
    gt2 = mod_ref[0, 5:6, :]
    o_ref[...] = x1_ref[...] + gt2 * _rms(h, gpo_ref[...])


def _combine(yg, x1, mod, rg, g_post_ffn, tokens_per_batch):
    n, d = x1.shape
    per_b = tokens_per_batch // COMB_TOK
    gpo = g_post_ffn.reshape(1, d)
    return pl.pallas_call(
        _combine_kernel,
        grid=(n // COMB_TOK,),
        in_specs=[pl.BlockSpec((COMB_TOK, TOP_K, ROW_SUB, ROW_LANE), lambda i: (i, 0, 0, 0)),
                  pl.BlockSpec((COMB_TOK, d), lambda i: (i, 0)),
                  pl.BlockSpec((1, 6, d), lambda i: (i // per_b, 0, 0)),
                  pl.BlockSpec((COMB_TOK, 128), lambda i: (i, 0)),
                  pl.BlockSpec((1, d), lambda i: (0, 0))],
        out_specs=pl.BlockSpec((COMB_TOK, d), lambda i: (i, 0)),
        out_shape=jax.ShapeDtypeStruct((n, d), F32),
        compiler_params=_cparams(("arbitrary",)),
        name="combine",
    )(yg, x1, mod, rg, gpo)


def _moe_plan(ri, counts, n_tok):
    idx = ri[:, :TOP_K]
    rank = ri[:, TOP_K:2 * TOP_K]
    cnt = counts.astype(jnp.int32)
    padded = (cnt + MOE_BLK - 1) // MOE_BLK * MOE_BLK
    pend = jnp.cumsum(padded)
    pstart = pend - padded
    dest = pstart[idx] + rank
    n_blocks = (n_tok * TOP_K + MOE_BLK - 1) // MOE_BLK + N_EXPERTS
    bstart = jnp.arange(n_blocks, dtype=jnp.int32) * MOE_BLK
    block_e = jnp.minimum(jnp.sum((bstart[:, None] >= pend[None, :]).astype(jnp.int32), axis=1), N_EXPERTS - 1)
    block_valid = jnp.clip(pstart[block_e] + cnt[block_e] - bstart, 0, MOE_BLK).astype(jnp.int32)
    return dest.astype(jnp.int32), block_e, block_valid, n_blocks


def _layer(x, mod, p):
    b, t, d = x.shape
    n = b * t
    proj = _in_proj(x, mod, p['g_pre_mix'], p['w_in'])
    o_fw, o_bw = _hgrn(proj, p['lb_logits'])
    o_b = _natten(proj, p['na_tabs'])
    x1, xn, ri, rg, counts = _mix(x, mod, proj, o_fw, o_bw, o_b, p['hgrn_norm'], p['g_post_mix'], p['g_pre_ffn'],
                                  p['w_branch_a'], p['w_branch_b'], p['w_out'], p['router_w'], p['router_b'])
    ri = ri.reshape(n, 128)
    rg = rg.reshape(n, 128)
    dest, block_e, block_valid, n_blocks = _moe_plan(ri, counts[0], n)
    xs = _scatter_rows(xn.reshape(n, ROW_SUB, ROW_LANE), dest, n_blocks * MOE_BLK)
    y = _experts(xs, block_e, block_valid, p['w_gu'], p['b_gu'], p['w_d'], p['b_d'])
    yg = _gather_rows(y, dest.reshape(-1)).reshape(n, TOP_K, ROW_SUB, ROW_LANE)
    out = _combine(yg, x1.reshape(n, d), mod, rg, p['g_post_ffn'], t)
    return out.reshape(b, t, d)


def kernel(x_prompt, x_sample, c_prompt, c_sample, ada_w, ada_b, g_pre_mix, g_post_mix, g_pre_ffn, g_post_ffn,
           w_in, hgrn_lb_logits, hgrn_norm, na_rpb, w_branch_a, w_branch_b, w_out, router_w, router_b,
           w_gate_up, b_gate_up, w_down, b_down):
    d = D_MODEL
    bp, bs = c_prompt.shape[0], c_sample.shape[0]
    pad = (-(bp + bs)) % 8
    c_all = jnp.concatenate([c_prompt, c_sample, jnp.zeros((pad, d), F32)], axis=0)
    mod = _ada_mod(c_all, ada_w[0], ada_b[0]).reshape(-1, 6, d)

    w = w_in[0]
    w_perm = jnp.concatenate([w[:, :5 * d], w[:, 5 * d + 3 * NA_W:], w[:, 5 * d:5 * d + 3 * NA_W]], axis=1).astype(BF16)
    w_gu, b_gu = _gate_up_layout(w_gate_up[0], b_gate_up[0])
    p = dict(
        g_pre_mix=g_pre_mix[0], g_post_mix=g_post_mix[0], g_pre_ffn=g_pre_ffn[0], g_post_ffn=g_post_ffn[0],
        w_in=w_perm, lb_logits=hgrn_lb_logits.astype(F32), hgrn_norm=hgrn_norm[0],
        na_tabs=_na_bias_tables(na_rpb[0]),
        w_branch_a=w_branch_a[0].astype(BF16), w_branch_b=w_branch_b[0].astype(BF16), w_out=w_out[0].astype(BF16),
        router_w=router_w[0].astype(BF16), router_b=router_b[0],
        w_gu=w_gu, b_gu=b_gu,
        w_d=w_down[0].astype(BF16), b_d=b_down[0][:, None, :],
    )
    y_p = _layer(x_prompt, mod[:bp], p)
    y_s = _layer(x_sample, mod[bp:bp + bs], p)
    return (y_p, y_s)
```

```python
import functools

import numpy as np
import jax
import jax.numpy as jnp
from jax import lax
from jax.experimental import pallas as pl
from jax.experimental.pallas import tpu as pltpu
from jax.experimental.pallas import tpu_sc as plsc

F32 = jnp.float32
BF16 = jnp.bfloat16

D_MODEL = 1024
GRID_W = 64
HA_HEADS = 8
HA_DK = 128
NA_HEADS = 8
NA_DH = 64
NA_W = NA_HEADS * NA_DH
NA_KH = 8
NA_KW = 16
N_EXPERTS = 32
TOP_K = 4
D_FF = 1024
SWIGLU_ALPHA = 1.702
SWIGLU_LIMIT = 7.0
RMS_EPS = 1e-6
NEG_BIG = -1e30

COL_A = 0
COL_GATES = 5 * 1024
COL_NA = 7 * 1024
PROJ_COLS = 7 * 1024 + 3 * NA_W

HGRN_CHUNK = 64
HGRN_TILE = 256
NA_ROWS_PER_STEP = 8
PROJ_TM = 1024
PROJ_TN = 2176
MIX_TM = 512
MOE_BLK = 512
COMB_TOK = 512
ROW_SUB, ROW_LANE = 8, 128
VMEM_LIMIT = 56 * 1024 * 1024


def _cparams(sem):
    return pltpu.CompilerParams(dimension_semantics=sem, vmem_limit_bytes=VMEM_LIMIT)


def _dot(a, b):
    return jnp.dot(a, b, preferred_element_type=F32)


def _dot_nt(a, b):
    return lax.dot_general(a, b, (((1,), (1,)), ((), ())), preferred_element_type=F32)


def _dot_tn(a, b):
    return lax.dot_general(a, b, (((0,), (0,)), ((), ())), preferred_element_type=F32)


def _split(a):
    hi = a.astype(BF16)
    lo = (a - hi.astype(F32)).astype(BF16)
    return hi, lo


def _dot3(a, w):
    ah, al = _split(a)
    wh, wl = _split(w)
    return _dot(ah, wh) + (_dot(al, wh) + _dot(ah, wl))


def _sigmoid(x):
    return 1.0 / (1.0 + jnp.exp(-x))


def _rms(x, g):
    return x * lax.rsqrt(jnp.mean(x * x, axis=-1, keepdims=True) + RMS_EPS) * g


def _ada_kernel(c_ref, w_ref, b_ref, o_ref):
    c = c_ref[...]
    o_ref[...] = _dot3(c * _sigmoid(c), w_ref[...]) + b_ref[...]


def _ada_mod(c, ada_w, ada_b):
    bp, d = c.shape
    n = ada_w.shape[1]
    tn = 1536
    return pl.pallas_call(
        _ada_kernel,
        grid=(n // tn,),
        in_specs=[pl.BlockSpec((bp, d), lambda j: (0, 0)),
                  pl.BlockSpec((d, tn), lambda j: (0, j)),
                  pl.BlockSpec((1, tn), lambda j: (0, j))],
        out_specs=pl.BlockSpec((bp, tn), lambda j: (0, j)),
        out_shape=jax.ShapeDtypeStruct((bp, n), F32),
        compiler_params=_cparams(("arbitrary",)),
        name="ada_mod",
    )(c, ada_w, ada_b.reshape(1, n))


def _in_proj_kernel(x_ref, mod_ref, g_ref, w_ref, o_ref, xn_ref):
    @pl.when(pl.program_id(2) == 0)
    def _():
        sh = mod_ref[0, 0:1, :]
        sc = mod_ref[0, 1:2, :]
        xn = _rms(x_ref[0], g_ref[...]) * (1.0 + sc) + sh
        xn_ref[...] = xn.astype(BF16)

    o_ref[0] = _dot(xn_ref[...], w_ref[...]).astype(BF16)


def _in_proj(x, mod, g_pre, w_in_bf):
    b, t, d = x.shape
    tm = min(PROJ_TM, t)
    n = w_in_bf.shape[1]
    return pl.pallas_call(
        _in_proj_kernel,
        grid=(b, t // tm, n // PROJ_TN),
        in_specs=[pl.BlockSpec((1, tm, d), lambda bi, i, j: (bi, i, 0)),
                  pl.BlockSpec((1, 6, d), lambda bi, i, j: (bi, 0, 0)),
                  pl.BlockSpec((1, d), lambda bi, i, j: (0, 0)),
                  pl.BlockSpec((d, PROJ_TN), lambda bi, i, j: (0, j))],
        out_specs=pl.BlockSpec((1, tm, PROJ_TN), lambda bi, i, j: (bi, i, j)),
        out_shape=jax.ShapeDtypeStruct((b, t, n), BF16),
        scratch_shapes=[pltpu.VMEM((tm, d), BF16)],
        compiler_params=_cparams(("arbitrary", "arbitrary", "arbitrary")),
        name="in_proj",
    )(x, mod, g_pre.reshape(1, d), w_in_bf)


def _cumsum_rows(x, reverse):
    n = x.shape[0]
    row = lax.broadcasted_iota(jnp.int32, x.shape, 0)
    step = 1
    while step < n:
        if reverse:
            x = x + jnp.where(row < n - step, pltpu.roll(x, n - step, axis=0), 0.0)
        else:
            x = x + jnp.where(row >= step, pltpu.roll(x, step, axis=0), 0.0)
        step *= 2
    return x


def _hgrn_dir(q_ref, f_ref, i_ref, o_ref, s_ref, sn_ref, lbd, row0, reverse):
    c = HGRN_CHUNK
    rows = pl.ds(row0, c)
    q = q_ref[0, rows, :].astype(F32)
    fpre = f_ref[0, rows, :].astype(F32)
    v = i_ref[0, rows, :]
    f = lbd + (1.0 - lbd) * _sigmoid(fpre)
    logf = jnp.log(f)
    k = 1.0 - f
    ri = lax.broadcasted_iota(jnp.int32, (c, c), 0)
    ci = lax.broadcasted_iota(jnp.int32, (c, c), 1)
    keep = (ci >= ri) if reverse else (ri >= ci)
    b = _cumsum_rows(logf, reverse)
    if reverse:
        b_mid = b[c // 2:c // 2 + 1, :]
        b_end = b[0:1, :]
    else:
        b_mid = b[c // 2 - 1:c // 2, :]
        b_end = b[c - 1:c, :]
    qm = (q * jnp.exp(b - b_mid)).astype(BF16)
    km = (k * jnp.exp(b_mid - b)).astype(BF16)
    qi = (q * jnp.exp(b)).astype(BF16)
    ks = (k * jnp.exp(b_end - b)).astype(BF16)
    dec = jnp.exp(b_end)
    for h in range(HA_HEADS):
        sl = slice(h * HA_DK, (h + 1) * HA_DK)
        a = _dot_nt(qm[:, sl], km[:, sl])
        a = jnp.where(keep, a, 0.0).astype(BF16)
        lhs = jnp.concatenate([qi[:, sl], a], axis=1)
        rhs = jnp.concatenate([sn_ref[h], v[:, sl]], axis=0)
        o_ref[0, rows, sl] = _dot(lhs, rhs).astype(BF16)
        st = s_ref[h] * dec[:, sl] + _dot_tn(v[:, sl], ks[:, sl])
        s_ref[h] = st
        sn_ref[h] = st.astype(BF16).T


def _hgrn_kernel(lbl_ref, qf_ref, ff_ref, if_ref, qb_ref, fb_ref, ib_ref, of_ref, ob_ref,
                 sf_ref, sb_ref, snf_ref, snb_ref):
    @pl.when(pl.program_id(1) == 0)
    def _():
        for ref in (sf_ref, sb_ref, snf_ref, snb_ref):
            ref[...] = jnp.zeros_like(ref)

    l0 = lbl_ref[0]
    l1 = lbl_ref[1]
    m = jnp.maximum(l0, l1)
    e0 = jnp.exp(l0 - m)
    e1 = jnp.exp(l1 - m)
    lb = e0 / (e0 + e1)
    n_chunks = HGRN_TILE // HGRN_CHUNK

    for ci in range(n_chunks):
        _hgrn_dir(qf_ref, ff_ref, if_ref, of_ref, sf_ref, snf_ref, lb[0:1, :], ci * HGRN_CHUNK, False)
        _hgrn_dir(qb_ref, fb_ref, ib_ref, ob_ref, sb_ref, snb_ref, lb[1:2, :],
                  (n_chunks - 1 - ci) * HGRN_CHUNK, True)


def _hgrn(proj, lb_logits):
    b, t, _ = proj.shape
    tt = HGRN_TILE
    nt = t // tt
    d = D_MODEL

    def fwd(col):
        return pl.BlockSpec((1, tt, d), lambda bi, ti: (bi, ti, col))

    def bwd(col):
        return pl.BlockSpec((1, tt, d), lambda bi, ti: (bi, nt - 1 - ti, col))

    return pl.pallas_call(
        _hgrn_kernel,
        grid=(b, nt),
        in_specs=[pl.BlockSpec((2, 2, d), lambda bi, ti: (0, 0, 0)),
                  fwd(0), fwd(1), fwd(3), bwd(0), bwd(2), bwd(3)],
        out_specs=[pl.BlockSpec((1, tt, d), lambda bi, ti: (bi, ti, 0)),
                   pl.BlockSpec((1, tt, d), lambda bi, ti: (bi, nt - 1 - ti, 0))],
        out_shape=[jax.ShapeDtypeStruct((b, t, d), BF16), jax.ShapeDtypeStruct((b, t, d), BF16)],
        scratch_shapes=[pltpu.VMEM((HA_HEADS, HA_DK, HA_DK), F32), pltpu.VMEM((HA_HEADS, HA_DK, HA_DK), F32),
                        pltpu.VMEM((HA_HEADS, HA_DK, HA_DK), BF16), pltpu.VMEM((HA_HEADS, HA_DK, HA_DK), BF16)],
        compiler_params=_cparams(("arbitrary", "arbitrary")),
        name="hgrn",
    )(lb_logits, proj, proj, proj, proj, proj, proj)


def _na_bias_tables(rpb):
    q = np.arange(GRID_W)[:, None]
    kc = np.arange(GRID_W)[None, :]
    c0 = np.clip(q - NA_KW // 2, 0, GRID_W - NA_KW)
    ok = (kc >= c0) & (kc < c0 + NA_KW)
    dc = np.clip(kc - q + (NA_KW - 1), 0, 2 * NA_KW - 2)
    base = jnp.where(ok[None, None], jnp.take(rpb.astype(F32), dc, axis=2), NEG_BIG)
    tabs = []
    for dr0 in range(NA_KH):
        w = base[:, dr0:dr0 + NA_KH]
        w = w.transpose(0, 2, 1, 3).reshape(NA_HEADS // 2, 2 * GRID_W, NA_KH * GRID_W)
        tabs.append(w)
    return jnp.stack(tabs, axis=1)


def _na_kernel(q_ref, k_ref, v_ref, tab_ref, o_ref, *, grid_rows):
    g = pl.program_id(1)
    lane = lax.broadcasted_iota(jnp.int32, (GRID_W, 2 * NA_DH), 1)
    low = lane < NA_DH
    scale = NA_DH ** -0.5

    def body(rr, carry):
        r = g * NA_ROWS_PER_STEP + rr
        r0 = jnp.clip(r - NA_KH // 2, 0, grid_rows - NA_KH)
        dr0 = r0 - r + (NA_KH - 1)
        qrow = pl.ds(pl.multiple_of(rr * GRID_W, GRID_W), GRID_W)
        krow = pl.ds(pl.multiple_of(r0 * GRID_W, GRID_W), NA_KH * GRID_W)
        for j in range(NA_HEADS // 2):
            sl = slice(j * 2 * NA_DH, (j + 1) * 2 * NA_DH)
            qp = q_ref[0, qrow, sl]
            zero = jnp.zeros_like(qp)
            ql = jnp.concatenate([jnp.where(low, qp, zero), jnp.where(low, zero, qp)], axis=0)
            s = _dot_nt(ql, k_ref[0, krow, sl]) * scale + tab_ref[j, dr0]
            m = jnp.max(s, axis=-1, keepdims=True)
            p = jnp.exp(s - m)
            l = jnp.sum(p, axis=-1, keepdims=True)
            pv = _dot(p.astype(BF16), v_ref[0, krow, sl]) / l
            o_ref[0, qrow, sl] = jnp.where(low, pv[:GRID_W], pv[GRID_W:]).astype(BF16)
        return carry

    lax.fori_loop(0, NA_ROWS_PER_STEP, body, 0, unroll=2)


def _natten(proj, tabs):
    b, t, _ = proj.shape
    grid_rows = t // GRID_W
    tq = NA_ROWS_PER_STEP * GRID_W
    cq, ck, cv = (COL_NA // NA_W, COL_NA // NA_W + 1, COL_NA // NA_W + 2)
    return pl.pallas_call(
        functools.partial(_na_kernel, grid_rows=grid_rows),
        grid=(b, t // tq),
        in_specs=[pl.BlockSpec((1, tq, NA_W), lambda bi, gi: (bi, gi, cq)),
                  pl.BlockSpec((1, t, NA_W), lambda bi, gi: (bi, 0, ck)),
                  pl.BlockSpec((1, t, NA_W), lambda bi, gi: (bi, 0, cv)),
                  pl.BlockSpec(tabs.shape, lambda bi, gi: (0, 0, 0, 0))],
        out_specs=pl.BlockSpec((1, tq, NA_W), lambda bi, gi: (bi, gi, 0)),
        out_shape=jax.ShapeDtypeStruct((b, t, NA_W), BF16),
        compiler_params=_cparams(("arbitrary", "arbitrary")),
        name="natten",
    )(proj, proj, proj, tabs)


def _mix_kernel(x_ref, mod_ref, of_ref, ob_ref, g_ref, ga_ref, gb_ref, nb_ref,
                gn_ref, gpm_ref, gpf_ref, wa_ref, wb_ref, wo_ref, rw_ref, rb_ref,
                x1_ref, xn_ref, ri_ref, rg_ref, cnt_ref, tri_ref, carry_ref):
    tm = MIX_TM
    first = (pl.program_id(0) == 0) & (pl.program_id(1) == 0)

    @pl.when(first)
    def _():
        r = lax.broadcasted_iota(jnp.int32, (tm, tm), 0)
        c = lax.broadcasted_iota(jnp.int32, (tm, tm), 1)
        tri_ref[...] = jnp.where(r > c, 1.0, 0.0).astype(BF16)
        carry_ref[...] = jnp.zeros_like(carry_ref)

    sc2 = mod_ref[0, 4:5, :]
    sh2 = mod_ref[0, 3:4, :]
    gt1 = mod_ref[0, 2:3, :]

    o = of_ref[0].astype(F32) + ob_ref[0].astype(F32)
    parts = []
    for h in range(HA_HEADS):
        oh = o[:, h * HA_DK:(h + 1) * HA_DK]
        parts.append(oh * lax.rsqrt(jnp.mean(oh * oh, axis=-1, keepdims=True) + RMS_EPS))
    g = g_ref[0].astype(F32)
    oa = jnp.concatenate(parts, axis=-1) * gn_ref[...] * (g * _sigmoid(g))
    ya = _dot(oa.astype(BF16), wa_ref[...])
    yb = _dot(nb_ref[0], wb_ref[...])
    mix = _sigmoid(ga_ref[0].astype(F32)) * ya + _sigmoid(gb_ref[0].astype(F32)) * yb
    mo = _dot(mix.astype(BF16), wo_ref[...])
    x1 = x_ref[0] + gt1 * _rms(mo, gpm_ref[...])
    x1_ref[0] = x1

    xn = _rms(x1, gpf_ref[...]) * (1.0 + sc2) + sh2
    xn_ref[0] = xn.reshape(tm, ROW_SUB, ROW_LANE)
    logits = _dot(xn.astype(BF16), rw_ref[...]) + rb_ref[...]
    lane = lax.broadcasted_iota(jnp.int32, (tm, N_EXPERTS), 1)
    work = logits
    vals, idxs, hots = [], [], []
    for _ in range(TOP_K):
        mv = jnp.max(work, axis=-1, keepdims=True)
        mi = jnp.min(jnp.where(work == mv, lane, N_EXPERTS), axis=-1, keepdims=True)
        hot = lane == mi
        vals.append(mv)
        idxs.append(mi)
        hots.append(hot)
        work = jnp.where(hot, -jnp.inf, work)
    es = [jnp.exp(v - vals[0]) for v in vals]
    den = es[0] + es[1] + es[2] + es[3]

    cnt = sum(jnp.where(h, 1.0, 0.0) for h in hots)
    before = _dot(tri_ref[...], cnt.astype(BF16)) + carry_ref[...]
    carry_ref[...] = carry_ref[...] + jnp.sum(cnt, axis=0, keepdims=True)
    cnt_ref[...] = jnp.broadcast_to(carry_ref[...], cnt_ref.shape)

    lane_o = lax.broadcasted_iota(jnp.int32, (tm, 128), 1)
    ri = jnp.zeros((tm, 128), jnp.int32)
    rg = jnp.zeros((tm, 128), F32)
    for kk in range(TOP_K):
        rank = jnp.sum(jnp.where(hots[kk], before, 0.0), axis=-1, keepdims=True).astype(jnp.int32)
        ri = jnp.where(lane_o == kk, idxs[kk], ri)
        ri = jnp.where(lane_o == TOP_K + kk, rank, ri)
        rg = jnp.where(lane_o == kk, es[kk] / den, rg)
    ri_ref[...] = ri.T[:8, :]
    rg_ref[0] = rg


def _mix(x, mod, proj, o_fw, o_bw, o_b, hgrn_norm, g_post_mix, g_pre_ffn, wa, wb, wo, rw, rb):
    b, t, d = x.shape
    tm = min(MIX_TM, t)
    assert tm == MIX_TM
    row = lambda a: a.reshape(1, -1)
    tok = lambda w, col: pl.BlockSpec((1, tm, w), lambda bi, i: (bi, i, col))
    full = lambda a: pl.BlockSpec(a.shape, lambda bi, i: (0,) * a.ndim)
    gn, gpm, gpf, rbr = row(hgrn_norm), row(g_post_mix), row(g_pre_ffn), row(rb)
    outs = pl.pallas_call(
        _mix_kernel,
        grid=(b, t // tm),
        in_specs=[tok(d, 0),
                  pl.BlockSpec((1, 6, d), lambda bi, i: (bi, 0, 0)),
                  tok(d, 0), tok(d, 0),
                  tok(d, 4),
                  tok(d, COL_GATES // d), tok(d, COL_GATES // d + 1),
                  tok(NA_W, 0),
                  full(gn), full(gpm), full(gpf), full(wa), full(wb), full(wo), full(rw), full(rbr)],
        out_specs=[tok(d, 0), pl.BlockSpec((1, tm, ROW_SUB, ROW_LANE), lambda bi, i: (bi, i, 0, 0)),
                   pl.BlockSpec((8, tm), lambda bi, i: (0, bi * (t // tm) + i)), tok(128, 0),
                   pl.BlockSpec((8, N_EXPERTS), lambda bi, i: (0, 0))],
        out_shape=[jax.ShapeDtypeStruct((b, t, d), F32), jax.ShapeDtypeStruct((b, t, ROW_SUB, ROW_LANE), F32),
                   jax.ShapeDtypeStruct((8, b * t), jnp.int32), jax.ShapeDtypeStruct((b, t, 128), F32),
                   jax.ShapeDtypeStruct((8, N_EXPERTS), F32)],
        scratch_shapes=[pltpu.VMEM((tm, tm), BF16), pltpu.VMEM((1, N_EXPERTS), F32)],
        compiler_params=_cparams(("arbitrary", "arbitrary")),
        name="mix",
    )(x, mod, o_fw, o_bw, proj, proj, proj, o_b, gn, gpm, gpf, wa, wb, wo, rw, rbr)
    return outs


SC_CORES, SC_SUBCORES = 2, 16
SC_ROWS = 32


def _scatter_rows(xn, dest_t, n_rows):
    n_tok = xn.shape[0]
    workers = SC_CORES * SC_SUBCORES
    per_w = n_tok // workers
    n_it = per_w // SC_ROWS
    assert per_w * workers == n_tok and n_it * SC_ROWS == per_w and n_it % 2 == 0
    idx = dest_t.reshape(TOP_K, n_tok // SC_ROWS, SC_ROWS).transpose(1, 0, 2)
    mesh = plsc.VectorSubcoreMesh(core_axis_name="c", subcore_axis_name="s")

    @functools.partial(
        pl.kernel, mesh=mesh,
        out_type=jax.ShapeDtypeStruct((n_rows, ROW_SUB, ROW_LANE), F32),
        scratch_types=[pltpu.VMEM((2, TOP_K, SC_ROWS), jnp.int32),
                       pltpu.VMEM((2, SC_ROWS, ROW_SUB, ROW_LANE), F32),
                       pltpu.SemaphoreType.DMA((2,)), pltpu.SemaphoreType.DMA((2,))],
    )
    def scatter_kernel(x_hbm, idx_hbm, out_hbm, idx_v, rows_v, sem_load, sem_store):
        chunk0 = (lax.axis_index("s") * SC_CORES + lax.axis_index("c")) * n_it

        def load(g, slot):
            tok = pl.ds(pl.multiple_of((chunk0 + g) * SC_ROWS, SC_ROWS), SC_ROWS)
            return pltpu.make_async_copy(x_hbm.at[tok], rows_v.at[slot], sem_load.at[slot])

        def start_load(g, slot):
            pltpu.sync_copy(idx_hbm.at[chunk0 + g], idx_v.at[slot])
            load(g, slot).start()

        def store(slot, kk):
            return pltpu.make_async_copy(rows_v.at[slot], out_hbm.at[idx_v.at[slot, kk]], sem_store.at[slot])

        def scatter(g, slot):
            load(g, slot).wait()
            for kk in range(TOP_K):
                store(slot, kk).start()
            for kk in range(TOP_K):
                store(slot, kk).wait()

        start_load(0, 0)

        @pl.loop(0, n_it, step=2)
        def _(g):
            start_load(g + 1, 1)
            scatter(g, 0)

            @pl.when(g + 2 < n_it)
            def _():
                start_load(g + 2, 0)

            scatter(g + 1, 1)

    return scatter_kernel(xn, idx)


GU_SLAB = 256


def _gate_up_layout_kernel(w_ref, o_ref):
    half = GU_SLAB // 2
    r = lax.broadcasted_iota(jnp.int32, (GU_SLAB, GU_SLAB), 0)
    c = lax.broadcasted_iota(jnp.int32, (GU_SLAB, GU_SLAB), 1)
    src = jnp.where(c < half, 2 * c, 2 * (c - half) + 1)
    pm = jnp.where(r == src, 1.0, 0.0).astype(BF16)
    for s in range(w_ref.shape[2] // GU_SLAB):
        cols = slice(s * GU_SLAB, (s + 1) * GU_SLAB)
        o_ref[0, :, cols] = _dot(w_ref[0, :, cols].astype(BF16), pm).astype(BF16)


def _gate_up_layout(w_gate_up, b_gate_up):
    e, k, n2 = w_gate_up.shape
    half = GU_SLAB // 2
    tk = 512
    w = pl.pallas_call(
        _gate_up_layout_kernel,
        grid=(e, k // tk),
        in_specs=[pl.BlockSpec((1, tk, n2), lambda i, j: (i, j, 0))],
        out_specs=pl.BlockSpec((1, tk, n2), lambda i, j: (i, j, 0)),
        out_shape=jax.ShapeDtypeStruct((e, k, n2), BF16),
        compiler_params=_cparams(("arbitrary", "arbitrary")),
        name="gate_up_layout",
    )(w_gate_up)
    perm = np.concatenate([s * GU_SLAB + np.concatenate([2 * np.arange(half), 2 * np.arange(half) + 1])
                           for s in range(n2 // GU_SLAB)])
    return w, b_gate_up[:, None, perm]


def _expert_kernel(be_ref, bv_ref, x_ref, wgu_ref, bgu_ref, wd_ref, bd_ref, o_ref, a_ref):
    i = pl.program_id(0)
    valid = bv_ref[i]
    half = GU_SLAB // 2

    @pl.when(valid > 0)
    def _():
        rowi = lax.broadcasted_iota(jnp.int32, (MOE_BLK, 1), 0)
        x = x_ref[...].reshape(MOE_BLK, D_MODEL)
        x = jnp.where(rowi < valid, x, 0.0).astype(BF16)
        for s in range(2 * D_FF // GU_SLAB):
            cols = slice(s * GU_SLAB, (s + 1) * GU_SLAB)
            h = _dot(x, wgu_ref[0, :, cols]) + bgu_ref[0, :, cols]
            glu = jnp.minimum(h[:, :half], SWIGLU_LIMIT)
            lin = jnp.clip(h[:, half:], -SWIGLU_LIMIT, SWIGLU_LIMIT)
            a = glu * _sigmoid(SWIGLU_ALPHA * glu) * (lin + 1.0)
            a_ref[:, s * half:(s + 1) * half] = a.astype(BF16)
        o_ref[...] = (_dot(a_ref[...], wd_ref[0]) + bd_ref[0]).reshape(o_ref.shape)

    @pl.when(valid <= 0)
    def _():
        o_ref[...] = jnp.zeros_like(o_ref)


def _experts(xs, block_e, block_valid, wgu, bgu, wd, bd):
    n_rows = xs.shape[0]
    d = D_MODEL
    n_blocks = n_rows // MOE_BLK
    emap = lambda i, be, bv: (be[i], 0, 0)
    rows = lambda: pl.BlockSpec((MOE_BLK, ROW_SUB, ROW_LANE), lambda i, be, bv: (i, 0, 0))
    return pl.pallas_call(
        _expert_kernel,
        grid_spec=pltpu.PrefetchScalarGridSpec(
            num_scalar_prefetch=2,
            grid=(n_blocks,),
            in_specs=[rows(),
                      pl.BlockSpec((1, d, 2 * D_FF), emap), pl.BlockSpec((1, 1, 2 * D_FF), emap),
                      pl.BlockSpec((1, D_FF, d), emap), pl.BlockSpec((1, 1, d), emap)],
            out_specs=rows(),
            scratch_shapes=[pltpu.VMEM((MOE_BLK, D_FF), BF16)],
        ),
        out_shape=jax.ShapeDtypeStruct((n_rows, ROW_SUB, ROW_LANE), F32),
        compiler_params=_cparams(("arbitrary",)),
        name="experts",
    )(block_e, block_valid, xs, wgu, bgu, wd, bd)


def _gather_rows(y, dest_flat):
    n_out = dest_flat.shape[0]
    workers = SC_CORES * SC_SUBCORES
    per_w = n_out // workers
    n_it = per_w // SC_ROWS
    assert per_w * workers == n_out and n_it * SC_ROWS == per_w and n_it % 2 == 0
    mesh = plsc.VectorSubcoreMesh(core_axis_name="c", subcore_axis_name="s")

    @functools.partial(
        pl.kernel, mesh=mesh,
        out_type=jax.ShapeDtypeStruct((n_out, ROW_SUB, ROW_LANE), F32),
        scratch_types=[pltpu.VMEM((2, SC_ROWS), jnp.int32),
                       pltpu.VMEM((2, SC_ROWS, ROW_SUB, ROW_LANE), F32),
                       pltpu.SemaphoreType.DMA((2,))],
    )
    def gather_kernel(y_hbm, idx_hbm, out_hbm, idx_v, rows_v, sem):
        base = (lax.axis_index("s") * SC_CORES + lax.axis_index("c")) * per_w

        def span(g):
            return pl.ds(pl.multiple_of(base + g * SC_ROWS, SC_ROWS), SC_ROWS)

        def row_gather(slot):
            return pltpu.make_async_copy(y_hbm.at[idx_v.at[slot]], rows_v.at[slot], sem.at[slot])

        def start(g, slot):
            pltpu.sync_copy(idx_hbm.at[span(g)], idx_v.at[slot])
            row_gather(slot).start()

        start(0, 0)

        @pl.loop(0, n_it, step=2)
        def _(g):
            start(g + 1, 1)
            row_gather(0).wait()
            pltpu.sync_copy(rows_v.at[0], out_hbm.at[span(g)])

            @pl.when(g + 2 < n_it)
            def _():
                start(g + 2, 0)

            row_gather(1).wait()
            pltpu.sync_copy(rows_v.at[1], out_hbm.at[span(g + 1)])

    return gather_kernel(y, dest_flat)


def _combine_kernel(yg_ref, x1_ref, mod_ref, rg_ref, gpo_ref, o_ref):
    rg = rg_ref[...]
    h = rg[:, 0:1] * yg_ref[0].reshape(COMB_TOK, D_MODEL)
    for kk in range(1, TOP_K):
        h = h + rg[:, kk:kk + 1] * yg_ref[kk].reshape(COMB_TOK, D_MODEL)
    gt2 = mod_ref[0, 5:6, :]
    o_ref[...] = x1_ref[...] + gt2 * _rms(h, gpo_ref[...])


def _combine(yg, x1, mod, rg, g_post_ffn, tokens_per_batch):
    n, d = x1.shape
    per_b = tokens_per_batch // COMB_TOK
    gpo = g_post_ffn.reshape(1, d)
    return pl.pallas_call(
        _combine_kernel,
        grid=(n // COMB_TOK,),
        in_specs=[pl.BlockSpec((TOP_K, COMB_TOK, ROW_SUB, ROW_LANE), lambda i: (0, i, 0, 0)),
                  pl.BlockSpec((COMB_TOK, d), lambda i: (i, 0)),
                  pl.BlockSpec((1, 6, d), lambda i: (i // per_b, 0, 0)),
                  pl.BlockSpec((COMB_TOK, 128), lambda i: (i, 0)),
                  pl.BlockSpec((1, d), lambda i: (0, 0))],
        out_specs=pl.BlockSpec((COMB_TOK, d), lambda i: (i, 0)),
        out_shape=jax.ShapeDtypeStruct((n, d), F32),
        compiler_params=_cparams(("arbitrary",)),
        name="combine",
    )(yg, x1, mod, rg, gpo)


def _moe_plan(ri_t, counts, n_tok):
    idx = ri_t[:TOP_K]
    rank = ri_t[TOP_K:2 * TOP_K]
    cnt = counts.astype(jnp.int32)
    padded = (cnt + MOE_BLK - 1) // MOE_BLK * MOE_BLK
    pend = jnp.cumsum(padded)
    pstart = pend - padded
    dest = pstart[idx] + rank
    n_blocks = (n_tok * TOP_K + MOE_BLK - 1) // MOE_BLK + N_EXPERTS
    bstart = jnp.arange(n_blocks, dtype=jnp.int32) * MOE_BLK
    block_e = jnp.minimum(jnp.sum((bstart[:, None] >= pend[None, :]).astype(jnp.int32), axis=1), N_EXPERTS - 1)
    block_valid = jnp.clip(pstart[block_e] + cnt[block_e] - bstart, 0, MOE_BLK).astype(jnp.int32)
    return dest.astype(jnp.int32), block_e, block_valid, n_blocks


def _layer(x, mod, p):
    b, t, d = x.shape
    n = b * t
    proj = _in_proj(x, mod, p['g_pre_mix'], p['w_in'])
    o_fw, o_bw = _hgrn(proj, p['lb_logits'])
    o_b = _natten(proj, p['na_tabs'])
    x1, xn, ri, rg, counts = _mix(x, mod, proj, o_fw, o_bw, o_b, p['hgrn_norm'], p['g_post_mix'], p['g_pre_ffn'],
                                  p['w_branch_a'], p['w_branch_b'], p['w_out'], p['router_w'], p['router_b'])
    rg = rg.reshape(n, 128)
    dest, block_e, block_valid, n_blocks = _moe_plan(ri, counts[0], n)
    xs = _scatter_rows(xn.reshape(n, ROW_SUB, ROW_LANE), dest, n_blocks * MOE_BLK)
    y = _experts(xs, block_e, block_valid, p['w_gu'], p['b_gu'], p['w_d'], p['b_d'])
    yg = _gather_rows(y, dest.reshape(-1)).reshape(TOP_K, n, ROW_SUB, ROW_LANE)
    out = _combine(yg, x1.reshape(n, d), mod, rg, p['g_post_ffn'], t)
    return out.reshape(b, t, d)


def kernel(x_prompt, x_sample, c_prompt, c_sample, ada_w, ada_b, g_pre_mix, g_post_mix, g_pre_ffn, g_post_ffn,
           w_in, hgrn_lb_logits, hgrn_norm, na_rpb, w_branch_a, w_branch_b, w_out, router_w, router_b,
           w_gate_up, b_gate_up, w_down, b_down):
    d = D_MODEL
    bp, bs = c_prompt.shape[0], c_sample.shape[0]
    pad = (-(bp + bs)) % 8
    c_all = jnp.concatenate([c_prompt, c_sample, jnp.zeros((pad, d), F32)], axis=0)
    mod = _ada_mod(c_all, ada_w[0], ada_b[0]).reshape(-1, 6, d)

    w = w_in[0]
    w_perm = jnp.concatenate([w[:, :5 * d], w[:, 5 * d + 3 * NA_W:], w[:, 5 * d:5 * d + 3 * NA_W]], axis=1).astype(BF16)
    w_gu, b_gu = _gate_up_layout(w_gate_up[0], b_gate_up[0])
    p = dict(
        g_pre_mix=g_pre_mix[0], g_post_mix=g_post_mix[0], g_pre_ffn=g_pre_ffn[0], g_post_ffn=g_post_ffn[0],
        w_in=w_perm, lb_logits=hgrn_lb_logits.astype(F32), hgrn_norm=hgrn_norm[0],
        na_tabs=_na_bias_tables(na_rpb[0]),
        w_branch_a=w_branch_a[0].astype(BF16), w_branch_b=w_branch_b[0].astype(BF16), w_out=w_out[0].astype(BF16),
        router_w=router_w[0].astype(BF16), router_b=router_b[0],
        w_gu=w_gu, b_gu=b_gu,
        w_d=w_down[0].astype(BF16), b_d=b_down[0][:, None, :],
    )
    y_p = _layer(x_prompt, mod[:bp], p)
    y_s = _layer(x_sample, mod[bp:bp + bs], p)
    return (y_p, y_s)
```

```python
import functools

import numpy as np
import jax
import jax.numpy as jnp
from jax import lax
from jax.experimental import pallas as pl
from jax.experimental.pallas import tpu as pltpu
from jax.experimental.pallas import tpu_sc as plsc

F32 = jnp.float32
BF16 = jnp.bfloat16

D_MODEL = 1024
GRID_W = 64
HA_HEADS = 8
HA_DK = 128
NA_HEADS = 8
NA_DH = 64
NA_W = NA_HEADS * NA_DH
NA_KH = 8
NA_KW = 16
N_EXPERTS = 32
TOP_K = 4
D_FF = 1024
SWIGLU_ALPHA = 1.702
SWIGLU_LIMIT = 7.0
RMS_EPS = 1e-6
NEG_BIG = -1e30

COL_A = 0
COL_GATES = 5 * 1024
COL_NA = 7 * 1024
PROJ_COLS = 7 * 1024 + 3 * NA_W

HGRN_CHUNK = 64
HGRN_TILE = 256
NA_ROWS_PER_STEP = 8
PROJ_TM = 1024
PROJ_TN = 2176
MIX_TM = 512
MOE_BLK = 512
COMB_TOK = 512
SUBLANES = 8
ROW_SUB, ROW_LANE = 8, 128
VMEM_LIMIT = 56 * 1024 * 1024


def _cparams(sem):
    return pltpu.CompilerParams(dimension_semantics=sem, vmem_limit_bytes=VMEM_LIMIT)


def _dot(a, b):
    return jnp.dot(a, b, preferred_element_type=F32)


def _dot_nt(a, b):
    return lax.dot_general(a, b, (((1,), (1,)), ((), ())), preferred_element_type=F32)


def _dot_tn(a, b):
    return lax.dot_general(a, b, (((0,), (0,)), ((), ())), preferred_element_type=F32)


def _split(a):
    hi = a.astype(BF16)
    lo = (a - hi.astype(F32)).astype(BF16)
    return hi, lo


def _dot3(a, w):
    ah, al = _split(a)
    wh, wl = _split(w)
    return _dot(ah, wh) + (_dot(al, wh) + _dot(ah, wl))


def _sigmoid(x):
    return 0.5 * jnp.tanh(0.5 * x) + 0.5


def _rms(x, g):
    return x * lax.rsqrt(jnp.mean(x * x, axis=-1, keepdims=True) + RMS_EPS) * g


def _ada_kernel(c_ref, w_ref, b_ref, o_ref):
    c = c_ref[...]
    o_ref[...] = _dot3(c * _sigmoid(c), w_ref[...]) + b_ref[...]


def _ada_mod(c, ada_w, ada_b):
    bp, d = c.shape
    n = ada_w.shape[1]
    tn = 1536
    return pl.pallas_call(
        _ada_kernel,
        grid=(n // tn,),
        in_specs=[pl.BlockSpec((bp, d), lambda j: (0, 0)),
                  pl.BlockSpec((d, tn), lambda j: (0, j)),
                  pl.BlockSpec((1, tn), lambda j: (0, j))],
        out_specs=pl.BlockSpec((bp, tn), lambda j: (0, j)),
        out_shape=jax.ShapeDtypeStruct((bp, n), F32),
        compiler_params=_cparams(("arbitrary",)),
        name="ada_mod",
    )(c, ada_w, ada_b.reshape(1, n))


def _in_proj_kernel(x_ref, mod_ref, g_ref, w_ref, o_ref, xn_ref):
    @pl.when(pl.program_id(2) == 0)
    def _():
        sh = mod_ref[0, 0:1, :]
        sc = mod_ref[0, 1:2, :]
        xn = _rms(x_ref[0], g_ref[...]) * (1.0 + sc) + sh
        xn_ref[...] = xn.astype(BF16)

    o_ref[0] = _dot(xn_ref[...], w_ref[...]).astype(BF16)


def _in_proj(x, mod, g_pre, w_in_bf):
    b, t, d = x.shape
    tm = min(PROJ_TM, t)
    n = w_in_bf.shape[1]
    return pl.pallas_call(
        _in_proj_kernel,
        grid=(b, t // tm, n // PROJ_TN),
        in_specs=[pl.BlockSpec((1, tm, d), lambda bi, i, j: (bi, i, 0)),
                  pl.BlockSpec((1, 6, d), lambda bi, i, j: (bi, 0, 0)),
                  pl.BlockSpec((1, d), lambda bi, i, j: (0, 0)),
                  pl.BlockSpec((d, PROJ_TN), lambda bi, i, j: (0, j))],
        out_specs=pl.BlockSpec((1, tm, PROJ_TN), lambda bi, i, j: (bi, i, j)),
        out_shape=jax.ShapeDtypeStruct((b, t, n), BF16),
        scratch_shapes=[pltpu.VMEM((tm, d), BF16)],
        compiler_params=_cparams(("arbitrary", "arbitrary", "arbitrary")),
        name="in_proj",
    )(x, mod, g_pre.reshape(1, d), w_in_bf)


def _cumsum_rows(x, reverse):
    groups = x.shape[0] // SUBLANES
    row = lax.broadcasted_iota(jnp.int32, (SUBLANES, x.shape[1]), 0)
    out = [None] * groups
    carry = None
    for g in (range(groups - 1, -1, -1) if reverse else range(groups)):
        blk = x[g * SUBLANES:(g + 1) * SUBLANES]
        step = 1
        while step < SUBLANES:
            if reverse:
                blk = blk + jnp.where(row < SUBLANES - step, pltpu.roll(blk, SUBLANES - step, axis=0), 0.0)
            else:
                blk = blk + jnp.where(row >= step, pltpu.roll(blk, step, axis=0), 0.0)
            step *= 2
        if carry is not None:
            blk = blk + carry
        carry = blk[0:1] if reverse else blk[SUBLANES - 1:SUBLANES]
        out[g] = blk
    return jnp.concatenate(out, axis=0)


def _hgrn_dir(q_ref, f_ref, i_ref, o_ref, s_ref, sn_ref, lbd, row0, reverse):
    c = HGRN_CHUNK
    rows = pl.ds(row0, c)
    q = q_ref[0, rows, :].astype(F32)
    fpre = f_ref[0, rows, :].astype(F32)
    v = i_ref[0, rows, :]
    f = lbd + (1.0 - lbd) * _sigmoid(fpre)
    logf = jnp.log(f)
    k = 1.0 - f
    ri = lax.broadcasted_iota(jnp.int32, (c, c), 0)
    ci = lax.broadcasted_iota(jnp.int32, (c, c), 1)
    keep = (ci >= ri) if reverse else (ri >= ci)
    b = _cumsum_rows(logf, reverse)
    if reverse:
        b_mid = b[c // 2:c // 2 + 1, :]
        b_end = b[0:1, :]
    else:
        b_mid = b[c // 2 - 1:c // 2, :]
        b_end = b[c - 1:c, :]
    d = b - b_mid
    qm = q * jnp.exp(d)
    km = k * jnp.exp(-d)
    qi = (qm * jnp.exp(b_mid)).astype(BF16)
    ks = (km * jnp.exp(b_end - b_mid)).astype(BF16)
    qm = qm.astype(BF16)
    km = km.astype(BF16)
    dec = jnp.exp(b_end)
    for h in range(HA_HEADS):
        sl = slice(h * HA_DK, (h + 1) * HA_DK)
        a = _dot_nt(qm[:, sl], km[:, sl])
        a = jnp.where(keep, a, 0.0).astype(BF16)
        lhs = jnp.concatenate([qi[:, sl], a], axis=1)
        rhs = jnp.concatenate([sn_ref[h], v[:, sl]], axis=0)
        o_ref[0, rows, sl] = _dot(lhs, rhs).astype(BF16)
        st = s_ref[h] * dec[:, sl] + _dot_tn(v[:, sl], ks[:, sl])
        s_ref[h] = st
        sn_ref[h] = st.astype(BF16).T


def _hgrn_kernel(lbl_ref, qf_ref, ff_ref, if_ref, qb_ref, fb_ref, ib_ref, of_ref, ob_ref,
                 sf_ref, sb_ref, snf_ref, snb_ref):
    @pl.when(pl.program_id(1) == 0)
    def _():
        for ref in (sf_ref, sb_ref, snf_ref, snb_ref):
            ref[...] = jnp.zeros_like(ref)

    l0 = lbl_ref[0]
    l1 = lbl_ref[1]
    m = jnp.maximum(l0, l1)
    e0 = jnp.exp(l0 - m)
    e1 = jnp.exp(l1 - m)
    lb = e0 / (e0 + e1)
    n_chunks = HGRN_TILE // HGRN_CHUNK

    for ci in range(n_chunks):
        _hgrn_dir(qf_ref, ff_ref, if_ref, of_ref, sf_ref, snf_ref, lb[0:1, :], ci * HGRN_CHUNK, False)
        _hgrn_dir(qb_ref, fb_ref, ib_ref, ob_ref, sb_ref, snb_ref, lb[1:2, :],
                  (n_chunks - 1 - ci) * HGRN_CHUNK, True)


def _hgrn(proj, lb_logits):
    b, t, _ = proj.shape
    tt = HGRN_TILE
    nt = t // tt
    d = D_MODEL

    def fwd(col):
        return pl.BlockSpec((1, tt, d), lambda bi, ti: (bi, ti, col))

    def bwd(col):
        return pl.BlockSpec((1, tt, d), lambda bi, ti: (bi, nt - 1 - ti, col))

    return pl.pallas_call(
        _hgrn_kernel,
        grid=(b, nt),
        in_specs=[pl.BlockSpec((2, 2, d), lambda bi, ti: (0, 0, 0)),
                  fwd(0), fwd(1), fwd(3), bwd(0), bwd(2), bwd(3)],
        out_specs=[pl.BlockSpec((1, tt, d), lambda bi, ti: (bi, ti, 0)),
                   pl.BlockSpec((1, tt, d), lambda bi, ti: (bi, nt - 1 - ti, 0))],
        out_shape=[jax.ShapeDtypeStruct((b, t, d), BF16), jax.ShapeDtypeStruct((b, t, d), BF16)],
        scratch_shapes=[pltpu.VMEM((HA_HEADS, HA_DK, HA_DK), F32), pltpu.VMEM((HA_HEADS, HA_DK, HA_DK), F32),
                        pltpu.VMEM((HA_HEADS, HA_DK, HA_DK), BF16), pltpu.VMEM((HA_HEADS, HA_DK, HA_DK), BF16)],
        compiler_params=_cparams(("arbitrary", "arbitrary")),
        name="hgrn",
    )(lb_logits, proj, proj, proj, proj, proj, proj)


def _na_bias_tables(rpb):
    q = np.arange(GRID_W)[:, None]
    kc = np.arange(GRID_W)[None, :]
    c0 = np.clip(q - NA_KW // 2, 0, GRID_W - NA_KW)
    ok = (kc >= c0) & (kc < c0 + NA_KW)
    dc = np.clip(kc - q + (NA_KW - 1), 0, 2 * NA_KW - 2)
    base = jnp.where(ok[None, None], jnp.take(rpb.astype(F32), dc, axis=2), NEG_BIG)
    tabs = []
    for dr0 in range(NA_KH):
        w = base[:, dr0:dr0 + NA_KH]
        w = w.transpose(0, 2, 1, 3).reshape(NA_HEADS // 2, 2 * GRID_W, NA_KH * GRID_W)
        tabs.append(w)
    return jnp.stack(tabs, axis=1)


def _na_kernel(q_ref, k_ref, v_ref, tab_ref, o_ref, *, grid_rows):
    g = pl.program_id(1)
    lane = lax.broadcasted_iota(jnp.int32, (GRID_W, 2 * NA_DH), 1)
    low = lane < NA_DH
    scale = NA_DH ** -0.5

    def body(rr, carry):
        r = g * NA_ROWS_PER_STEP + rr
        r0 = jnp.clip(r - NA_KH // 2, 0, grid_rows - NA_KH)
        dr0 = r0 - r + (NA_KH - 1)
        qrow = pl.ds(pl.multiple_of(rr * GRID_W, GRID_W), GRID_W)
        krow = pl.ds(pl.multiple_of(r0 * GRID_W, GRID_W), NA_KH * GRID_W)
        for j in range(NA_HEADS // 2):
            sl = slice(j * 2 * NA_DH, (j + 1) * 2 * NA_DH)
            qp = q_ref[0, qrow, sl]
            zero = jnp.zeros_like(qp)
            ql = jnp.concatenate([jnp.where(low, qp, zero), jnp.where(low, zero, qp)], axis=0)
            s = _dot_nt(ql, k_ref[0, krow, sl]) * scale + tab_ref[j, dr0]
            m = jnp.max(s, axis=-1, keepdims=True)
            p = jnp.exp(s - m)
            l = jnp.sum(p, axis=-1, keepdims=True)
            pv = _dot(p.astype(BF16), v_ref[0, krow, sl]) / l
            o_ref[0, qrow, sl] = jnp.where(low, pv[:GRID_W], pv[GRID_W:]).astype(BF16)
        return carry

    lax.fori_loop(0, NA_ROWS_PER_STEP, body, 0, unroll=4)


def _natten(proj, tabs):
    b, t, _ = proj.shape
    grid_rows = t // GRID_W
    tq = NA_ROWS_PER_STEP * GRID_W
    cq, ck, cv = (COL_NA // NA_W, COL_NA // NA_W + 1, COL_NA // NA_W + 2)
    return pl.pallas_call(
        functools.partial(_na_kernel, grid_rows=grid_rows),
        grid=(b, t // tq),
        in_specs=[pl.BlockSpec((1, tq, NA_W), lambda bi, gi: (bi, gi, cq)),
                  pl.BlockSpec((1, t, NA_W), lambda bi, gi: (bi, 0, ck)),
                  pl.BlockSpec((1, t, NA_W), lambda bi, gi: (bi, 0, cv)),
                  pl.BlockSpec(tabs.shape, lambda bi, gi: (0, 0, 0, 0))],
        out_specs=pl.BlockSpec((1, tq, NA_W), lambda bi, gi: (bi, gi, 0)),
        out_shape=jax.ShapeDtypeStruct((b, t, NA_W), BF16),
        compiler_params=_cparams(("arbitrary", "arbitrary")),
        name="natten",
    )(proj, proj, proj, tabs)


def _mix_kernel(x_ref, mod_ref, of_ref, ob_ref, g_ref, ga_ref, gb_ref, nb_ref,
                gn_ref, gpm_ref, gpf_ref, wa_ref, wb_ref, wo_ref, rw_ref, rb_ref,
                x1_ref, xn_ref, ri_ref, rg_ref, cnt_ref, tri_ref, carry_ref):
    tm = MIX_TM
    first = (pl.program_id(0) == 0) & (pl.program_id(1) == 0)

    @pl.when(first)
    def _():
        r = lax.broadcasted_iota(jnp.int32, (tm, tm), 0)
        c = lax.broadcasted_iota(jnp.int32, (tm, tm), 1)
        tri_ref[...] = jnp.where(r > c, 1.0, 0.0).astype(BF16)
        carry_ref[...] = jnp.zeros_like(carry_ref)

    sc2 = mod_ref[0, 4:5, :]
    sh2 = mod_ref[0, 3:4, :]
    gt1 = mod_ref[0, 2:3, :]

    o = of_ref[0].astype(F32) + ob_ref[0].astype(F32)
    parts = []
    for h in range(HA_HEADS):
        oh = o[:, h * HA_DK:(h + 1) * HA_DK]
        parts.append(oh * lax.rsqrt(jnp.mean(oh * oh, axis=-1, keepdims=True) + RMS_EPS))
    g = g_ref[0].astype(F32)
    oa = jnp.concatenate(parts, axis=-1) * gn_ref[...] * (g * _sigmoid(g))
    ya = _dot(oa.astype(BF16), wa_ref[...])
    yb = _dot(nb_ref[0], wb_ref[...])
    mix = _sigmoid(ga_ref[0].astype(F32)) * ya + _sigmoid(gb_ref[0].astype(F32)) * yb
    mo = _dot(mix.astype(BF16), wo_ref[...])
    x1 = x_ref[0] + gt1 * _rms(mo, gpm_ref[...])
    x1_ref[0] = x1

    xn = _rms(x1, gpf_ref[...]) * (1.0 + sc2) + sh2
    xn_ref[0] = xn.reshape(tm, ROW_SUB, ROW_LANE)
    logits = _dot(xn.astype(BF16), rw_ref[...]) + rb_ref[...]
    lane = lax.broadcasted_iota(jnp.int32, (tm, N_EXPERTS), 1)
    work = logits
    vals, idxs, hots = [], [], []
    for _ in range(TOP_K):
        mv = jnp.max(work, axis=-1, keepdims=True)
        mi = jnp.min(jnp.where(work == mv, lane, N_EXPERTS), axis=-1, keepdims=True)
        hot = lane == mi
        vals.append(mv)
        idxs.append(mi)
        hots.append(hot)
        work = jnp.where(hot, -jnp.inf, work)
    es = [jnp.exp(v - vals[0]) for v in vals]
    den = es[0] + es[1] + es[2] + es[3]

    cnt = sum(jnp.where(h, 1.0, 0.0) for h in hots)
    before = _dot(tri_ref[...], cnt.astype(BF16)) + carry_ref[...]
    carry_ref[...] = carry_ref[...] + jnp.sum(cnt, axis=0, keepdims=True)
    cnt_ref[...] = jnp.broadcast_to(carry_ref[...], cnt_ref.shape)

    lane_o = lax.broadcasted_iota(jnp.int32, (tm, 128), 1)
    ri = jnp.zeros((tm, 128), jnp.int32)
    rg = jnp.zeros((tm, 128), F32)
    for kk in range(TOP_K):
        rank = jnp.sum(jnp.where(hots[kk], before, 0.0), axis=-1, keepdims=True).astype(jnp.int32)
        ri = jnp.where(lane_o == kk, idxs[kk], ri)
        ri = jnp.where(lane_o == TOP_K + kk, rank, ri)
        rg = jnp.where(lane_o == kk, es[kk] / den, rg)
    ri_ref[...] = ri.T[:8, :]
    rg_ref[0] = rg


def _mix(x, mod, proj, o_fw, o_bw, o_b, hgrn_norm, g_post_mix, g_pre_ffn, wa, wb, wo, rw, rb):
    b, t, d = x.shape
    tm = min(MIX_TM, t)
    assert tm == MIX_TM
    row = lambda a: a.reshape(1, -1)
    tok = lambda w, col: pl.BlockSpec((1, tm, w), lambda bi, i: (bi, i, col))
    full = lambda a: pl.BlockSpec(a.shape, lambda bi, i: (0,) * a.ndim)
    gn, gpm, gpf, rbr = row(hgrn_norm), row(g_post_mix), row(g_pre_ffn), row(rb)
    outs = pl.pallas_call(
        _mix_kernel,
        grid=(b, t // tm),
        in_specs=[tok(d, 0),
                  pl.BlockSpec((1, 6, d), lambda bi, i: (bi, 0, 0)),
                  tok(d, 0), tok(d, 0),
                  tok(d, 4),
                  tok(d, COL_GATES // d), tok(d, COL_GATES // d + 1),
                  tok(NA_W, 0),
                  full(gn), full(gpm), full(gpf), full(wa), full(wb), full(wo), full(rw), full(rbr)],
        out_specs=[tok(d, 0), pl.BlockSpec((1, tm, ROW_SUB, ROW_LANE), lambda bi, i: (bi, i, 0, 0)),
                   pl.BlockSpec((8, tm), lambda bi, i: (0, bi * (t // tm) + i)), tok(128, 0),
                   pl.BlockSpec((8, N_EXPERTS), lambda bi, i: (0, 0))],
        out_shape=[jax.ShapeDtypeStruct((b, t, d), F32), jax.ShapeDtypeStruct((b, t, ROW_SUB, ROW_LANE), F32),
                   jax.ShapeDtypeStruct((8, b * t), jnp.int32), jax.ShapeDtypeStruct((b, t, 128), F32),
                   jax.ShapeDtypeStruct((8, N_EXPERTS), F32)],
        scratch_shapes=[pltpu.VMEM((tm, tm), BF16), pltpu.VMEM((1, N_EXPERTS), F32)],
        compiler_params=_cparams(("arbitrary", "arbitrary")),
        name="mix",
    )(x, mod, o_fw, o_bw, proj, proj, proj, o_b, gn, gpm, gpf, wa, wb, wo, rw, rbr)
    return outs


SC_CORES, SC_SUBCORES = 2, 16
SC_ROWS = 32


def _scatter_rows(xn, dest_t, n_rows):
    n_tok = xn.shape[0]
    workers = SC_CORES * SC_SUBCORES
    per_w = n_tok // workers
    n_it = per_w // SC_ROWS
    assert per_w * workers == n_tok and n_it * SC_ROWS == per_w and n_it % 2 == 0
    idx = dest_t.reshape(TOP_K, n_tok // SC_ROWS, SC_ROWS).transpose(1, 0, 2)
    mesh = plsc.VectorSubcoreMesh(core_axis_name="c", subcore_axis_name="s")

    @functools.partial(
        pl.kernel, mesh=mesh,
        out_type=jax.ShapeDtypeStruct((n_rows, ROW_SUB, ROW_LANE), F32),
        scratch_types=[pltpu.VMEM((2, TOP_K, SC_ROWS), jnp.int32),
                       pltpu.VMEM((2, SC_ROWS, ROW_SUB, ROW_LANE), F32),
                       pltpu.SemaphoreType.DMA((2,)), pltpu.SemaphoreType.DMA((2,))],
    )
    def scatter_kernel(x_hbm, idx_hbm, out_hbm, idx_v, rows_v, sem_load, sem_store):
        chunk0 = (lax.axis_index("s") * SC_CORES + lax.axis_index("c")) * n_it

        def load(g, slot):
            tok = pl.ds(pl.multiple_of((chunk0 + g) * SC_ROWS, SC_ROWS), SC_ROWS)
            return pltpu.make_async_copy(x_hbm.at[tok], rows_v.at[slot], sem_load.at[slot])

        def start_load(g, slot):
            pltpu.sync_copy(idx_hbm.at[chunk0 + g], idx_v.at[slot])
            load(g, slot).start()

        def store(slot, kk):
            return pltpu.make_async_copy(rows_v.at[slot], out_hbm.at[idx_v.at[slot, kk]], sem_store.at[slot])

        def scatter(g, slot):
            load(g, slot).wait()
            for kk in range(TOP_K):
                store(slot, kk).start()
            for kk in range(TOP_K):
                store(slot, kk).wait()

        start_load(0, 0)

        @pl.loop(0, n_it, step=2)
        def _(g):
            start_load(g + 1, 1)
            scatter(g, 0)

            @pl.when(g + 2 < n_it)
            def _():
                start_load(g + 2, 0)

            scatter(g + 1, 1)

    return scatter_kernel(xn, idx)


GU_SLAB = 256


def _gate_up_layout_kernel(w_ref, o_ref):
    half = GU_SLAB // 2
    r = lax.broadcasted_iota(jnp.int32, (GU_SLAB, GU_SLAB), 0)
    c = lax.broadcasted_iota(jnp.int32, (GU_SLAB, GU_SLAB), 1)
    src = jnp.where(c < half, 2 * c, 2 * (c - half) + 1)
    pm = jnp.where(r == src, 1.0, 0.0).astype(BF16)
    for s in range(w_ref.shape[2] // GU_SLAB):
        cols = slice(s * GU_SLAB, (s + 1) * GU_SLAB)
        o_ref[0, :, cols] = _dot(w_ref[0, :, cols].astype(BF16), pm).astype(BF16)


def _gate_up_layout(w_gate_up, b_gate_up):
    e, k, n2 = w_gate_up.shape
    half = GU_SLAB // 2
    tk = 512
    w = pl.pallas_call(
        _gate_up_layout_kernel,
        grid=(e, k // tk),
        in_specs=[pl.BlockSpec((1, tk, n2), lambda i, j: (i, j, 0))],
        out_specs=pl.BlockSpec((1, tk, n2), lambda i, j: (i, j, 0)),
        out_shape=jax.ShapeDtypeStruct((e, k, n2), BF16),
        compiler_params=_cparams(("arbitrary", "arbitrary")),
        name="gate_up_layout",
    )(w_gate_up)
    perm = np.concatenate([s * GU_SLAB + np.concatenate([2 * np.arange(half), 2 * np.arange(half) + 1])
                           for s in range(n2 // GU_SLAB)])
    return w, b_gate_up[:, None, perm]


def _expert_kernel(be_ref, bv_ref, x_ref, wgu_ref, bgu_ref, wd_ref, bd_ref, o_ref, a_ref):
    i = pl.program_id(0)
    valid = bv_ref[i]
    half = GU_SLAB // 2

    @pl.when(valid > 0)
    def _():
        rowi = lax.broadcasted_iota(jnp.int32, (MOE_BLK, 1), 0)
        x = x_ref[...].reshape(MOE_BLK, D_MODEL)
        x = jnp.where(rowi < valid, x, 0.0).astype(BF16)
        for s in range(2 * D_FF // GU_SLAB):
            cols = slice(s * GU_SLAB, (s + 1) * GU_SLAB)
            h = _dot(x, wgu_ref[0, :, cols]) + bgu_ref[0, :, cols]
            glu = jnp.minimum(h[:, :half], SWIGLU_LIMIT)
            lin = jnp.clip(h[:, half:], -SWIGLU_LIMIT, SWIGLU_LIMIT)
            a = glu * _sigmoid(SWIGLU_ALPHA * glu) * (lin + 1.0)
            a_ref[:, s * half:(s + 1) * half] = a.astype(BF16)
        o_ref[...] = (_dot(a_ref[...], wd_ref[0]) + bd_ref[0]).reshape(o_ref.shape)

    @pl.when(valid <= 0)
    def _():
        o_ref[...] = jnp.zeros_like(o_ref)


def _experts(xs, block_e, block_valid, wgu, bgu, wd, bd):
    n_rows = xs.shape[0]
    d = D_MODEL
    n_blocks = n_rows // MOE_BLK
    emap = lambda i, be, bv: (be[i], 0, 0)
    rows = lambda: pl.BlockSpec((MOE_BLK, ROW_SUB, ROW_LANE), lambda i, be, bv: (i, 0, 0))
    return pl.pallas_call(
        _expert_kernel,
        grid_spec=pltpu.PrefetchScalarGridSpec(
            num_scalar_prefetch=2,
            grid=(n_blocks,),
            in_specs=[rows(),
                      pl.BlockSpec((1, d, 2 * D_FF), emap), pl.BlockSpec((1, 1, 2 * D_FF), emap),
                      pl.BlockSpec((1, D_FF, d), emap), pl.BlockSpec((1, 1, d), emap)],
            out_specs=rows(),
            scratch_shapes=[pltpu.VMEM((MOE_BLK, D_FF), BF16)],
        ),
        out_shape=jax.ShapeDtypeStruct((n_rows, ROW_SUB, ROW_LANE), F32),
        compiler_params=_cparams(("arbitrary",)),
        name="experts",
    )(block_e, block_valid, xs, wgu, bgu, wd, bd)


def _gather_rows(y, dest_flat):
    n_out = dest_flat.shape[0]
    workers = SC_CORES * SC_SUBCORES
    per_w = n_out // workers
    n_it = per_w // SC_ROWS
    assert per_w * workers == n_out and n_it * SC_ROWS == per_w and n_it % 2 == 0
    mesh = plsc.VectorSubcoreMesh(core_axis_name="c", subcore_axis_name="s")

    @functools.partial(
        pl.kernel, mesh=mesh,
        out_type=jax.ShapeDtypeStruct((n_out, ROW_SUB, ROW_LANE), F32),
        scratch_types=[pltpu.VMEM((2, SC_ROWS), jnp.int32),
                       pltpu.VMEM((2, SC_ROWS, ROW_SUB, ROW_LANE), F32),
                       pltpu.SemaphoreType.DMA((2,))],
    )
    def gather_kernel(y_hbm, idx_hbm, out_hbm, idx_v, rows_v, sem):
        base = (lax.axis_index("s") * SC_CORES + lax.axis_index("c")) * per_w

        def span(g):
            return pl.ds(pl.multiple_of(base + g * SC_ROWS, SC_ROWS), SC_ROWS)

        def row_gather(slot):
            return pltpu.make_async_copy(y_hbm.at[idx_v.at[slot]], rows_v.at[slot], sem.at[slot])

        def start(g, slot):
            pltpu.sync_copy(idx_hbm.at[span(g)], idx_v.at[slot])
            row_gather(slot).start()

        start(0, 0)

        @pl.loop(0, n_it, step=2)
        def _(g):
            start(g + 1, 1)
            row_gather(0).wait()
            pltpu.sync_copy(rows_v.at[0], out_hbm.at[span(g)])

            @pl.when(g + 2 < n_it)
            def _():
                start(g + 2, 0)

            row_gather(1).wait()
            pltpu.sync_copy(rows_v.at[1], out_hbm.at[span(g + 1)])

    return gather_kernel(y, dest_flat)


def _combine_kernel(yg_ref, x1_ref, mod_ref, rg_ref, gpo_ref, o_ref):
    rg = rg_ref[...]
    h = rg[:, 0:1] * yg_ref[0].reshape(COMB_TOK, D_MODEL)
    for kk in range(1, TOP_K):
        h = h + rg[:, kk:kk + 1] * yg_ref[kk].reshape(COMB_TOK, D_MODEL)
    gt2 = mod_ref[0, 5:6, :]
    o_ref[...] = x1_ref[...] + gt2 * _rms(h, gpo_ref[...])


def _combine(yg, x1, mod, rg, g_post_ffn, tokens_per_batch):
    n, d = x1.shape
    per_b = tokens_per_batch // COMB_TOK
    gpo = g_post_ffn.reshape(1, d)
    return pl.pallas_call(
        _combine_kernel,
        grid=(n // COMB_TOK,),
        in_specs=[pl.BlockSpec((TOP_K, COMB_TOK, ROW_SUB, ROW_LANE), lambda i: (0, i, 0, 0)),
                  pl.BlockSpec((COMB_TOK, d), lambda i: (i, 0)),
                  pl.BlockSpec((1, 6, d), lambda i: (i // per_b, 0, 0)),
                  pl.BlockSpec((COMB_TOK, 128), lambda i: (i, 0)),
                  pl.BlockSpec((1, d), lambda i: (0, 0))],
        out_specs=pl.BlockSpec((COMB_TOK, d), lambda i: (i, 0)),
        out_shape=jax.ShapeDtypeStruct((n, d), F32),
        compiler_params=_cparams(("arbitrary",)),
        name="combine",
    )(yg, x1, mod, rg, gpo)


def _moe_plan(ri_t, counts, n_tok):
    idx = ri_t[:TOP_K]
    rank = ri_t[TOP_K:2 * TOP_K]
    cnt = counts.astype(jnp.int32)
    padded = (cnt + MOE_BLK - 1) // MOE_BLK * MOE_BLK
    pend = jnp.cumsum(padded)
    pstart = pend - padded
    dest = rank
    for e in range(N_EXPERTS):
        dest = dest + jnp.where(idx == e, pstart[e], 0)
    n_blocks = (n_tok * TOP_K + MOE_BLK - 1) // MOE_BLK + N_EXPERTS
    bstart = jnp.arange(n_blocks, dtype=jnp.int32) * MOE_BLK
    block_e = jnp.minimum(jnp.sum((bstart[:, None] >= pend[None, :]).astype(jnp.int32), axis=1), N_EXPERTS - 1)
    block_valid = jnp.clip(pstart[block_e] + cnt[block_e] - bstart, 0, MOE_BLK).astype(jnp.int32)
    return dest.astype(jnp.int32), block_e, block_valid, n_blocks


def _layer(x, mod, p):
    b, t, d = x.shape
    n = b * t
    proj = _in_proj(x, mod, p['g_pre_mix'], p['w_in'])
    o_fw, o_bw = _hgrn(proj, p['lb_logits'])
    o_b = _natten(proj, p['na_tabs'])
    x1, xn, ri, rg, counts = _mix(x, mod, proj, o_fw, o_bw, o_b, p['hgrn_norm'], p['g_post_mix'], p['g_pre_ffn'],
                                  p['w_branch_a'], p['w_branch_b'], p['w_out'], p['router_w'], p['router_b'])
    rg = rg.reshape(n, 128)
    dest, block_e, block_valid, n_blocks = _moe_plan(ri, counts[0], n)
    xs = _scatter_rows(xn.reshape(n, ROW_SUB, ROW_LANE), dest, n_blocks * MOE_BLK)
    y = _experts(xs, block_e, block_valid, p['w_gu'], p['b_gu'], p['w_d'], p['b_d'])
    yg = _gather_rows(y, dest.reshape(-1)).reshape(TOP_K, n, ROW_SUB, ROW_LANE)
    out = _combine(yg, x1.reshape(n, d), mod, rg, p['g_post_ffn'], t)
    return out.reshape(b, t, d)


def kernel(x_prompt, x_sample, c_prompt, c_sample, ada_w, ada_b, g_pre_mix, g_post_mix, g_pre_ffn, g_post_ffn,
           w_in, hgrn_lb_logits, hgrn_norm, na_rpb, w_branch_a, w_branch_b, w_out, router_w, router_b,
           w_gate_up, b_gate_up, w_down, b_down):
    d = D_MODEL
    bp, bs = c_prompt.shape[0], c_sample.shape[0]
    pad = (-(bp + bs)) % 8
    c_all = jnp.concatenate([c_prompt, c_sample, jnp.zeros((pad, d), F32)], axis=0)
    mod = _ada_mod(c_all, ada_w[0], ada_b[0]).reshape(-1, 6, d)

    w = w_in[0]
    w_perm = jnp.concatenate([w[:, :5 * d], w[:, 5 * d + 3 * NA_W:], w[:, 5 * d:5 * d + 3 * NA_W]], axis=1).astype(BF16)
    w_gu, b_gu = _gate_up_layout(w_gate_up[0], b_gate_up[0])
    p = dict(
        g_pre_mix=g_pre_mix[0], g_post_mix=g_post_mix[0], g_pre_ffn=g_pre_ffn[0], g_post_ffn=g_post_ffn[0],
        w_in=w_perm, lb_logits=hgrn_lb_logits.astype(F32), hgrn_norm=hgrn_norm[0],
        na_tabs=_na_bias_tables(na_rpb[0]),
        w_branch_a=w_branch_a[0].astype(BF16), w_branch_b=w_branch_b[0].astype(BF16), w_out=w_out[0].astype(BF16),
        router_w=router_w[0].astype(BF16), router_b=router_b[0],
        w_gu=w_gu, b_gu=b_gu,
        w_d=w_down[0].astype(BF16), b_d=b_down[0][:, None, :],
    )
    y_p = _layer(x_prompt, mod[:bp], p)
    y_s = _layer(x_sample, mod[bp:bp + bs], p)
    return (y_p, y_s)
```

```python
import functools

import numpy as np
import jax
import jax.numpy as jnp
from jax import lax
from jax.experimental import pallas as pl
from jax.experimental.pallas import tpu as pltpu
from jax.experimental.pallas import tpu_sc as plsc

F32 = jnp.float32
BF16 = jnp.bfloat16

D_MODEL = 1024
GRID_W = 64
HA_HEADS = 8
HA_DK = 128
NA_HEADS = 8
NA_DH = 64
NA_W = NA_HEADS * NA_DH
NA_KH = 8
NA_KW = 16
N_EXPERTS = 32
TOP_K = 4
D_FF = 1024
SWIGLU_ALPHA = 1.702
SWIGLU_LIMIT = 7.0
RMS_EPS = 1e-6
NEG_BIG = -1e30

COL_A = 0
COL_GATES = 5 * 1024
COL_NA = 7 * 1024
PROJ_COLS = 7 * 1024 + 3 * NA_W

HGRN_CHUNK = 64
HGRN_TILE = 256
NA_ROWS_PER_STEP = 8
PROJ_TM = 1024
PROJ_TN = 2176
MIX_TM = 512
MOE_BLK_SMALL, MOE_BLK_LARGE = 512, 1024
COMB_TOK = 512
SUBLANES = 8
ROW_SUB, ROW_LANE = 8, 128
VMEM_LIMIT = 56 * 1024 * 1024


def _cparams(sem):
    return pltpu.CompilerParams(dimension_semantics=sem, vmem_limit_bytes=VMEM_LIMIT)


def _dot(a, b):
    return jnp.dot(a, b, preferred_element_type=F32)


def _dot_nt(a, b):
    return lax.dot_general(a, b, (((1,), (1,)), ((), ())), preferred_element_type=F32)


def _dot_tn(a, b):
    return lax.dot_general(a, b, (((0,), (0,)), ((), ())), preferred_element_type=F32)


def _split(a):
    hi = a.astype(BF16)
    lo = (a - hi.astype(F32)).astype(BF16)
    return hi, lo


def _dot3(a, w):
    ah, al = _split(a)
    wh, wl = _split(w)
    return _dot(ah, wh) + (_dot(al, wh) + _dot(ah, wl))


def _sigmoid(x):
    return 0.5 * jnp.tanh(0.5 * x) + 0.5


def _rms(x, g):
    return x * lax.rsqrt(jnp.mean(x * x, axis=-1, keepdims=True) + RMS_EPS) * g


def _ada_kernel(c_ref, w_ref, b_ref, o_ref):
    c = c_ref[...]
    o_ref[...] = _dot3(c * _sigmoid(c), w_ref[...]) + b_ref[...]


def _ada_mod(c, ada_w, ada_b):
    bp, d = c.shape
    n = ada_w.shape[1]
    tn = 1536
    return pl.pallas_call(
        _ada_kernel,
        grid=(n // tn,),
        in_specs=[pl.BlockSpec((bp, d), lambda j: (0, 0)),
                  pl.BlockSpec((d, tn), lambda j: (0, j)),
                  pl.BlockSpec((1, tn), lambda j: (0, j))],
        out_specs=pl.BlockSpec((bp, tn), lambda j: (0, j)),
        out_shape=jax.ShapeDtypeStruct((bp, n), F32),
        compiler_params=_cparams(("arbitrary",)),
        name="ada_mod",
    )(c, ada_w, ada_b.reshape(1, n))


def _in_proj_kernel(x_ref, mod_ref, g_ref, w_ref, o_ref, xn_ref):
    @pl.when(pl.program_id(2) == 0)
    def _():
        sh = mod_ref[0, 0:1, :]
        sc = mod_ref[0, 1:2, :]
        xn = _rms(x_ref[0], g_ref[...]) * (1.0 + sc) + sh
        xn_ref[...] = xn.astype(BF16)

    o_ref[0] = _dot(xn_ref[...], w_ref[...]).astype(BF16)


def _in_proj(x, mod, g_pre, w_in_bf):
    b, t, d = x.shape
    tm = min(PROJ_TM, t)
    n = w_in_bf.shape[1]
    return pl.pallas_call(
        _in_proj_kernel,
        grid=(b, t // tm, n // PROJ_TN),
        in_specs=[pl.BlockSpec((1, tm, d), lambda bi, i, j: (bi, i, 0)),
                  pl.BlockSpec((1, 6, d), lambda bi, i, j: (bi, 0, 0)),
                  pl.BlockSpec((1, d), lambda bi, i, j: (0, 0)),
                  pl.BlockSpec((d, PROJ_TN), lambda bi, i, j: (0, j))],
        out_specs=pl.BlockSpec((1, tm, PROJ_TN), lambda bi, i, j: (bi, i, j)),
        out_shape=jax.ShapeDtypeStruct((b, t, n), BF16),
        scratch_shapes=[pltpu.VMEM((tm, d), BF16)],
        compiler_params=_cparams(("arbitrary", "arbitrary", "arbitrary")),
        name="in_proj",
    )(x, mod, g_pre.reshape(1, d), w_in_bf)


def _cumsum_rows(x, reverse):
    groups = x.shape[0] // SUBLANES
    row = lax.broadcasted_iota(jnp.int32, (SUBLANES, x.shape[1]), 0)
    out = [None] * groups
    carry = None
    for g in (range(groups - 1, -1, -1) if reverse else range(groups)):
        blk = x[g * SUBLANES:(g + 1) * SUBLANES]
        step = 1
        while step < SUBLANES:
            if reverse:
                blk = blk + jnp.where(row < SUBLANES - step, pltpu.roll(blk, SUBLANES - step, axis=0), 0.0)
            else:
                blk = blk + jnp.where(row >= step, pltpu.roll(blk, step, axis=0), 0.0)
            step *= 2
        if carry is not None:
            blk = blk + carry
        carry = blk[0:1] if reverse else blk[SUBLANES - 1:SUBLANES]
        out[g] = blk
    return jnp.concatenate(out, axis=0)


def _hgrn_dir(q_ref, f_ref, i_ref, o_ref, s_ref, sn_ref, lbd, row0, reverse):
    c = HGRN_CHUNK
    rows = pl.ds(row0, c)
    q = q_ref[0, rows, :].astype(F32)
    fpre = f_ref[0, rows, :].astype(F32)
    v = i_ref[0, rows, :]
    f = 0.5 * (1.0 + lbd) + (0.5 * (1.0 - lbd)) * jnp.tanh(0.5 * fpre)
    logf = jnp.log(f)
    k = 1.0 - f
    ri = lax.broadcasted_iota(jnp.int32, (c, c), 0)
    ci = lax.broadcasted_iota(jnp.int32, (c, c), 1)
    keep = (ci >= ri) if reverse else (ri >= ci)
    b = _cumsum_rows(logf, reverse)
    if reverse:
        b_mid = b[c // 2:c // 2 + 1, :]
        b_end = b[0:1, :]
    else:
        b_mid = b[c // 2 - 1:c // 2, :]
        b_end = b[c - 1:c, :]
    d = b - b_mid
    qm = q * jnp.exp(d)
    km = k * jnp.exp(-d)
    qi = (qm * jnp.exp(b_mid)).astype(BF16)
    ks = (km * jnp.exp(b_end - b_mid)).astype(BF16)
    qm = qm.astype(BF16)
    km = km.astype(BF16)
    dec = jnp.exp(b_end)
    for h in range(HA_HEADS):
        sl = slice(h * HA_DK, (h + 1) * HA_DK)
        a = _dot_nt(qm[:, sl], km[:, sl])
        a = jnp.where(keep, a, 0.0).astype(BF16)
        lhs = jnp.concatenate([qi[:, sl], a], axis=1)
        rhs = jnp.concatenate([sn_ref[h], v[:, sl]], axis=0)
        o_ref[0, rows, sl] = _dot(lhs, rhs).astype(BF16)
        st = s_ref[h] * dec[:, sl] + _dot_tn(v[:, sl], ks[:, sl])
        s_ref[h] = st
        sn_ref[h] = st.astype(BF16).T


def _hgrn_kernel(lbl_ref, qf_ref, ff_ref, if_ref, qb_ref, fb_ref, ib_ref, of_ref, ob_ref,
                 sf_ref, sb_ref, snf_ref, snb_ref):
    @pl.when(pl.program_id(1) == 0)
    def _():
        for ref in (sf_ref, sb_ref, snf_ref, snb_ref):
            ref[...] = jnp.zeros_like(ref)

    l0 = lbl_ref[0]
    l1 = lbl_ref[1]
    m = jnp.maximum(l0, l1)
    e0 = jnp.exp(l0 - m)
    e1 = jnp.exp(l1 - m)
    lb = e0 / (e0 + e1)
    n_chunks = HGRN_TILE // HGRN_CHUNK

    for ci in range(n_chunks):
        _hgrn_dir(qf_ref, ff_ref, if_ref, of_ref, sf_ref, snf_ref, lb[0:1, :], ci * HGRN_CHUNK, False)
        _hgrn_dir(qb_ref, fb_ref, ib_ref, ob_ref, sb_ref, snb_ref, lb[1:2, :],
                  (n_chunks - 1 - ci) * HGRN_CHUNK, True)


def _hgrn(proj, lb_logits):
    b, t, _ = proj.shape
    tt = HGRN_TILE
    nt = t // tt
    d = D_MODEL

    def fwd(col):
        return pl.BlockSpec((1, tt, d), lambda bi, ti: (bi, ti, col))

    def bwd(col):
        return pl.BlockSpec((1, tt, d), lambda bi, ti: (bi, nt - 1 - ti, col))

    return pl.pallas_call(
        _hgrn_kernel,
        grid=(b, nt),
        in_specs=[pl.BlockSpec((2, 2, d), lambda bi, ti: (0, 0, 0)),
                  fwd(0), fwd(1), fwd(3), bwd(0), bwd(2), bwd(3)],
        out_specs=[pl.BlockSpec((1, tt, d), lambda bi, ti: (bi, ti, 0)),
                   pl.BlockSpec((1, tt, d), lambda bi, ti: (bi, nt - 1 - ti, 0))],
        out_shape=[jax.ShapeDtypeStruct((b, t, d), BF16), jax.ShapeDtypeStruct((b, t, d), BF16)],
        scratch_shapes=[pltpu.VMEM((HA_HEADS, HA_DK, HA_DK), F32), pltpu.VMEM((HA_HEADS, HA_DK, HA_DK), F32),
                        pltpu.VMEM((HA_HEADS, HA_DK, HA_DK), BF16), pltpu.VMEM((HA_HEADS, HA_DK, HA_DK), BF16)],
        compiler_params=_cparams(("arbitrary", "arbitrary")),
        name="hgrn",
    )(lb_logits, proj, proj, proj, proj, proj, proj)


def _na_bias_tables(rpb):
    q = np.arange(GRID_W)[:, None]
    kc = np.arange(GRID_W)[None, :]
    c0 = np.clip(q - NA_KW // 2, 0, GRID_W - NA_KW)
    ok = (kc >= c0) & (kc < c0 + NA_KW)
    dc = np.clip(kc - q + (NA_KW - 1), 0, 2 * NA_KW - 2)
    base = jnp.where(ok[None, None], jnp.take(rpb.astype(F32), dc, axis=2), NEG_BIG)
    tabs = []
    for dr0 in range(NA_KH):
        w = base[:, dr0:dr0 + NA_KH]
        w = w.transpose(0, 2, 1, 3).reshape(NA_HEADS // 2, 2 * GRID_W, NA_KH * GRID_W)
        tabs.append(w)
    return jnp.stack(tabs, axis=1)


def _na_kernel(q_ref, k_ref, v_ref, tab_ref, o_ref, *, grid_rows):
    g = pl.program_id(1)
    lane = lax.broadcasted_iota(jnp.int32, (GRID_W, 2 * NA_DH), 1)
    low = lane < NA_DH
    scale = NA_DH ** -0.5

    def body(rr, carry):
        r = g * NA_ROWS_PER_STEP + rr
        r0 = jnp.clip(r - NA_KH // 2, 0, grid_rows - NA_KH)
        dr0 = r0 - r + (NA_KH - 1)
        qrow = pl.ds(pl.multiple_of(rr * GRID_W, GRID_W), GRID_W)
        krow = pl.ds(pl.multiple_of(r0 * GRID_W, GRID_W), NA_KH * GRID_W)
        for j in range(NA_HEADS // 2):
            sl = slice(j * 2 * NA_DH, (j + 1) * 2 * NA_DH)
            qp = q_ref[0, qrow, sl]
            zero = jnp.zeros_like(qp)
            ql = jnp.concatenate([jnp.where(low, qp, zero), jnp.where(low, zero, qp)], axis=0)
            s = _dot_nt(ql, k_ref[0, krow, sl]) * scale + tab_ref[j, dr0]
            m = jnp.max(s, axis=-1, keepdims=True)
            p = jnp.exp(s - m)
            l = jnp.sum(p, axis=-1, keepdims=True)
            pv = _dot(p.astype(BF16), v_ref[0, krow, sl]) / l
            o_ref[0, qrow, sl] = jnp.where(low, pv[:GRID_W], pv[GRID_W:]).astype(BF16)
        return carry

    lax.fori_loop(0, NA_ROWS_PER_STEP, body, 0, unroll=4)


def _natten(proj, tabs):
    b, t, _ = proj.shape
    grid_rows = t // GRID_W
    tq = NA_ROWS_PER_STEP * GRID_W
    cq, ck, cv = (COL_NA // NA_W, COL_NA // NA_W + 1, COL_NA // NA_W + 2)
    return pl.pallas_call(
        functools.partial(_na_kernel, grid_rows=grid_rows),
        grid=(b, t // tq),
        in_specs=[pl.BlockSpec((1, tq, NA_W), lambda bi, gi: (bi, gi, cq)),
                  pl.BlockSpec((1, t, NA_W), lambda bi, gi: (bi, 0, ck)),
                  pl.BlockSpec((1, t, NA_W), lambda bi, gi: (bi, 0, cv)),
                  pl.BlockSpec(tabs.shape, lambda bi, gi: (0, 0, 0, 0))],
        out_specs=pl.BlockSpec((1, tq, NA_W), lambda bi, gi: (bi, gi, 0)),
        out_shape=jax.ShapeDtypeStruct((b, t, NA_W), BF16),
        compiler_params=_cparams(("arbitrary", "arbitrary")),
        name="natten",
    )(proj, proj, proj, tabs)


def _mix_kernel(x_ref, mod_ref, of_ref, ob_ref, g_ref, ga_ref, gb_ref, nb_ref,
                gn_ref, gpm_ref, gpf_ref, wa_ref, wb_ref, wo_ref, rw_ref, rb_ref,
                x1_ref, xn_ref, ri_ref, rg_ref, cnt_ref, tri_ref, carry_ref):
    tm = MIX_TM
    first = (pl.program_id(0) == 0) & (pl.program_id(1) == 0)

    @pl.when(first)
    def _():
        r = lax.broadcasted_iota(jnp.int32, (tm, tm), 0)
        c = lax.broadcasted_iota(jnp.int32, (tm, tm), 1)
        tri_ref[...] = jnp.where(r > c, 1.0, 0.0).astype(BF16)
        carry_ref[...] = jnp.zeros_like(carry_ref)

    sc2 = mod_ref[0, 4:5, :]
    sh2 = mod_ref[0, 3:4, :]
    gt1 = mod_ref[0, 2:3, :]

    o = of_ref[0].astype(F32) + ob_ref[0].astype(F32)
    parts = []
    for h in range(HA_HEADS):
        oh = o[:, h * HA_DK:(h + 1) * HA_DK]
        parts.append(oh * lax.rsqrt(jnp.mean(oh * oh, axis=-1, keepdims=True) + RMS_EPS))
    g = g_ref[0].astype(F32)
    oa = jnp.concatenate(parts, axis=-1) * gn_ref[...] * (g * _sigmoid(g))
    ya = _dot(oa.astype(BF16), wa_ref[...])
    yb = _dot(nb_ref[0], wb_ref[...])
    mix = _sigmoid(ga_ref[0].astype(F32)) * ya + _sigmoid(gb_ref[0].astype(F32)) * yb
    mo = _dot(mix.astype(BF16), wo_ref[...])
    x1 = x_ref[0] + gt1 * _rms(mo, gpm_ref[...])
    x1_ref[0] = x1

    xn = _rms(x1, gpf_ref[...]) * (1.0 + sc2) + sh2
    xn_ref[0] = xn.reshape(tm, ROW_SUB, ROW_LANE)
    logits = _dot(xn.astype(BF16), rw_ref[...]) + rb_ref[...]
    lane = lax.broadcasted_iota(jnp.int32, (tm, N_EXPERTS), 1)
    work = logits
    vals, idxs, hots = [], [], []
    for _ in range(TOP_K):
        mv = jnp.max(work, axis=-1, keepdims=True)
        mi = jnp.min(jnp.where(work == mv, lane, N_EXPERTS), axis=-1, keepdims=True)
        hot = lane == mi
        vals.append(mv)
        idxs.append(mi)
        hots.append(hot)
        work = jnp.where(hot, -jnp.inf, work)
    es = [jnp.exp(v - vals[0]) for v in vals]
    den = es[0] + es[1] + es[2] + es[3]

    cnt = sum(jnp.where(h, 1.0, 0.0) for h in hots)
    before = _dot(tri_ref[...], cnt.astype(BF16)) + carry_ref[...]
    carry_ref[...] = carry_ref[...] + jnp.sum(cnt, axis=0, keepdims=True)
    cnt_ref[...] = jnp.broadcast_to(carry_ref[...], cnt_ref.shape)

    lane_o = lax.broadcasted_iota(jnp.int32, (tm, 128), 1)
    ri = jnp.zeros((tm, 128), jnp.int32)
    rg = jnp.zeros((tm, 128), F32)
    for kk in range(TOP_K):
        rank = jnp.sum(jnp.where(hots[kk], before, 0.0), axis=-1, keepdims=True).astype(jnp.int32)
        ri = jnp.where(lane_o == kk, idxs[kk], ri)
        ri = jnp.where(lane_o == TOP_K + kk, rank, ri)
        rg = jnp.where(lane_o == kk, es[kk] / den, rg)
    ri_ref[...] = ri.T[:8, :]
    rg_ref[0] = rg


def _mix(x, mod, proj, o_fw, o_bw, o_b, hgrn_norm, g_post_mix, g_pre_ffn, wa, wb, wo, rw, rb):
    b, t, d = x.shape
    tm = min(MIX_TM, t)
    assert tm == MIX_TM
    row = lambda a: a.reshape(1, -1)
    tok = lambda w, col: pl.BlockSpec((1, tm, w), lambda bi, i: (bi, i, col))
    full = lambda a: pl.BlockSpec(a.shape, lambda bi, i: (0,) * a.ndim)
    gn, gpm, gpf, rbr = row(hgrn_norm), row(g_post_mix), row(g_pre_ffn), row(rb)
    outs = pl.pallas_call(
        _mix_kernel,
        grid=(b, t // tm),
        in_specs=[tok(d, 0),
                  pl.BlockSpec((1, 6, d), lambda bi, i: (bi, 0, 0)),
                  tok(d, 0), tok(d, 0),
                  tok(d, 4),
                  tok(d, COL_GATES // d), tok(d, COL_GATES // d + 1),
                  tok(NA_W, 0),
                  full(gn), full(gpm), full(gpf), full(wa), full(wb), full(wo), full(rw), full(rbr)],
        out_specs=[tok(d, 0), pl.BlockSpec((1, tm, ROW_SUB, ROW_LANE), lambda bi, i: (bi, i, 0, 0)),
                   pl.BlockSpec((8, tm), lambda bi, i: (0, bi * (t // tm) + i)), tok(128, 0),
                   pl.BlockSpec((8, N_EXPERTS), lambda bi, i: (0, 0))],
        out_shape=[jax.ShapeDtypeStruct((b, t, d), F32), jax.ShapeDtypeStruct((b, t, ROW_SUB, ROW_LANE), F32),
                   jax.ShapeDtypeStruct((8, b * t), jnp.int32), jax.ShapeDtypeStruct((b, t, 128), F32),
                   jax.ShapeDtypeStruct((8, N_EXPERTS), F32)],
        scratch_shapes=[pltpu.VMEM((tm, tm), BF16), pltpu.VMEM((1, N_EXPERTS), F32)],
        compiler_params=_cparams(("arbitrary", "arbitrary")),
        name="mix",
    )(x, mod, o_fw, o_bw, proj, proj, proj, o_b, gn, gpm, gpf, wa, wb, wo, rw, rbr)
    return outs


SC_CORES, SC_SUBCORES = 2, 16
SC_ROWS = 32


def _scatter_rows(xn, dest_t, n_rows):
    n_tok = xn.shape[0]
    workers = SC_CORES * SC_SUBCORES
    per_w = n_tok // workers
    n_it = per_w // SC_ROWS
    assert per_w * workers == n_tok and n_it * SC_ROWS == per_w and n_it % 2 == 0
    idx = dest_t.reshape(TOP_K, n_tok // SC_ROWS, SC_ROWS)
    mesh = plsc.VectorSubcoreMesh(core_axis_name="c", subcore_axis_name="s")

    @functools.partial(
        pl.kernel, mesh=mesh,
        out_type=jax.ShapeDtypeStruct((n_rows, ROW_SUB, ROW_LANE), xn.dtype),
        scratch_types=[pltpu.VMEM((2, TOP_K, SC_ROWS), jnp.int32),
                       pltpu.VMEM((2, SC_ROWS, ROW_SUB, ROW_LANE), xn.dtype),
                       pltpu.SemaphoreType.DMA((2,)), pltpu.SemaphoreType.DMA((2,))],
    )
    def scatter_kernel(x_hbm, idx_hbm, out_hbm, idx_v, rows_v, sem_load, sem_store):
        chunk0 = (lax.axis_index("s") * SC_CORES + lax.axis_index("c")) * n_it

        def load(g, slot):
            tok = pl.ds(pl.multiple_of((chunk0 + g) * SC_ROWS, SC_ROWS), SC_ROWS)
            return pltpu.make_async_copy(x_hbm.at[tok], rows_v.at[slot], sem_load.at[slot])

        def start_load(g, slot):
            load(g, slot).start()
            for kk in range(TOP_K):
                pltpu.sync_copy(idx_hbm.at[kk, chunk0 + g], idx_v.at[slot, kk])

        def store(slot, kk):
            return pltpu.make_async_copy(rows_v.at[slot], out_hbm.at[idx_v.at[slot, kk]], sem_store.at[slot])

        def scatter(g, slot):
            load(g, slot).wait()
            for kk in range(TOP_K):
                store(slot, kk).start()
            for kk in range(TOP_K):
                store(slot, kk).wait()

        start_load(0, 0)

        @pl.loop(0, n_it, step=2)
        def _(g):
            start_load(g + 1, 1)
            scatter(g, 0)

            @pl.when(g + 2 < n_it)
            def _():
                start_load(g + 2, 0)

            scatter(g + 1, 1)

    return scatter_kernel(xn, idx)


GU_SLAB = 256


def _gate_up_layout_kernel(w_ref, o_ref):
    half = GU_SLAB // 2
    r = lax.broadcasted_iota(jnp.int32, (GU_SLAB, GU_SLAB), 0)
    c = lax.broadcasted_iota(jnp.int32, (GU_SLAB, GU_SLAB), 1)
    src = jnp.where(c < half, 2 * c, 2 * (c - half) + 1)
    pm = jnp.where(r == src, 1.0, 0.0).astype(BF16)
    for s in range(w_ref.shape[2] // GU_SLAB):
        cols = slice(s * GU_SLAB, (s + 1) * GU_SLAB)
        o_ref[0, :, cols] = _dot(w_ref[0, :, cols].astype(BF16), pm).astype(BF16)


def _gate_up_layout(w_gate_up, b_gate_up):
    e, k, n2 = w_gate_up.shape
    half = GU_SLAB // 2
    tk = 512
    w = pl.pallas_call(
        _gate_up_layout_kernel,
        grid=(e, k // tk),
        in_specs=[pl.BlockSpec((1, tk, n2), lambda i, j: (i, j, 0))],
        out_specs=pl.BlockSpec((1, tk, n2), lambda i, j: (i, j, 0)),
        out_shape=jax.ShapeDtypeStruct((e, k, n2), BF16),
        compiler_params=_cparams(("arbitrary", "arbitrary")),
        name="gate_up_layout",
    )(w_gate_up)
    perm = np.concatenate([s * GU_SLAB + np.concatenate([2 * np.arange(half), 2 * np.arange(half) + 1])
                           for s in range(n2 // GU_SLAB)])
    return w, b_gate_up[:, None, perm]


def _expert_kernel(be_ref, bv_ref, x_ref, wgu_ref, bgu_ref, wd_ref, bd_ref, o_ref, a_ref):
    i = pl.program_id(0)
    valid = bv_ref[i]
    half = GU_SLAB // 2

    @pl.when(valid > 0)
    def _():
        blk = x_ref.shape[0]
        rowi = lax.broadcasted_iota(jnp.int32, (blk, 1), 0)
        x = x_ref[...].reshape(blk, D_MODEL)
        x = jnp.where(rowi < valid, x, 0.0).astype(BF16)
        for s in range(2 * D_FF // GU_SLAB):
            cols = slice(s * GU_SLAB, (s + 1) * GU_SLAB)
            h = _dot(x, wgu_ref[0, :, cols]) + bgu_ref[0, :, cols]
            glu = jnp.minimum(h[:, :half], SWIGLU_LIMIT)
            lin = jnp.clip(h[:, half:], -SWIGLU_LIMIT, SWIGLU_LIMIT)
            a = glu * _sigmoid(SWIGLU_ALPHA * glu) * (lin + 1.0)
            a_ref[:, s * half:(s + 1) * half] = a.astype(BF16)
        o_ref[...] = (_dot(a_ref[...], wd_ref[0]) + bd_ref[0]).reshape(o_ref.shape)

    @pl.when(valid <= 0)
    def _():
        o_ref[...] = jnp.zeros_like(o_ref)


def _experts(xs, block_e, block_valid, wgu, bgu, wd, bd):
    n_rows = xs.shape[0]
    d = D_MODEL
    n_blocks = block_e.shape[0]
    blk = n_rows // n_blocks
    emap = lambda i, be, bv: (be[i], 0, 0)
    rows = lambda: pl.BlockSpec((blk, ROW_SUB, ROW_LANE), lambda i, be, bv: (i, 0, 0))
    return pl.pallas_call(
        _expert_kernel,
        grid_spec=pltpu.PrefetchScalarGridSpec(
            num_scalar_prefetch=2,
            grid=(n_blocks,),
            in_specs=[rows(),
                      pl.BlockSpec((1, d, 2 * D_FF), emap), pl.BlockSpec((1, 1, 2 * D_FF), emap),
                      pl.BlockSpec((1, D_FF, d), emap), pl.BlockSpec((1, 1, d), emap)],
            out_specs=rows(),
            scratch_shapes=[pltpu.VMEM((blk, D_FF), BF16)],
        ),
        out_shape=jax.ShapeDtypeStruct((n_rows, ROW_SUB, ROW_LANE), F32),
        compiler_params=_cparams(("arbitrary",)),
        name="experts",
    )(block_e, block_valid, xs, wgu, bgu, wd, bd)


def _gather_rows(y, dest_flat):
    n_out = dest_flat.shape[0]
    workers = SC_CORES * SC_SUBCORES
    per_w = n_out // workers
    n_it = per_w // SC_ROWS
    assert per_w * workers == n_out and n_it * SC_ROWS == per_w and n_it % 2 == 0
    mesh = plsc.VectorSubcoreMesh(core_axis_name="c", subcore_axis_name="s")

    @functools.partial(
        pl.kernel, mesh=mesh,
        out_type=jax.ShapeDtypeStruct((n_out, ROW_SUB, ROW_LANE), y.dtype),
        scratch_types=[pltpu.VMEM((2, SC_ROWS), jnp.int32),
                       pltpu.VMEM((2, SC_ROWS, ROW_SUB, ROW_LANE), y.dtype),
                       pltpu.SemaphoreType.DMA((2,))],
    )
    def gather_kernel(y_hbm, idx_hbm, out_hbm, idx_v, rows_v, sem):
        base = (lax.axis_index("s") * SC_CORES + lax.axis_index("c")) * per_w

        def span(g):
            return pl.ds(pl.multiple_of(base + g * SC_ROWS, SC_ROWS), SC_ROWS)

        def row_gather(slot):
            return pltpu.make_async_copy(y_hbm.at[idx_v.at[slot]], rows_v.at[slot], sem.at[slot])

        def start(g, slot):
            pltpu.sync_copy(idx_hbm.at[span(g)], idx_v.at[slot])
            row_gather(slot).start()

        start(0, 0)

        @pl.loop(0, n_it, step=2)
        def _(g):
            start(g + 1, 1)
            row_gather(0).wait()
            pltpu.sync_copy(rows_v.at[0], out_hbm.at[span(g)])

            @pl.when(g + 2 < n_it)
            def _():
                start(g + 2, 0)

            row_gather(1).wait()
            pltpu.sync_copy(rows_v.at[1], out_hbm.at[span(g + 1)])

    return gather_kernel(y, dest_flat)


def _combine_kernel(yg_ref, x1_ref, mod_ref, rg_ref, gpo_ref, o_ref):
    rg = rg_ref[...]
    h = rg[:, 0:1] * yg_ref[0].reshape(COMB_TOK, D_MODEL)
    for kk in range(1, TOP_K):
        h = h + rg[:, kk:kk + 1] * yg_ref[kk].reshape(COMB_TOK, D_MODEL)
    gt2 = mod_ref[0, 5:6, :]
    o_ref[...] = x1_ref[...] + gt2 * _rms(h, gpo_ref[...])


def _combine(yg, x1, mod, rg, g_post_ffn, tokens_per_batch):
    n, d = x1.shape
    per_b = tokens_per_batch // COMB_TOK
    gpo = g_post_ffn.reshape(1, d)
    return pl.pallas_call(
        _combine_kernel,
        grid=(n // COMB_TOK,),
        in_specs=[pl.BlockSpec((TOP_K, COMB_TOK, ROW_SUB, ROW_LANE), lambda i: (0, i, 0, 0)),
                  pl.BlockSpec((COMB_TOK, d), lambda i: (i, 0)),
                  pl.BlockSpec((1, 6, d), lambda i: (i // per_b, 0, 0)),
                  pl.BlockSpec((COMB_TOK, 128), lambda i: (i, 0)),
                  pl.BlockSpec((1, d), lambda i: (0, 0))],
        out_specs=pl.BlockSpec((COMB_TOK, d), lambda i: (i, 0)),
        out_shape=jax.ShapeDtypeStruct((n, d), F32),
        compiler_params=_cparams(("arbitrary",)),
        name="combine",
    )(yg, x1, mod, rg, gpo)


def _moe_plan(ri_t, counts, n_tok):
    idx = ri_t[:TOP_K]
    rank = ri_t[TOP_K:2 * TOP_K]
    cnt = counts.astype(jnp.int32)
    blk = MOE_BLK_LARGE if n_tok * TOP_K >= 8 * MOE_BLK_LARGE * N_EXPERTS else MOE_BLK_SMALL
    padded = (cnt + blk - 1) // blk * blk
    pend = jnp.cumsum(padded)
    pstart = pend - padded
    dest = rank
    for e in range(N_EXPERTS):
        dest = dest + jnp.where(idx == e, pstart[e], 0)
    n_blocks = (n_tok * TOP_K + blk - 1) // blk + N_EXPERTS
    bstart = jnp.arange(n_blocks, dtype=jnp.int32) * blk
    block_e = jnp.minimum(jnp.sum((bstart[:, None] >= pend[None, :]).astype(jnp.int32), axis=1), N_EXPERTS - 1)
    block_valid = jnp.clip(pstart[block_e] + cnt[block_e] - bstart, 0, blk).astype(jnp.int32)
    return dest.astype(jnp.int32), block_e, block_valid, n_blocks * blk


def _layer(x, mod, p):
    b, t, d = x.shape
    n = b * t
    proj = _in_proj(x, mod, p['g_pre_mix'], p['w_in'])
    o_fw, o_bw = _hgrn(proj, p['lb_logits'])
    o_b = _natten(proj, p['na_tabs'])
    x1, xn, ri, rg, counts = _mix(x, mod, proj, o_fw, o_bw, o_b, p['hgrn_norm'], p['g_post_mix'], p['g_pre_ffn'],
                                  p['w_branch_a'], p['w_branch_b'], p['w_out'], p['router_w'], p['router_b'])
    rg = rg.reshape(n, 128)
    dest, block_e, block_valid, n_rows = _moe_plan(ri, counts[0], n)
    xs = _scatter_rows(xn.reshape(n, ROW_SUB, ROW_LANE), dest, n_rows)
    y = _experts(xs, block_e, block_valid, p['w_gu'], p['b_gu'], p['w_d'], p['b_d'])
    yg = _gather_rows(y, dest.reshape(-1)).reshape(TOP_K, n, ROW_SUB, ROW_LANE)
    out = _combine(yg, x1.reshape(n, d), mod, rg, p['g_post_ffn'], t)
    return out.reshape(b, t, d)


def kernel(x_prompt, x_sample, c_prompt, c_sample, ada_w, ada_b, g_pre_mix, g_post_mix, g_pre_ffn, g_post_ffn,
           w_in, hgrn_lb_logits, hgrn_norm, na_rpb, w_branch_a, w_branch_b, w_out, router_w, router_b,
           w_gate_up, b_gate_up, w_down, b_down):
    d = D_MODEL
    bp, bs = c_prompt.shape[0], c_sample.shape[0]
    pad = (-(bp + bs)) % 8
    c_all = jnp.concatenate([c_prompt, c_sample, jnp.zeros((pad, d), F32)], axis=0)
    mod = _ada_mod(c_all, ada_w[0], ada_b[0]).reshape(-1, 6, d)

    w = w_in[0]
    w_perm = jnp.concatenate([w[:, :5 * d], w[:, 5 * d + 3 * NA_W:], w[:, 5 * d:5 * d + 3 * NA_W]], axis=1).astype(BF16)
    w_gu, b_gu = _gate_up_layout(w_gate_up[0], b_gate_up[0])
    p = dict(
        g_pre_mix=g_pre_mix[0], g_post_mix=g_post_mix[0], g_pre_ffn=g_pre_ffn[0], g_post_ffn=g_post_ffn[0],
        w_in=w_perm, lb_logits=hgrn_lb_logits.astype(F32), hgrn_norm=hgrn_norm[0],
        na_tabs=_na_bias_tables(na_rpb[0]),
        w_branch_a=w_branch_a[0].astype(BF16), w_branch_b=w_branch_b[0].astype(BF16), w_out=w_out[0].astype(BF16),
        router_w=router_w[0].astype(BF16), router_b=router_b[0],
        w_gu=w_gu, b_gu=b_gu,
        w_d=w_down[0].astype(BF16), b_d=b_down[0][:, None, :],
    )
    y_p = _layer(x_prompt, mod[:bp], p)
    y_s = _layer(x_sample, mod[bp:bp + bs], p)
    return (y_p, y_s)
```

```python
import functools

import numpy as np
import jax
import jax.numpy as jnp
from jax import lax
from jax.experimental import pallas as pl
from jax.experimental.pallas import tpu as pltpu
from jax.experimental.pallas import tpu_sc as plsc

F32 = jnp.float32
BF16 = jnp.bfloat16

D_MODEL = 1024
GRID_W = 64
HA_HEADS = 8
HA_DK = 128
NA_HEADS = 8
NA_DH = 64
NA_W = NA_HEADS * NA_DH
NA_KH = 8
NA_KW = 16
N_EXPERTS = 32
TOP_K = 4
D_FF = 1024
SWIGLU_ALPHA = 1.702
SWIGLU_LIMIT = 7.0
RMS_EPS = 1e-6
NEG_BIG = -1e30

COL_A = 0
COL_GATES = 5 * 1024
COL_NA = 7 * 1024
PROJ_COLS = 7 * 1024 + 3 * NA_W

HGRN_CHUNK = 64
HGRN_TILE = 256
PROJ_TM = 1024
PROJ_TN = 2176
MIX_TM = 512
MOE_BLK_SMALL, MOE_BLK_LARGE = 512, 1024
COMB_TOK = 512
SUBLANES = 8
ROW_SUB, ROW_LANE = 8, 128
VMEM_LIMIT = 56 * 1024 * 1024


def _cparams(sem):
    return pltpu.CompilerParams(dimension_semantics=sem, vmem_limit_bytes=VMEM_LIMIT)


def _dot(a, b):
    return jnp.dot(a, b, preferred_element_type=F32)


def _dot_nt(a, b):
    return lax.dot_general(a, b, (((1,), (1,)), ((), ())), preferred_element_type=F32)


def _dot_tn(a, b):
    return lax.dot_general(a, b, (((0,), (0,)), ((), ())), preferred_element_type=F32)


def _split(a):
    hi = a.astype(BF16)
    lo = (a - hi.astype(F32)).astype(BF16)
    return hi, lo


def _dot3(a, w):
    ah, al = _split(a)
    wh, wl = _split(w)
    return _dot(ah, wh) + (_dot(al, wh) + _dot(ah, wl))


def _sigmoid(x):
    return 0.5 * jnp.tanh(0.5 * x) + 0.5


def _rms(x, g):
    return x * lax.rsqrt(jnp.mean(x * x, axis=-1, keepdims=True) + RMS_EPS) * g


def _ada_kernel(c_ref, w_ref, b_ref, o_ref):
    c = c_ref[...]
    o_ref[...] = _dot3(c * _sigmoid(c), w_ref[...]) + b_ref[...]


def _ada_mod(c, ada_w, ada_b):
    bp, d = c.shape
    n = ada_w.shape[1]
    tn = 1536
    return pl.pallas_call(
        _ada_kernel,
        grid=(n // tn,),
        in_specs=[pl.BlockSpec((bp, d), lambda j: (0, 0)),
                  pl.BlockSpec((d, tn), lambda j: (0, j)),
                  pl.BlockSpec((1, tn), lambda j: (0, j))],
        out_specs=pl.BlockSpec((bp, tn), lambda j: (0, j)),
        out_shape=jax.ShapeDtypeStruct((bp, n), F32),
        compiler_params=_cparams(("arbitrary",)),
        name="ada_mod",
    )(c, ada_w, ada_b.reshape(1, n))


def _in_proj_kernel(x_ref, mod_ref, g_ref, w_ref, o_ref, xn_ref):
    @pl.when(pl.program_id(2) == 0)
    def _():
        sh = mod_ref[0, 0:1, :]
        sc = mod_ref[0, 1:2, :]
        xn = _rms(x_ref[0], g_ref[...]) * (1.0 + sc) + sh
        xn_ref[...] = xn.astype(BF16)

    o_ref[0] = _dot(xn_ref[...], w_ref[...]).astype(BF16)


def _in_proj(x, mod, g_pre, w_in_bf):
    b, t, d = x.shape
    tm = min(PROJ_TM, t)
    n = w_in_bf.shape[1]
    return pl.pallas_call(
        _in_proj_kernel,
        grid=(b, t // tm, n // PROJ_TN),
        in_specs=[pl.BlockSpec((1, tm, d), lambda bi, i, j: (bi, i, 0)),
                  pl.BlockSpec((1, 6, d), lambda bi, i, j: (bi, 0, 0)),
                  pl.BlockSpec((1, d), lambda bi, i, j: (0, 0)),
                  pl.BlockSpec((d, PROJ_TN), lambda bi, i, j: (0, j))],
        out_specs=pl.BlockSpec((1, tm, PROJ_TN), lambda bi, i, j: (bi, i, j)),
        out_shape=jax.ShapeDtypeStruct((b, t, n), BF16),
        scratch_shapes=[pltpu.VMEM((tm, d), BF16)],
        compiler_params=_cparams(("arbitrary", "arbitrary", "arbitrary")),
        name="in_proj",
    )(x, mod, g_pre.reshape(1, d), w_in_bf)


def _cumsum_rows(x, reverse):
    groups = x.shape[0] // SUBLANES
    row = lax.broadcasted_iota(jnp.int32, (SUBLANES, x.shape[1]), 0)
    out = [None] * groups
    carry = None
    for g in (range(groups - 1, -1, -1) if reverse else range(groups)):
        blk = x[g * SUBLANES:(g + 1) * SUBLANES]
        step = 1
        while step < SUBLANES:
            if reverse:
                blk = blk + jnp.where(row < SUBLANES - step, pltpu.roll(blk, SUBLANES - step, axis=0), 0.0)
            else:
                blk = blk + jnp.where(row >= step, pltpu.roll(blk, step, axis=0), 0.0)
            step *= 2
        if carry is not None:
            blk = blk + carry
        carry = blk[0:1] if reverse else blk[SUBLANES - 1:SUBLANES]
        out[g] = blk
    return jnp.concatenate(out, axis=0)


def _hgrn_dir(q_ref, f_ref, i_ref, o_ref, s_ref, sn_ref, lbd, row0, reverse):
    c = HGRN_CHUNK
    rows = pl.ds(row0, c)
    q = q_ref[0, rows, :].astype(F32)
    fpre = f_ref[0, rows, :].astype(F32)
    v = i_ref[0, rows, :]
    f = 0.5 * (1.0 + lbd) + (0.5 * (1.0 - lbd)) * jnp.tanh(0.5 * fpre)
    logf = jnp.log(f)
    k = 1.0 - f
    ri = lax.broadcasted_iota(jnp.int32, (c, c), 0)
    ci = lax.broadcasted_iota(jnp.int32, (c, c), 1)
    keep = (ci >= ri) if reverse else (ri >= ci)
    b = _cumsum_rows(logf, reverse)
    if reverse:
        b_mid = b[c // 2:c // 2 + 1, :]
        b_end = b[0:1, :]
    else:
        b_mid = b[c // 2 - 1:c // 2, :]
        b_end = b[c - 1:c, :]
    d = b - b_mid
    qm = q * jnp.exp(d)
    km = k * jnp.exp(-d)
    qi = (qm * jnp.exp(b_mid)).astype(BF16)
    ks = (km * jnp.exp(b_end - b_mid)).astype(BF16)
    qm = qm.astype(BF16)
    km = km.astype(BF16)
    dec = jnp.exp(b_end)
    for h in range(HA_HEADS):
        sl = slice(h * HA_DK, (h + 1) * HA_DK)
        a = _dot_nt(qm[:, sl], km[:, sl])
        a = jnp.where(keep, a, 0.0).astype(BF16)
        lhs = jnp.concatenate([qi[:, sl], a], axis=1)
        rhs = jnp.concatenate([sn_ref[h], v[:, sl]], axis=0)
        o_ref[0, rows, sl] = _dot(lhs, rhs).astype(BF16)
        st = s_ref[h] * dec[:, sl] + _dot_tn(v[:, sl], ks[:, sl])
        s_ref[h] = st
        sn_ref[h] = st.astype(BF16).T


def _hgrn_tile(lbl_ref, qf_ref, ff_ref, if_ref, qb_ref, fb_ref, ib_ref, of_ref, ob_ref,
               sf_ref, sb_ref, snf_ref, snb_ref):
    @pl.when(pl.program_id(1) == 0)
    def _():
        for ref in (sf_ref, sb_ref, snf_ref, snb_ref):
            ref[...] = jnp.zeros_like(ref)

    l0 = lbl_ref[0]
    l1 = lbl_ref[1]
    m = jnp.maximum(l0, l1)
    e0 = jnp.exp(l0 - m)
    e1 = jnp.exp(l1 - m)
    lb = e0 / (e0 + e1)
    n_chunks = HGRN_TILE // HGRN_CHUNK

    def chunk(ci):
        _hgrn_dir(qf_ref, ff_ref, if_ref, of_ref, sf_ref, snf_ref, lb[0:1, :], ci * HGRN_CHUNK, False)
        _hgrn_dir(qb_ref, fb_ref, ib_ref, ob_ref, sb_ref, snb_ref, lb[1:2, :],
                  (n_chunks - 1 - ci) * HGRN_CHUNK, True)

    return chunk


def _mixers_kernel(lbl_ref, qf_ref, ff_ref, if_ref, qb_ref, fb_ref, ib_ref, nq_ref, nk_ref, nv_ref, tab_ref,
                   of_ref, ob_ref, on_ref, sf_ref, sb_ref, snf_ref, snb_ref, *, grid_rows):
    chunk = _hgrn_tile(lbl_ref, qf_ref, ff_ref, if_ref, qb_ref, fb_ref, ib_ref, of_ref, ob_ref,
                       sf_ref, sb_ref, snf_ref, snb_ref)
    na_row = _na_rows(nq_ref, nk_ref, nv_ref, tab_ref, on_ref, grid_rows)
    for ci in range(HGRN_TILE // HGRN_CHUNK):
        chunk(ci)
        na_row(ci)


def _mixers(proj, lb_logits, tabs):
    b, t, _ = proj.shape
    tt = HGRN_TILE
    nt = t // tt
    d = D_MODEL
    cq, ck, cv = (COL_NA // NA_W, COL_NA // NA_W + 1, COL_NA // NA_W + 2)

    def fwd(col):
        return pl.BlockSpec((1, tt, d), lambda bi, ti: (bi, ti, col))

    def bwd(col):
        return pl.BlockSpec((1, tt, d), lambda bi, ti: (bi, nt - 1 - ti, col))

    return pl.pallas_call(
        functools.partial(_mixers_kernel, grid_rows=t // GRID_W),
        grid=(b, nt),
        in_specs=[pl.BlockSpec((2, 2, d), lambda bi, ti: (0, 0, 0)),
                  fwd(0), fwd(1), fwd(3), bwd(0), bwd(2), bwd(3),
                  pl.BlockSpec((1, tt, NA_W), lambda bi, ti: (bi, ti, cq)),
                  pl.BlockSpec((1, t, NA_W), lambda bi, ti: (bi, 0, ck), pipeline_mode=pl.Buffered(1)),
                  pl.BlockSpec((1, t, NA_W), lambda bi, ti: (bi, 0, cv), pipeline_mode=pl.Buffered(1)),
                  pl.BlockSpec(tabs.shape, lambda bi, ti: (0, 0, 0, 0), pipeline_mode=pl.Buffered(1))],
        out_specs=[pl.BlockSpec((1, tt, d), lambda bi, ti: (bi, ti, 0)),
                   pl.BlockSpec((1, tt, d), lambda bi, ti: (bi, nt - 1 - ti, 0)),
                   pl.BlockSpec((1, tt, NA_W), lambda bi, ti: (bi, ti, 0))],
        out_shape=[jax.ShapeDtypeStruct((b, t, d), BF16), jax.ShapeDtypeStruct((b, t, d), BF16),
                   jax.ShapeDtypeStruct((b, t, NA_W), BF16)],
        scratch_shapes=[pltpu.VMEM((HA_HEADS, HA_DK, HA_DK), F32), pltpu.VMEM((HA_HEADS, HA_DK, HA_DK), F32),
                        pltpu.VMEM((HA_HEADS, HA_DK, HA_DK), BF16), pltpu.VMEM((HA_HEADS, HA_DK, HA_DK), BF16)],
        compiler_params=_cparams(("arbitrary", "arbitrary")),
        name="mixers",
    )(lb_logits, proj, proj, proj, proj, proj, proj, proj, proj, proj, tabs)


def _na_bias_tables(rpb):
    q = np.arange(GRID_W)[:, None]
    kc = np.arange(GRID_W)[None, :]
    c0 = np.clip(q - NA_KW // 2, 0, GRID_W - NA_KW)
    ok = (kc >= c0) & (kc < c0 + NA_KW)
    dc = np.clip(kc - q + (NA_KW - 1), 0, 2 * NA_KW - 2)
    base = jnp.where(ok[None, None], jnp.take(rpb.astype(F32), dc, axis=2), NEG_BIG)
    tabs = []
    for dr0 in range(NA_KH):
        w = base[:, dr0:dr0 + NA_KH]
        w = w.transpose(0, 2, 1, 3).reshape(NA_HEADS // 2, 2 * GRID_W, NA_KH * GRID_W)
        tabs.append(w)
    return jnp.stack(tabs, axis=1)


def _na_rows(q_ref, k_ref, v_ref, tab_ref, o_ref, grid_rows):
    g = pl.program_id(1)
    lane = lax.broadcasted_iota(jnp.int32, (GRID_W, 2 * NA_DH), 1)
    low = lane < NA_DH
    scale = NA_DH ** -0.5
    rows_per_step = HGRN_TILE // GRID_W

    def na_row(rr):
        r = g * rows_per_step + rr
        r0 = jnp.clip(r - NA_KH // 2, 0, grid_rows - NA_KH)
        dr0 = r0 - r + (NA_KH - 1)
        qrow = pl.ds(rr * GRID_W, GRID_W)
        krow = pl.ds(pl.multiple_of(r0 * GRID_W, GRID_W), NA_KH * GRID_W)
        for j in range(NA_HEADS // 2):
            sl = slice(j * 2 * NA_DH, (j + 1) * 2 * NA_DH)
            qp = q_ref[0, qrow, sl]
            zero = jnp.zeros_like(qp)
            ql = jnp.concatenate([jnp.where(low, qp, zero), jnp.where(low, zero, qp)], axis=0)
            s = _dot_nt(ql, k_ref[0, krow, sl]) * scale + tab_ref[j, dr0]
            m = jnp.max(s, axis=-1, keepdims=True)
            p = jnp.exp(s - m)
            l = jnp.sum(p, axis=-1, keepdims=True)
            pv = _dot(p.astype(BF16), v_ref[0, krow, sl]) / l
            o_ref[0, qrow, sl] = jnp.where(low, pv[:GRID_W], pv[GRID_W:]).astype(BF16)

    return na_row


def _mix_kernel(x_ref, mod_ref, of_ref, ob_ref, g_ref, ga_ref, gb_ref, nb_ref,
                gn_ref, gpm_ref, gpf_ref, wa_ref, wb_ref, wo_ref, rw_ref, rb_ref,
                x1_ref, xn_ref, ri_ref, rg_ref, cnt_ref, tri_ref, carry_ref):
    tm = MIX_TM
    first = (pl.program_id(0) == 0) & (pl.program_id(1) == 0)

    @pl.when(first)
    def _():
        r = lax.broadcasted_iota(jnp.int32, (tm, tm), 0)
        c = lax.broadcasted_iota(jnp.int32, (tm, tm), 1)
        tri_ref[...] = jnp.where(r > c, 1.0, 0.0).astype(BF16)
        carry_ref[...] = jnp.zeros_like(carry_ref)

    sc2 = mod_ref[0, 4:5, :]
    sh2 = mod_ref[0, 3:4, :]
    gt1 = mod_ref[0, 2:3, :]

    o = of_ref[0].astype(F32) + ob_ref[0].astype(F32)
    parts = []
    for h in range(HA_HEADS):
        oh = o[:, h * HA_DK:(h + 1) * HA_DK]
        parts.append(oh * lax.rsqrt(jnp.mean(oh * oh, axis=-1, keepdims=True) + RMS_EPS))
    g = g_ref[0].astype(F32)
    oa = jnp.concatenate(parts, axis=-1) * gn_ref[...] * (g * _sigmoid(g))
    ya = _dot(oa.astype(BF16), wa_ref[...])
    yb = _dot(nb_ref[0], wb_ref[...])
    mix = _sigmoid(ga_ref[0].astype(F32)) * ya + _sigmoid(gb_ref[0].astype(F32)) * yb
    mo = _dot(mix.astype(BF16), wo_ref[...])
    x1 = x_ref[0] + gt1 * _rms(mo, gpm_ref[...])
    x1_ref[0] = x1

    xn = _rms(x1, gpf_ref[...]) * (1.0 + sc2) + sh2
    xn_ref[0] = xn.reshape(tm, ROW_SUB, ROW_LANE)
    logits = _dot(xn.astype(BF16), rw_ref[...]) + rb_ref[...]
    lane = lax.broadcasted_iota(jnp.int32, (tm, N_EXPERTS), 1)
    work = logits
    vals, idxs, hots = [], [], []
    for _ in range(TOP_K):
        mv = jnp.max(work, axis=-1, keepdims=True)
        mi = jnp.min(jnp.where(work == mv, lane, N_EXPERTS), axis=-1, keepdims=True)
        hot = lane == mi
        vals.append(mv)
        idxs.append(mi)
        hots.append(hot)
        work = jnp.where(hot, -jnp.inf, work)
    es = [jnp.exp(v - vals[0]) for v in vals]
    den = es[0] + es[1] + es[2] + es[3]

    cnt = sum(jnp.where(h, 1.0, 0.0) for h in hots)
    before = _dot(tri_ref[...], cnt.astype(BF16)) + carry_ref[...]
    carry_ref[...] = carry_ref[...] + jnp.sum(cnt, axis=0, keepdims=True)
    cnt_ref[...] = jnp.broadcast_to(carry_ref[...], cnt_ref.shape)

    lane_o = lax.broadcasted_iota(jnp.int32, (tm, 128), 1)
    ri = jnp.zeros((tm, 128), jnp.int32)
    rg = jnp.zeros((tm, 128), F32)
    for kk in range(TOP_K):
        rank = jnp.sum(jnp.where(hots[kk], before, 0.0), axis=-1, keepdims=True).astype(jnp.int32)
        ri = jnp.where(lane_o == kk, idxs[kk], ri)
        ri = jnp.where(lane_o == TOP_K + kk, rank, ri)
        rg = jnp.where(lane_o == kk, es[kk] / den, rg)
    ri_ref[...] = ri.T[:8, :]
    rg_ref[0] = rg


def _mix(x, mod, proj, o_fw, o_bw, o_b, hgrn_norm, g_post_mix, g_pre_ffn, wa, wb, wo, rw, rb):
    b, t, d = x.shape
    tm = min(MIX_TM, t)
    assert tm == MIX_TM
    row = lambda a: a.reshape(1, -1)
    tok = lambda w, col: pl.BlockSpec((1, tm, w), lambda bi, i: (bi, i, col))
    full = lambda a: pl.BlockSpec(a.shape, lambda bi, i: (0,) * a.ndim)
    gn, gpm, gpf, rbr = row(hgrn_norm), row(g_post_mix), row(g_pre_ffn), row(rb)
    outs = pl.pallas_call(
        _mix_kernel,
        grid=(b, t // tm),
        in_specs=[tok(d, 0),
                  pl.BlockSpec((1, 6, d), lambda bi, i: (bi, 0, 0)),
                  tok(d, 0), tok(d, 0),
                  tok(d, 4),
                  tok(d, COL_GATES // d), tok(d, COL_GATES // d + 1),
                  tok(NA_W, 0),
                  full(gn), full(gpm), full(gpf), full(wa), full(wb), full(wo), full(rw), full(rbr)],
        out_specs=[tok(d, 0), pl.BlockSpec((1, tm, ROW_SUB, ROW_LANE), lambda bi, i: (bi, i, 0, 0)),
                   pl.BlockSpec((8, tm), lambda bi, i: (0, bi * (t // tm) + i)), tok(128, 0),
                   pl.BlockSpec((8, N_EXPERTS), lambda bi, i: (0, 0))],
        out_shape=[jax.ShapeDtypeStruct((b, t, d), F32), jax.ShapeDtypeStruct((b, t, ROW_SUB, ROW_LANE), F32),
                   jax.ShapeDtypeStruct((8, b * t), jnp.int32), jax.ShapeDtypeStruct((b, t, 128), F32),
                   jax.ShapeDtypeStruct((8, N_EXPERTS), F32)],
        scratch_shapes=[pltpu.VMEM((tm, tm), BF16), pltpu.VMEM((1, N_EXPERTS), F32)],
        compiler_params=_cparams(("arbitrary", "arbitrary")),
        name="mix",
    )(x, mod, o_fw, o_bw, proj, proj, proj, o_b, gn, gpm, gpf, wa, wb, wo, rw, rbr)
    return outs


SC_CORES, SC_SUBCORES = 2, 16
SC_ROWS = 32


def _scatter_rows(xn, dest_t, n_rows):
    n_tok = xn.shape[0]
    workers = SC_CORES * SC_SUBCORES
    per_w = n_tok // workers
    n_it = per_w // SC_ROWS
    assert per_w * workers == n_tok and n_it * SC_ROWS == per_w and n_it % 2 == 0
    idx = dest_t.reshape(TOP_K, n_tok // SC_ROWS, SC_ROWS)
    mesh = plsc.VectorSubcoreMesh(core_axis_name="c", subcore_axis_name="s")

    @functools.partial(
        pl.kernel, mesh=mesh,
        out_type=jax.ShapeDtypeStruct((n_rows, ROW_SUB, ROW_LANE), xn.dtype),
        scratch_types=[pltpu.VMEM((2, TOP_K, SC_ROWS), jnp.int32),
                       pltpu.VMEM((2, SC_ROWS, ROW_SUB, ROW_LANE), xn.dtype),
                       pltpu.SemaphoreType.DMA((2,)), pltpu.SemaphoreType.DMA((2,))],
    )
    def scatter_kernel(x_hbm, idx_hbm, out_hbm, idx_v, rows_v, sem_load, sem_store):
        chunk0 = (lax.axis_index("s") * SC_CORES + lax.axis_index("c")) * n_it

        def load(g, slot):
            tok = pl.ds(pl.multiple_of((chunk0 + g) * SC_ROWS, SC_ROWS), SC_ROWS)
            return pltpu.make_async_copy(x_hbm.at[tok], rows_v.at[slot], sem_load.at[slot])

        def start_load(g, slot):
            load(g, slot).start()
            for kk in range(TOP_K):
                pltpu.sync_copy(idx_hbm.at[kk, chunk0 + g], idx_v.at[slot, kk])

        def store(slot, kk):
            return pltpu.make_async_copy(rows_v.at[slot], out_hbm.at[idx_v.at[slot, kk]], sem_store.at[slot])

        def scatter(g, slot):
            load(g, slot).wait()
            for kk in range(TOP_K):
                store(slot, kk).start()
            for kk in range(TOP_K):
                store(slot, kk).wait()

        start_load(0, 0)

        @pl.loop(0, n_it, step=2)
        def _(g):
            start_load(g + 1, 1)
            scatter(g, 0)

            @pl.when(g + 2 < n_it)
            def _():
                start_load(g + 2, 0)

            scatter(g + 1, 1)

    return scatter_kernel(xn, idx)


GU_SLAB = 256


def _gate_up_layout_kernel(w_ref, o_ref):
    half = GU_SLAB // 2
    r = lax.broadcasted_iota(jnp.int32, (GU_SLAB, GU_SLAB), 0)
    c = lax.broadcasted_iota(jnp.int32, (GU_SLAB, GU_SLAB), 1)
    src = jnp.where(c < half, 2 * c, 2 * (c - half) + 1)
    pm = jnp.where(r == src, 1.0, 0.0).astype(BF16)
    for s in range(w_ref.shape[2] // GU_SLAB):
        cols = slice(s * GU_SLAB, (s + 1) * GU_SLAB)
        o_ref[0, :, cols] = _dot(w_ref[0, :, cols].astype(BF16), pm).astype(BF16)


def _gate_up_layout(w_gate_up, b_gate_up):
    e, k, n2 = w_gate_up.shape
    half = GU_SLAB // 2
    tk = 512
    w = pl.pallas_call(
        _gate_up_layout_kernel,
        grid=(e, k // tk),
        in_specs=[pl.BlockSpec((1, tk, n2), lambda i, j: (i, j, 0))],
        out_specs=pl.BlockSpec((1, tk, n2), lambda i, j: (i, j, 0)),
        out_shape=jax.ShapeDtypeStruct((e, k, n2), BF16),
        compiler_params=_cparams(("arbitrary", "arbitrary")),
        name="gate_up_layout",
    )(w_gate_up)
    perm = np.concatenate([s * GU_SLAB + np.concatenate([2 * np.arange(half), 2 * np.arange(half) + 1])
                           for s in range(n2 // GU_SLAB)])
    return w, b_gate_up[:, None, perm]


def _expert_kernel(be_ref, bv_ref, x_ref, wgu_ref, bgu_ref, wd_ref, bd_ref, o_ref, a_ref):
    i = pl.program_id(0)
    valid = bv_ref[i]
    half = GU_SLAB // 2

    @pl.when(valid > 0)
    def _():
        blk = x_ref.shape[0]
        rowi = lax.broadcasted_iota(jnp.int32, (blk, 1), 0)
        x = x_ref[...].reshape(blk, D_MODEL)
        x = jnp.where(rowi < valid, x, 0.0).astype(BF16)
        for s in range(2 * D_FF // GU_SLAB):
            cols = slice(s * GU_SLAB, (s + 1) * GU_SLAB)
            h = _dot(x, wgu_ref[0, :, cols]) + bgu_ref[0, :, cols]
            glu = jnp.minimum(h[:, :half], SWIGLU_LIMIT)
            lin = jnp.clip(h[:, half:], -SWIGLU_LIMIT, SWIGLU_LIMIT)
            a = glu * _sigmoid(SWIGLU_ALPHA * glu) * (lin + 1.0)
            a_ref[:, s * half:(s + 1) * half] = a.astype(BF16)
        o_ref[...] = (_dot(a_ref[...], wd_ref[0]) + bd_ref[0]).reshape(o_ref.shape)

    @pl.when(valid <= 0)
    def _():
        o_ref[...] = jnp.zeros_like(o_ref)


def _experts(xs, block_e, block_valid, wgu, bgu, wd, bd):
    n_rows = xs.shape[0]
    d = D_MODEL
    n_blocks = block_e.shape[0]
    blk = n_rows // n_blocks
    emap = lambda i, be, bv: (be[i], 0, 0)
    rows = lambda: pl.BlockSpec((blk, ROW_SUB, ROW_LANE), lambda i, be, bv: (i, 0, 0))
    return pl.pallas_call(
        _expert_kernel,
        grid_spec=pltpu.PrefetchScalarGridSpec(
            num_scalar_prefetch=2,
            grid=(n_blocks,),
            in_specs=[rows(),
                      pl.BlockSpec((1, d, 2 * D_FF), emap), pl.BlockSpec((1, 1, 2 * D_FF), emap),
                      pl.BlockSpec((1, D_FF, d), emap), pl.BlockSpec((1, 1, d), emap)],
            out_specs=rows(),
            scratch_shapes=[pltpu.VMEM((blk, D_FF), BF16)],
        ),
        out_shape=jax.ShapeDtypeStruct((n_rows, ROW_SUB, ROW_LANE), F32),
        compiler_params=_cparams(("arbitrary",)),
        name="experts",
    )(block_e, block_valid, xs, wgu, bgu, wd, bd)


def _gather_rows(y, dest_flat):
    n_out = dest_flat.shape[0]
    workers = SC_CORES * SC_SUBCORES
    per_w = n_out // workers
    n_it = per_w // SC_ROWS
    assert per_w * workers == n_out and n_it * SC_ROWS == per_w and n_it % 2 == 0
    mesh = plsc.VectorSubcoreMesh(core_axis_name="c", subcore_axis_name="s")

    @functools.partial(
        pl.kernel, mesh=mesh,
        out_type=jax.ShapeDtypeStruct((n_out, ROW_SUB, ROW_LANE), y.dtype),
        scratch_types=[pltpu.VMEM((2, SC_ROWS), jnp.int32),
                       pltpu.VMEM((2, SC_ROWS, ROW_SUB, ROW_LANE), y.dtype),
                       pltpu.SemaphoreType.DMA((2,))],
    )
    def gather_kernel(y_hbm, idx_hbm, out_hbm, idx_v, rows_v, sem):
        base = (lax.axis_index("s") * SC_CORES + lax.axis_index("c")) * per_w

        def span(g):
            return pl.ds(pl.multiple_of(base + g * SC_ROWS, SC_ROWS), SC_ROWS)

        def row_gather(slot):
            return pltpu.make_async_copy(y_hbm.at[idx_v.at[slot]], rows_v.at[slot], sem.at[slot])

        def start(g, slot):
            pltpu.sync_copy(idx_hbm.at[span(g)], idx_v.at[slot])
            row_gather(slot).start()

        start(0, 0)

        @pl.loop(0, n_it, step=2)
        def _(g):
            start(g + 1, 1)
            row_gather(0).wait()
            pltpu.sync_copy(rows_v.at[0], out_hbm.at[span(g)])

            @pl.when(g + 2 < n_it)
            def _():
                start(g + 2, 0)

            row_gather(1).wait()
            pltpu.sync_copy(rows_v.at[1], out_hbm.at[span(g + 1)])

    return gather_kernel(y, dest_flat)


def _combine_kernel(yg_ref, x1_ref, mod_ref, rg_ref, gpo_ref, o_ref):
    rg = rg_ref[...]
    h = rg[:, 0:1] * yg_ref[0].reshape(COMB_TOK, D_MODEL)
    for kk in range(1, TOP_K):
        h = h + rg[:, kk:kk + 1] * yg_ref[kk].reshape(COMB_TOK, D_MODEL)
    gt2 = mod_ref[0, 5:6, :]
    o_ref[...] = x1_ref[...] + gt2 * _rms(h, gpo_ref[...])


def _combine(yg, x1, mod, rg, g_post_ffn, tokens_per_batch):
    n, d = x1.shape
    per_b = tokens_per_batch // COMB_TOK
    gpo = g_post_ffn.reshape(1, d)
    return pl.pallas_call(
        _combine_kernel,
        grid=(n // COMB_TOK,),
        in_specs=[pl.BlockSpec((TOP_K, COMB_TOK, ROW_SUB, ROW_LANE), lambda i: (0, i, 0, 0)),
                  pl.BlockSpec((COMB_TOK, d), lambda i: (i, 0)),
                  pl.BlockSpec((1, 6, d), lambda i: (i // per_b, 0, 0)),
                  pl.BlockSpec((COMB_TOK, 128), lambda i: (i, 0)),
                  pl.BlockSpec((1, d), lambda i: (0, 0))],
        out_specs=pl.BlockSpec((COMB_TOK, d), lambda i: (i, 0)),
        out_shape=jax.ShapeDtypeStruct((n, d), F32),
        compiler_params=_cparams(("arbitrary",)),
        name="combine",
    )(yg, x1, mod, rg, gpo)


def _moe_plan(ri_t, counts, n_tok):
    idx = ri_t[:TOP_K]
    rank = ri_t[TOP_K:2 * TOP_K]
    cnt = counts.astype(jnp.int32)
    blk = MOE_BLK_LARGE if n_tok * TOP_K >= 8 * MOE_BLK_LARGE * N_EXPERTS else MOE_BLK_SMALL
    padded = (cnt + blk - 1) // blk * blk
    pend = jnp.cumsum(padded)
    pstart = pend - padded
    dest = rank
    for e in range(N_EXPERTS):
        dest = dest + jnp.where(idx == e, pstart[e], 0)
    n_blocks = (n_tok * TOP_K + blk - 1) // blk + N_EXPERTS
    bstart = jnp.arange(n_blocks, dtype=jnp.int32) * blk
    block_e = jnp.minimum(jnp.sum((bstart[:, None] >= pend[None, :]).astype(jnp.int32), axis=1), N_EXPERTS - 1)
    block_valid = jnp.clip(pstart[block_e] + cnt[block_e] - bstart, 0, blk).astype(jnp.int32)
    return dest.astype(jnp.int32), block_e, block_valid, n_blocks * blk


def _layer(x, mod, p):
    b, t, d = x.shape
    n = b * t
    proj = _in_proj(x, mod, p['g_pre_mix'], p['w_in'])
    o_fw, o_bw, o_b = _mixers(proj, p['lb_logits'], p['na_tabs'])
    x1, xn, ri, rg, counts = _mix(x, mod, proj, o_fw, o_bw, o_b, p['hgrn_norm'], p['g_post_mix'], p['g_pre_ffn'],
                                  p['w_branch_a'], p['w_branch_b'], p['w_out'], p['router_w'], p['router_b'])
    rg = rg.reshape(n, 128)
    dest, block_e, block_valid, n_rows = _moe_plan(ri, counts[0], n)
    xs = _scatter_rows(xn.reshape(n, ROW_SUB, ROW_LANE), dest, n_rows)
    y = _experts(xs, block_e, block_valid, p['w_gu'], p['b_gu'], p['w_d'], p['b_d'])
    yg = _gather_rows(y, dest.reshape(-1)).reshape(TOP_K, n, ROW_SUB, ROW_LANE)
    out = _combine(yg, x1.reshape(n, d), mod, rg, p['g_post_ffn'], t)
    return out.reshape(b, t, d)


def kernel(x_prompt, x_sample, c_prompt, c_sample, ada_w, ada_b, g_pre_mix, g_post_mix, g_pre_ffn, g_post_ffn,
           w_in, hgrn_lb_logits, hgrn_norm, na_rpb, w_branch_a, w_branch_b, w_out, router_w, router_b,
           w_gate_up, b_gate_up, w_down, b_down):
    d = D_MODEL
    bp, bs = c_prompt.shape[0], c_sample.shape[0]
    pad = (-(bp + bs)) % 8
    c_all = jnp.concatenate([c_prompt, c_sample, jnp.zeros((pad, d), F32)], axis=0)
    mod = _ada_mod(c_all, ada_w[0], ada_b[0]).reshape(-1, 6, d)

    w = w_in[0]
    w_perm = jnp.concatenate([w[:, :5 * d], w[:, 5 * d + 3 * NA_W:], w[:, 5 * d:5 * d + 3 * NA_W]], axis=1).astype(BF16)
    w_gu, b_gu = _gate_up_layout(w_gate_up[0], b_gate_up[0])
    p = dict(
        g_pre_mix=g_pre_mix[0], g_post_mix=g_post_mix[0], g_pre_ffn=g_pre_ffn[0], g_post_ffn=g_post_ffn[0],
        w_in=w_perm, lb_logits=hgrn_lb_logits.astype(F32), hgrn_norm=hgrn_norm[0],
        na_tabs=_na_bias_tables(na_rpb[0]),
        w_branch_a=w_branch_a[0].astype(BF16), w_branch_b=w_branch_b[0].astype(BF16), w_out=w_out[0].astype(BF16),
        router_w=router_w[0].astype(BF16), router_b=router_b[0],
        w_gu=w_gu, b_gu=b_gu,
        w_d=w_down[0].astype(BF16), b_d=b_down[0][:, None, :],
    )
    y_p = _layer(x_prompt, mod[:bp], p)
    y_s = _layer(x_sample, mod[bp:bp + bs], p)
    return (y_p, y_s)
```

```python
import functools

import numpy as np
import jax
import jax.numpy as jnp
from jax import lax
from jax.experimental import pallas as pl
from jax.experimental.pallas import tpu as pltpu
from jax.experimental.pallas import tpu_sc as plsc

F32 = jnp.float32
BF16 = jnp.bfloat16

D_MODEL = 1024
GRID_W = 64
HA_HEADS = 8
HA_DK = 128
NA_HEADS = 8
NA_DH = 64
NA_W = NA_HEADS * NA_DH
NA_KH = 8
NA_KW = 16
N_EXPERTS = 32
TOP_K = 4
D_FF = 1024
SWIGLU_ALPHA = 1.702
SWIGLU_LIMIT = 7.0
RMS_EPS = 1e-6
NEG_BIG = -1e30

COL_A = 0
COL_GATES = 5 * 1024
COL_NA = 7 * 1024
PROJ_COLS = 7 * 1024 + 3 * NA_W

HGRN_CHUNK = 64
HGRN_TILE = 256
PROJ_TM = 1024
PROJ_TN = 2176
MIX_TM = 512
MOE_BLK_SMALL, MOE_BLK_LARGE = 512, 1024
COMB_TOK = 512
SUBLANES = 8
ROW_SUB, ROW_LANE = 8, 128
VMEM_LIMIT = 56 * 1024 * 1024


def _cparams(sem):
    return pltpu.CompilerParams(dimension_semantics=sem, vmem_limit_bytes=VMEM_LIMIT)


def _dot(a, b):
    return jnp.dot(a, b, preferred_element_type=F32)


def _dot_nt(a, b):
    return lax.dot_general(a, b, (((1,), (1,)), ((), ())), preferred_element_type=F32)


def _dot_tn(a, b):
    return lax.dot_general(a, b, (((0,), (0,)), ((), ())), preferred_element_type=F32)


def _split(a):
    hi = a.astype(BF16)
    lo = (a - hi.astype(F32)).astype(BF16)
    return hi, lo


def _dot3(a, w):
    ah, al = _split(a)
    wh, wl = _split(w)
    return _dot(ah, wh) + (_dot(al, wh) + _dot(ah, wl))


def _sigmoid(x):
    return 0.5 * jnp.tanh(0.5 * x) + 0.5


def _rms(x, g):
    return x * lax.rsqrt(jnp.mean(x * x, axis=-1, keepdims=True) + RMS_EPS) * g


def _ada_kernel(c_ref, w_ref, b_ref, o_ref):
    c = c_ref[...]
    o_ref[...] = _dot3(c * _sigmoid(c), w_ref[...]) + b_ref[...]


def _ada_mod(c, ada_w, ada_b):
    bp, d = c.shape
    n = ada_w.shape[1]
    tn = 1536
    return pl.pallas_call(
        _ada_kernel,
        grid=(n // tn,),
        in_specs=[pl.BlockSpec((bp, d), lambda j: (0, 0)),
                  pl.BlockSpec((d, tn), lambda j: (0, j)),
                  pl.BlockSpec((1, tn), lambda j: (0, j))],
        out_specs=pl.BlockSpec((bp, tn), lambda j: (0, j)),
        out_shape=jax.ShapeDtypeStruct((bp, n), F32),
        compiler_params=_cparams(("arbitrary",)),
        name="ada_mod",
    )(c, ada_w, ada_b.reshape(1, n))


def _in_proj_kernel(x_ref, mod_ref, g_ref, w_ref, o_ref, xn_ref):
    @pl.when(pl.program_id(2) == 0)
    def _():
        sh = mod_ref[0, 0:1, :]
        sc = mod_ref[0, 1:2, :]
        xn = _rms(x_ref[0], g_ref[...]) * (1.0 + sc) + sh
        xn_ref[...] = xn.astype(BF16)

    o_ref[0] = _dot(xn_ref[...], w_ref[...]).astype(BF16)


def _in_proj(x, mod, g_pre, w_in_bf):
    b, t, d = x.shape
    tm = min(PROJ_TM, t)
    n = w_in_bf.shape[1]
    return pl.pallas_call(
        _in_proj_kernel,
        grid=(b, t // tm, n // PROJ_TN),
        in_specs=[pl.BlockSpec((1, tm, d), lambda bi, i, j: (bi, i, 0)),
                  pl.BlockSpec((1, 6, d), lambda bi, i, j: (bi, 0, 0)),
                  pl.BlockSpec((1, d), lambda bi, i, j: (0, 0)),
                  pl.BlockSpec((d, PROJ_TN), lambda bi, i, j: (0, j))],
        out_specs=pl.BlockSpec((1, tm, PROJ_TN), lambda bi, i, j: (bi, i, j)),
        out_shape=jax.ShapeDtypeStruct((b, t, n), BF16),
        scratch_shapes=[pltpu.VMEM((tm, d), BF16)],
        compiler_params=_cparams(("arbitrary", "arbitrary", "arbitrary")),
        name="in_proj",
    )(x, mod, g_pre.reshape(1, d), w_in_bf)


def _cumsum_rows(x, reverse):
    groups = x.shape[0] // SUBLANES
    row = lax.broadcasted_iota(jnp.int32, (SUBLANES, x.shape[1]), 0)
    out = [None] * groups
    carry = None
    for g in (range(groups - 1, -1, -1) if reverse else range(groups)):
        blk = x[g * SUBLANES:(g + 1) * SUBLANES]
        step = 1
        while step < SUBLANES:
            if reverse:
                blk = blk + jnp.where(row < SUBLANES - step, pltpu.roll(blk, SUBLANES - step, axis=0), 0.0)
            else:
                blk = blk + jnp.where(row >= step, pltpu.roll(blk, step, axis=0), 0.0)
            step *= 2
        if carry is not None:
            blk = blk + carry
        carry = blk[0:1] if reverse else blk[SUBLANES - 1:SUBLANES]
        out[g] = blk
    return jnp.concatenate(out, axis=0)


def _hgrn_dir(q_ref, f_ref, i_ref, o_ref, s_ref, sn_ref, lbd, row0, reverse):
    c = HGRN_CHUNK
    rows = pl.ds(row0, c)
    q = q_ref[0, rows, :].astype(F32)
    fpre = f_ref[0, rows, :].astype(F32)
    v = i_ref[0, rows, :]
    f = 0.5 * (1.0 + lbd) + (0.5 * (1.0 - lbd)) * jnp.tanh(0.5 * fpre)
    logf = jnp.log(f)
    k = 1.0 - f
    ri = lax.broadcasted_iota(jnp.int32, (c, c), 0)
    ci = lax.broadcasted_iota(jnp.int32, (c, c), 1)
    keep = (ci >= ri) if reverse else (ri >= ci)
    b = _cumsum_rows(logf, reverse)
    if reverse:
        b_mid = b[c // 2:c // 2 + 1, :]
        b_end = b[0:1, :]
    else:
        b_mid = b[c // 2 - 1:c // 2, :]
        b_end = b[c - 1:c, :]
    d = b - b_mid
    qm = q * jnp.exp(d)
    km = k * jnp.exp(-d)
    qi = (qm * jnp.exp(b_mid)).astype(BF16)
    ks = (km * jnp.exp(b_end - b_mid)).astype(BF16)
    qm = qm.astype(BF16)
    km = km.astype(BF16)
    dec = jnp.exp(b_end)
    for h in range(HA_HEADS):
        sl = slice(h * HA_DK, (h + 1) * HA_DK)
        a = _dot_nt(qm[:, sl], km[:, sl])
        a = jnp.where(keep, a, 0.0).astype(BF16)
        lhs = jnp.concatenate([qi[:, sl], a], axis=1)
        rhs = jnp.concatenate([sn_ref[h], v[:, sl]], axis=0)
        o_ref[0, rows, sl] = _dot(lhs, rhs).astype(BF16)
        st = s_ref[h] * dec[:, sl] + _dot_tn(v[:, sl], ks[:, sl])
        s_ref[h] = st
        sn_ref[h] = st.astype(BF16).T


def _hgrn_tile(lbl_ref, qf_ref, ff_ref, if_ref, qb_ref, fb_ref, ib_ref, of_ref, ob_ref,
               sf_ref, sb_ref, snf_ref, snb_ref):
    @pl.when(pl.program_id(1) == 0)
    def _():
        for ref in (sf_ref, sb_ref, snf_ref, snb_ref):
            ref[...] = jnp.zeros_like(ref)

    l0 = lbl_ref[0]
    l1 = lbl_ref[1]
    m = jnp.maximum(l0, l1)
    e0 = jnp.exp(l0 - m)
    e1 = jnp.exp(l1 - m)
    lb = e0 / (e0 + e1)
    n_chunks = HGRN_TILE // HGRN_CHUNK

    def chunk(ci):
        _hgrn_dir(qf_ref, ff_ref, if_ref, of_ref, sf_ref, snf_ref, lb[0:1, :], ci * HGRN_CHUNK, False)
        _hgrn_dir(qb_ref, fb_ref, ib_ref, ob_ref, sb_ref, snb_ref, lb[1:2, :],
                  (n_chunks - 1 - ci) * HGRN_CHUNK, True)

    return chunk


def _mixers_kernel(lbl_ref, qf_ref, ff_ref, if_ref, qb_ref, fb_ref, ib_ref, nq_ref, nk_ref, nv_ref, tab_ref,
                   of_ref, ob_ref, on_ref, sf_ref, sb_ref, snf_ref, snb_ref, *, grid_rows):
    chunk = _hgrn_tile(lbl_ref, qf_ref, ff_ref, if_ref, qb_ref, fb_ref, ib_ref, of_ref, ob_ref,
                       sf_ref, sb_ref, snf_ref, snb_ref)
    na_row = _na_rows(nq_ref, nk_ref, nv_ref, tab_ref, on_ref, grid_rows)
    for ci in range(HGRN_TILE // HGRN_CHUNK):
        chunk(ci)
        na_row(ci)


def _mixers(proj, lb_logits, tabs):
    b, t, _ = proj.shape
    tt = HGRN_TILE
    nt = t // tt
    d = D_MODEL
    cq, ck, cv = (COL_NA // NA_W, COL_NA // NA_W + 1, COL_NA // NA_W + 2)

    def fwd(col):
        return pl.BlockSpec((1, tt, d), lambda bi, ti: (bi, ti, col))

    def bwd(col):
        return pl.BlockSpec((1, tt, d), lambda bi, ti: (bi, nt - 1 - ti, col))

    return pl.pallas_call(
        functools.partial(_mixers_kernel, grid_rows=t // GRID_W),
        grid=(b, nt),
        in_specs=[pl.BlockSpec((2, 2, d), lambda bi, ti: (0, 0, 0)),
                  fwd(0), fwd(1), fwd(3), bwd(0), bwd(2), bwd(3),
                  pl.BlockSpec((1, tt, NA_W), lambda bi, ti: (bi, ti, cq)),
                  pl.BlockSpec((1, t, NA_W), lambda bi, ti: (bi, 0, ck), pipeline_mode=pl.Buffered(1)),
                  pl.BlockSpec((1, t, NA_W), lambda bi, ti: (bi, 0, cv), pipeline_mode=pl.Buffered(1)),
                  pl.BlockSpec(tabs.shape, lambda bi, ti: (0, 0, 0, 0), pipeline_mode=pl.Buffered(1))],
        out_specs=[pl.BlockSpec((1, tt, d), lambda bi, ti: (bi, ti, 0)),
                   pl.BlockSpec((1, tt, d), lambda bi, ti: (bi, nt - 1 - ti, 0)),
                   pl.BlockSpec((1, tt, NA_W), lambda bi, ti: (bi, ti, 0))],
        out_shape=[jax.ShapeDtypeStruct((b, t, d), BF16), jax.ShapeDtypeStruct((b, t, d), BF16),
                   jax.ShapeDtypeStruct((b, t, NA_W), BF16)],
        scratch_shapes=[pltpu.VMEM((HA_HEADS, HA_DK, HA_DK), F32), pltpu.VMEM((HA_HEADS, HA_DK, HA_DK), F32),
                        pltpu.VMEM((HA_HEADS, HA_DK, HA_DK), BF16), pltpu.VMEM((HA_HEADS, HA_DK, HA_DK), BF16)],
        compiler_params=_cparams(("arbitrary", "arbitrary")),
        name="mixers",
    )(lb_logits, proj, proj, proj, proj, proj, proj, proj, proj, proj, tabs)


def _na_bias_tables(rpb):
    q = np.arange(GRID_W)[:, None]
    kc = np.arange(GRID_W)[None, :]
    c0 = np.clip(q - NA_KW // 2, 0, GRID_W - NA_KW)
    ok = (kc >= c0) & (kc < c0 + NA_KW)
    dc = np.clip(kc - q + (NA_KW - 1), 0, 2 * NA_KW - 2)
    base = jnp.where(ok[None, None], jnp.take(rpb.astype(F32), dc, axis=2), NEG_BIG)
    tabs = []
    for dr0 in range(NA_KH):
        w = base[:, dr0:dr0 + NA_KH]
        w = w.transpose(0, 2, 1, 3).reshape(NA_HEADS // 2, 2 * GRID_W, NA_KH * GRID_W)
        tabs.append(w)
    return jnp.stack(tabs, axis=1)


def _na_rows(q_ref, k_ref, v_ref, tab_ref, o_ref, grid_rows):
    g = pl.program_id(1)
    lane = lax.broadcasted_iota(jnp.int32, (GRID_W, 2 * NA_DH), 1)
    low = lane < NA_DH
    scale = NA_DH ** -0.5
    rows_per_step = HGRN_TILE // GRID_W

    def na_row(rr):
        r = g * rows_per_step + rr
        r0 = jnp.clip(r - NA_KH // 2, 0, grid_rows - NA_KH)
        dr0 = r0 - r + (NA_KH - 1)
        qrow = pl.ds(rr * GRID_W, GRID_W)
        krow = pl.ds(pl.multiple_of(r0 * GRID_W, GRID_W), NA_KH * GRID_W)
        for j in range(NA_HEADS // 2):
            sl = slice(j * 2 * NA_DH, (j + 1) * 2 * NA_DH)
            qp = q_ref[0, qrow, sl]
            zero = jnp.zeros_like(qp)
            ql = jnp.concatenate([jnp.where(low, qp, zero), jnp.where(low, zero, qp)], axis=0)
            s = _dot_nt(ql, k_ref[0, krow, sl]) * scale + tab_ref[j, dr0]
            m = jnp.max(s, axis=-1, keepdims=True)
            p = jnp.exp(s - m)
            l = jnp.sum(p, axis=-1, keepdims=True)
            pv = _dot(p.astype(BF16), v_ref[0, krow, sl]) / l
            o_ref[0, qrow, sl] = jnp.where(low, pv[:GRID_W], pv[GRID_W:]).astype(BF16)

    return na_row


def _mix_kernel(x_ref, mod_ref, of_ref, ob_ref, g_ref, ga_ref, gb_ref, nb_ref,
                gn_ref, gpm_ref, gpf_ref, wa_ref, wb_ref, wo_ref, rw_ref, rb_ref,
                x1_ref, xn_ref, ri_ref, rg_ref, cnt_ref, tri_ref, carry_ref):
    tm = MIX_TM
    first = (pl.program_id(0) == 0) & (pl.program_id(1) == 0)

    @pl.when(first)
    def _():
        r = lax.broadcasted_iota(jnp.int32, (tm, tm), 0)
        c = lax.broadcasted_iota(jnp.int32, (tm, tm), 1)
        tri_ref[...] = jnp.where(r < c, 1.0, 0.0).astype(BF16)
        carry_ref[...] = jnp.zeros_like(carry_ref)

    sc2 = mod_ref[0, 4:5, :]
    sh2 = mod_ref[0, 3:4, :]
    gt1 = mod_ref[0, 2:3, :]

    o = of_ref[0].astype(F32) + ob_ref[0].astype(F32)
    parts = []
    for h in range(HA_HEADS):
        oh = o[:, h * HA_DK:(h + 1) * HA_DK]
        parts.append(oh * lax.rsqrt(jnp.mean(oh * oh, axis=-1, keepdims=True) + RMS_EPS))
    g = g_ref[0].astype(F32)
    oa = jnp.concatenate(parts, axis=-1) * gn_ref[...] * (g * _sigmoid(g))
    ya = _dot(oa.astype(BF16), wa_ref[...])
    yb = _dot(nb_ref[0], wb_ref[...])
    mix = _sigmoid(ga_ref[0].astype(F32)) * ya + _sigmoid(gb_ref[0].astype(F32)) * yb
    mo = _dot(mix.astype(BF16), wo_ref[...])
    x1 = x_ref[0] + gt1 * _rms(mo, gpm_ref[...])
    x1_ref[0] = x1

    xn = _rms(x1, gpf_ref[...]) * (1.0 + sc2) + sh2
    xn_ref[0] = xn.reshape(tm, ROW_SUB, ROW_LANE)
    logits = _dot_nt(rw_ref[...], xn.astype(BF16)) + rb_ref[...]
    sub = lax.broadcasted_iota(jnp.int32, (N_EXPERTS, tm), 0)
    work = logits
    vals, idxs, hots = [], [], []
    for _ in range(TOP_K):
        mv = jnp.max(work, axis=0, keepdims=True)
        mi = jnp.min(jnp.where(work == mv, sub, N_EXPERTS), axis=0, keepdims=True)
        hot = sub == mi
        vals.append(mv)
        idxs.append(mi)
        hots.append(hot)
        work = jnp.where(hot, -jnp.inf, work)
    es = [jnp.exp(v - vals[0]) for v in vals]
    den = es[0] + es[1] + es[2] + es[3]

    cnt = sum(jnp.where(h, 1.0, 0.0) for h in hots)
    before = _dot(cnt.astype(BF16), tri_ref[...]) + carry_ref[...]
    carry_ref[...] = carry_ref[...] + jnp.sum(cnt, axis=1, keepdims=True)
    cnt_ref[...] = jnp.broadcast_to(carry_ref[...], cnt_ref.shape)

    ranks = [jnp.sum(jnp.where(hots[kk], before, 0.0), axis=0, keepdims=True).astype(jnp.int32)
             for kk in range(TOP_K)]
    ri_ref[...] = jnp.concatenate(idxs + ranks, axis=0)
    gates = jnp.concatenate([e / den for e in es] + [jnp.zeros((128 - TOP_K, tm), F32)], axis=0)
    rg_ref[0] = gates.T


def _mix(x, mod, proj, o_fw, o_bw, o_b, hgrn_norm, g_post_mix, g_pre_ffn, wa, wb, wo, rw, rb):
    b, t, d = x.shape
    tm = min(MIX_TM, t)
    assert tm == MIX_TM
    row = lambda a: a.reshape(1, -1)
    tok = lambda w, col: pl.BlockSpec((1, tm, w), lambda bi, i: (bi, i, col))
    full = lambda a: pl.BlockSpec(a.shape, lambda bi, i: (0,) * a.ndim)
    gn, gpm, gpf, rbr = row(hgrn_norm), row(g_post_mix), row(g_pre_ffn), rb.reshape(-1, 1)
    rw = rw.T
    outs = pl.pallas_call(
        _mix_kernel,
        grid=(b, t // tm),
        in_specs=[tok(d, 0),
                  pl.BlockSpec((1, 6, d), lambda bi, i: (bi, 0, 0)),
                  tok(d, 0), tok(d, 0),
                  tok(d, 4),
                  tok(d, COL_GATES // d), tok(d, COL_GATES // d + 1),
                  tok(NA_W, 0),
                  full(gn), full(gpm), full(gpf), full(wa), full(wb), full(wo), full(rw), full(rbr)],
        out_specs=[tok(d, 0), pl.BlockSpec((1, tm, ROW_SUB, ROW_LANE), lambda bi, i: (bi, i, 0, 0)),
                   pl.BlockSpec((8, tm), lambda bi, i: (0, bi * (t // tm) + i)), tok(128, 0),
                   pl.BlockSpec((N_EXPERTS, 128), lambda bi, i: (0, 0))],
        out_shape=[jax.ShapeDtypeStruct((b, t, d), F32), jax.ShapeDtypeStruct((b, t, ROW_SUB, ROW_LANE), F32),
                   jax.ShapeDtypeStruct((8, b * t), jnp.int32), jax.ShapeDtypeStruct((b, t, 128), F32),
                   jax.ShapeDtypeStruct((N_EXPERTS, 128), F32)],
        scratch_shapes=[pltpu.VMEM((tm, tm), BF16), pltpu.VMEM((N_EXPERTS, 1), F32)],
        compiler_params=_cparams(("arbitrary", "arbitrary")),
        name="mix",
    )(x, mod, o_fw, o_bw, proj, proj, proj, o_b, gn, gpm, gpf, wa, wb, wo, rw, rbr)
    return outs


SC_CORES, SC_SUBCORES = 2, 16
SC_ROWS = 32


def _scatter_rows(xn, dest_t, n_rows):
    n_tok = xn.shape[0]
    workers = SC_CORES * SC_SUBCORES
    per_w = n_tok // workers
    n_it = per_w // SC_ROWS
    assert per_w * workers == n_tok and n_it * SC_ROWS == per_w and n_it % 2 == 0
    idx = dest_t.reshape(TOP_K, n_tok // SC_ROWS, SC_ROWS)
    mesh = plsc.VectorSubcoreMesh(core_axis_name="c", subcore_axis_name="s")

    @functools.partial(
        pl.kernel, mesh=mesh,
        out_type=jax.ShapeDtypeStruct((n_rows, ROW_SUB, ROW_LANE), xn.dtype),
        scratch_types=[pltpu.VMEM((2, TOP_K, SC_ROWS), jnp.int32),
                       pltpu.VMEM((2, SC_ROWS, ROW_SUB, ROW_LANE), xn.dtype),
                       pltpu.SemaphoreType.DMA((2,)), pltpu.SemaphoreType.DMA((2,))],
    )
    def scatter_kernel(x_hbm, idx_hbm, out_hbm, idx_v, rows_v, sem_load, sem_store):
        chunk0 = (lax.axis_index("s") * SC_CORES + lax.axis_index("c")) * n_it

        def load(g, slot):
            tok = pl.ds(pl.multiple_of((chunk0 + g) * SC_ROWS, SC_ROWS), SC_ROWS)
            return pltpu.make_async_copy(x_hbm.at[tok], rows_v.at[slot], sem_load.at[slot])

        def start_load(g, slot):
            load(g, slot).start()
            for kk in range(TOP_K):
                pltpu.sync_copy(idx_hbm.at[kk, chunk0 + g], idx_v.at[slot, kk])

        def store(slot, kk):
            return pltpu.make_async_copy(rows_v.at[slot], out_hbm.at[idx_v.at[slot, kk]], sem_store.at[slot])

        def scatter(g, slot):
            load(g, slot).wait()
            for kk in range(TOP_K):
                store(slot, kk).start()
            for kk in range(TOP_K):
                store(slot, kk).wait()

        start_load(0, 0)

        @pl.loop(0, n_it, step=2)
        def _(g):
            start_load(g + 1, 1)
            scatter(g, 0)

            @pl.when(g + 2 < n_it)
            def _():
                start_load(g + 2, 0)

            scatter(g + 1, 1)

    return scatter_kernel(xn, idx)


GU_SLAB = 256


def _gate_up_layout_kernel(w_ref, o_ref):
    half = GU_SLAB // 2
    r = lax.broadcasted_iota(jnp.int32, (GU_SLAB, GU_SLAB), 0)
    c = lax.broadcasted_iota(jnp.int32, (GU_SLAB, GU_SLAB), 1)
    src = jnp.where(c < half, 2 * c, 2 * (c - half) + 1)
    pm = jnp.where(r == src, 1.0, 0.0).astype(BF16)
    for s in range(w_ref.shape[2] // GU_SLAB):
        cols = slice(s * GU_SLAB, (s + 1) * GU_SLAB)
        o_ref[0, :, cols] = _dot(w_ref[0, :, cols].astype(BF16), pm).astype(BF16)


def _gate_up_layout(w_gate_up, b_gate_up):
    e, k, n2 = w_gate_up.shape
    half = GU_SLAB // 2
    tk = 512
    w = pl.pallas_call(
        _gate_up_layout_kernel,
        grid=(e, k // tk),
        in_specs=[pl.BlockSpec((1, tk, n2), lambda i, j: (i, j, 0))],
        out_specs=pl.BlockSpec((1, tk, n2), lambda i, j: (i, j, 0)),
        out_shape=jax.ShapeDtypeStruct((e, k, n2), BF16),
        compiler_params=_cparams(("arbitrary", "arbitrary")),
        name="gate_up_layout",
    )(w_gate_up)
    perm = np.concatenate([s * GU_SLAB + np.concatenate([2 * np.arange(half), 2 * np.arange(half) + 1])
                           for s in range(n2 // GU_SLAB)])
    return w, b_gate_up[:, None, perm]


def _expert_kernel(be_ref, bv_ref, x_ref, wgu_ref, bgu_ref, wd_ref, bd_ref, o_ref, a_ref):
    i = pl.program_id(0)
    valid = bv_ref[i]
    half = GU_SLAB // 2

    @pl.when(valid > 0)
    def _():
        blk = x_ref.shape[0]
        rowi = lax.broadcasted_iota(jnp.int32, (blk, 1), 0)
        x = x_ref[...].reshape(blk, D_MODEL)
        x = jnp.where(rowi < valid, x, 0.0).astype(BF16)
        for s in range(2 * D_FF // GU_SLAB):
            cols = slice(s * GU_SLAB, (s + 1) * GU_SLAB)
            h = _dot(x, wgu_ref[0, :, cols]) + bgu_ref[0, :, cols]
            glu = jnp.minimum(h[:, :half], SWIGLU_LIMIT)
            lin = jnp.clip(h[:, half:], -SWIGLU_LIMIT, SWIGLU_LIMIT)
            a = glu * _sigmoid(SWIGLU_ALPHA * glu) * (lin + 1.0)
            a_ref[:, s * half:(s + 1) * half] = a.astype(BF16)
        o_ref[...] = (_dot(a_ref[...], wd_ref[0]) + bd_ref[0]).reshape(o_ref.shape)

    @pl.when(valid <= 0)
    def _():
        o_ref[...] = jnp.zeros_like(o_ref)


def _experts(xs, block_e, block_valid, wgu, bgu, wd, bd):
    n_rows = xs.shape[0]
    d = D_MODEL
    n_blocks = block_e.shape[0]
    blk = n_rows // n_blocks
    emap = lambda i, be, bv: (be[i], 0, 0)
    rows = lambda: pl.BlockSpec((blk, ROW_SUB, ROW_LANE), lambda i, be, bv: (i, 0, 0))
    return pl.pallas_call(
        _expert_kernel,
        grid_spec=pltpu.PrefetchScalarGridSpec(
            num_scalar_prefetch=2,
            grid=(n_blocks,),
            in_specs=[rows(),
                      pl.BlockSpec((1, d, 2 * D_FF), emap), pl.BlockSpec((1, 1, 2 * D_FF), emap),
                      pl.BlockSpec((1, D_FF, d), emap), pl.BlockSpec((1, 1, d), emap)],
            out_specs=rows(),
            scratch_shapes=[pltpu.VMEM((blk, D_FF), BF16)],
        ),
        out_shape=jax.ShapeDtypeStruct((n_rows, ROW_SUB, ROW_LANE), F32),
        compiler_params=_cparams(("arbitrary",)),
        name="experts",
    )(block_e, block_valid, xs, wgu, bgu, wd, bd)


def _gather_rows(y, dest_flat):
    n_out = dest_flat.shape[0]
    workers = SC_CORES * SC_SUBCORES
    per_w = n_out // workers
    n_it = per_w // SC_ROWS
    assert per_w * workers == n_out and n_it * SC_ROWS == per_w and n_it % 2 == 0
    mesh = plsc.VectorSubcoreMesh(core_axis_name="c", subcore_axis_name="s")

    @functools.partial(
        pl.kernel, mesh=mesh,
        out_type=jax.ShapeDtypeStruct((n_out, ROW_SUB, ROW_LANE), y.dtype),
        scratch_types=[pltpu.VMEM((2, SC_ROWS), jnp.int32),
                       pltpu.VMEM((2, SC_ROWS, ROW_SUB, ROW_LANE), y.dtype),
                       pltpu.SemaphoreType.DMA((2,))],
    )
    def gather_kernel(y_hbm, idx_hbm, out_hbm, idx_v, rows_v, sem):
        base = (lax.axis_index("s") * SC_CORES + lax.axis_index("c")) * per_w

        def span(g):
            return pl.ds(pl.multiple_of(base + g * SC_ROWS, SC_ROWS), SC_ROWS)

        def row_gather(slot):
            return pltpu.make_async_copy(y_hbm.at[idx_v.at[slot]], rows_v.at[slot], sem.at[slot])

        def start(g, slot):
            pltpu.sync_copy(idx_hbm.at[span(g)], idx_v.at[slot])
            row_gather(slot).start()

        start(0, 0)

        @pl.loop(0, n_it, step=2)
        def _(g):
            start(g + 1, 1)
            row_gather(0).wait()
            pltpu.sync_copy(rows_v.at[0], out_hbm.at[span(g)])

            @pl.when(g + 2 < n_it)
            def _():
                start(g + 2, 0)

            row_gather(1).wait()
            pltpu.sync_copy(rows_v.at[1], out_hbm.at[span(g + 1)])

    return gather_kernel(y, dest_flat)


def _combine_kernel(yg_ref, x1_ref, mod_ref, rg_ref, gpo_ref, o_ref):
    rg = rg_ref[...]
    h = rg[:, 0:1] * yg_ref[0].reshape(COMB_TOK, D_MODEL)
    for kk in range(1, TOP_K):
        h = h + rg[:, kk:kk + 1] * yg_ref[kk].reshape(COMB_TOK, D_MODEL)
    gt2 = mod_ref[0, 5:6, :]
    o_ref[...] = x1_ref[...] + gt2 * _rms(h, gpo_ref[...])


def _combine(yg, x1, mod, rg, g_post_ffn, tokens_per_batch):
    n, d = x1.shape
    per_b = tokens_per_batch // COMB_TOK
    gpo = g_post_ffn.reshape(1, d)
    return pl.pallas_call(
        _combine_kernel,
        grid=(n // COMB_TOK,),
        in_specs=[pl.BlockSpec((TOP_K, COMB_TOK, ROW_SUB, ROW_LANE), lambda i: (0, i, 0, 0)),
                  pl.BlockSpec((COMB_TOK, d), lambda i: (i, 0)),
                  pl.BlockSpec((1, 6, d), lambda i: (i // per_b, 0, 0)),
                  pl.BlockSpec((COMB_TOK, 128), lambda i: (i, 0)),
                  pl.BlockSpec((1, d), lambda i: (0, 0))],
        out_specs=pl.BlockSpec((COMB_TOK, d), lambda i: (i, 0)),
        out_shape=jax.ShapeDtypeStruct((n, d), F32),
        compiler_params=_cparams(("arbitrary",)),
        name="combine",
    )(yg, x1, mod, rg, gpo)


def _moe_plan(ri_t, counts, n_tok):
    idx = ri_t[:TOP_K]
    rank = ri_t[TOP_K:2 * TOP_K]
    cnt = counts.astype(jnp.int32)
    blk = MOE_BLK_LARGE if n_tok * TOP_K >= 8 * MOE_BLK_LARGE * N_EXPERTS else MOE_BLK_SMALL
    padded = (cnt + blk - 1) // blk * blk
    pend = jnp.cumsum(padded)
    pstart = pend - padded
    dest = rank
    for e in range(N_EXPERTS):
        dest = dest + jnp.where(idx == e, pstart[e], 0)
    n_blocks = (n_tok * TOP_K + blk - 1) // blk + N_EXPERTS
    bstart = jnp.arange(n_blocks, dtype=jnp.int32) * blk
    block_e = jnp.minimum(jnp.sum((bstart[:, None] >= pend[None, :]).astype(jnp.int32), axis=1), N_EXPERTS - 1)
    block_valid = jnp.clip(pstart[block_e] + cnt[block_e] - bstart, 0, blk).astype(jnp.int32)
    return dest.astype(jnp.int32), block_e, block_valid, n_blocks * blk


def _layer(x, mod, p):
    b, t, d = x.shape
    n = b * t
    proj = _in_proj(x, mod, p['g_pre_mix'], p['w_in'])
    o_fw, o_bw, o_b = _mixers(proj, p['lb_logits'], p['na_tabs'])
    x1, xn, ri, rg, counts = _mix(x, mod, proj, o_fw, o_bw, o_b, p['hgrn_norm'], p['g_post_mix'], p['g_pre_ffn'],
                                  p['w_branch_a'], p['w_branch_b'], p['w_out'], p['router_w'], p['router_b'])
    rg = rg.reshape(n, 128)
    dest, block_e, block_valid, n_rows = _moe_plan(ri, counts[:, 0], n)
    xs = _scatter_rows(xn.reshape(n, ROW_SUB, ROW_LANE), dest, n_rows)
    y = _experts(xs, block_e, block_valid, p['w_gu'], p['b_gu'], p['w_d'], p['b_d'])
    yg = _gather_rows(y, dest.reshape(-1)).reshape(TOP_K, n, ROW_SUB, ROW_LANE)
    out = _combine(yg, x1.reshape(n, d), mod, rg, p['g_post_ffn'], t)
    return out.reshape(b, t, d)


def kernel(x_prompt, x_sample, c_prompt, c_sample, ada_w, ada_b, g_pre_mix, g_post_mix, g_pre_ffn, g_post_ffn,
           w_in, hgrn_lb_logits, hgrn_norm, na_rpb, w_branch_a, w_branch_b, w_out, router_w, router_b,
           w_gate_up, b_gate_up, w_down, b_down):
    d = D_MODEL
    bp, bs = c_prompt.shape[0], c_sample.shape[0]
    pad = (-(bp + bs)) % 8
    c_all = jnp.concatenate([c_prompt, c_sample, jnp.zeros((pad, d), F32)], axis=0)
    mod = _ada_mod(c_all, ada_w[0], ada_b[0]).reshape(-1, 6, d)

    w = w_in[0]
    w_perm = jnp.concatenate([w[:, :5 * d], w[:, 5 * d + 3 * NA_W:], w[:, 5 * d:5 * d + 3 * NA_W]], axis=1).astype(BF16)
    w_gu, b_gu = _gate_up_layout(w_gate_up[0], b_gate_up[0])
    p = dict(
        g_pre_mix=g_pre_mix[0], g_post_mix=g_post_mix[0], g_pre_ffn=g_pre_ffn[0], g_post_ffn=g_post_ffn[0],
        w_in=w_perm, lb_logits=hgrn_lb_logits.astype(F32), hgrn_norm=hgrn_norm[0],
        na_tabs=_na_bias_tables(na_rpb[0]),
        w_branch_a=w_branch_a[0].astype(BF16), w_branch_b=w_branch_b[0].astype(BF16), w_out=w_out[0].astype(BF16),
        router_w=router_w[0].astype(BF16), router_b=router_b[0],
        w_gu=w_gu, b_gu=b_gu,
        w_d=w_down[0].astype(BF16), b_d=b_down[0][:, None, :],
    )
    y_p = _layer(x_prompt, mod[:bp], p)
    y_s = _layer(x_sample, mod[bp:bp + bs], p)
    return (y_p, y_s)
```

```python
import functools

import numpy as np
import jax
import jax.numpy as jnp
from jax import lax
from jax.experimental import pallas as pl
from jax.experimental.pallas import tpu as pltpu
from jax.experimental.pallas import tpu_sc as plsc

F32 = jnp.float32
BF16 = jnp.bfloat16

D_MODEL = 1024
GRID_W = 64
HA_HEADS = 8
HA_DK = 128
NA_HEADS = 8
NA_DH = 64
NA_W = NA_HEADS * NA_DH
NA_KH = 8
NA_KW = 16
N_EXPERTS = 32
TOP_K = 4
D_FF = 1024
SWIGLU_ALPHA = 1.702
SWIGLU_LIMIT = 7.0
RMS_EPS = 1e-6
NEG_BIG = -1e30

COL_A = 0
COL_GATES = 5 * 1024
COL_NA = 7 * 1024
PROJ_COLS = 7 * 1024 + 3 * NA_W

HGRN_CHUNK = 64
HGRN_TILE = 256
PROJ_TM = 1024
PROJ_TN = 2176
MIX_TM = 512
MOE_BLK_SMALL, MOE_BLK_LARGE = 512, 1024
COMB_TOK = 512
SUBLANES = 8
ROW_SUB, ROW_LANE = 8, 128
VMEM_LIMIT = 56 * 1024 * 1024


def _cparams(sem):
    return pltpu.CompilerParams(dimension_semantics=sem, vmem_limit_bytes=VMEM_LIMIT)


def _dot(a, b):
    return jnp.dot(a, b, preferred_element_type=F32)


def _dot_nt(a, b):
    return lax.dot_general(a, b, (((1,), (1,)), ((), ())), preferred_element_type=F32)


def _dot_tn(a, b):
    return lax.dot_general(a, b, (((0,), (0,)), ((), ())), preferred_element_type=F32)


def _split(a):
    hi = a.astype(BF16)
    lo = (a - hi.astype(F32)).astype(BF16)
    return hi, lo


def _dot3(a, w):
    ah, al = _split(a)
    wh, wl = _split(w)
    return _dot(ah, wh) + (_dot(al, wh) + _dot(ah, wl))


def _sigmoid(x):
    return 0.5 * jnp.tanh(0.5 * x) + 0.5


def _rms(x, g):
    return x * lax.rsqrt(jnp.mean(x * x, axis=-1, keepdims=True) + RMS_EPS) * g


def _ada_kernel(c_ref, w_ref, b_ref, o_ref):
    c = c_ref[...]
    o_ref[...] = _dot3(c * _sigmoid(c), w_ref[...]) + b_ref[...]


def _ada_mod(c, ada_w, ada_b):
    bp, d = c.shape
    n = ada_w.shape[1]
    tn = 1536
    return pl.pallas_call(
        _ada_kernel,
        grid=(n // tn,),
        in_specs=[pl.BlockSpec((bp, d), lambda j: (0, 0)),
                  pl.BlockSpec((d, tn), lambda j: (0, j)),
                  pl.BlockSpec((1, tn), lambda j: (0, j))],
        out_specs=pl.BlockSpec((bp, tn), lambda j: (0, j)),
        out_shape=jax.ShapeDtypeStruct((bp, n), F32),
        compiler_params=_cparams(("arbitrary",)),
        name="ada_mod",
    )(c, ada_w, ada_b.reshape(1, n))


def _in_proj_kernel(x_ref, mod_ref, g_ref, w_ref, o_ref, xn_ref):
    @pl.when(pl.program_id(2) == 0)
    def _():
        sh = mod_ref[0, 0:1, :]
        sc = mod_ref[0, 1:2, :]
        xn = _rms(x_ref[0], g_ref[...]) * (1.0 + sc) + sh
        xn_ref[...] = xn.astype(BF16)

    o_ref[0] = _dot(xn_ref[...], w_ref[...]).astype(BF16)


def _in_proj(x, mod, g_pre, w_in_bf):
    b, t, d = x.shape
    tm = min(PROJ_TM, t)
    n = w_in_bf.shape[1]
    return pl.pallas_call(
        _in_proj_kernel,
        grid=(b, t // tm, n // PROJ_TN),
        in_specs=[pl.BlockSpec((1, tm, d), lambda bi, i, j: (bi, i, 0)),
                  pl.BlockSpec((1, 6, d), lambda bi, i, j: (bi, 0, 0)),
                  pl.BlockSpec((1, d), lambda bi, i, j: (0, 0)),
                  pl.BlockSpec((d, PROJ_TN), lambda bi, i, j: (0, j))],
        out_specs=pl.BlockSpec((1, tm, PROJ_TN), lambda bi, i, j: (bi, i, j)),
        out_shape=jax.ShapeDtypeStruct((b, t, n), BF16),
        scratch_shapes=[pltpu.VMEM((tm, d), BF16)],
        compiler_params=_cparams(("arbitrary", "arbitrary", "arbitrary")),
        name="in_proj",
    )(x, mod, g_pre.reshape(1, d), w_in_bf)


def _cumsum_rows(x, reverse):
    groups = x.shape[0] // SUBLANES
    row = lax.broadcasted_iota(jnp.int32, (SUBLANES, x.shape[1]), 0)
    out = [None] * groups
    carry = None
    for g in (range(groups - 1, -1, -1) if reverse else range(groups)):
        blk = x[g * SUBLANES:(g + 1) * SUBLANES]
        step = 1
        while step < SUBLANES:
            if reverse:
                blk = blk + jnp.where(row < SUBLANES - step, pltpu.roll(blk, SUBLANES - step, axis=0), 0.0)
            else:
                blk = blk + jnp.where(row >= step, pltpu.roll(blk, step, axis=0), 0.0)
            step *= 2
        if carry is not None:
            blk = blk + carry
        carry = blk[0:1] if reverse else blk[SUBLANES - 1:SUBLANES]
        out[g] = blk
    return jnp.concatenate(out, axis=0)


def _hgrn_dir(q_ref, f_ref, i_ref, o_ref, s_ref, sn_ref, lbd, row0, reverse):
    c = HGRN_CHUNK
    rows = pl.ds(row0, c)
    q = q_ref[0, rows, :].astype(F32)
    fpre = f_ref[0, rows, :].astype(F32)
    v = i_ref[0, rows, :]
    f = 0.5 * (1.0 + lbd) + (0.5 * (1.0 - lbd)) * jnp.tanh(0.5 * fpre)
    logf = jnp.log(f)
    k = 1.0 - f
    ri = lax.broadcasted_iota(jnp.int32, (c, c), 0)
    ci = lax.broadcasted_iota(jnp.int32, (c, c), 1)
    keep = (ci >= ri) if reverse else (ri >= ci)
    b = _cumsum_rows(logf, reverse)
    if reverse:
        b_mid = b[c // 2:c // 2 + 1, :]
        b_end = b[0:1, :]
    else:
        b_mid = b[c // 2 - 1:c // 2, :]
        b_end = b[c - 1:c, :]
    d = b - b_mid
    qm = q * jnp.exp(d)
    km = k * jnp.exp(-d)
    qi = (qm * jnp.exp(b_mid)).astype(BF16)
    ks = (km * jnp.exp(b_end - b_mid)).astype(BF16)
    qm = qm.astype(BF16)
    km = km.astype(BF16)
    dec = jnp.exp(b_end)
    for h in range(HA_HEADS):
        sl = slice(h * HA_DK, (h + 1) * HA_DK)
        a = _dot_nt(qm[:, sl], km[:, sl])
        a = jnp.where(keep, a, 0.0).astype(BF16)
        lhs = jnp.concatenate([qi[:, sl], a], axis=1)
        rhs = jnp.concatenate([sn_ref[h], v[:, sl]], axis=0)
        o_ref[0, rows, sl] = _dot(lhs, rhs).astype(BF16)
        st = s_ref[h] * dec[:, sl] + _dot_tn(v[:, sl], ks[:, sl])
        s_ref[h] = st
        sn_ref[h] = st.astype(BF16).T


def _hgrn_tile(lbl_ref, qf_ref, ff_ref, if_ref, qb_ref, fb_ref, ib_ref, of_ref, ob_ref,
               sf_ref, sb_ref, snf_ref, snb_ref):
    @pl.when(pl.program_id(1) == 0)
    def _():
        for ref in (sf_ref, sb_ref, snf_ref, snb_ref):
            ref[...] = jnp.zeros_like(ref)

    l0 = lbl_ref[0]
    l1 = lbl_ref[1]
    m = jnp.maximum(l0, l1)
    e0 = jnp.exp(l0 - m)
    e1 = jnp.exp(l1 - m)
    lb = e0 / (e0 + e1)
    n_chunks = HGRN_TILE // HGRN_CHUNK

    def chunk(ci):
        _hgrn_dir(qf_ref, ff_ref, if_ref, of_ref, sf_ref, snf_ref, lb[0:1, :], ci * HGRN_CHUNK, False)
        _hgrn_dir(qb_ref, fb_ref, ib_ref, ob_ref, sb_ref, snb_ref, lb[1:2, :],
                  (n_chunks - 1 - ci) * HGRN_CHUNK, True)

    return chunk


def _mixers_kernel(lbl_ref, qf_ref, ff_ref, if_ref, qb_ref, fb_ref, ib_ref, nq_ref, nk_ref, nv_ref, tab_ref,
                   of_ref, ob_ref, on_ref, sf_ref, sb_ref, snf_ref, snb_ref, *, grid_rows):
    chunk = _hgrn_tile(lbl_ref, qf_ref, ff_ref, if_ref, qb_ref, fb_ref, ib_ref, of_ref, ob_ref,
                       sf_ref, sb_ref, snf_ref, snb_ref)
    na_row = _na_rows(nq_ref, nk_ref, nv_ref, tab_ref, on_ref, grid_rows)
    for ci in range(HGRN_TILE // HGRN_CHUNK):
        chunk(ci)
        na_row(ci)


def _mixers(proj, lb_logits, tabs):
    b, t, _ = proj.shape
    tt = HGRN_TILE
    nt = t // tt
    d = D_MODEL
    cq, ck, cv = (COL_NA // NA_W, COL_NA // NA_W + 1, COL_NA // NA_W + 2)

    def fwd(col):
        return pl.BlockSpec((1, tt, d), lambda bi, ti: (bi, ti, col))

    def bwd(col):
        return pl.BlockSpec((1, tt, d), lambda bi, ti: (bi, nt - 1 - ti, col))

    return pl.pallas_call(
        functools.partial(_mixers_kernel, grid_rows=t // GRID_W),
        grid=(b, nt),
        in_specs=[pl.BlockSpec((2, 2, d), lambda bi, ti: (0, 0, 0)),
                  fwd(0), fwd(1), fwd(3), bwd(0), bwd(2), bwd(3),
                  pl.BlockSpec((1, tt, NA_W), lambda bi, ti: (bi, ti, cq)),
                  pl.BlockSpec((1, t, NA_W), lambda bi, ti: (bi, 0, ck), pipeline_mode=pl.Buffered(1)),
                  pl.BlockSpec((1, t, NA_W), lambda bi, ti: (bi, 0, cv), pipeline_mode=pl.Buffered(1)),
                  pl.BlockSpec(tabs.shape, lambda bi, ti: (0, 0, 0, 0), pipeline_mode=pl.Buffered(1))],
        out_specs=[pl.BlockSpec((1, tt, d), lambda bi, ti: (bi, ti, 0)),
                   pl.BlockSpec((1, tt, d), lambda bi, ti: (bi, nt - 1 - ti, 0)),
                   pl.BlockSpec((1, tt, NA_W), lambda bi, ti: (bi, ti, 0))],
        out_shape=[jax.ShapeDtypeStruct((b, t, d), BF16), jax.ShapeDtypeStruct((b, t, d), BF16),
                   jax.ShapeDtypeStruct((b, t, NA_W), BF16)],
        scratch_shapes=[pltpu.VMEM((HA_HEADS, HA_DK, HA_DK), F32), pltpu.VMEM((HA_HEADS, HA_DK, HA_DK), F32),
                        pltpu.VMEM((HA_HEADS, HA_DK, HA_DK), BF16), pltpu.VMEM((HA_HEADS, HA_DK, HA_DK), BF16)],
        compiler_params=_cparams(("arbitrary", "arbitrary")),
        name="mixers",
    )(lb_logits, proj, proj, proj, proj, proj, proj, proj, proj, proj, tabs)


def _na_bias_tables(rpb):
    q = np.arange(GRID_W)[:, None]
    kc = np.arange(GRID_W)[None, :]
    c0 = np.clip(q - NA_KW // 2, 0, GRID_W - NA_KW)
    ok = (kc >= c0) & (kc < c0 + NA_KW)
    dc = np.clip(kc - q + (NA_KW - 1), 0, 2 * NA_KW - 2)
    base = jnp.where(ok[None, None], jnp.take(rpb.astype(F32), dc, axis=2), NEG_BIG)
    tabs = []
    for dr0 in range(NA_KH):
        w = base[:, dr0:dr0 + NA_KH]
        w = w.transpose(0, 2, 1, 3).reshape(NA_HEADS // 2, 2 * GRID_W, NA_KH * GRID_W)
        tabs.append(w)
    return jnp.stack(tabs, axis=1)


def _na_rows(q_ref, k_ref, v_ref, tab_ref, o_ref, grid_rows):
    g = pl.program_id(1)
    lane = lax.broadcasted_iota(jnp.int32, (GRID_W, 2 * NA_DH), 1)
    low = lane < NA_DH
    scale = NA_DH ** -0.5
    rows_per_step = HGRN_TILE // GRID_W

    def na_row(rr):
        r = g * rows_per_step + rr
        r0 = jnp.clip(r - NA_KH // 2, 0, grid_rows - NA_KH)
        dr0 = r0 - r + (NA_KH - 1)
        qrow = pl.ds(rr * GRID_W, GRID_W)
        krow = pl.ds(pl.multiple_of(r0 * GRID_W, GRID_W), NA_KH * GRID_W)
        for j in range(NA_HEADS // 2):
            sl = slice(j * 2 * NA_DH, (j + 1) * 2 * NA_DH)
            qp = q_ref[0, qrow, sl]
            zero = jnp.zeros_like(qp)
            ql = jnp.concatenate([jnp.where(low, qp, zero), jnp.where(low, zero, qp)], axis=0)
            s = _dot_nt(ql, k_ref[0, krow, sl]) * scale + tab_ref[j, dr0]
            m = jnp.max(s, axis=-1, keepdims=True)
            p = jnp.exp(s - m)
            l = jnp.sum(p, axis=-1, keepdims=True)
            pv = _dot(p.astype(BF16), v_ref[0, krow, sl]) / l
            o_ref[0, qrow, sl] = jnp.where(low, pv[:GRID_W], pv[GRID_W:]).astype(BF16)

    return na_row


def _mix_kernel(x_ref, mod_ref, of_ref, ob_ref, g_ref, ga_ref, gb_ref, nb_ref,
                gn_ref, gpm_ref, gpf_ref, wa_ref, wb_ref, wo_ref, rw_ref, rb_ref,
                x1_ref, xn_ref, ri_ref, rg_ref, cnt_ref, tri_ref, carry_ref):
    tm = MIX_TM
    first = (pl.program_id(0) == 0) & (pl.program_id(1) == 0)

    @pl.when(first)
    def _():
        r = lax.broadcasted_iota(jnp.int32, (tm, tm), 0)
        c = lax.broadcasted_iota(jnp.int32, (tm, tm), 1)
        tri_ref[...] = jnp.where(r < c, 1.0, 0.0).astype(BF16)
        carry_ref[...] = jnp.zeros_like(carry_ref)

    sc2 = mod_ref[0, 4:5, :]
    sh2 = mod_ref[0, 3:4, :]
    gt1 = mod_ref[0, 2:3, :]

    o = of_ref[0].astype(F32) + ob_ref[0].astype(F32)
    parts = []
    for h in range(HA_HEADS):
        oh = o[:, h * HA_DK:(h + 1) * HA_DK]
        parts.append(oh * lax.rsqrt(jnp.mean(oh * oh, axis=-1, keepdims=True) + RMS_EPS))
    g = g_ref[0].astype(F32)
    oa = jnp.concatenate(parts, axis=-1) * gn_ref[...] * (g * _sigmoid(g))
    ya = _dot(oa.astype(BF16), wa_ref[...])
    yb = _dot(nb_ref[0], wb_ref[...])
    mix = _sigmoid(ga_ref[0].astype(F32)) * ya + _sigmoid(gb_ref[0].astype(F32)) * yb
    mo = _dot(mix.astype(BF16), wo_ref[...])
    x1 = x_ref[0] + gt1 * _rms(mo, gpm_ref[...])
    x1_ref[0] = x1

    xn = _rms(x1, gpf_ref[...]) * (1.0 + sc2) + sh2
    xn_ref[0] = xn.reshape(tm, ROW_SUB, ROW_LANE)
    logits = _dot_nt(rw_ref[...], xn.astype(BF16)) + rb_ref[...]
    sub = lax.broadcasted_iota(jnp.int32, (N_EXPERTS, tm), 0)
    work = logits
    vals, idxs, hots = [], [], []
    for _ in range(TOP_K):
        mv = jnp.max(work, axis=0, keepdims=True)
        mi = jnp.min(jnp.where(work == mv, sub, N_EXPERTS), axis=0, keepdims=True)
        hot = sub == mi
        vals.append(mv)
        idxs.append(mi)
        hots.append(hot)
        work = jnp.where(hot, -jnp.inf, work)
    es = [jnp.exp(v - vals[0]) for v in vals]
    den = es[0] + es[1] + es[2] + es[3]

    cnt = sum(jnp.where(h, 1.0, 0.0) for h in hots)
    before = _dot(cnt.astype(BF16), tri_ref[...]) + carry_ref[...]
    carry_ref[...] = carry_ref[...] + jnp.sum(cnt, axis=1, keepdims=True)
    cnt_ref[...] = jnp.broadcast_to(carry_ref[...], cnt_ref.shape)

    ranks = [jnp.sum(jnp.where(hots[kk], before, 0.0), axis=0, keepdims=True).astype(jnp.int32)
             for kk in range(TOP_K)]
    ri_ref[...] = jnp.concatenate(idxs + ranks, axis=0)
    gates = jnp.concatenate([e / den for e in es] + [jnp.zeros((128 - TOP_K, tm), F32)], axis=0)
    rg_ref[0] = gates.T


def _mix(x, mod, proj, o_fw, o_bw, o_b, hgrn_norm, g_post_mix, g_pre_ffn, wa, wb, wo, rw, rb):
    b, t, d = x.shape
    tm = min(MIX_TM, t)
    assert tm == MIX_TM
    row = lambda a: a.reshape(1, -1)
    tok = lambda w, col: pl.BlockSpec((1, tm, w), lambda bi, i: (bi, i, col))
    full = lambda a: pl.BlockSpec(a.shape, lambda bi, i: (0,) * a.ndim)
    gn, gpm, gpf, rbr = row(hgrn_norm), row(g_post_mix), row(g_pre_ffn), rb.reshape(-1, 1)
    rw = rw.T
    outs = pl.pallas_call(
        _mix_kernel,
        grid=(b, t // tm),
        in_specs=[tok(d, 0),
                  pl.BlockSpec((1, 6, d), lambda bi, i: (bi, 0, 0)),
                  tok(d, 0), tok(d, 0),
                  tok(d, 4),
                  tok(d, COL_GATES // d), tok(d, COL_GATES // d + 1),
                  tok(NA_W, 0),
                  full(gn), full(gpm), full(gpf), full(wa), full(wb), full(wo), full(rw), full(rbr)],
        out_specs=[tok(d, 0), pl.BlockSpec((1, tm, ROW_SUB, ROW_LANE), lambda bi, i: (bi, i, 0, 0)),
                   pl.BlockSpec((8, tm), lambda bi, i: (0, bi * (t // tm) + i)), tok(128, 0),
                   pl.BlockSpec((N_EXPERTS, 128), lambda bi, i: (0, 0))],
        out_shape=[jax.ShapeDtypeStruct((b, t, d), F32), jax.ShapeDtypeStruct((b, t, ROW_SUB, ROW_LANE), F32),
                   jax.ShapeDtypeStruct((8, b * t), jnp.int32), jax.ShapeDtypeStruct((b, t, 128), F32),
                   jax.ShapeDtypeStruct((N_EXPERTS, 128), F32)],
        scratch_shapes=[pltpu.VMEM((tm, tm), BF16), pltpu.VMEM((N_EXPERTS, 1), F32)],
        compiler_params=_cparams(("arbitrary", "arbitrary")),
        name="mix",
    )(x, mod, o_fw, o_bw, proj, proj, proj, o_b, gn, gpm, gpf, wa, wb, wo, rw, rbr)
    return outs


SC_CORES, SC_SUBCORES = 2, 16
SC_ROWS = 32


def _scatter_rows(xn, dest_t, n_rows):
    n_tok = xn.shape[0]
    workers = SC_CORES * SC_SUBCORES
    per_w = n_tok // workers
    n_it = per_w // SC_ROWS
    assert per_w * workers == n_tok and n_it * SC_ROWS == per_w and n_it % 2 == 0
    idx = dest_t.reshape(TOP_K, n_tok // SC_ROWS, SC_ROWS)
    mesh = plsc.VectorSubcoreMesh(core_axis_name="c", subcore_axis_name="s")

    @functools.partial(
        pl.kernel, mesh=mesh,
        out_type=jax.ShapeDtypeStruct((n_rows, ROW_SUB, ROW_LANE), xn.dtype),
        scratch_types=[pltpu.VMEM((2, TOP_K, SC_ROWS), jnp.int32),
                       pltpu.VMEM((2, SC_ROWS, ROW_SUB, ROW_LANE), xn.dtype),
                       pltpu.SemaphoreType.DMA((2,)), pltpu.SemaphoreType.DMA((2,))],
    )
    def scatter_kernel(x_hbm, idx_hbm, out_hbm, idx_v, rows_v, sem_load, sem_store):
        chunk0 = (lax.axis_index("s") * SC_CORES + lax.axis_index("c")) * n_it

        def load(g, slot):
            tok = pl.ds(pl.multiple_of((chunk0 + g) * SC_ROWS, SC_ROWS), SC_ROWS)
            return pltpu.make_async_copy(x_hbm.at[tok], rows_v.at[slot], sem_load.at[slot])

        def start_load(g, slot):
            load(g, slot).start()
            for kk in range(TOP_K):
                pltpu.sync_copy(idx_hbm.at[kk, chunk0 + g], idx_v.at[slot, kk])

        def store(slot, kk):
            return pltpu.make_async_copy(rows_v.at[slot], out_hbm.at[idx_v.at[slot, kk]], sem_store.at[slot])

        def scatter(g, slot):
            load(g, slot).wait()
            for kk in range(TOP_K):
                store(slot, kk).start()
            for kk in range(TOP_K):
                store(slot, kk).wait()

        start_load(0, 0)

        @pl.loop(0, n_it, step=2)
        def _(g):
            start_load(g + 1, 1)
            scatter(g, 0)

            @pl.when(g + 2 < n_it)
            def _():
                start_load(g + 2, 0)

            scatter(g + 1, 1)

    return scatter_kernel(xn, idx)


GU_SLAB = 256


def _gate_up_layout_kernel(w_ref, o_ref):
    half = GU_SLAB // 2
    r = lax.broadcasted_iota(jnp.int32, (GU_SLAB, GU_SLAB), 0)
    c = lax.broadcasted_iota(jnp.int32, (GU_SLAB, GU_SLAB), 1)
    src = jnp.where(c < half, 2 * c, 2 * (c - half) + 1)
    pm = jnp.where(r == src, 1.0, 0.0).astype(BF16)
    for s in range(w_ref.shape[2] // GU_SLAB):
        cols = slice(s * GU_SLAB, (s + 1) * GU_SLAB)
        o_ref[0, :, cols] = _dot(w_ref[0, :, cols].astype(BF16), pm).astype(BF16)


def _gate_up_layout(w_gate_up, b_gate_up):
    e, k, n2 = w_gate_up.shape
    half = GU_SLAB // 2
    tk = 512
    w = pl.pallas_call(
        _gate_up_layout_kernel,
        grid=(e, k // tk),
        in_specs=[pl.BlockSpec((1, tk, n2), lambda i, j: (i, j, 0))],
        out_specs=pl.BlockSpec((1, tk, n2), lambda i, j: (i, j, 0)),
        out_shape=jax.ShapeDtypeStruct((e, k, n2), BF16),
        compiler_params=_cparams(("arbitrary", "arbitrary")),
        name="gate_up_layout",
    )(w_gate_up)
    b = b_gate_up.reshape(e, n2 // GU_SLAB, half, 2).transpose(0, 1, 3, 2).reshape(e, 1, n2)
    return w, b


def _expert_kernel(be_ref, bv_ref, x_ref, wgu_ref, bgu_ref, wd_ref, bd_ref, o_ref, a_ref):
    i = pl.program_id(0)
    valid = bv_ref[i]
    half = GU_SLAB // 2

    @pl.when(valid > 0)
    def _():
        blk = x_ref.shape[0]
        rowi = lax.broadcasted_iota(jnp.int32, (blk, 1), 0)
        x = x_ref[...].reshape(blk, D_MODEL)
        x = jnp.where(rowi < valid, x, 0.0).astype(BF16)
        for s in range(2 * D_FF // GU_SLAB):
            cols = slice(s * GU_SLAB, (s + 1) * GU_SLAB)
            h = _dot(x, wgu_ref[0, :, cols]) + bgu_ref[0, :, cols]
            glu = jnp.minimum(h[:, :half], SWIGLU_LIMIT)
            lin = jnp.clip(h[:, half:], -SWIGLU_LIMIT, SWIGLU_LIMIT)
            a = glu * _sigmoid(SWIGLU_ALPHA * glu) * (lin + 1.0)
            a_ref[:, s * half:(s + 1) * half] = a.astype(BF16)
        o_ref[...] = (_dot(a_ref[...], wd_ref[0]) + bd_ref[0]).reshape(o_ref.shape)

    @pl.when(valid <= 0)
    def _():
        o_ref[...] = jnp.zeros_like(o_ref)


def _experts(xs, block_e, block_valid, wgu, bgu, wd, bd):
    n_rows = xs.shape[0]
    d = D_MODEL
    n_blocks = block_e.shape[0]
    blk = n_rows // n_blocks
    emap = lambda i, be, bv: (be[i], 0, 0)
    rows = lambda: pl.BlockSpec((blk, ROW_SUB, ROW_LANE), lambda i, be, bv: (i, 0, 0))
    return pl.pallas_call(
        _expert_kernel,
        grid_spec=pltpu.PrefetchScalarGridSpec(
            num_scalar_prefetch=2,
            grid=(n_blocks,),
            in_specs=[rows(),
                      pl.BlockSpec((1, d, 2 * D_FF), emap), pl.BlockSpec((1, 1, 2 * D_FF), emap),
                      pl.BlockSpec((1, D_FF, d), emap), pl.BlockSpec((1, 1, d), emap)],
            out_specs=rows(),
            scratch_shapes=[pltpu.VMEM((blk, D_FF), BF16)],
        ),
        out_shape=jax.ShapeDtypeStruct((n_rows, ROW_SUB, ROW_LANE), F32),
        compiler_params=_cparams(("arbitrary",)),
        name="experts",
    )(block_e, block_valid, xs, wgu, bgu, wd, bd)


def _gather_rows(y, dest_flat):
    n_out = dest_flat.shape[0]
    workers = SC_CORES * SC_SUBCORES
    per_w = n_out // workers
    n_it = per_w // SC_ROWS
    assert per_w * workers == n_out and n_it * SC_ROWS == per_w and n_it % 2 == 0
    mesh = plsc.VectorSubcoreMesh(core_axis_name="c", subcore_axis_name="s")

    @functools.partial(
        pl.kernel, mesh=mesh,
        out_type=jax.ShapeDtypeStruct((n_out, ROW_SUB, ROW_LANE), y.dtype),
        scratch_types=[pltpu.VMEM((2, SC_ROWS), jnp.int32),
                       pltpu.VMEM((2, SC_ROWS, ROW_SUB, ROW_LANE), y.dtype),
                       pltpu.SemaphoreType.DMA((2,))],
    )
    def gather_kernel(y_hbm, idx_hbm, out_hbm, idx_v, rows_v, sem):
        base = (lax.axis_index("s") * SC_CORES + lax.axis_index("c")) * per_w

        def span(g):
            return pl.ds(pl.multiple_of(base + g * SC_ROWS, SC_ROWS), SC_ROWS)

        def row_gather(slot):
            return pltpu.make_async_copy(y_hbm.at[idx_v.at[slot]], rows_v.at[slot], sem.at[slot])

        def start(g, slot):
            pltpu.sync_copy(idx_hbm.at[span(g)], idx_v.at[slot])
            row_gather(slot).start()

        start(0, 0)

        @pl.loop(0, n_it, step=2)
        def _(g):
            start(g + 1, 1)
            row_gather(0).wait()
            pltpu.sync_copy(rows_v.at[0], out_hbm.at[span(g)])

            @pl.when(g + 2 < n_it)
            def _():
                start(g + 2, 0)

            row_gather(1).wait()
            pltpu.sync_copy(rows_v.at[1], out_hbm.at[span(g + 1)])

    return gather_kernel(y, dest_flat)


def _combine_kernel(yg_ref, x1_ref, mod_ref, rg_ref, gpo_ref, o_ref):
    rg = rg_ref[...]
    h = rg[:, 0:1] * yg_ref[0].reshape(COMB_TOK, D_MODEL)
    for kk in range(1, TOP_K):
        h = h + rg[:, kk:kk + 1] * yg_ref[kk].reshape(COMB_TOK, D_MODEL)
    gt2 = mod_ref[0, 5:6, :]
    o_ref[...] = x1_ref[...] + gt2 * _rms(h, gpo_ref[...])


def _combine(yg, x1, mod, rg, g_post_ffn, tokens_per_batch):
    n, d = x1.shape
    per_b = tokens_per_batch // COMB_TOK
    gpo = g_post_ffn.reshape(1, d)
    return pl.pallas_call(
        _combine_kernel,
        grid=(n // COMB_TOK,),
        in_specs=[pl.BlockSpec((TOP_K, COMB_TOK, ROW_SUB, ROW_LANE), lambda i: (0, i, 0, 0)),
                  pl.BlockSpec((COMB_TOK, d), lambda i: (i, 0)),
                  pl.BlockSpec((1, 6, d), lambda i: (i // per_b, 0, 0)),
                  pl.BlockSpec((COMB_TOK, 128), lambda i: (i, 0)),
                  pl.BlockSpec((1, d), lambda i: (0, 0))],
        out_specs=pl.BlockSpec((COMB_TOK, d), lambda i: (i, 0)),
        out_shape=jax.ShapeDtypeStruct((n, d), F32),
        compiler_params=_cparams(("arbitrary",)),
        name="combine",
    )(yg, x1, mod, rg, gpo)


def _moe_plan(ri_t, counts, n_tok):
    idx = ri_t[:TOP_K]
    rank = ri_t[TOP_K:2 * TOP_K]
    cnt = counts.astype(jnp.int32)
    blk = MOE_BLK_LARGE if n_tok * TOP_K >= 8 * MOE_BLK_LARGE * N_EXPERTS else MOE_BLK_SMALL
    padded = (cnt + blk - 1) // blk * blk
    pend = jnp.cumsum(padded)
    pstart = pend - padded
    dest = rank
    for e in range(N_EXPERTS):
        dest = dest + jnp.where(idx == e, pstart[e], 0)
    n_blocks = (n_tok * TOP_K + blk - 1) // blk + N_EXPERTS
    bstart = jnp.arange(n_blocks, dtype=jnp.int32) * blk
    block_e = jnp.minimum(jnp.sum((bstart[:, None] >= pend[None, :]).astype(jnp.int32), axis=1), N_EXPERTS - 1)
    block_valid = jnp.clip(pstart[block_e] + cnt[block_e] - bstart, 0, blk).astype(jnp.int32)
    return dest.astype(jnp.int32), block_e, block_valid, n_blocks * blk


def _layer(x, mod, p):
    b, t, d = x.shape
    n = b * t
    proj = _in_proj(x, mod, p['g_pre_mix'], p['w_in'])
    o_fw, o_bw, o_b = _mixers(proj, p['lb_logits'], p['na_tabs'])
    x1, xn, ri, rg, counts = _mix(x, mod, proj, o_fw, o_bw, o_b, p['hgrn_norm'], p['g_post_mix'], p['g_pre_ffn'],
                                  p['w_branch_a'], p['w_branch_b'], p['w_out'], p['router_w'], p['router_b'])
    rg = rg.reshape(n, 128)
    dest, block_e, block_valid, n_rows = _moe_plan(ri, counts[:, 0], n)
    xs = _scatter_rows(xn.reshape(n, ROW_SUB, ROW_LANE), dest, n_rows)
    y = _experts(xs, block_e, block_valid, p['w_gu'], p['b_gu'], p['w_d'], p['b_d'])
    yg = _gather_rows(y, dest.reshape(-1)).reshape(TOP_K, n, ROW_SUB, ROW_LANE)
    out = _combine(yg, x1.reshape(n, d), mod, rg, p['g_post_ffn'], t)
    return out.reshape(b, t, d)


def kernel(x_prompt, x_sample, c_prompt, c_sample, ada_w, ada_b, g_pre_mix, g_post_mix, g_pre_ffn, g_post_ffn,
           w_in, hgrn_lb_logits, hgrn_norm, na_rpb, w_branch_a, w_branch_b, w_out, router_w, router_b,
           w_gate_up, b_gate_up, w_down, b_down):
    d = D_MODEL
    bp, bs = c_prompt.shape[0], c_sample.shape[0]
    pad = (-(bp + bs)) % 8
    c_all = jnp.concatenate([c_prompt, c_sample, jnp.zeros((pad, d), F32)], axis=0)
    mod = _ada_mod(c_all, ada_w[0], ada_b[0]).reshape(-1, 6, d)

    w = w_in[0]
    w_perm = jnp.concatenate([w[:, :5 * d], w[:, 5 * d + 3 * NA_W:], w[:, 5 * d:5 * d + 3 * NA_W]], axis=1).astype(BF16)
    w_gu, b_gu = _gate_up_layout(w_gate_up[0], b_gate_up[0])
    p = dict(
        g_pre_mix=g_pre_mix[0], g_post_mix=g_post_mix[0], g_pre_ffn=g_pre_ffn[0], g_post_ffn=g_post_ffn[0],
        w_in=w_perm, lb_logits=hgrn_lb_logits.astype(F32), hgrn_norm=hgrn_norm[0],
        na_tabs=_na_bias_tables(na_rpb[0]),
        w_branch_a=w_branch_a[0].astype(BF16), w_branch_b=w_branch_b[0].astype(BF16), w_out=w_out[0].astype(BF16),
        router_w=router_w[0].astype(BF16), router_b=router_b[0],
        w_gu=w_gu, b_gu=b_gu,
        w_d=w_down[0].astype(BF16), b_d=b_down[0][:, None, :],
    )
    y_p = _layer(x_prompt, mod[:bp], p)
    y_s = _layer(x_sample, mod[bp:bp + bs], p)
    return (y_p, y_s)
```

```python
import functools

import numpy as np
import jax
import jax.numpy as jnp
from jax import lax
from jax.experimental import pallas as pl
from jax.experimental.pallas import tpu as pltpu
from jax.experimental.pallas import tpu_sc as plsc

F32 = jnp.float32
BF16 = jnp.bfloat16

D_MODEL = 1024
GRID_W = 64
HA_HEADS = 8
HA_DK = 128
NA_HEADS = 8
NA_DH = 64
NA_W = NA_HEADS * NA_DH
NA_KH = 8
NA_KW = 16
N_EXPERTS = 32
TOP_K = 4
D_FF = 1024
SWIGLU_ALPHA = 1.702
SWIGLU_LIMIT = 7.0
RMS_EPS = 1e-6
NEG_BIG = -1e30

COL_A = 0
COL_GATES = 5 * 1024
COL_NA = 7 * 1024
PROJ_COLS = 7 * 1024 + 3 * NA_W

HGRN_CHUNK = 64
HGRN_TILE = 256
PROJ_TM = 1024
PROJ_TN = 2176
MIX_TM = 512
MOE_BLK_SMALL, MOE_BLK_LARGE = 512, 1024
COMB_TOK = 512
SUBLANES = 8
ROW_SUB, ROW_LANE = 8, 128
VMEM_LIMIT = 56 * 1024 * 1024


def _cparams(sem):
    return pltpu.CompilerParams(dimension_semantics=sem, vmem_limit_bytes=VMEM_LIMIT)


def _dot(a, b):
    return jnp.dot(a, b, preferred_element_type=F32)


def _dot_nt(a, b):
    return lax.dot_general(a, b, (((1,), (1,)), ((), ())), preferred_element_type=F32)


def _dot_tn(a, b):
    return lax.dot_general(a, b, (((0,), (0,)), ((), ())), preferred_element_type=F32)


def _split(a):
    hi = a.astype(BF16)
    lo = (a - hi.astype(F32)).astype(BF16)
    return hi, lo


def _dot3(a, w):
    ah, al = _split(a)
    wh, wl = _split(w)
    return _dot(ah, wh) + (_dot(al, wh) + _dot(ah, wl))


def _sigmoid(x):
    return 0.5 * jnp.tanh(0.5 * x) + 0.5


def _rms(x, g):
    return x * lax.rsqrt(jnp.mean(x * x, axis=-1, keepdims=True) + RMS_EPS) * g


def _ada_kernel(c_ref, w_ref, b_ref, o_ref):
    c = c_ref[...]
    o_ref[...] = _dot3(c * _sigmoid(c), w_ref[...]) + b_ref[...]


def _ada_mod(c, ada_w, ada_b):
    bp, d = c.shape
    n = ada_w.shape[1]
    tn = 1536
    return pl.pallas_call(
        _ada_kernel,
        grid=(n // tn,),
        in_specs=[pl.BlockSpec((bp, d), lambda j: (0, 0)),
                  pl.BlockSpec((d, tn), lambda j: (0, j)),
                  pl.BlockSpec((1, tn), lambda j: (0, j))],
        out_specs=pl.BlockSpec((bp, tn), lambda j: (0, j)),
        out_shape=jax.ShapeDtypeStruct((bp, n), F32),
        compiler_params=_cparams(("arbitrary",)),
        name="ada_mod",
    )(c, ada_w, ada_b.reshape(1, n))


def _in_proj_kernel(x_ref, mod_ref, g_ref, w_ref, o_ref, xn_ref):
    @pl.when(pl.program_id(2) == 0)
    def _():
        sh = mod_ref[0, 0:1, :]
        sc = mod_ref[0, 1:2, :]
        xn = _rms(x_ref[0], g_ref[...]) * (1.0 + sc) + sh
        xn_ref[...] = xn.astype(BF16)

    o_ref[0] = _dot(xn_ref[...], w_ref[...]).astype(BF16)


def _in_proj(x, mod, g_pre, w_in_bf):
    b, t, d = x.shape
    tm = min(PROJ_TM, t)
    n = w_in_bf.shape[1]
    return pl.pallas_call(
        _in_proj_kernel,
        grid=(b, t // tm, n // PROJ_TN),
        in_specs=[pl.BlockSpec((1, tm, d), lambda bi, i, j: (bi, i, 0)),
                  pl.BlockSpec((1, 6, d), lambda bi, i, j: (bi, 0, 0)),
                  pl.BlockSpec((1, d), lambda bi, i, j: (0, 0)),
                  pl.BlockSpec((d, PROJ_TN), lambda bi, i, j: (0, j))],
        out_specs=pl.BlockSpec((1, tm, PROJ_TN), lambda bi, i, j: (bi, i, j)),
        out_shape=jax.ShapeDtypeStruct((b, t, n), BF16),
        scratch_shapes=[pltpu.VMEM((tm, d), BF16)],
        compiler_params=_cparams(("arbitrary", "arbitrary", "arbitrary")),
        name="in_proj",
    )(x, mod, g_pre.reshape(1, d), w_in_bf)


def _cumsum_rows(x, reverse):
    groups = x.shape[0] // SUBLANES
    row = lax.broadcasted_iota(jnp.int32, (SUBLANES, x.shape[1]), 0)
    out = [None] * groups
    carry = None
    for g in (range(groups - 1, -1, -1) if reverse else range(groups)):
        blk = x[g * SUBLANES:(g + 1) * SUBLANES]
        step = 1
        while step < SUBLANES:
            if reverse:
                blk = blk + jnp.where(row < SUBLANES - step, pltpu.roll(blk, SUBLANES - step, axis=0), 0.0)
            else:
                blk = blk + jnp.where(row >= step, pltpu.roll(blk, step, axis=0), 0.0)
            step *= 2
        if carry is not None:
            blk = blk + carry
        carry = blk[0:1] if reverse else blk[SUBLANES - 1:SUBLANES]
        out[g] = blk
    return jnp.concatenate(out, axis=0)


def _hgrn_dir(q_ref, f_ref, i_ref, o_ref, s_ref, sn_ref, lbd, row0, reverse):
    c = HGRN_CHUNK
    rows = pl.ds(row0, c)
    q = q_ref[0, rows, :].astype(F32)
    fpre = f_ref[0, rows, :].astype(F32)
    v = i_ref[0, rows, :]
    f = 0.5 * (1.0 + lbd) + (0.5 * (1.0 - lbd)) * jnp.tanh(0.5 * fpre)
    logf = jnp.log(f)
    k = 1.0 - f
    ri = lax.broadcasted_iota(jnp.int32, (c, c), 0)
    ci = lax.broadcasted_iota(jnp.int32, (c, c), 1)
    keep = (ci >= ri) if reverse else (ri >= ci)
    b = _cumsum_rows(logf, reverse)
    if reverse:
        b_mid = b[c // 2:c // 2 + 1, :]
        b_end = b[0:1, :]
    else:
        b_mid = b[c // 2 - 1:c // 2, :]
        b_end = b[c - 1:c, :]
    d = b - b_mid
    qm = q * jnp.exp(d)
    km = k * jnp.exp(-d)
    qi = (qm * jnp.exp(b_mid)).astype(BF16)
    ks = (km * jnp.exp(b_end - b_mid)).astype(BF16)
    qm = qm.astype(BF16)
    km = km.astype(BF16)
    dec = jnp.exp(b_end)
    for h in range(HA_HEADS):
        sl = slice(h * HA_DK, (h + 1) * HA_DK)
        a = _dot_nt(qm[:, sl], km[:, sl])
        a = jnp.where(keep, a, 0.0).astype(BF16)
        lhs = jnp.concatenate([qi[:, sl], a], axis=1)
        rhs = jnp.concatenate([sn_ref[h], v[:, sl]], axis=0)
        o_ref[0, rows, sl] = _dot(lhs, rhs).astype(BF16)
        st = s_ref[h] * dec[:, sl] + _dot_tn(v[:, sl], ks[:, sl])
        s_ref[h] = st
        sn_ref[h] = st.astype(BF16).T


def _hgrn_tile(lbl_ref, qf_ref, ff_ref, if_ref, qb_ref, fb_ref, ib_ref, of_ref, ob_ref,
               sf_ref, sb_ref, snf_ref, snb_ref):
    @pl.when(pl.program_id(1) == 0)
    def _():
        for ref in (sf_ref, sb_ref, snf_ref, snb_ref):
            ref[...] = jnp.zeros_like(ref)

    l0 = lbl_ref[0]
    l1 = lbl_ref[1]
    m = jnp.maximum(l0, l1)
    e0 = jnp.exp(l0 - m)
    e1 = jnp.exp(l1 - m)
    lb = e0 / (e0 + e1)
    n_chunks = HGRN_TILE // HGRN_CHUNK

    def chunk(ci):
        _hgrn_dir(qf_ref, ff_ref, if_ref, of_ref, sf_ref, snf_ref, lb[0:1, :], ci * HGRN_CHUNK, False)
        _hgrn_dir(qb_ref, fb_ref, ib_ref, ob_ref, sb_ref, snb_ref, lb[1:2, :],
                  (n_chunks - 1 - ci) * HGRN_CHUNK, True)

    return chunk


def _mixers_kernel(lbl_ref, qf_ref, ff_ref, if_ref, qb_ref, fb_ref, ib_ref, nq_ref, nk_ref, nv_ref, tab_ref,
                   of_ref, ob_ref, on_ref, sf_ref, sb_ref, snf_ref, snb_ref, *, grid_rows):
    chunk = _hgrn_tile(lbl_ref, qf_ref, ff_ref, if_ref, qb_ref, fb_ref, ib_ref, of_ref, ob_ref,
                       sf_ref, sb_ref, snf_ref, snb_ref)
    na_row = _na_rows(nq_ref, nk_ref, nv_ref, tab_ref, on_ref, grid_rows)
    for ci in range(HGRN_TILE // HGRN_CHUNK):
        chunk(ci)
        na_row(ci)


def _mixers(proj, lb_logits, tabs):
    b, t, _ = proj.shape
    tt = HGRN_TILE
    nt = t // tt
    d = D_MODEL
    cq, ck, cv = (COL_NA // NA_W, COL_NA // NA_W + 1, COL_NA // NA_W + 2)

    def fwd(col):
        return pl.BlockSpec((1, tt, d), lambda bi, ti: (bi, ti, col))

    def bwd(col):
        return pl.BlockSpec((1, tt, d), lambda bi, ti: (bi, nt - 1 - ti, col))

    return pl.pallas_call(
        functools.partial(_mixers_kernel, grid_rows=t // GRID_W),
        grid=(b, nt),
        in_specs=[pl.BlockSpec((2, 2, d), lambda bi, ti: (0, 0, 0)),
                  fwd(0), fwd(1), fwd(3), bwd(0), bwd(2), bwd(3),
                  pl.BlockSpec((1, tt, NA_W), lambda bi, ti: (bi, ti, cq)),
                  pl.BlockSpec((1, t, NA_W), lambda bi, ti: (bi, 0, ck), pipeline_mode=pl.Buffered(1)),
                  pl.BlockSpec((1, t, NA_W), lambda bi, ti: (bi, 0, cv), pipeline_mode=pl.Buffered(1)),
                  pl.BlockSpec(tabs.shape, lambda bi, ti: (0, 0, 0, 0), pipeline_mode=pl.Buffered(1))],
        out_specs=[pl.BlockSpec((1, tt, d), lambda bi, ti: (bi, ti, 0)),
                   pl.BlockSpec((1, tt, d), lambda bi, ti: (bi, nt - 1 - ti, 0)),
                   pl.BlockSpec((1, tt, NA_W), lambda bi, ti: (bi, ti, 0))],
        out_shape=[jax.ShapeDtypeStruct((b, t, d), BF16), jax.ShapeDtypeStruct((b, t, d), BF16),
                   jax.ShapeDtypeStruct((b, t, NA_W), BF16)],
        scratch_shapes=[pltpu.VMEM((HA_HEADS, HA_DK, HA_DK), F32), pltpu.VMEM((HA_HEADS, HA_DK, HA_DK), F32),
                        pltpu.VMEM((HA_HEADS, HA_DK, HA_DK), BF16), pltpu.VMEM((HA_HEADS, HA_DK, HA_DK), BF16)],
        compiler_params=_cparams(("arbitrary", "arbitrary")),
        name="mixers",
    )(lb_logits, proj, proj, proj, proj, proj, proj, proj, proj, proj, tabs)


def _na_bias_tables(rpb):
    q = np.arange(GRID_W)[:, None]
    kc = np.arange(GRID_W)[None, :]
    c0 = np.clip(q - NA_KW // 2, 0, GRID_W - NA_KW)
    ok = (kc >= c0) & (kc < c0 + NA_KW)
    dc = np.clip(kc - q + (NA_KW - 1), 0, 2 * NA_KW - 2)
    base = jnp.where(ok[None, None], jnp.take(rpb.astype(F32), dc, axis=2), NEG_BIG)
    tabs = []
    for dr0 in range(NA_KH):
        w = base[:, dr0:dr0 + NA_KH]
        w = w.transpose(0, 2, 1, 3).reshape(NA_HEADS // 2, 2 * GRID_W, NA_KH * GRID_W)
        tabs.append(w)
    return jnp.stack(tabs, axis=1)


def _na_rows(q_ref, k_ref, v_ref, tab_ref, o_ref, grid_rows):
    g = pl.program_id(1)
    lane = lax.broadcasted_iota(jnp.int32, (GRID_W, 2 * NA_DH), 1)
    low = lane < NA_DH
    scale = NA_DH ** -0.5
    rows_per_step = HGRN_TILE // GRID_W

    def na_row(rr):
        r = g * rows_per_step + rr
        r0 = jnp.clip(r - NA_KH // 2, 0, grid_rows - NA_KH)
        dr0 = r0 - r + (NA_KH - 1)
        qrow = pl.ds(rr * GRID_W, GRID_W)
        krow = pl.ds(pl.multiple_of(r0 * GRID_W, GRID_W), NA_KH * GRID_W)
        for j in range(NA_HEADS // 2):
            sl = slice(j * 2 * NA_DH, (j + 1) * 2 * NA_DH)
            qp = q_ref[0, qrow, sl]
            zero = jnp.zeros_like(qp)
            ql = jnp.concatenate([jnp.where(low, qp, zero), jnp.where(low, zero, qp)], axis=0)
            s = _dot_nt(ql, k_ref[0, krow, sl]) * scale + tab_ref[j, dr0]
            m = jnp.max(s, axis=-1, keepdims=True)
            p = jnp.exp(s - m)
            l = jnp.sum(p, axis=-1, keepdims=True)
            pv = _dot(p.astype(BF16), v_ref[0, krow, sl]) / l
            o_ref[0, qrow, sl] = jnp.where(low, pv[:GRID_W], pv[GRID_W:]).astype(BF16)

    return na_row


def _mix_kernel(x_ref, mod_ref, of_ref, ob_ref, g_ref, ga_ref, gb_ref, nb_ref,
                gn_ref, gpm_ref, gpf_ref, wa_ref, wb_ref, wo_ref, rw_ref, rb_ref,
                x1_ref, xn_ref, ri_ref, rg_ref, cnt_ref, tri_ref, carry_ref):
    tm = MIX_TM
    first = (pl.program_id(0) == 0) & (pl.program_id(1) == 0)

    @pl.when(first)
    def _():
        r = lax.broadcasted_iota(jnp.int32, (tm, tm), 0)
        c = lax.broadcasted_iota(jnp.int32, (tm, tm), 1)
        tri_ref[...] = jnp.where(r < c, 1.0, 0.0).astype(BF16)
        carry_ref[...] = jnp.zeros_like(carry_ref)

    sc2 = mod_ref[0, 4:5, :]
    sh2 = mod_ref[0, 3:4, :]
    gt1 = mod_ref[0, 2:3, :]

    o = of_ref[0].astype(F32) + ob_ref[0].astype(F32)
    parts = []
    for h in range(HA_HEADS):
        oh = o[:, h * HA_DK:(h + 1) * HA_DK]
        parts.append(oh * lax.rsqrt(jnp.mean(oh * oh, axis=-1, keepdims=True) + RMS_EPS))
    g = g_ref[0].astype(F32)
    oa = jnp.concatenate(parts, axis=-1) * gn_ref[...] * (g * _sigmoid(g))
    ya = _dot(oa.astype(BF16), wa_ref[...])
    yb = _dot(nb_ref[0], wb_ref[...])
    mix = _sigmoid(ga_ref[0].astype(F32)) * ya + _sigmoid(gb_ref[0].astype(F32)) * yb
    mo = _dot(mix.astype(BF16), wo_ref[...])
    x1 = x_ref[0] + gt1 * _rms(mo, gpm_ref[...])
    x1_ref[0] = x1

    xn = _rms(x1, gpf_ref[...]) * (1.0 + sc2) + sh2
    xn_ref[0] = xn.reshape(tm, ROW_SUB, ROW_LANE)
    logits = _dot_nt(rw_ref[...], xn.astype(BF16)) + rb_ref[...]
    sub = lax.broadcasted_iota(jnp.int32, (N_EXPERTS, tm), 0)
    work = logits
    vals, idxs, hots = [], [], []
    for _ in range(TOP_K):
        mv = jnp.max(work, axis=0, keepdims=True)
        mi = jnp.min(jnp.where(work == mv, sub, N_EXPERTS), axis=0, keepdims=True)
        hot = sub == mi
        vals.append(mv)
        idxs.append(mi)
        hots.append(hot)
        work = jnp.where(hot, -jnp.inf, work)
    es = [jnp.exp(v - vals[0]) for v in vals]
    den = es[0] + es[1] + es[2] + es[3]

    cnt = sum(jnp.where(h, 1.0, 0.0) for h in hots)
    before = _dot(cnt.astype(BF16), tri_ref[...]) + carry_ref[...]
    carry_ref[...] = carry_ref[...] + jnp.sum(cnt, axis=1, keepdims=True)
    cnt_ref[...] = jnp.broadcast_to(carry_ref[...], cnt_ref.shape)

    ranks = [jnp.sum(jnp.where(hots[kk], before, 0.0), axis=0, keepdims=True).astype(jnp.int32)
             for kk in range(TOP_K)]
    ri_ref[...] = jnp.concatenate(idxs + ranks, axis=0)
    gates = jnp.concatenate([e / den for e in es] + [jnp.zeros((128 - TOP_K, tm), F32)], axis=0)
    rg_ref[0] = gates.T


def _mix(x, mod, proj, o_fw, o_bw, o_b, hgrn_norm, g_post_mix, g_pre_ffn, wa, wb, wo, rw, rb):
    b, t, d = x.shape
    tm = min(MIX_TM, t)
    assert tm == MIX_TM
    row = lambda a: a.reshape(1, -1)
    tok = lambda w, col: pl.BlockSpec((1, tm, w), lambda bi, i: (bi, i, col))
    full = lambda a: pl.BlockSpec(a.shape, lambda bi, i: (0,) * a.ndim)
    gn, gpm, gpf, rbr = row(hgrn_norm), row(g_post_mix), row(g_pre_ffn), rb.reshape(-1, 1)
    rw = rw.T
    outs = pl.pallas_call(
        _mix_kernel,
        grid=(b, t // tm),
        in_specs=[tok(d, 0),
                  pl.BlockSpec((1, 6, d), lambda bi, i: (bi, 0, 0)),
                  tok(d, 0), tok(d, 0),
                  tok(d, 4),
                  tok(d, COL_GATES // d), tok(d, COL_GATES // d + 1),
                  tok(NA_W, 0),
                  full(gn), full(gpm), full(gpf), full(wa), full(wb), full(wo), full(rw), full(rbr)],
        out_specs=[tok(d, 0), pl.BlockSpec((1, tm, ROW_SUB, ROW_LANE), lambda bi, i: (bi, i, 0, 0)),
                   pl.BlockSpec((8, tm), lambda bi, i: (0, bi * (t // tm) + i)), tok(128, 0),
                   pl.BlockSpec((N_EXPERTS, 128), lambda bi, i: (0, 0))],
        out_shape=[jax.ShapeDtypeStruct((b, t, d), F32), jax.ShapeDtypeStruct((b, t, ROW_SUB, ROW_LANE), F32),
                   jax.ShapeDtypeStruct((8, b * t), jnp.int32), jax.ShapeDtypeStruct((b, t, 128), F32),
                   jax.ShapeDtypeStruct((N_EXPERTS, 128), F32)],
        scratch_shapes=[pltpu.VMEM((tm, tm), BF16), pltpu.VMEM((N_EXPERTS, 1), F32)],
        compiler_params=_cparams(("arbitrary", "arbitrary")),
        name="mix",
    )(x, mod, o_fw, o_bw, proj, proj, proj, o_b, gn, gpm, gpf, wa, wb, wo, rw, rbr)
    return outs


SC_CORES, SC_SUBCORES = 2, 16
SC_ROWS = 32


def _scatter_rows(xn, dest_t, n_rows):
    n_tok = xn.shape[0]
    workers = SC_CORES * SC_SUBCORES
    per_w = n_tok // workers
    n_it = per_w // SC_ROWS
    assert per_w * workers == n_tok and n_it * SC_ROWS == per_w and n_it % 2 == 0
    idx = dest_t.reshape(TOP_K, n_tok // SC_ROWS, SC_ROWS)
    mesh = plsc.VectorSubcoreMesh(core_axis_name="c", subcore_axis_name="s")

    @functools.partial(
        pl.kernel, mesh=mesh,
        out_type=jax.ShapeDtypeStruct((n_rows, ROW_SUB, ROW_LANE), xn.dtype),
        scratch_types=[pltpu.VMEM((2, TOP_K, SC_ROWS), jnp.int32),
                       pltpu.VMEM((2, SC_ROWS, ROW_SUB, ROW_LANE), xn.dtype),
                       pltpu.SemaphoreType.DMA((2,)), pltpu.SemaphoreType.DMA((2,))],
    )
    def scatter_kernel(x_hbm, idx_hbm, out_hbm, idx_v, rows_v, sem_load, sem_store):
        chunk0 = (lax.axis_index("s") * SC_CORES + lax.axis_index("c")) * n_it

        def load(g, slot):
            tok = pl.ds(pl.multiple_of((chunk0 + g) * SC_ROWS, SC_ROWS), SC_ROWS)
            return pltpu.make_async_copy(x_hbm.at[tok], rows_v.at[slot], sem_load.at[slot])

        def start_load(g, slot):
            load(g, slot).start()
            for kk in range(TOP_K):
                pltpu.sync_copy(idx_hbm.at[kk, chunk0 + g], idx_v.at[slot, kk])

        def store(slot, kk):
            return pltpu.make_async_copy(rows_v.at[slot], out_hbm.at[idx_v.at[slot, kk]], sem_store.at[slot])

        def scatter(g, slot):
            load(g, slot).wait()
            for kk in range(TOP_K):
                store(slot, kk).start()
            for kk in range(TOP_K):
                store(slot, kk).wait()

        start_load(0, 0)

        @pl.loop(0, n_it, step=2)
        def _(g):
            start_load(g + 1, 1)
            scatter(g, 0)

            @pl.when(g + 2 < n_it)
            def _():
                start_load(g + 2, 0)

            scatter(g + 1, 1)

    return scatter_kernel(xn, idx)


GU_SLAB = 256


def _gate_up_layout_kernel(w_ref, o_ref):
    half = GU_SLAB // 2
    r = lax.broadcasted_iota(jnp.int32, (GU_SLAB, GU_SLAB), 0)
    c = lax.broadcasted_iota(jnp.int32, (GU_SLAB, GU_SLAB), 1)
    src = jnp.where(c < half, 2 * c, 2 * (c - half) + 1)
    pm = jnp.where(r == src, 1.0, 0.0).astype(BF16)
    for s in range(w_ref.shape[2] // GU_SLAB):
        cols = slice(s * GU_SLAB, (s + 1) * GU_SLAB)
        o_ref[0, :, cols] = _dot(w_ref[0, :, cols].astype(BF16), pm).astype(BF16)


def _gate_up_layout(w_gate_up, b_gate_up):
    e, k, n2 = w_gate_up.shape
    half = GU_SLAB // 2
    tk = 512
    w = pl.pallas_call(
        _gate_up_layout_kernel,
        grid=(e, k // tk),
        in_specs=[pl.BlockSpec((1, tk, n2), lambda i, j: (i, j, 0))],
        out_specs=pl.BlockSpec((1, tk, n2), lambda i, j: (i, j, 0)),
        out_shape=jax.ShapeDtypeStruct((e, k, n2), BF16),
        compiler_params=_cparams(("arbitrary", "arbitrary")),
        name="gate_up_layout",
    )(w_gate_up)
    b = b_gate_up.reshape(e, n2 // GU_SLAB, half, 2).transpose(0, 1, 3, 2).reshape(e, 1, n2)
    return w, b


def _expert_kernel(be_ref, bv_ref, x_ref, wgu_ref, bgu_ref, wd_ref, bd_ref, o_ref, a_ref):
    i = pl.program_id(0)
    valid = bv_ref[i]
    half = GU_SLAB // 2

    @pl.when(valid > 0)
    def _():
        blk = x_ref.shape[0]
        rowi = lax.broadcasted_iota(jnp.int32, (blk, 1), 0)
        x = x_ref[...].reshape(blk, D_MODEL)
        x = jnp.where(rowi < valid, x, 0.0).astype(BF16)
        for s in range(2 * D_FF // GU_SLAB):
            cols = slice(s * GU_SLAB, (s + 1) * GU_SLAB)
            h = _dot(x, wgu_ref[0, :, cols]) + bgu_ref[0, :, cols]
            glu = jnp.minimum(h[:, :half], SWIGLU_LIMIT)
            lin = jnp.clip(h[:, half:], -SWIGLU_LIMIT, SWIGLU_LIMIT)
            a = glu * _sigmoid(SWIGLU_ALPHA * glu) * (lin + 1.0)
            a_ref[:, s * half:(s + 1) * half] = a.astype(BF16)
        o_ref[...] = (_dot(a_ref[...], wd_ref[0]) + bd_ref[0]).reshape(o_ref.shape)

    @pl.when(valid <= 0)
    def _():
        o_ref[...] = jnp.zeros_like(o_ref)


def _experts(xs, block_e, block_valid, wgu, bgu, wd, bd):
    n_rows = xs.shape[0]
    d = D_MODEL
    n_blocks = block_e.shape[0]
    blk = n_rows // n_blocks
    emap = lambda i, be, bv: (be[i], 0, 0)
    rows = lambda: pl.BlockSpec((blk, ROW_SUB, ROW_LANE), lambda i, be, bv: (i, 0, 0))
    return pl.pallas_call(
        _expert_kernel,
        grid_spec=pltpu.PrefetchScalarGridSpec(
            num_scalar_prefetch=2,
            grid=(n_blocks,),
            in_specs=[rows(),
                      pl.BlockSpec((1, d, 2 * D_FF), emap), pl.BlockSpec((1, 1, 2 * D_FF), emap),
                      pl.BlockSpec((1, D_FF, d), emap), pl.BlockSpec((1, 1, d), emap)],
            out_specs=rows(),
            scratch_shapes=[pltpu.VMEM((blk, D_FF), BF16)],
        ),
        out_shape=jax.ShapeDtypeStruct((n_rows, ROW_SUB, ROW_LANE), F32),
        compiler_params=_cparams(("arbitrary",)),
        name="experts",
    )(block_e, block_valid, xs, wgu, bgu, wd, bd)


def _gather_rows(y, dest_flat):
    n_out = dest_flat.shape[0]
    workers = SC_CORES * SC_SUBCORES
    per_w = n_out // workers
    n_it = per_w // SC_ROWS
    assert per_w * workers == n_out and n_it * SC_ROWS == per_w and n_it % 2 == 0
    mesh = plsc.VectorSubcoreMesh(core_axis_name="c", subcore_axis_name="s")

    @functools.partial(
        pl.kernel, mesh=mesh,
        out_type=jax.ShapeDtypeStruct((n_out, ROW_SUB, ROW_LANE), y.dtype),
        scratch_types=[pltpu.VMEM((2, SC_ROWS), jnp.int32),
                       pltpu.VMEM((2, SC_ROWS, ROW_SUB, ROW_LANE), y.dtype),
                       pltpu.SemaphoreType.DMA((2,))],
    )
    def gather_kernel(y_hbm, idx_hbm, out_hbm, idx_v, rows_v, sem):
        base = (lax.axis_index("s") * SC_CORES + lax.axis_index("c")) * per_w

        def span(g):
            return pl.ds(pl.multiple_of(base + g * SC_ROWS, SC_ROWS), SC_ROWS)

        def row_gather(slot):
            return pltpu.make_async_copy(y_hbm.at[idx_v.at[slot]], rows_v.at[slot], sem.at[slot])

        def start(g, slot):
            pltpu.sync_copy(idx_hbm.at[span(g)], idx_v.at[slot])
            row_gather(slot).start()

        start(0, 0)

        @pl.loop(0, n_it, step=2)
        def _(g):
            start(g + 1, 1)
            row_gather(0).wait()
            pltpu.sync_copy(rows_v.at[0], out_hbm.at[span(g)])

            @pl.when(g + 2 < n_it)
            def _():
                start(g + 2, 0)

            row_gather(1).wait()
            pltpu.sync_copy(rows_v.at[1], out_hbm.at[span(g + 1)])

    return gather_kernel(y, dest_flat)


def _combine_kernel(yg_ref, x1_ref, mod_ref, rg_ref, gpo_ref, o_ref):
    rg = rg_ref[...]
    h = rg[:, 0:1] * yg_ref[0].reshape(COMB_TOK, D_MODEL)
    for kk in range(1, TOP_K):
        h = h + rg[:, kk:kk + 1] * yg_ref[kk].reshape(COMB_TOK, D_MODEL)
    gt2 = mod_ref[0, 5:6, :]
    o_ref[...] = x1_ref[...] + gt2 * _rms(h, gpo_ref[...])


def _combine(yg, x1, mod, rg, g_post_ffn, tokens_per_batch):
    n, d = x1.shape
    per_b = tokens_per_batch // COMB_TOK
    gpo = g_post_ffn.reshape(1, d)
    return pl.pallas_call(
        _combine_kernel,
        grid=(n // COMB_TOK,),
        in_specs=[pl.BlockSpec((TOP_K, COMB_TOK, ROW_SUB, ROW_LANE), lambda i: (0, i, 0, 0)),
                  pl.BlockSpec((COMB_TOK, d), lambda i: (i, 0)),
                  pl.BlockSpec((1, 6, d), lambda i: (i // per_b, 0, 0)),
                  pl.BlockSpec((COMB_TOK, 128), lambda i: (i, 0)),
                  pl.BlockSpec((1, d), lambda i: (0, 0))],
        out_specs=pl.BlockSpec((COMB_TOK, d), lambda i: (i, 0)),
        out_shape=jax.ShapeDtypeStruct((n, d), F32),
        compiler_params=_cparams(("arbitrary",)),
        name="combine",
    )(yg, x1, mod, rg, gpo)


def _moe_plan(ri_t, counts, n_tok):
    idx = ri_t[:TOP_K]
    rank = ri_t[TOP_K:2 * TOP_K]
    cnt = counts.astype(jnp.int32)
    blk = MOE_BLK_LARGE if n_tok * TOP_K >= 8 * MOE_BLK_LARGE * N_EXPERTS else MOE_BLK_SMALL
    padded = (cnt + blk - 1) // blk * blk
    pend = jnp.cumsum(padded)
    pstart = pend - padded
    dest = rank
    for e in range(N_EXPERTS):
        dest = dest + jnp.where(idx == e, pstart[e], 0)
    n_blocks = (n_tok * TOP_K + blk - 1) // blk + N_EXPERTS
    bstart = jnp.arange(n_blocks, dtype=jnp.int32) * blk
    block_e = jnp.minimum(jnp.sum((bstart[:, None] >= pend[None, :]).astype(jnp.int32), axis=1), N_EXPERTS - 1)
    block_valid = jnp.clip(pstart[block_e] + cnt[block_e] - bstart, 0, blk).astype(jnp.int32)
    return dest.astype(jnp.int32), block_e, block_valid, n_blocks * blk


def _token_mixing(x, mod, p):
    b, t, d = x.shape
    n = b * t
    proj = _in_proj(x, mod, p['g_pre_mix'], p['w_in'])
    o_fw, o_bw, o_b = _mixers(proj, p['lb_logits'], p['na_tabs'])
    x1, xn, ri, rg, counts = _mix(x, mod, proj, o_fw, o_bw, o_b, p['hgrn_norm'], p['g_post_mix'], p['g_pre_ffn'],
                                  p['w_branch_a'], p['w_branch_b'], p['w_out'], p['router_w'], p['router_b'])
    dest, block_e, block_valid, n_rows = _moe_plan(ri, counts[:, 0], n)
    xs = _scatter_rows(xn.reshape(n, ROW_SUB, ROW_LANE), dest, n_rows)
    return dict(x1=x1.reshape(n, d), rg=rg.reshape(n, 128), dest=dest, block_e=block_e, block_valid=block_valid,
                xs=xs, mod=mod, shape=(b, t, d))


def _expert_rows(s, p):
    return _experts(s['xs'], s['block_e'], s['block_valid'], p['w_gu'], p['b_gu'], p['w_d'], p['b_d'])


def _finish(s, y, p):
    b, t, d = s['shape']
    yg = _gather_rows(y, s['dest'].reshape(-1)).reshape(TOP_K, b * t, ROW_SUB, ROW_LANE)
    return _combine(yg, s['x1'], s['mod'], s['rg'], p['g_post_ffn'], t).reshape(b, t, d)


def _after(first, then):
    return lax.optimization_barrier((first, then))


def kernel(x_prompt, x_sample, c_prompt, c_sample, ada_w, ada_b, g_pre_mix, g_post_mix, g_pre_ffn, g_post_ffn,
           w_in, hgrn_lb_logits, hgrn_norm, na_rpb, w_branch_a, w_branch_b, w_out, router_w, router_b,
           w_gate_up, b_gate_up, w_down, b_down):
    d = D_MODEL
    bp, bs = c_prompt.shape[0], c_sample.shape[0]
    pad = (-(bp + bs)) % 8
    c_all = jnp.concatenate([c_prompt, c_sample, jnp.zeros((pad, d), F32)], axis=0)
    mod = _ada_mod(c_all, ada_w[0], ada_b[0]).reshape(-1, 6, d)

    w = w_in[0]
    w_perm = jnp.concatenate([w[:, :5 * d], w[:, 5 * d + 3 * NA_W:], w[:, 5 * d:5 * d + 3 * NA_W]], axis=1).astype(BF16)
    w_gu, b_gu = _gate_up_layout(w_gate_up[0], b_gate_up[0])
    p = dict(
        g_pre_mix=g_pre_mix[0], g_post_mix=g_post_mix[0], g_pre_ffn=g_pre_ffn[0], g_post_ffn=g_post_ffn[0],
        w_in=w_perm, lb_logits=hgrn_lb_logits.astype(F32), hgrn_norm=hgrn_norm[0],
        na_tabs=_na_bias_tables(na_rpb[0]),
        w_branch_a=w_branch_a[0].astype(BF16), w_branch_b=w_branch_b[0].astype(BF16), w_out=w_out[0].astype(BF16),
        router_w=router_w[0].astype(BF16), router_b=router_b[0],
        w_gu=w_gu, b_gu=b_gu,
        w_d=w_down[0].astype(BF16), b_d=b_down[0][:, None, :],
    )
    sp = _token_mixing(x_prompt, mod[:bp], p)
    sp['x1'], x_sample = _after(sp['x1'], x_sample)
    ss = _token_mixing(x_sample, mod[bp:bp + bs], p)
    ss['x1'], sp['xs'] = _after(ss['x1'], sp['xs'])
    y_p = _expert_rows(sp, p)
    y_p, ss['xs'] = _after(y_p, ss['xs'])
    y_s = _expert_rows(ss, p)
    out_p = _finish(sp, y_p, p)
    out_s = _finish(ss, y_s, p)
    return (out_p, out_s)
```

```python
import functools

import numpy as np
import jax
import jax.numpy as jnp
from jax import lax
from jax.experimental import pallas as pl
from jax.experimental.pallas import tpu as pltpu
from jax.experimental.pallas import tpu_sc as plsc

F32 = jnp.float32
BF16 = jnp.bfloat16

D_MODEL = 1024
GRID_W = 64
HA_HEADS = 8
HA_DK = 128
NA_HEADS = 8
NA_DH = 64
NA_W = NA_HEADS * NA_DH
NA_KH = 8
NA_KW = 16
N_EXPERTS = 32
TOP_K = 4
D_FF = 1024
SWIGLU_ALPHA = 1.702
SWIGLU_LIMIT = 7.0
RMS_EPS = 1e-6
NEG_BIG = -1e30

COL_A = 0
COL_GATES = 5 * 1024
COL_NA = 7 * 1024
PROJ_COLS = 7 * 1024 + 3 * NA_W

HGRN_CHUNK = 64
HGRN_TILE = 256
PROJ_TM = 1024
PROJ_TN = 2176
MIX_TM = 512
MOE_BLK_SMALL, MOE_BLK_LARGE = 512, 1024
COMB_TOK = 512
SUBLANES = 8
ROW_SUB, ROW_LANE = 8, 128
VMEM_LIMIT = 56 * 1024 * 1024


def _cparams(sem):
    return pltpu.CompilerParams(dimension_semantics=sem, vmem_limit_bytes=VMEM_LIMIT)


def _dot(a, b):
    return jnp.dot(a, b, preferred_element_type=F32)


def _dot_nt(a, b):
    return lax.dot_general(a, b, (((1,), (1,)), ((), ())), preferred_element_type=F32)


def _dot_tn(a, b):
    return lax.dot_general(a, b, (((0,), (0,)), ((), ())), preferred_element_type=F32)


def _split(a):
    hi = a.astype(BF16)
    lo = (a - hi.astype(F32)).astype(BF16)
    return hi, lo


def _dot3(a, w):
    ah, al = _split(a)
    wh, wl = _split(w)
    return _dot(ah, wh) + (_dot(al, wh) + _dot(ah, wl))


def _sigmoid(x):
    return 0.5 * jnp.tanh(0.5 * x) + 0.5


def _rms(x, g):
    return x * lax.rsqrt(jnp.mean(x * x, axis=-1, keepdims=True) + RMS_EPS) * g


def _ada_kernel(c_ref, w_ref, b_ref, o_ref):
    c = c_ref[...]
    o_ref[...] = _dot3(c * _sigmoid(c), w_ref[...]) + b_ref[...]


def _ada_mod(c, ada_w, ada_b):
    bp, d = c.shape
    n = ada_w.shape[1]
    tn = 1536
    return pl.pallas_call(
        _ada_kernel,
        grid=(n // tn,),
        in_specs=[pl.BlockSpec((bp, d), lambda j: (0, 0)),
                  pl.BlockSpec((d, tn), lambda j: (0, j)),
                  pl.BlockSpec((1, tn), lambda j: (0, j))],
        out_specs=pl.BlockSpec((bp, tn), lambda j: (0, j)),
        out_shape=jax.ShapeDtypeStruct((bp, n), F32),
        compiler_params=_cparams(("arbitrary",)),
        name="ada_mod",
    )(c, ada_w, ada_b.reshape(1, n))


def _in_proj_kernel(x_ref, mod_ref, g_ref, w_ref, o_ref, xn_ref):
    @pl.when(pl.program_id(2) == 0)
    def _():
        sh = mod_ref[0, 0:1, :]
        sc = mod_ref[0, 1:2, :]
        xn = _rms(x_ref[0], g_ref[...]) * (1.0 + sc) + sh
        xn_ref[...] = xn.astype(BF16)

    o_ref[0] = _dot(xn_ref[...], w_ref[...]).astype(BF16)


def _in_proj(x, mod, g_pre, w_in_bf):
    b, t, d = x.shape
    tm = min(PROJ_TM, t)
    n = w_in_bf.shape[1]
    return pl.pallas_call(
        _in_proj_kernel,
        grid=(b, t // tm, n // PROJ_TN),
        in_specs=[pl.BlockSpec((1, tm, d), lambda bi, i, j: (bi, i, 0)),
                  pl.BlockSpec((1, 6, d), lambda bi, i, j: (bi, 0, 0)),
                  pl.BlockSpec((1, d), lambda bi, i, j: (0, 0)),
                  pl.BlockSpec((d, PROJ_TN), lambda bi, i, j: (0, j))],
        out_specs=pl.BlockSpec((1, tm, PROJ_TN), lambda bi, i, j: (bi, i, j)),
        out_shape=jax.ShapeDtypeStruct((b, t, n), BF16),
        scratch_shapes=[pltpu.VMEM((tm, d), BF16)],
        compiler_params=_cparams(("arbitrary", "arbitrary", "arbitrary")),
        name="in_proj",
    )(x, mod, g_pre.reshape(1, d), w_in_bf)


def _cumsum_rows(x, reverse):
    groups = x.shape[0] // SUBLANES
    row = lax.broadcasted_iota(jnp.int32, (SUBLANES, x.shape[1]), 0)
    out = [None] * groups
    carry = None
    for g in (range(groups - 1, -1, -1) if reverse else range(groups)):
        blk = x[g * SUBLANES:(g + 1) * SUBLANES]
        step = 1
        while step < SUBLANES:
            if reverse:
                blk = blk + jnp.where(row < SUBLANES - step, pltpu.roll(blk, SUBLANES - step, axis=0), 0.0)
            else:
                blk = blk + jnp.where(row >= step, pltpu.roll(blk, step, axis=0), 0.0)
            step *= 2
        if carry is not None:
            blk = blk + carry
        carry = blk[0:1] if reverse else blk[SUBLANES - 1:SUBLANES]
        out[g] = blk
    return jnp.concatenate(out, axis=0)


def _hgrn_dir(q_ref, f_ref, i_ref, o_ref, s_ref, sn_ref, lbd, row0, reverse):
    c = HGRN_CHUNK
    rows = pl.ds(row0, c)
    q = q_ref[0, rows, :].astype(F32)
    fpre = f_ref[0, rows, :].astype(F32)
    v = i_ref[0, rows, :]
    f = 0.5 * (1.0 + lbd) + (0.5 * (1.0 - lbd)) * jnp.tanh(0.5 * fpre)
    logf = jnp.log(f)
    k = 1.0 - f
    ri = lax.broadcasted_iota(jnp.int32, (c, c), 0)
    ci = lax.broadcasted_iota(jnp.int32, (c, c), 1)
    keep = (ci >= ri) if reverse else (ri >= ci)
    b = _cumsum_rows(logf, reverse)
    if reverse:
        b_mid = b[c // 2:c // 2 + 1, :]
        b_end = b[0:1, :]
    else:
        b_mid = b[c // 2 - 1:c // 2, :]
        b_end = b[c - 1:c, :]
    d = b - b_mid
    qm = q * jnp.exp(d)
    km = k * jnp.exp(-d)
    qi = (qm * jnp.exp(b_mid)).astype(BF16)
    ks = (km * jnp.exp(b_end - b_mid)).astype(BF16)
    qm = qm.astype(BF16)
    km = km.astype(BF16)
    dec = jnp.exp(b_end)
    for h in range(HA_HEADS):
        sl = slice(h * HA_DK, (h + 1) * HA_DK)
        a = _dot_nt(qm[:, sl], km[:, sl])
        a = jnp.where(keep, a, 0.0).astype(BF16)
        lhs = jnp.concatenate([qi[:, sl], a], axis=1)
        rhs = jnp.concatenate([sn_ref[h], v[:, sl]], axis=0)
        o_ref[0, rows, sl] = _dot(lhs, rhs).astype(BF16)
        st = s_ref[h] * dec[:, sl] + _dot_tn(v[:, sl], ks[:, sl])
        s_ref[h] = st
        sn_ref[h] = st.astype(BF16).T


def _hgrn_tile(lbl_ref, qf_ref, ff_ref, if_ref, qb_ref, fb_ref, ib_ref, of_ref, ob_ref,
               sf_ref, sb_ref, snf_ref, snb_ref):
    @pl.when(pl.program_id(1) == 0)
    def _():
        for ref in (sf_ref, sb_ref, snf_ref, snb_ref):
            ref[...] = jnp.zeros_like(ref)

    l0 = lbl_ref[0]
    l1 = lbl_ref[1]
    m = jnp.maximum(l0, l1)
    e0 = jnp.exp(l0 - m)
    e1 = jnp.exp(l1 - m)
    lb = e0 / (e0 + e1)
    n_chunks = HGRN_TILE // HGRN_CHUNK

    def chunk(ci):
        _hgrn_dir(qf_ref, ff_ref, if_ref, of_ref, sf_ref, snf_ref, lb[0:1, :], ci * HGRN_CHUNK, False)
        _hgrn_dir(qb_ref, fb_ref, ib_ref, ob_ref, sb_ref, snb_ref, lb[1:2, :],
                  (n_chunks - 1 - ci) * HGRN_CHUNK, True)

    return chunk


def _mixers_kernel(lbl_ref, qf_ref, ff_ref, if_ref, qb_ref, fb_ref, ib_ref, nq_ref, nk_ref, nv_ref, tab_ref,
                   of_ref, ob_ref, on_ref, sf_ref, sb_ref, snf_ref, snb_ref, *, grid_rows):
    chunk = _hgrn_tile(lbl_ref, qf_ref, ff_ref, if_ref, qb_ref, fb_ref, ib_ref, of_ref, ob_ref,
                       sf_ref, sb_ref, snf_ref, snb_ref)
    na_row = _na_rows(nq_ref, nk_ref, nv_ref, tab_ref, on_ref, grid_rows)
    for ci in range(HGRN_TILE // HGRN_CHUNK):
        chunk(ci)
        na_row(ci)


def _mixers(proj, lb_logits, tabs):
    b, t, _ = proj.shape
    tt = HGRN_TILE
    nt = t // tt
    d = D_MODEL
    cq, ck, cv = (COL_NA // NA_W, COL_NA // NA_W + 1, COL_NA // NA_W + 2)

    def fwd(col):
        return pl.BlockSpec((1, tt, d), lambda bi, ti: (bi, ti, col))

    def bwd(col):
        return pl.BlockSpec((1, tt, d), lambda bi, ti: (bi, nt - 1 - ti, col))

    return pl.pallas_call(
        functools.partial(_mixers_kernel, grid_rows=t // GRID_W),
        grid=(b, nt),
        in_specs=[pl.BlockSpec((2, 2, d), lambda bi, ti: (0, 0, 0)),
                  fwd(0), fwd(1), fwd(3), bwd(0), bwd(2), bwd(3),
                  pl.BlockSpec((1, tt, NA_W), lambda bi, ti: (bi, ti, cq)),
                  pl.BlockSpec((1, t, NA_W), lambda bi, ti: (bi, 0, ck), pipeline_mode=pl.Buffered(1)),
                  pl.BlockSpec((1, t, NA_W), lambda bi, ti: (bi, 0, cv), pipeline_mode=pl.Buffered(1)),
                  pl.BlockSpec(tabs.shape, lambda bi, ti: (0, 0, 0, 0), pipeline_mode=pl.Buffered(1))],
        out_specs=[pl.BlockSpec((1, tt, d), lambda bi, ti: (bi, ti, 0)),
                   pl.BlockSpec((1, tt, d), lambda bi, ti: (bi, nt - 1 - ti, 0)),
                   pl.BlockSpec((1, tt, NA_W), lambda bi, ti: (bi, ti, 0))],
        out_shape=[jax.ShapeDtypeStruct((b, t, d), BF16), jax.ShapeDtypeStruct((b, t, d), BF16),
                   jax.ShapeDtypeStruct((b, t, NA_W), BF16)],
        scratch_shapes=[pltpu.VMEM((HA_HEADS, HA_DK, HA_DK), F32), pltpu.VMEM((HA_HEADS, HA_DK, HA_DK), F32),
                        pltpu.VMEM((HA_HEADS, HA_DK, HA_DK), BF16), pltpu.VMEM((HA_HEADS, HA_DK, HA_DK), BF16)],
        compiler_params=_cparams(("arbitrary", "arbitrary")),
        name="mixers",
    )(lb_logits, proj, proj, proj, proj, proj, proj, proj, proj, proj, tabs)


def _na_bias_tables(rpb):
    q = np.arange(GRID_W)[:, None]
    kc = np.arange(GRID_W)[None, :]
    c0 = np.clip(q - NA_KW // 2, 0, GRID_W - NA_KW)
    ok = (kc >= c0) & (kc < c0 + NA_KW)
    dc = np.clip(kc - q + (NA_KW - 1), 0, 2 * NA_KW - 2)
    base = jnp.where(ok[None, None], jnp.take(rpb.astype(F32), dc, axis=2), NEG_BIG)
    tabs = []
    for dr0 in range(NA_KH):
        w = base[:, dr0:dr0 + NA_KH]
        w = w.transpose(0, 2, 1, 3).reshape(NA_HEADS // 2, 2 * GRID_W, NA_KH * GRID_W)
        tabs.append(w)
    return jnp.stack(tabs, axis=1)


def _na_rows(q_ref, k_ref, v_ref, tab_ref, o_ref, grid_rows):
    g = pl.program_id(1)
    lane = lax.broadcasted_iota(jnp.int32, (GRID_W, 2 * NA_DH), 1)
    low = lane < NA_DH
    scale = NA_DH ** -0.5
    rows_per_step = HGRN_TILE // GRID_W

    def na_row(rr):
        r = g * rows_per_step + rr
        r0 = jnp.clip(r - NA_KH // 2, 0, grid_rows - NA_KH)
        dr0 = r0 - r + (NA_KH - 1)
        qrow = pl.ds(rr * GRID_W, GRID_W)
        krow = pl.ds(pl.multiple_of(r0 * GRID_W, GRID_W), NA_KH * GRID_W)
        for j in range(NA_HEADS // 2):
            sl = slice(j * 2 * NA_DH, (j + 1) * 2 * NA_DH)
            qp = q_ref[0, qrow, sl]
            zero = jnp.zeros_like(qp)
            ql = jnp.concatenate([jnp.where(low, qp, zero), jnp.where(low, zero, qp)], axis=0)
            s = _dot_nt(ql, k_ref[0, krow, sl]) * scale + tab_ref[j, dr0]
            m = jnp.max(s, axis=-1, keepdims=True)
            p = jnp.exp(s - m)
            l = jnp.sum(p, axis=-1, keepdims=True)
            pv = _dot(p.astype(BF16), v_ref[0, krow, sl]) / l
            o_ref[0, qrow, sl] = jnp.where(low, pv[:GRID_W], pv[GRID_W:]).astype(BF16)

    return na_row


def _mix_kernel(x_ref, mod_ref, of_ref, ob_ref, g_ref, ga_ref, gb_ref, nb_ref,
                gn_ref, gpm_ref, gpf_ref, wa_ref, wb_ref, wo_ref, rw_ref, rb_ref,
                x1_ref, xn_ref, ri_ref, rg_ref, cnt_ref, tri_ref, carry_ref):
    tm = MIX_TM
    first = (pl.program_id(0) == 0) & (pl.program_id(1) == 0)

    @pl.when(first)
    def _():
        r = lax.broadcasted_iota(jnp.int32, (tm, tm), 0)
        c = lax.broadcasted_iota(jnp.int32, (tm, tm), 1)
        tri_ref[...] = jnp.where(r < c, 1.0, 0.0).astype(BF16)
        carry_ref[...] = jnp.zeros_like(carry_ref)

    sc2 = mod_ref[0, 4:5, :]
    sh2 = mod_ref[0, 3:4, :]
    gt1 = mod_ref[0, 2:3, :]

    o = of_ref[0].astype(F32) + ob_ref[0].astype(F32)
    parts = []
    for h in range(HA_HEADS):
        oh = o[:, h * HA_DK:(h + 1) * HA_DK]
        parts.append(oh * lax.rsqrt(jnp.mean(oh * oh, axis=-1, keepdims=True) + RMS_EPS))
    g = g_ref[0].astype(F32)
    oa = jnp.concatenate(parts, axis=-1) * gn_ref[...] * (g * _sigmoid(g))
    ya = _dot(oa.astype(BF16), wa_ref[...])
    yb = _dot(nb_ref[0], wb_ref[...])
    mix = _sigmoid(ga_ref[0].astype(F32)) * ya + _sigmoid(gb_ref[0].astype(F32)) * yb
    mo = _dot(mix.astype(BF16), wo_ref[...])
    x1 = x_ref[0] + gt1 * _rms(mo, gpm_ref[...])
    x1_ref[0] = x1

    xn = _rms(x1, gpf_ref[...]) * (1.0 + sc2) + sh2
    xn_ref[0] = xn.reshape(tm, ROW_SUB, ROW_LANE)
    logits = _dot_nt(rw_ref[...], xn.astype(BF16)) + rb_ref[...]
    sub = lax.broadcasted_iota(jnp.int32, (N_EXPERTS, tm), 0)
    work = logits
    vals, idxs, hots = [], [], []
    for _ in range(TOP_K):
        mv = jnp.max(work, axis=0, keepdims=True)
        mi = jnp.min(jnp.where(work == mv, sub, N_EXPERTS), axis=0, keepdims=True)
        hot = sub == mi
        vals.append(mv)
        idxs.append(mi)
        hots.append(hot)
        work = jnp.where(hot, -jnp.inf, work)
    es = [jnp.exp(v - vals[0]) for v in vals]
    den = es[0] + es[1] + es[2] + es[3]

    cnt = sum(jnp.where(h, 1.0, 0.0) for h in hots)
    before = _dot(cnt.astype(BF16), tri_ref[...]) + carry_ref[...]
    carry_ref[...] = carry_ref[...] + jnp.sum(cnt, axis=1, keepdims=True)
    cnt_ref[...] = jnp.broadcast_to(carry_ref[...], cnt_ref.shape)

    ranks = [jnp.sum(jnp.where(hots[kk], before, 0.0), axis=0, keepdims=True).astype(jnp.int32)
             for kk in range(TOP_K)]
    ri_ref[...] = jnp.concatenate(idxs + ranks, axis=0)
    gates = jnp.concatenate([e / den for e in es] + [jnp.zeros((128 - TOP_K, tm), F32)], axis=0)
    rg_ref[0] = gates.T


def _mix(x, mod, proj, o_fw, o_bw, o_b, hgrn_norm, g_post_mix, g_pre_ffn, wa, wb, wo, rw, rb):
    b, t, d = x.shape
    tm = min(MIX_TM, t)
    assert tm == MIX_TM
    row = lambda a: a.reshape(1, -1)
    tok = lambda w, col: pl.BlockSpec((1, tm, w), lambda bi, i: (bi, i, col))
    full = lambda a: pl.BlockSpec(a.shape, lambda bi, i: (0,) * a.ndim)
    gn, gpm, gpf, rbr = row(hgrn_norm), row(g_post_mix), row(g_pre_ffn), rb.reshape(-1, 1)
    rw = rw.T
    outs = pl.pallas_call(
        _mix_kernel,
        grid=(b, t // tm),
        in_specs=[tok(d, 0),
                  pl.BlockSpec((1, 6, d), lambda bi, i: (bi, 0, 0)),
                  tok(d, 0), tok(d, 0),
                  tok(d, 4),
                  tok(d, COL_GATES // d), tok(d, COL_GATES // d + 1),
                  tok(NA_W, 0),
                  full(gn), full(gpm), full(gpf), full(wa), full(wb), full(wo), full(rw), full(rbr)],
        out_specs=[tok(d, 0), pl.BlockSpec((1, tm, ROW_SUB, ROW_LANE), lambda bi, i: (bi, i, 0, 0)),
                   pl.BlockSpec((8, tm), lambda bi, i: (0, bi * (t // tm) + i)), tok(128, 0),
                   pl.BlockSpec((N_EXPERTS, 128), lambda bi, i: (0, 0))],
        out_shape=[jax.ShapeDtypeStruct((b, t, d), F32), jax.ShapeDtypeStruct((b, t, ROW_SUB, ROW_LANE), F32),
                   jax.ShapeDtypeStruct((8, b * t), jnp.int32), jax.ShapeDtypeStruct((b, t, 128), F32),
                   jax.ShapeDtypeStruct((N_EXPERTS, 128), F32)],
        scratch_shapes=[pltpu.VMEM((tm, tm), BF16), pltpu.VMEM((N_EXPERTS, 1), F32)],
        compiler_params=_cparams(("arbitrary", "arbitrary")),
        name="mix",
    )(x, mod, o_fw, o_bw, proj, proj, proj, o_b, gn, gpm, gpf, wa, wb, wo, rw, rbr)
    return outs


SC_CORES, SC_SUBCORES = 2, 16
SC_ROWS = 32


def _scatter_rows(xn, dest_t, n_rows):
    n_tok = xn.shape[0]
    workers = SC_CORES * SC_SUBCORES
    per_w = n_tok // workers
    n_it = per_w // SC_ROWS
    assert per_w * workers == n_tok and n_it * SC_ROWS == per_w and n_it % 2 == 0
    idx = dest_t.reshape(TOP_K, n_tok // SC_ROWS, SC_ROWS)
    mesh = plsc.VectorSubcoreMesh(core_axis_name="c", subcore_axis_name="s")

    @functools.partial(
        pl.kernel, mesh=mesh,
        out_type=jax.ShapeDtypeStruct((n_rows, ROW_SUB, ROW_LANE), xn.dtype),
        scratch_types=[pltpu.VMEM((2, TOP_K, SC_ROWS), jnp.int32),
                       pltpu.VMEM((2, SC_ROWS, ROW_SUB, ROW_LANE), xn.dtype),
                       pltpu.SemaphoreType.DMA((2,)), pltpu.SemaphoreType.DMA((2,))],
    )
    def scatter_kernel(x_hbm, idx_hbm, out_hbm, idx_v, rows_v, sem_load, sem_store):
        chunk0 = (lax.axis_index("s") * SC_CORES + lax.axis_index("c")) * n_it

        def load(g, slot):
            tok = pl.ds(pl.multiple_of((chunk0 + g) * SC_ROWS, SC_ROWS), SC_ROWS)
            return pltpu.make_async_copy(x_hbm.at[tok], rows_v.at[slot], sem_load.at[slot])

        def start_load(g, slot):
            load(g, slot).start()
            for kk in range(TOP_K):
                pltpu.sync_copy(idx_hbm.at[kk, chunk0 + g], idx_v.at[slot, kk])

        def store(slot, kk):
            return pltpu.make_async_copy(rows_v.at[slot], out_hbm.at[idx_v.at[slot, kk]], sem_store.at[slot])

        def scatter(g, slot):
            load(g, slot).wait()
            for kk in range(TOP_K):
                store(slot, kk).start()
            for kk in range(TOP_K):
                store(slot, kk).wait()

        start_load(0, 0)

        @pl.loop(0, n_it, step=2)
        def _(g):
            start_load(g + 1, 1)
            scatter(g, 0)

            @pl.when(g + 2 < n_it)
            def _():
                start_load(g + 2, 0)

            scatter(g + 1, 1)

    return scatter_kernel(xn, idx)


GU_SLAB = 256


def _gate_up_layout_kernel(w_ref, o_ref):
    half = GU_SLAB // 2
    r = lax.broadcasted_iota(jnp.int32, (GU_SLAB, GU_SLAB), 0)
    c = lax.broadcasted_iota(jnp.int32, (GU_SLAB, GU_SLAB), 1)
    src = jnp.where(c < half, 2 * c, 2 * (c - half) + 1)
    pm = jnp.where(r == src, 1.0, 0.0).astype(BF16)
    for s in range(w_ref.shape[2] // GU_SLAB):
        cols = slice(s * GU_SLAB, (s + 1) * GU_SLAB)
        o_ref[0, :, cols] = _dot(w_ref[0, :, cols].astype(BF16), pm).astype(BF16)


def _gate_up_layout(w_gate_up, b_gate_up):
    e, k, n2 = w_gate_up.shape
    half = GU_SLAB // 2
    tk = 512
    w = pl.pallas_call(
        _gate_up_layout_kernel,
        grid=(e, k // tk),
        in_specs=[pl.BlockSpec((1, tk, n2), lambda i, j: (i, j, 0))],
        out_specs=pl.BlockSpec((1, tk, n2), lambda i, j: (i, j, 0)),
        out_shape=jax.ShapeDtypeStruct((e, k, n2), BF16),
        compiler_params=_cparams(("arbitrary", "arbitrary")),
        name="gate_up_layout",
    )(w_gate_up)
    b = b_gate_up.reshape(e, n2 // GU_SLAB, half, 2).transpose(0, 1, 3, 2).reshape(e, 1, n2)
    return w, b


def _expert_kernel(be_ref, bv_ref, x_ref, wgu_ref, bgu_ref, wd_ref, bd_ref, o_ref, a_ref):
    i = pl.program_id(0)
    valid = bv_ref[i]
    half = GU_SLAB // 2

    @pl.when(valid > 0)
    def _():
        blk = x_ref.shape[0]
        rowi = lax.broadcasted_iota(jnp.int32, (blk, 1), 0)
        x = x_ref[...].reshape(blk, D_MODEL)
        x = jnp.where(rowi < valid, x, 0.0).astype(BF16)
        for s in range(2 * D_FF // GU_SLAB):
            cols = slice(s * GU_SLAB, (s + 1) * GU_SLAB)
            h = _dot(x, wgu_ref[0, :, cols]) + bgu_ref[0, :, cols]
            glu = jnp.minimum(h[:, :half], SWIGLU_LIMIT)
            lin = jnp.clip(h[:, half:], -SWIGLU_LIMIT, SWIGLU_LIMIT)
            a = glu * _sigmoid(SWIGLU_ALPHA * glu) * (lin + 1.0)
            a_ref[:, s * half:(s + 1) * half] = a.astype(BF16)
        o_ref[...] = (_dot(a_ref[...], wd_ref[0]) + bd_ref[0]).reshape(o_ref.shape)

    @pl.when(valid <= 0)
    def _():
        o_ref[...] = jnp.zeros_like(o_ref)


def _experts(xs, block_e, block_valid, wgu, bgu, wd, bd):
    n_rows = xs.shape[0]
    d = D_MODEL
    n_blocks = block_e.shape[0]
    blk = n_rows // n_blocks
    emap = lambda i, be, bv: (be[i], 0, 0)
    rows = lambda: pl.BlockSpec((blk, ROW_SUB, ROW_LANE), lambda i, be, bv: (i, 0, 0))
    return pl.pallas_call(
        _expert_kernel,
        grid_spec=pltpu.PrefetchScalarGridSpec(
            num_scalar_prefetch=2,
            grid=(n_blocks,),
            in_specs=[rows(),
                      pl.BlockSpec((1, d, 2 * D_FF), emap), pl.BlockSpec((1, 1, 2 * D_FF), emap),
                      pl.BlockSpec((1, D_FF, d), emap), pl.BlockSpec((1, 1, d), emap)],
            out_specs=rows(),
            scratch_shapes=[pltpu.VMEM((blk, D_FF), BF16)],
        ),
        out_shape=jax.ShapeDtypeStruct((n_rows, ROW_SUB, ROW_LANE), F32),
        compiler_params=_cparams(("arbitrary",)),
        name="experts",
    )(block_e, block_valid, xs, wgu, bgu, wd, bd)


def _gather_rows(y, dest_flat):
    n_out = dest_flat.shape[0]
    workers = SC_CORES * SC_SUBCORES
    per_w = n_out // workers
    n_it = per_w // SC_ROWS
    assert per_w * workers == n_out and n_it * SC_ROWS == per_w and n_it % 2 == 0
    mesh = plsc.VectorSubcoreMesh(core_axis_name="c", subcore_axis_name="s")

    @functools.partial(
        pl.kernel, mesh=mesh,
        out_type=jax.ShapeDtypeStruct((n_out, ROW_SUB, ROW_LANE), y.dtype),
        scratch_types=[pltpu.VMEM((2, SC_ROWS), jnp.int32),
                       pltpu.VMEM((2, SC_ROWS, ROW_SUB, ROW_LANE), y.dtype),
                       pltpu.SemaphoreType.DMA((2,))],
    )
    def gather_kernel(y_hbm, idx_hbm, out_hbm, idx_v, rows_v, sem):
        base = (lax.axis_index("s") * SC_CORES + lax.axis_index("c")) * per_w

        def span(g):
            return pl.ds(pl.multiple_of(base + g * SC_ROWS, SC_ROWS), SC_ROWS)

        def row_gather(slot):
            return pltpu.make_async_copy(y_hbm.at[idx_v.at[slot]], rows_v.at[slot], sem.at[slot])

        def start(g, slot):
            pltpu.sync_copy(idx_hbm.at[span(g)], idx_v.at[slot])
            row_gather(slot).start()

        start(0, 0)

        @pl.loop(0, n_it, step=2)
        def _(g):
            start(g + 1, 1)
            row_gather(0).wait()
            pltpu.sync_copy(rows_v.at[0], out_hbm.at[span(g)])

            @pl.when(g + 2 < n_it)
            def _():
                start(g + 2, 0)

            row_gather(1).wait()
            pltpu.sync_copy(rows_v.at[1], out_hbm.at[span(g + 1)])

    return gather_kernel(y, dest_flat)


def _combine_kernel(yg_ref, x1_ref, mod_ref, rg_ref, gpo_ref, o_ref):
    rg = rg_ref[...]
    h = rg[:, 0:1] * yg_ref[0].reshape(COMB_TOK, D_MODEL)
    for kk in range(1, TOP_K):
        h = h + rg[:, kk:kk + 1] * yg_ref[kk].reshape(COMB_TOK, D_MODEL)
    gt2 = mod_ref[0, 5:6, :]
    o_ref[...] = x1_ref[...] + gt2 * _rms(h, gpo_ref[...])


def _combine(yg, x1, mod, rg, g_post_ffn, tokens_per_batch):
    n, d = x1.shape
    per_b = tokens_per_batch // COMB_TOK
    gpo = g_post_ffn.reshape(1, d)
    return pl.pallas_call(
        _combine_kernel,
        grid=(n // COMB_TOK,),
        in_specs=[pl.BlockSpec((TOP_K, COMB_TOK, ROW_SUB, ROW_LANE), lambda i: (0, i, 0, 0)),
                  pl.BlockSpec((COMB_TOK, d), lambda i: (i, 0)),
                  pl.BlockSpec((1, 6, d), lambda i: (i // per_b, 0, 0)),
                  pl.BlockSpec((COMB_TOK, 128), lambda i: (i, 0)),
                  pl.BlockSpec((1, d), lambda i: (0, 0))],
        out_specs=pl.BlockSpec((COMB_TOK, d), lambda i: (i, 0)),
        out_shape=jax.ShapeDtypeStruct((n, d), F32),
        compiler_params=_cparams(("arbitrary",)),
        name="combine",
    )(yg, x1, mod, rg, gpo)


def _moe_plan(ri_t, counts, n_tok):
    idx = ri_t[:TOP_K]
    rank = ri_t[TOP_K:2 * TOP_K]
    cnt = counts.astype(jnp.int32)
    blk = MOE_BLK_LARGE if n_tok * TOP_K >= 8 * MOE_BLK_LARGE * N_EXPERTS else MOE_BLK_SMALL
    padded = (cnt + blk - 1) // blk * blk
    pend = jnp.cumsum(padded)
    pstart = pend - padded
    dest = rank
    for e in range(N_EXPERTS):
        dest = dest + jnp.where(idx == e, pstart[e], 0)
    n_blocks = (n_tok * TOP_K + blk - 1) // blk + N_EXPERTS
    bstart = jnp.arange(n_blocks, dtype=jnp.int32) * blk
    block_e = jnp.minimum(jnp.sum((bstart[:, None] >= pend[None, :]).astype(jnp.int32), axis=1), N_EXPERTS - 1)
    block_valid = jnp.clip(pstart[block_e] + cnt[block_e] - bstart, 0, blk).astype(jnp.int32)
    return dest.astype(jnp.int32), block_e, block_valid, n_blocks * blk


def _token_mixing(x, mod, p):
    b, t, d = x.shape
    n = b * t
    proj = _in_proj(x, mod, p['g_pre_mix'], p['w_in'])
    o_fw, o_bw, o_b = _mixers(proj, p['lb_logits'], p['na_tabs'])
    x1, xn, ri, rg, counts = _mix(x, mod, proj, o_fw, o_bw, o_b, p['hgrn_norm'], p['g_post_mix'], p['g_pre_ffn'],
                                  p['w_branch_a'], p['w_branch_b'], p['w_out'], p['router_w'], p['router_b'])
    dest, block_e, block_valid, n_rows = _moe_plan(ri, counts[:, 0], n)
    xs = _scatter_rows(xn.reshape(n, ROW_SUB, ROW_LANE), dest, n_rows)
    return dict(x1=x1.reshape(n, d), rg=rg.reshape(n, 128), dest=dest, block_e=block_e, block_valid=block_valid,
                xs=xs, mod=mod, shape=(b, t, d))


def _expert_rows(s, p):
    return _experts(s['xs'], s['block_e'], s['block_valid'], p['w_gu'], p['b_gu'], p['w_d'], p['b_d'])


def _finish(s, y, p):
    b, t, d = s['shape']
    yg = _gather_rows(y, s['dest'].reshape(-1)).reshape(TOP_K, b * t, ROW_SUB, ROW_LANE)
    return _combine(yg, s['x1'], s['mod'], s['rg'], p['g_post_ffn'], t).reshape(b, t, d)


def _after(first, then):
    return lax.optimization_barrier((first, then))


def kernel(x_prompt, x_sample, c_prompt, c_sample, ada_w, ada_b, g_pre_mix, g_post_mix, g_pre_ffn, g_post_ffn,
           w_in, hgrn_lb_logits, hgrn_norm, na_rpb, w_branch_a, w_branch_b, w_out, router_w, router_b,
           w_gate_up, b_gate_up, w_down, b_down):
    d = D_MODEL
    bp, bs = c_prompt.shape[0], c_sample.shape[0]
    pad = (-(bp + bs)) % 8
    c_all = jnp.concatenate([c_prompt, c_sample, jnp.zeros((pad, d), F32)], axis=0)
    mod = _ada_mod(c_all, ada_w[0], ada_b[0]).reshape(-1, 6, d)

    w = w_in[0]
    w_perm = jnp.concatenate([w[:, :5 * d], w[:, 5 * d + 3 * NA_W:], w[:, 5 * d:5 * d + 3 * NA_W]], axis=1).astype(BF16)
    w_gu, b_gu = _gate_up_layout(w_gate_up[0], b_gate_up[0])
    p = dict(
        g_pre_mix=g_pre_mix[0], g_post_mix=g_post_mix[0], g_pre_ffn=g_pre_ffn[0], g_post_ffn=g_post_ffn[0],
        w_in=w_perm, lb_logits=hgrn_lb_logits.astype(F32), hgrn_norm=hgrn_norm[0],
        na_tabs=_na_bias_tables(na_rpb[0]),
        w_branch_a=w_branch_a[0].astype(BF16), w_branch_b=w_branch_b[0].astype(BF16), w_out=w_out[0].astype(BF16),
        router_w=router_w[0].astype(BF16), router_b=router_b[0],
        w_gu=w_gu, b_gu=b_gu,
        w_d=w_down[0].astype(BF16), b_d=b_down[0][:, None, :],
    )
    sp = _token_mixing(x_prompt, mod[:bp], p)
    sp['dest'], x_sample = _after(sp['dest'], x_sample)
    ss = _token_mixing(x_sample, mod[bp:bp + bs], p)
    ss['x1'], sp['xs'] = _after(ss['x1'], sp['xs'])
    y_p = _expert_rows(sp, p)
    y_p, ss['xs'] = _after(y_p, ss['xs'])
    y_s = _expert_rows(ss, p)
    out_p = _finish(sp, y_p, p)
    out_s = _finish(ss, y_s, p)
    return (out_p, out_s)
```

```python
import functools

import numpy as np
import jax
import jax.numpy as jnp
from jax import lax
from jax.experimental import pallas as pl
from jax.experimental.pallas import tpu as pltpu
from jax.experimental.pallas import tpu_sc as plsc

F32 = jnp.float32
BF16 = jnp.bfloat16

D_MODEL = 1024
GRID_W = 64
HA_HEADS = 8
HA_DK = 128
NA_HEADS = 8
NA_DH = 64
NA_W = NA_HEADS * NA_DH
NA_KH = 8
NA_KW = 16
N_EXPERTS = 32
TOP_K = 4
D_FF = 1024
SWIGLU_ALPHA = 1.702
SWIGLU_LIMIT = 7.0
RMS_EPS = 1e-6
NEG_BIG = -1e30

COL_A = 0
COL_GATES = 5 * 1024
COL_NA = 7 * 1024
PROJ_COLS = 7 * 1024 + 3 * NA_W

HGRN_CHUNK = 64
HGRN_TILE = 256
PROJ_TM = 2048
PROJ_TN = 2176
MIX_TM = 512
MOE_BLK_SMALL, MOE_BLK_LARGE = 512, 1024
COMB_TOK = 512
SUBLANES = 8
ROW_SUB, ROW_LANE = 8, 128
VMEM_LIMIT = 56 * 1024 * 1024


def _cparams(sem):
    return pltpu.CompilerParams(dimension_semantics=sem, vmem_limit_bytes=VMEM_LIMIT)


def _dot(a, b):
    return jnp.dot(a, b, preferred_element_type=F32)


def _dot_nt(a, b):
    return lax.dot_general(a, b, (((1,), (1,)), ((), ())), preferred_element_type=F32)


def _dot_tn(a, b):
    return lax.dot_general(a, b, (((0,), (0,)), ((), ())), preferred_element_type=F32)


def _split(a):
    hi = a.astype(BF16)
    lo = (a - hi.astype(F32)).astype(BF16)
    return hi, lo


def _dot3(a, w):
    ah, al = _split(a)
    wh, wl = _split(w)
    return _dot(ah, wh) + (_dot(al, wh) + _dot(ah, wl))


def _sigmoid(x):
    return 0.5 * jnp.tanh(0.5 * x) + 0.5


def _rms(x, g):
    return x * lax.rsqrt(jnp.mean(x * x, axis=-1, keepdims=True) + RMS_EPS) * g


def _ada_kernel(c_ref, w_ref, b_ref, o_ref):
    c = c_ref[...]
    o_ref[...] = _dot3(c * _sigmoid(c), w_ref[...]) + b_ref[...]


def _ada_mod(c, ada_w, ada_b):
    bp, d = c.shape
    n = ada_w.shape[1]
    tn = 1536
    return pl.pallas_call(
        _ada_kernel,
        grid=(n // tn,),
        in_specs=[pl.BlockSpec((bp, d), lambda j: (0, 0)),
                  pl.BlockSpec((d, tn), lambda j: (0, j)),
                  pl.BlockSpec((1, tn), lambda j: (0, j))],
        out_specs=pl.BlockSpec((bp, tn), lambda j: (0, j)),
        out_shape=jax.ShapeDtypeStruct((bp, n), F32),
        compiler_params=_cparams(("arbitrary",)),
        name="ada_mod",
    )(c, ada_w, ada_b.reshape(1, n))


def _in_proj_kernel(x_ref, mod_ref, g_ref, w_ref, o_ref, xn_ref):
    @pl.when(pl.program_id(2) == 0)
    def _():
        sh = mod_ref[0, 0:1, :]
        sc = mod_ref[0, 1:2, :]
        xn = _rms(x_ref[0], g_ref[...]) * (1.0 + sc) + sh
        xn_ref[...] = xn.astype(BF16)

    o_ref[0] = _dot(xn_ref[...], w_ref[...]).astype(BF16)


def _in_proj(x, mod, g_pre, w_in_bf):
    b, t, d = x.shape
    tm = min(PROJ_TM, t)
    n = w_in_bf.shape[1]
    return pl.pallas_call(
        _in_proj_kernel,
        grid=(b, t // tm, n // PROJ_TN),
        in_specs=[pl.BlockSpec((1, tm, d), lambda bi, i, j: (bi, i, 0)),
                  pl.BlockSpec((1, 6, d), lambda bi, i, j: (bi, 0, 0)),
                  pl.BlockSpec((1, d), lambda bi, i, j: (0, 0)),
                  pl.BlockSpec((d, PROJ_TN), lambda bi, i, j: (0, j))],
        out_specs=pl.BlockSpec((1, tm, PROJ_TN), lambda bi, i, j: (bi, i, j)),
        out_shape=jax.ShapeDtypeStruct((b, t, n), BF16),
        scratch_shapes=[pltpu.VMEM((tm, d), BF16)],
        compiler_params=_cparams(("arbitrary", "arbitrary", "arbitrary")),
        name="in_proj",
    )(x, mod, g_pre.reshape(1, d), w_in_bf)


def _cumsum_rows(x, reverse):
    groups = x.shape[0] // SUBLANES
    row = lax.broadcasted_iota(jnp.int32, (SUBLANES, x.shape[1]), 0)
    out = [None] * groups
    carry = None
    for g in (range(groups - 1, -1, -1) if reverse else range(groups)):
        blk = x[g * SUBLANES:(g + 1) * SUBLANES]
        step = 1
        while step < SUBLANES:
            if reverse:
                blk = blk + jnp.where(row < SUBLANES - step, pltpu.roll(blk, SUBLANES - step, axis=0), 0.0)
            else:
                blk = blk + jnp.where(row >= step, pltpu.roll(blk, step, axis=0), 0.0)
            step *= 2
        if carry is not None:
            blk = blk + carry
        carry = blk[0:1] if reverse else blk[SUBLANES - 1:SUBLANES]
        out[g] = blk
    return jnp.concatenate(out, axis=0)


def _hgrn_dir(q_ref, f_ref, i_ref, o_ref, s_ref, sn_ref, lbd, row0, reverse):
    c = HGRN_CHUNK
    rows = pl.ds(row0, c)
    q = q_ref[0, rows, :].astype(F32)
    fpre = f_ref[0, rows, :].astype(F32)
    v = i_ref[0, rows, :]
    f = 0.5 * (1.0 + lbd) + (0.5 * (1.0 - lbd)) * jnp.tanh(0.5 * fpre)
    logf = jnp.log(f)
    k = 1.0 - f
    ri = lax.broadcasted_iota(jnp.int32, (c, c), 0)
    ci = lax.broadcasted_iota(jnp.int32, (c, c), 1)
    keep = (ci >= ri) if reverse else (ri >= ci)
    b = _cumsum_rows(logf, reverse)
    if reverse:
        b_mid = b[c // 2:c // 2 + 1, :]
        b_end = b[0:1, :]
    else:
        b_mid = b[c // 2 - 1:c // 2, :]
        b_end = b[c - 1:c, :]
    d = b - b_mid
    qm = q * jnp.exp(d)
    km = k * jnp.exp(-d)
    qi = (qm * jnp.exp(b_mid)).astype(BF16)
    ks = (km * jnp.exp(b_end - b_mid)).astype(BF16)
    qm = qm.astype(BF16)
    km = km.astype(BF16)
    dec = jnp.exp(b_end)
    for h in range(HA_HEADS):
        sl = slice(h * HA_DK, (h + 1) * HA_DK)
        a = _dot_nt(qm[:, sl], km[:, sl])
        a = jnp.where(keep, a, 0.0).astype(BF16)
        lhs = jnp.concatenate([qi[:, sl], a], axis=1)
        rhs = jnp.concatenate([sn_ref[h], v[:, sl]], axis=0)
        o_ref[0, rows, sl] = _dot(lhs, rhs).astype(BF16)
        st = s_ref[h] * dec[:, sl] + _dot_tn(v[:, sl], ks[:, sl])
        s_ref[h] = st
        sn_ref[h] = st.astype(BF16).T


def _hgrn_tile(lbl_ref, qf_ref, ff_ref, if_ref, qb_ref, fb_ref, ib_ref, of_ref, ob_ref,
               sf_ref, sb_ref, snf_ref, snb_ref):
    @pl.when(pl.program_id(1) == 0)
    def _():
        for ref in (sf_ref, sb_ref, snf_ref, snb_ref):
            ref[...] = jnp.zeros_like(ref)

    l0 = lbl_ref[0]
    l1 = lbl_ref[1]
    m = jnp.maximum(l0, l1)
    e0 = jnp.exp(l0 - m)
    e1 = jnp.exp(l1 - m)
    lb = e0 / (e0 + e1)
    n_chunks = HGRN_TILE // HGRN_CHUNK

    def chunk(ci):
        _hgrn_dir(qf_ref, ff_ref, if_ref, of_ref, sf_ref, snf_ref, lb[0:1, :], ci * HGRN_CHUNK, False)
        _hgrn_dir(qb_ref, fb_ref, ib_ref, ob_ref, sb_ref, snb_ref, lb[1:2, :],
                  (n_chunks - 1 - ci) * HGRN_CHUNK, True)

    return chunk


def _mixers_kernel(lbl_ref, qf_ref, ff_ref, if_ref, qb_ref, fb_ref, ib_ref, nq_ref, nk_ref, nv_ref, tab_ref,
                   of_ref, ob_ref, on_ref, sf_ref, sb_ref, snf_ref, snb_ref, *, grid_rows):
    chunk = _hgrn_tile(lbl_ref, qf_ref, ff_ref, if_ref, qb_ref, fb_ref, ib_ref, of_ref, ob_ref,
                       sf_ref, sb_ref, snf_ref, snb_ref)
    na_row = _na_rows(nq_ref, nk_ref, nv_ref, tab_ref, on_ref, grid_rows)
    for ci in range(HGRN_TILE // HGRN_CHUNK):
        chunk(ci)
        na_row(ci)


def _mixers(proj, lb_logits, tabs):
    b, t, _ = proj.shape
    tt = HGRN_TILE
    nt = t // tt
    d = D_MODEL
    cq, ck, cv = (COL_NA // NA_W, COL_NA // NA_W + 1, COL_NA // NA_W + 2)

    def fwd(col):
        return pl.BlockSpec((1, tt, d), lambda bi, ti: (bi, ti, col))

    def bwd(col):
        return pl.BlockSpec((1, tt, d), lambda bi, ti: (bi, nt - 1 - ti, col))

    return pl.pallas_call(
        functools.partial(_mixers_kernel, grid_rows=t // GRID_W),
        grid=(b, nt),
        in_specs=[pl.BlockSpec((2, 2, d), lambda bi, ti: (0, 0, 0)),
                  fwd(0), fwd(1), fwd(3), bwd(0), bwd(2), bwd(3),
                  pl.BlockSpec((1, tt, NA_W), lambda bi, ti: (bi, ti, cq)),
                  pl.BlockSpec((1, t, NA_W), lambda bi, ti: (bi, 0, ck), pipeline_mode=pl.Buffered(1)),
                  pl.BlockSpec((1, t, NA_W), lambda bi, ti: (bi, 0, cv), pipeline_mode=pl.Buffered(1)),
                  pl.BlockSpec(tabs.shape, lambda bi, ti: (0, 0, 0, 0), pipeline_mode=pl.Buffered(1))],
        out_specs=[pl.BlockSpec((1, tt, d), lambda bi, ti: (bi, ti, 0)),
                   pl.BlockSpec((1, tt, d), lambda bi, ti: (bi, nt - 1 - ti, 0)),
                   pl.BlockSpec((1, tt, NA_W), lambda bi, ti: (bi, ti, 0))],
        out_shape=[jax.ShapeDtypeStruct((b, t, d), BF16), jax.ShapeDtypeStruct((b, t, d), BF16),
                   jax.ShapeDtypeStruct((b, t, NA_W), BF16)],
        scratch_shapes=[pltpu.VMEM((HA_HEADS, HA_DK, HA_DK), F32), pltpu.VMEM((HA_HEADS, HA_DK, HA_DK), F32),
                        pltpu.VMEM((HA_HEADS, HA_DK, HA_DK), BF16), pltpu.VMEM((HA_HEADS, HA_DK, HA_DK), BF16)],
        compiler_params=_cparams(("arbitrary", "arbitrary")),
        name="mixers",
    )(lb_logits, proj, proj, proj, proj, proj, proj, proj, proj, proj, tabs)


def _na_bias_tables(rpb):
    q = np.arange(GRID_W)[:, None]
    kc = np.arange(GRID_W)[None, :]
    c0 = np.clip(q - NA_KW // 2, 0, GRID_W - NA_KW)
    ok = (kc >= c0) & (kc < c0 + NA_KW)
    dc = np.clip(kc - q + (NA_KW - 1), 0, 2 * NA_KW - 2)
    base = jnp.where(ok[None, None], jnp.take(rpb.astype(F32), dc, axis=2), NEG_BIG)
    tabs = []
    for dr0 in range(NA_KH):
        w = base[:, dr0:dr0 + NA_KH]
        w = w.transpose(0, 2, 1, 3).reshape(NA_HEADS // 2, 2 * GRID_W, NA_KH * GRID_W)
        tabs.append(w)
    return jnp.stack(tabs, axis=1)


def _na_rows(q_ref, k_ref, v_ref, tab_ref, o_ref, grid_rows):
    g = pl.program_id(1)
    lane = lax.broadcasted_iota(jnp.int32, (GRID_W, 2 * NA_DH), 1)
    low = lane < NA_DH
    scale = NA_DH ** -0.5
    rows_per_step = HGRN_TILE // GRID_W

    def na_row(rr):
        r = g * rows_per_step + rr
        r0 = jnp.clip(r - NA_KH // 2, 0, grid_rows - NA_KH)
        dr0 = r0 - r + (NA_KH - 1)
        qrow = pl.ds(rr * GRID_W, GRID_W)
        krow = pl.ds(pl.multiple_of(r0 * GRID_W, GRID_W), NA_KH * GRID_W)
        for j in range(NA_HEADS // 2):
            sl = slice(j * 2 * NA_DH, (j + 1) * 2 * NA_DH)
            qp = q_ref[0, qrow, sl]
            zero = jnp.zeros_like(qp)
            ql = jnp.concatenate([jnp.where(low, qp, zero), jnp.where(low, zero, qp)], axis=0)
            s = _dot_nt(ql, k_ref[0, krow, sl]) * scale + tab_ref[j, dr0]
            m = jnp.max(s, axis=-1, keepdims=True)
            p = jnp.exp(s - m)
            l = jnp.sum(p, axis=-1, keepdims=True)
            pv = _dot(p.astype(BF16), v_ref[0, krow, sl]) / l
            o_ref[0, qrow, sl] = jnp.where(low, pv[:GRID_W], pv[GRID_W:]).astype(BF16)

    return na_row


def _mix_kernel(x_ref, mod_ref, of_ref, ob_ref, g_ref, ga_ref, gb_ref, nb_ref,
                gn_ref, gpm_ref, gpf_ref, wa_ref, wb_ref, wo_ref, rw_ref, rb_ref,
                x1_ref, xn_ref, ri_ref, rg_ref, cnt_ref, tri_ref, carry_ref):
    tm = MIX_TM
    first = (pl.program_id(0) == 0) & (pl.program_id(1) == 0)

    @pl.when(first)
    def _():
        r = lax.broadcasted_iota(jnp.int32, (tm, tm), 0)
        c = lax.broadcasted_iota(jnp.int32, (tm, tm), 1)
        tri_ref[...] = jnp.where(r < c, 1.0, 0.0).astype(BF16)
        carry_ref[...] = jnp.zeros_like(carry_ref)

    sc2 = mod_ref[0, 4:5, :]
    sh2 = mod_ref[0, 3:4, :]
    gt1 = mod_ref[0, 2:3, :]

    o = of_ref[0].astype(F32) + ob_ref[0].astype(F32)
    parts = []
    for h in range(HA_HEADS):
        oh = o[:, h * HA_DK:(h + 1) * HA_DK]
        parts.append(oh * lax.rsqrt(jnp.mean(oh * oh, axis=-1, keepdims=True) + RMS_EPS))
    g = g_ref[0].astype(F32)
    oa = jnp.concatenate(parts, axis=-1) * gn_ref[...] * (g * _sigmoid(g))
    ya = _dot(oa.astype(BF16), wa_ref[...])
    yb = _dot(nb_ref[0], wb_ref[...])
    mix = _sigmoid(ga_ref[0].astype(F32)) * ya + _sigmoid(gb_ref[0].astype(F32)) * yb
    mo = _dot(mix.astype(BF16), wo_ref[...])
    x1 = x_ref[0] + gt1 * _rms(mo, gpm_ref[...])
    x1_ref[0] = x1

    xn = _rms(x1, gpf_ref[...]) * (1.0 + sc2) + sh2
    xn_ref[0] = xn.reshape(tm, ROW_SUB, ROW_LANE)
    logits = _dot_nt(rw_ref[...], xn.astype(BF16)) + rb_ref[...]
    sub = lax.broadcasted_iota(jnp.int32, (N_EXPERTS, tm), 0)
    work = logits
    vals, idxs, hots = [], [], []
    for _ in range(TOP_K):
        mv = jnp.max(work, axis=0, keepdims=True)
        mi = jnp.min(jnp.where(work == mv, sub, N_EXPERTS), axis=0, keepdims=True)
        hot = sub == mi
        vals.append(mv)
        idxs.append(mi)
        hots.append(hot)
        work = jnp.where(hot, -jnp.inf, work)
    es = [jnp.exp(v - vals[0]) for v in vals]
    den = es[0] + es[1] + es[2] + es[3]

    cnt = sum(jnp.where(h, 1.0, 0.0) for h in hots)
    before = _dot(cnt.astype(BF16), tri_ref[...]) + carry_ref[...]
    carry_ref[...] = carry_ref[...] + jnp.sum(cnt, axis=1, keepdims=True)
    cnt_ref[...] = jnp.broadcast_to(carry_ref[...], cnt_ref.shape)

    ranks = [jnp.sum(jnp.where(hots[kk], before, 0.0), axis=0, keepdims=True).astype(jnp.int32)
             for kk in range(TOP_K)]
    ri_ref[...] = jnp.concatenate(idxs + ranks, axis=0)
    gates = jnp.concatenate([e / den for e in es] + [jnp.zeros((128 - TOP_K, tm), F32)], axis=0)
    rg_ref[0] = gates.T


def _mix(x, mod, proj, o_fw, o_bw, o_b, hgrn_norm, g_post_mix, g_pre_ffn, wa, wb, wo, rw, rb):
    b, t, d = x.shape
    tm = min(MIX_TM, t)
    assert tm == MIX_TM
    row = lambda a: a.reshape(1, -1)
    tok = lambda w, col: pl.BlockSpec((1, tm, w), lambda bi, i: (bi, i, col))
    full = lambda a: pl.BlockSpec(a.shape, lambda bi, i: (0,) * a.ndim)
    gn, gpm, gpf, rbr = row(hgrn_norm), row(g_post_mix), row(g_pre_ffn), rb.reshape(-1, 1)
    rw = rw.T
    outs = pl.pallas_call(
        _mix_kernel,
        grid=(b, t // tm),
        in_specs=[tok(d, 0),
                  pl.BlockSpec((1, 6, d), lambda bi, i: (bi, 0, 0)),
                  tok(d, 0), tok(d, 0),
                  tok(d, 4),
                  tok(d, COL_GATES // d), tok(d, COL_GATES // d + 1),
                  tok(NA_W, 0),
                  full(gn), full(gpm), full(gpf), full(wa), full(wb), full(wo), full(rw), full(rbr)],
        out_specs=[tok(d, 0), pl.BlockSpec((1, tm, ROW_SUB, ROW_LANE), lambda bi, i: (bi, i, 0, 0)),
                   pl.BlockSpec((8, tm), lambda bi, i: (0, bi * (t // tm) + i)), tok(128, 0),
                   pl.BlockSpec((N_EXPERTS, 128), lambda bi, i: (0, 0))],
        out_shape=[jax.ShapeDtypeStruct((b, t, d), F32), jax.ShapeDtypeStruct((b, t, ROW_SUB, ROW_LANE), F32),
                   jax.ShapeDtypeStruct((8, b * t), jnp.int32), jax.ShapeDtypeStruct((b, t, 128), F32),
                   jax.ShapeDtypeStruct((N_EXPERTS, 128), F32)],
        scratch_shapes=[pltpu.VMEM((tm, tm), BF16), pltpu.VMEM((N_EXPERTS, 1), F32)],
        compiler_params=_cparams(("arbitrary", "arbitrary")),
        name="mix",
    )(x, mod, o_fw, o_bw, proj, proj, proj, o_b, gn, gpm, gpf, wa, wb, wo, rw, rbr)
    return outs


SC_CORES, SC_SUBCORES = 2, 16
SC_ROWS = 32


def _scatter_rows(xn, dest_t, n_rows):
    n_tok = xn.shape[0]
    workers = SC_CORES * SC_SUBCORES
    per_w = n_tok // workers
    n_it = per_w // SC_ROWS
    assert per_w * workers == n_tok and n_it * SC_ROWS == per_w and n_it % 2 == 0
    idx = dest_t.reshape(TOP_K, n_tok // SC_ROWS, SC_ROWS)
    mesh = plsc.VectorSubcoreMesh(core_axis_name="c", subcore_axis_name="s")

    @functools.partial(
        pl.kernel, mesh=mesh,
        out_type=jax.ShapeDtypeStruct((n_rows, ROW_SUB, ROW_LANE), xn.dtype),
        scratch_types=[pltpu.VMEM((2, TOP_K, SC_ROWS), jnp.int32),
                       pltpu.VMEM((2, SC_ROWS, ROW_SUB, ROW_LANE), xn.dtype),
                       pltpu.SemaphoreType.DMA((2,)), pltpu.SemaphoreType.DMA((2,))],
    )
    def scatter_kernel(x_hbm, idx_hbm, out_hbm, idx_v, rows_v, sem_load, sem_store):
        chunk0 = (lax.axis_index("s") * SC_CORES + lax.axis_index("c")) * n_it

        def load(g, slot):
            tok = pl.ds(pl.multiple_of((chunk0 + g) * SC_ROWS, SC_ROWS), SC_ROWS)
            return pltpu.make_async_copy(x_hbm.at[tok], rows_v.at[slot], sem_load.at[slot])

        def start_load(g, slot):
            load(g, slot).start()
            for kk in range(TOP_K):
                pltpu.sync_copy(idx_hbm.at[kk, chunk0 + g], idx_v.at[slot, kk])

        def store(slot, kk):
            return pltpu.make_async_copy(rows_v.at[slot], out_hbm.at[idx_v.at[slot, kk]], sem_store.at[slot])

        def scatter(g, slot):
            load(g, slot).wait()
            for kk in range(TOP_K):
                store(slot, kk).start()
            for kk in range(TOP_K):
                store(slot, kk).wait()

        start_load(0, 0)

        @pl.loop(0, n_it, step=2)
        def _(g):
            start_load(g + 1, 1)
            scatter(g, 0)

            @pl.when(g + 2 < n_it)
            def _():
                start_load(g + 2, 0)

            scatter(g + 1, 1)

    return scatter_kernel(xn, idx)


GU_SLAB = 256


def _gate_up_layout_kernel(w_ref, o_ref):
    half = GU_SLAB // 2
    r = lax.broadcasted_iota(jnp.int32, (GU_SLAB, GU_SLAB), 0)
    c = lax.broadcasted_iota(jnp.int32, (GU_SLAB, GU_SLAB), 1)
    src = jnp.where(c < half, 2 * c, 2 * (c - half) + 1)
    pm = jnp.where(r == src, 1.0, 0.0).astype(BF16)
    for s in range(w_ref.shape[2] // GU_SLAB):
        cols = slice(s * GU_SLAB, (s + 1) * GU_SLAB)
        o_ref[0, :, cols] = _dot(w_ref[0, :, cols].astype(BF16), pm).astype(BF16)


def _gate_up_layout(w_gate_up, b_gate_up):
    e, k, n2 = w_gate_up.shape
    half = GU_SLAB // 2
    tk = 512
    w = pl.pallas_call(
        _gate_up_layout_kernel,
        grid=(e, k // tk),
        in_specs=[pl.BlockSpec((1, tk, n2), lambda i, j: (i, j, 0))],
        out_specs=pl.BlockSpec((1, tk, n2), lambda i, j: (i, j, 0)),
        out_shape=jax.ShapeDtypeStruct((e, k, n2), BF16),
        compiler_params=_cparams(("arbitrary", "arbitrary")),
        name="gate_up_layout",
    )(w_gate_up)
    b = b_gate_up.reshape(e, n2 // GU_SLAB, half, 2).transpose(0, 1, 3, 2).reshape(e, 1, n2)
    return w, b


def _expert_kernel(be_ref, bv_ref, x_ref, wgu_ref, bgu_ref, wd_ref, bd_ref, o_ref, a_ref):
    i = pl.program_id(0)
    valid = bv_ref[i]
    half = GU_SLAB // 2

    @pl.when(valid > 0)
    def _():
        blk = x_ref.shape[0]
        rowi = lax.broadcasted_iota(jnp.int32, (blk, 1), 0)
        x = x_ref[...].reshape(blk, D_MODEL)
        x = jnp.where(rowi < valid, x, 0.0).astype(BF16)
        for s in range(2 * D_FF // GU_SLAB):
            cols = slice(s * GU_SLAB, (s + 1) * GU_SLAB)
            h = _dot(x, wgu_ref[0, :, cols]) + bgu_ref[0, :, cols]
            glu = jnp.minimum(h[:, :half], SWIGLU_LIMIT)
            lin = jnp.clip(h[:, half:], -SWIGLU_LIMIT, SWIGLU_LIMIT)
            a = glu * _sigmoid(SWIGLU_ALPHA * glu) * (lin + 1.0)
            a_ref[:, s * half:(s + 1) * half] = a.astype(BF16)
        o_ref[...] = (_dot(a_ref[...], wd_ref[0].astype(BF16)) + bd_ref[0]).reshape(o_ref.shape)

    @pl.when(valid <= 0)
    def _():
        o_ref[...] = jnp.zeros_like(o_ref)


def _experts(xs, block_e, block_valid, wgu, bgu, wd, bd):
    n_rows = xs.shape[0]
    d = D_MODEL
    n_blocks = block_e.shape[0]
    blk = n_rows // n_blocks
    emap = lambda i, be, bv: (be[i], 0, 0)
    rows = lambda: pl.BlockSpec((blk, ROW_SUB, ROW_LANE), lambda i, be, bv: (i, 0, 0))
    return pl.pallas_call(
        _expert_kernel,
        grid_spec=pltpu.PrefetchScalarGridSpec(
            num_scalar_prefetch=2,
            grid=(n_blocks,),
            in_specs=[rows(),
                      pl.BlockSpec((1, d, 2 * D_FF), emap), pl.BlockSpec((1, 1, 2 * D_FF), emap),
                      pl.BlockSpec((1, D_FF, d), emap), pl.BlockSpec((1, 1, d), emap)],
            out_specs=rows(),
            scratch_shapes=[pltpu.VMEM((blk, D_FF), BF16)],
        ),
        out_shape=jax.ShapeDtypeStruct((n_rows, ROW_SUB, ROW_LANE), F32),
        compiler_params=_cparams(("arbitrary",)),
        name="experts",
    )(block_e, block_valid, xs, wgu, bgu, wd, bd)


def _gather_rows(y, dest_flat):
    n_out = dest_flat.shape[0]
    workers = SC_CORES * SC_SUBCORES
    per_w = n_out // workers
    n_it = per_w // SC_ROWS
    assert per_w * workers == n_out and n_it * SC_ROWS == per_w and n_it % 2 == 0
    mesh = plsc.VectorSubcoreMesh(core_axis_name="c", subcore_axis_name="s")

    @functools.partial(
        pl.kernel, mesh=mesh,
        out_type=jax.ShapeDtypeStruct((n_out, ROW_SUB, ROW_LANE), y.dtype),
        scratch_types=[pltpu.VMEM((2, SC_ROWS), jnp.int32),
                       pltpu.VMEM((2, SC_ROWS, ROW_SUB, ROW_LANE), y.dtype),
                       pltpu.SemaphoreType.DMA((2,))],
    )
    def gather_kernel(y_hbm, idx_hbm, out_hbm, idx_v, rows_v, sem):
        base = (lax.axis_index("s") * SC_CORES + lax.axis_index("c")) * per_w

        def span(g):
            return pl.ds(pl.multiple_of(base + g * SC_ROWS, SC_ROWS), SC_ROWS)

        def row_gather(slot):
            return pltpu.make_async_copy(y_hbm.at[idx_v.at[slot]], rows_v.at[slot], sem.at[slot])

        def start(g, slot):
            pltpu.sync_copy(idx_hbm.at[span(g)], idx_v.at[slot])
            row_gather(slot).start()

        start(0, 0)

        @pl.loop(0, n_it, step=2)
        def _(g):
            start(g + 1, 1)
            row_gather(0).wait()
            pltpu.sync_copy(rows_v.at[0], out_hbm.at[span(g)])

            @pl.when(g + 2 < n_it)
            def _():
                start(g + 2, 0)

            row_gather(1).wait()
            pltpu.sync_copy(rows_v.at[1], out_hbm.at[span(g + 1)])

    return gather_kernel(y, dest_flat)


def _combine_kernel(yg_ref, x1_ref, mod_ref, rg_ref, gpo_ref, o_ref):
    rg = rg_ref[...]
    h = rg[:, 0:1] * yg_ref[0].reshape(COMB_TOK, D_MODEL)
    for kk in range(1, TOP_K):
        h = h + rg[:, kk:kk + 1] * yg_ref[kk].reshape(COMB_TOK, D_MODEL)
    gt2 = mod_ref[0, 5:6, :]
    o_ref[...] = x1_ref[...] + gt2 * _rms(h, gpo_ref[...])


def _combine(yg, x1, mod, rg, g_post_ffn, tokens_per_batch):
    n, d = x1.shape
    per_b = tokens_per_batch // COMB_TOK
    gpo = g_post_ffn.reshape(1, d)
    return pl.pallas_call(
        _combine_kernel,
        grid=(n // COMB_TOK,),
        in_specs=[pl.BlockSpec((TOP_K, COMB_TOK, ROW_SUB, ROW_LANE), lambda i: (0, i, 0, 0)),
                  pl.BlockSpec((COMB_TOK, d), lambda i: (i, 0)),
                  pl.BlockSpec((1, 6, d), lambda i: (i // per_b, 0, 0)),
                  pl.BlockSpec((COMB_TOK, 128), lambda i: (i, 0)),
                  pl.BlockSpec((1, d), lambda i: (0, 0))],
        out_specs=pl.BlockSpec((COMB_TOK, d), lambda i: (i, 0)),
        out_shape=jax.ShapeDtypeStruct((n, d), F32),
        compiler_params=_cparams(("arbitrary",)),
        name="combine",
    )(yg, x1, mod, rg, gpo)


def _moe_plan(ri_t, counts, n_tok):
    idx = ri_t[:TOP_K]
    rank = ri_t[TOP_K:2 * TOP_K]
    cnt = counts.astype(jnp.int32)
    blk = MOE_BLK_LARGE if n_tok * TOP_K >= 8 * MOE_BLK_LARGE * N_EXPERTS else MOE_BLK_SMALL
    padded = (cnt + blk - 1) // blk * blk
    pend = jnp.cumsum(padded)
    pstart = pend - padded
    dest = rank
    for e in range(N_EXPERTS):
        dest = dest + jnp.where(idx == e, pstart[e], 0)
    n_blocks = (n_tok * TOP_K + blk - 1) // blk + N_EXPERTS
    bstart = jnp.arange(n_blocks, dtype=jnp.int32) * blk
    block_e = jnp.minimum(jnp.sum((bstart[:, None] >= pend[None, :]).astype(jnp.int32), axis=1), N_EXPERTS - 1)
    block_valid = jnp.clip(pstart[block_e] + cnt[block_e] - bstart, 0, blk).astype(jnp.int32)
    return dest.astype(jnp.int32), block_e, block_valid, n_blocks * blk


def _token_mixing(x, mod, p):
    b, t, d = x.shape
    n = b * t
    proj = _in_proj(x, mod, p['g_pre_mix'], p['w_in'])
    o_fw, o_bw, o_b = _mixers(proj, p['lb_logits'], p['na_tabs'])
    x1, xn, ri, rg, counts = _mix(x, mod, proj, o_fw, o_bw, o_b, p['hgrn_norm'], p['g_post_mix'], p['g_pre_ffn'],
                                  p['w_branch_a'], p['w_branch_b'], p['w_out'], p['router_w'], p['router_b'])
    dest, block_e, block_valid, n_rows = _moe_plan(ri, counts[:, 0], n)
    xs = _scatter_rows(xn.reshape(n, ROW_SUB, ROW_LANE), dest, n_rows)
    return dict(x1=x1.reshape(n, d), rg=rg.reshape(n, 128), dest=dest, block_e=block_e, block_valid=block_valid,
                xs=xs, mod=mod, shape=(b, t, d))


def _expert_rows(s, p):
    return _experts(s['xs'], s['block_e'], s['block_valid'], p['w_gu'], p['b_gu'], p['w_d'], p['b_d'])


def _finish(s, y, p):
    b, t, d = s['shape']
    yg = _gather_rows(y, s['dest'].reshape(-1)).reshape(TOP_K, b * t, ROW_SUB, ROW_LANE)
    return _combine(yg, s['x1'], s['mod'], s['rg'], p['g_post_ffn'], t).reshape(b, t, d)


def _after(first, then):
    return lax.optimization_barrier((first, then))


def kernel(x_prompt, x_sample, c_prompt, c_sample, ada_w, ada_b, g_pre_mix, g_post_mix, g_pre_ffn, g_post_ffn,
           w_in, hgrn_lb_logits, hgrn_norm, na_rpb, w_branch_a, w_branch_b, w_out, router_w, router_b,
           w_gate_up, b_gate_up, w_down, b_down):
    d = D_MODEL
    bp, bs = c_prompt.shape[0], c_sample.shape[0]
    pad = (-(bp + bs)) % 8
    c_all = jnp.concatenate([c_prompt, c_sample, jnp.zeros((pad, d), F32)], axis=0)
    mod = _ada_mod(c_all, ada_w[0], ada_b[0]).reshape(-1, 6, d)

    w = w_in[0]
    w_perm = jnp.concatenate([w[:, :5 * d], w[:, 5 * d + 3 * NA_W:], w[:, 5 * d:5 * d + 3 * NA_W]], axis=1).astype(BF16)
    w_gu, b_gu = _gate_up_layout(w_gate_up[0], b_gate_up[0])
    p = dict(
        g_pre_mix=g_pre_mix[0], g_post_mix=g_post_mix[0], g_pre_ffn=g_pre_ffn[0], g_post_ffn=g_post_ffn[0],
        w_in=w_perm, lb_logits=hgrn_lb_logits.astype(F32), hgrn_norm=hgrn_norm[0],
        na_tabs=_na_bias_tables(na_rpb[0]),
        w_branch_a=w_branch_a[0].astype(BF16), w_branch_b=w_branch_b[0].astype(BF16), w_out=w_out[0].astype(BF16),
        router_w=router_w[0].astype(BF16), router_b=router_b[0],
        w_gu=w_gu, b_gu=b_gu,
        w_d=w_down[0], b_d=b_down[0][:, None, :],
    )
    sp = _token_mixing(x_prompt, mod[:bp], p)
    sp['dest'], x_sample = _after(sp['dest'], x_sample)
    ss = _token_mixing(x_sample, mod[bp:bp + bs], p)
    ss['x1'], sp['xs'] = _after(ss['x1'], sp['xs'])
    y_p = _expert_rows(sp, p)
    y_p, ss['xs'] = _after(y_p, ss['xs'])
    y_s = _expert_rows(ss, p)
    out_p = _finish(sp, y_p, p)
    out_s = _finish(ss, y_s, p)
    return (out_p, out_s)
```

```python
import functools

import numpy as np
import jax
import jax.numpy as jnp
from jax import lax
from jax.experimental import pallas as pl
from jax.experimental.pallas import tpu as pltpu
from jax.experimental.pallas import tpu_sc as plsc

F32 = jnp.float32
BF16 = jnp.bfloat16

D_MODEL = 1024
GRID_W = 64
HA_HEADS = 8
HA_DK = 128
NA_HEADS = 8
NA_DH = 64
NA_W = NA_HEADS * NA_DH
NA_KH = 8
NA_KW = 16
N_EXPERTS = 32
TOP_K = 4
D_FF = 1024
SWIGLU_ALPHA = 1.702
SWIGLU_LIMIT = 7.0
RMS_EPS = 1e-6
NEG_BIG = -1e30

COL_A = 0
COL_GATES = 5 * 1024
COL_NA = 7 * 1024
PROJ_COLS = 7 * 1024 + 3 * NA_W

HGRN_CHUNK = 64
HGRN_TILE = 256
PROJ_TM = 2048
PROJ_TN = 2176
MIX_TM = 512
MOE_BLK_SMALL, MOE_BLK_LARGE = 512, 1024
COMB_TOK = 512
SUBLANES = 8
ROW_SUB, ROW_LANE = 8, 128
VMEM_LIMIT = 56 * 1024 * 1024


def _cparams(sem):
    return pltpu.CompilerParams(dimension_semantics=sem, vmem_limit_bytes=VMEM_LIMIT)


def _dot(a, b):
    return jnp.dot(a, b, preferred_element_type=F32)


def _dot_nt(a, b):
    return lax.dot_general(a, b, (((1,), (1,)), ((), ())), preferred_element_type=F32)


def _dot_tn(a, b):
    return lax.dot_general(a, b, (((0,), (0,)), ((), ())), preferred_element_type=F32)


def _split(a):
    hi = a.astype(BF16)
    lo = (a - hi.astype(F32)).astype(BF16)
    return hi, lo


def _dot3(a, w):
    ah, al = _split(a)
    wh, wl = _split(w)
    return _dot(ah, wh) + (_dot(al, wh) + _dot(ah, wl))


def _sigmoid(x):
    return 0.5 * jnp.tanh(0.5 * x) + 0.5


def _rms(x, g):
    return x * lax.rsqrt(jnp.mean(x * x, axis=-1, keepdims=True) + RMS_EPS) * g


def _ada_kernel(c_ref, w_ref, b_ref, o_ref):
    c = c_ref[...]
    o_ref[...] = _dot3(c * _sigmoid(c), w_ref[...]) + b_ref[...]


def _ada_mod(c, ada_w, ada_b):
    bp, d = c.shape
    n = ada_w.shape[1]
    tn = 1536
    return pl.pallas_call(
        _ada_kernel,
        grid=(n // tn,),
        in_specs=[pl.BlockSpec((bp, d), lambda j: (0, 0)),
                  pl.BlockSpec((d, tn), lambda j: (0, j)),
                  pl.BlockSpec((1, tn), lambda j: (0, j))],
        out_specs=pl.BlockSpec((bp, tn), lambda j: (0, j)),
        out_shape=jax.ShapeDtypeStruct((bp, n), F32),
        compiler_params=_cparams(("arbitrary",)),
        name="ada_mod",
    )(c, ada_w, ada_b.reshape(1, n))


def _in_proj_kernel(x_ref, mod_ref, g_ref, w_ref, o_ref, xn_ref):
    @pl.when(pl.program_id(2) == 0)
    def _():
        sh = mod_ref[0, 0:1, :]
        sc = mod_ref[0, 1:2, :]
        xn = _rms(x_ref[0], g_ref[...]) * (1.0 + sc) + sh
        xn_ref[...] = xn.astype(BF16)

    o_ref[0] = _dot(xn_ref[...], w_ref[...]).astype(BF16)


def _in_proj(x, mod, g_pre, w_in_bf):
    b, t, d = x.shape
    tm = min(PROJ_TM, t)
    n = w_in_bf.shape[1]
    return pl.pallas_call(
        _in_proj_kernel,
        grid=(b, t // tm, n // PROJ_TN),
        in_specs=[pl.BlockSpec((1, tm, d), lambda bi, i, j: (bi, i, 0)),
                  pl.BlockSpec((1, 6, d), lambda bi, i, j: (bi, 0, 0)),
                  pl.BlockSpec((1, d), lambda bi, i, j: (0, 0)),
                  pl.BlockSpec((d, PROJ_TN), lambda bi, i, j: (0, j))],
        out_specs=pl.BlockSpec((1, tm, PROJ_TN), lambda bi, i, j: (bi, i, j)),
        out_shape=jax.ShapeDtypeStruct((b, t, n), BF16),
        scratch_shapes=[pltpu.VMEM((tm, d), BF16)],
        compiler_params=_cparams(("arbitrary", "arbitrary", "arbitrary")),
        name="in_proj",
    )(x, mod, g_pre.reshape(1, d), w_in_bf)


def _cumsum_rows(x, reverse):
    groups = x.shape[0] // SUBLANES
    row = lax.broadcasted_iota(jnp.int32, (SUBLANES, x.shape[1]), 0)
    out = [None] * groups
    carry = None
    for g in (range(groups - 1, -1, -1) if reverse else range(groups)):
        blk = x[g * SUBLANES:(g + 1) * SUBLANES]
        step = 1
        while step < SUBLANES:
            if reverse:
                blk = blk + jnp.where(row < SUBLANES - step, pltpu.roll(blk, SUBLANES - step, axis=0), 0.0)
            else:
                blk = blk + jnp.where(row >= step, pltpu.roll(blk, step, axis=0), 0.0)
            step *= 2
        if carry is not None:
            blk = blk + carry
        carry = blk[0:1] if reverse else blk[SUBLANES - 1:SUBLANES]
        out[g] = blk
    return jnp.concatenate(out, axis=0)


def _hgrn_dir(q_ref, f_ref, i_ref, o_ref, s_ref, sn_ref, lbd, row0, reverse):
    c = HGRN_CHUNK
    rows = pl.ds(row0, c)
    q = q_ref[0, rows, :].astype(F32)
    fpre = f_ref[0, rows, :].astype(F32)
    v = i_ref[0, rows, :]
    f = 0.5 * (1.0 + lbd) + (0.5 * (1.0 - lbd)) * jnp.tanh(0.5 * fpre)
    logf = jnp.log(f)
    k = 1.0 - f
    ri = lax.broadcasted_iota(jnp.int32, (c, c), 0)
    ci = lax.broadcasted_iota(jnp.int32, (c, c), 1)
    keep = (ci >= ri) if reverse else (ri >= ci)
    b = _cumsum_rows(logf, reverse)
    if reverse:
        b_mid = b[c // 2:c // 2 + 1, :]
        b_end = b[0:1, :]
    else:
        b_mid = b[c // 2 - 1:c // 2, :]
        b_end = b[c - 1:c, :]
    d = b - b_mid
    qm = q * jnp.exp(d)
    km = k * jnp.exp(-d)
    qi = (qm * jnp.exp(b_mid)).astype(BF16)
    ks = (km * jnp.exp(b_end - b_mid)).astype(BF16)
    qm = qm.astype(BF16)
    km = km.astype(BF16)
    dec = jnp.exp(b_end)
    for h in range(HA_HEADS):
        sl = slice(h * HA_DK, (h + 1) * HA_DK)
        a = _dot_nt(qm[:, sl], km[:, sl])
        a = jnp.where(keep, a, 0.0).astype(BF16)
        lhs = jnp.concatenate([qi[:, sl], a], axis=1)
        rhs = jnp.concatenate([sn_ref[h], v[:, sl]], axis=0)
        o_ref[0, rows, sl] = _dot(lhs, rhs).astype(BF16)
        st = s_ref[h] * dec[:, sl] + _dot_tn(v[:, sl], ks[:, sl])
        s_ref[h] = st
        sn_ref[h] = st.astype(BF16).T


def _hgrn_tile(lbl_ref, qf_ref, ff_ref, if_ref, qb_ref, fb_ref, ib_ref, of_ref, ob_ref,
               sf_ref, sb_ref, snf_ref, snb_ref):
    @pl.when(pl.program_id(1) == 0)
    def _():
        for ref in (sf_ref, sb_ref, snf_ref, snb_ref):
            ref[...] = jnp.zeros_like(ref)

    l0 = lbl_ref[0]
    l1 = lbl_ref[1]
    m = jnp.maximum(l0, l1)
    e0 = jnp.exp(l0 - m)
    e1 = jnp.exp(l1 - m)
    lb = e0 / (e0 + e1)
    n_chunks = HGRN_TILE // HGRN_CHUNK

    def chunk(ci):
        _hgrn_dir(qf_ref, ff_ref, if_ref, of_ref, sf_ref, snf_ref, lb[0:1, :], ci * HGRN_CHUNK, False)
        _hgrn_dir(qb_ref, fb_ref, ib_ref, ob_ref, sb_ref, snb_ref, lb[1:2, :],
                  (n_chunks - 1 - ci) * HGRN_CHUNK, True)

    return chunk


def _mixers_kernel(lbl_ref, qf_ref, ff_ref, if_ref, qb_ref, fb_ref, ib_ref, nq_ref, nk_ref, nv_ref, tab_ref,
                   of_ref, ob_ref, on_ref, sf_ref, sb_ref, snf_ref, snb_ref, *, grid_rows):
    chunk = _hgrn_tile(lbl_ref, qf_ref, ff_ref, if_ref, qb_ref, fb_ref, ib_ref, of_ref, ob_ref,
                       sf_ref, sb_ref, snf_ref, snb_ref)
    na_row = _na_rows(nq_ref, nk_ref, nv_ref, tab_ref, on_ref, grid_rows)
    for ci in range(HGRN_TILE // HGRN_CHUNK):
        chunk(ci)
        na_row(ci)


def _mixers(proj, lb_logits, tabs):
    b, t, _ = proj.shape
    tt = HGRN_TILE
    nt = t // tt
    d = D_MODEL
    cq, ck, cv = (COL_NA // NA_W, COL_NA // NA_W + 1, COL_NA // NA_W + 2)

    def fwd(col):
        return pl.BlockSpec((1, tt, d), lambda bi, ti: (bi, ti, col))

    def bwd(col):
        return pl.BlockSpec((1, tt, d), lambda bi, ti: (bi, nt - 1 - ti, col))

    return pl.pallas_call(
        functools.partial(_mixers_kernel, grid_rows=t // GRID_W),
        grid=(b, nt),
        in_specs=[pl.BlockSpec((2, 2, d), lambda bi, ti: (0, 0, 0)),
                  fwd(0), fwd(1), fwd(3), bwd(0), bwd(2), bwd(3),
                  pl.BlockSpec((1, tt, NA_W), lambda bi, ti: (bi, ti, cq)),
                  pl.BlockSpec((1, t, NA_W), lambda bi, ti: (bi, 0, ck), pipeline_mode=pl.Buffered(1)),
                  pl.BlockSpec((1, t, NA_W), lambda bi, ti: (bi, 0, cv), pipeline_mode=pl.Buffered(1)),
                  pl.BlockSpec(tabs.shape, lambda bi, ti: (0, 0, 0, 0), pipeline_mode=pl.Buffered(1))],
        out_specs=[pl.BlockSpec((1, tt, d), lambda bi, ti: (bi, ti, 0)),
                   pl.BlockSpec((1, tt, d), lambda bi, ti: (bi, nt - 1 - ti, 0)),
                   pl.BlockSpec((1, tt, NA_W), lambda bi, ti: (bi, ti, 0))],
        out_shape=[jax.ShapeDtypeStruct((b, t, d), BF16), jax.ShapeDtypeStruct((b, t, d), BF16),
                   jax.ShapeDtypeStruct((b, t, NA_W), BF16)],
        scratch_shapes=[pltpu.VMEM((HA_HEADS, HA_DK, HA_DK), F32), pltpu.VMEM((HA_HEADS, HA_DK, HA_DK), F32),
                        pltpu.VMEM((HA_HEADS, HA_DK, HA_DK), BF16), pltpu.VMEM((HA_HEADS, HA_DK, HA_DK), BF16)],
        compiler_params=_cparams(("arbitrary", "arbitrary")),
        name="mixers",
    )(lb_logits, proj, proj, proj, proj, proj, proj, proj, proj, proj, tabs)


def _na_bias_tables(rpb):
    q = np.arange(GRID_W)[:, None]
    kc = np.arange(GRID_W)[None, :]
    c0 = np.clip(q - NA_KW // 2, 0, GRID_W - NA_KW)
    ok = (kc >= c0) & (kc < c0 + NA_KW)
    dc = np.clip(kc - q + (NA_KW - 1), 0, 2 * NA_KW - 2)
    rp = rpb.astype(F32)
    base = jnp.full(rp.shape[:2] + ok.shape, NEG_BIG, F32)
    for j in range(rp.shape[2]):
        base = jnp.where((ok & (dc == j))[None, None], rp[:, :, j, None, None], base)
    tabs = []
    for dr0 in range(NA_KH):
        w = base[:, dr0:dr0 + NA_KH]
        w = w.transpose(0, 2, 1, 3).reshape(NA_HEADS // 2, 2 * GRID_W, NA_KH * GRID_W)
        tabs.append(w)
    return jnp.stack(tabs, axis=1)


def _na_rows(q_ref, k_ref, v_ref, tab_ref, o_ref, grid_rows):
    g = pl.program_id(1)
    lane = lax.broadcasted_iota(jnp.int32, (GRID_W, 2 * NA_DH), 1)
    low = lane < NA_DH
    scale = NA_DH ** -0.5
    rows_per_step = HGRN_TILE // GRID_W

    def na_row(rr):
        r = g * rows_per_step + rr
        r0 = jnp.clip(r - NA_KH // 2, 0, grid_rows - NA_KH)
        dr0 = r0 - r + (NA_KH - 1)
        qrow = pl.ds(rr * GRID_W, GRID_W)
        krow = pl.ds(pl.multiple_of(r0 * GRID_W, GRID_W), NA_KH * GRID_W)
        for j in range(NA_HEADS // 2):
            sl = slice(j * 2 * NA_DH, (j + 1) * 2 * NA_DH)
            qp = q_ref[0, qrow, sl]
            zero = jnp.zeros_like(qp)
            ql = jnp.concatenate([jnp.where(low, qp, zero), jnp.where(low, zero, qp)], axis=0)
            s = _dot_nt(ql, k_ref[0, krow, sl]) * scale + tab_ref[j, dr0]
            m = jnp.max(s, axis=-1, keepdims=True)
            p = jnp.exp(s - m)
            l = jnp.sum(p, axis=-1, keepdims=True)
            pv = _dot(p.astype(BF16), v_ref[0, krow, sl]) / l
            o_ref[0, qrow, sl] = jnp.where(low, pv[:GRID_W], pv[GRID_W:]).astype(BF16)

    return na_row


def _mix_kernel(x_ref, mod_ref, of_ref, ob_ref, g_ref, ga_ref, gb_ref, nb_ref,
                gn_ref, gpm_ref, gpf_ref, wa_ref, wb_ref, wo_ref, rw_ref, rb_ref,
                x1_ref, xn_ref, ri_ref, rg_ref, cnt_ref, tri_ref, carry_ref):
    tm = MIX_TM
    first = (pl.program_id(0) == 0) & (pl.program_id(1) == 0)

    @pl.when(first)
    def _():
        r = lax.broadcasted_iota(jnp.int32, (tm, tm), 0)
        c = lax.broadcasted_iota(jnp.int32, (tm, tm), 1)
        tri_ref[...] = jnp.where(r < c, 1.0, 0.0).astype(BF16)
        carry_ref[...] = jnp.zeros_like(carry_ref)

    sc2 = mod_ref[0, 4:5, :]
    sh2 = mod_ref[0, 3:4, :]
    gt1 = mod_ref[0, 2:3, :]

    o = of_ref[0].astype(F32) + ob_ref[0].astype(F32)
    parts = []
    for h in range(HA_HEADS):
        oh = o[:, h * HA_DK:(h + 1) * HA_DK]
        parts.append(oh * lax.rsqrt(jnp.mean(oh * oh, axis=-1, keepdims=True) + RMS_EPS))
    g = g_ref[0].astype(F32)
    oa = jnp.concatenate(parts, axis=-1) * gn_ref[...] * (g * _sigmoid(g))
    ya = _dot(oa.astype(BF16), wa_ref[...])
    yb = _dot(nb_ref[0], wb_ref[...])
    mix = _sigmoid(ga_ref[0].astype(F32)) * ya + _sigmoid(gb_ref[0].astype(F32)) * yb
    mo = _dot(mix.astype(BF16), wo_ref[...])
    x1 = x_ref[0] + gt1 * _rms(mo, gpm_ref[...])
    x1_ref[0] = x1

    xn = _rms(x1, gpf_ref[...]) * (1.0 + sc2) + sh2
    xn_ref[0] = xn.reshape(tm, ROW_SUB, ROW_LANE)
    logits = _dot_nt(rw_ref[...], xn.astype(BF16)) + rb_ref[...]
    sub = lax.broadcasted_iota(jnp.int32, (N_EXPERTS, tm), 0)
    work = logits
    vals, idxs, hots = [], [], []
    for _ in range(TOP_K):
        mv = jnp.max(work, axis=0, keepdims=True)
        mi = jnp.min(jnp.where(work == mv, sub, N_EXPERTS), axis=0, keepdims=True)
        hot = sub == mi
        vals.append(mv)
        idxs.append(mi)
        hots.append(hot)
        work = jnp.where(hot, -jnp.inf, work)
    es = [jnp.exp(v - vals[0]) for v in vals]
    den = es[0] + es[1] + es[2] + es[3]

    cnt = sum(jnp.where(h, 1.0, 0.0) for h in hots)
    before = _dot(cnt.astype(BF16), tri_ref[...]) + carry_ref[...]
    carry_ref[...] = carry_ref[...] + jnp.sum(cnt, axis=1, keepdims=True)
    cnt_ref[...] = jnp.broadcast_to(carry_ref[...], cnt_ref.shape)

    ranks = [jnp.sum(jnp.where(hots[kk], before, 0.0), axis=0, keepdims=True).astype(jnp.int32)
             for kk in range(TOP_K)]
    ri_ref[...] = jnp.concatenate(idxs + ranks, axis=0)
    gates = jnp.concatenate([e / den for e in es] + [jnp.zeros((128 - TOP_K, tm), F32)], axis=0)
    rg_ref[0] = gates.T


def _mix(x, mod, proj, o_fw, o_bw, o_b, hgrn_norm, g_post_mix, g_pre_ffn, wa, wb, wo, rw, rb):
    b, t, d = x.shape
    tm = min(MIX_TM, t)
    assert tm == MIX_TM
    row = lambda a: a.reshape(1, -1)
    tok = lambda w, col: pl.BlockSpec((1, tm, w), lambda bi, i: (bi, i, col))
    full = lambda a: pl.BlockSpec(a.shape, lambda bi, i: (0,) * a.ndim)
    gn, gpm, gpf, rbr = row(hgrn_norm), row(g_post_mix), row(g_pre_ffn), rb.reshape(-1, 1)
    rw = rw.T
    outs = pl.pallas_call(
        _mix_kernel,
        grid=(b, t // tm),
        in_specs=[tok(d, 0),
                  pl.BlockSpec((1, 6, d), lambda bi, i: (bi, 0, 0)),
                  tok(d, 0), tok(d, 0),
                  tok(d, 4),
                  tok(d, COL_GATES // d), tok(d, COL_GATES // d + 1),
                  tok(NA_W, 0),
                  full(gn), full(gpm), full(gpf), full(wa), full(wb), full(wo), full(rw), full(rbr)],
        out_specs=[tok(d, 0), pl.BlockSpec((1, tm, ROW_SUB, ROW_LANE), lambda bi, i: (bi, i, 0, 0)),
                   pl.BlockSpec((8, tm), lambda bi, i: (0, bi * (t // tm) + i)), tok(128, 0),
                   pl.BlockSpec((N_EXPERTS, 128), lambda bi, i: (0, 0))],
        out_shape=[jax.ShapeDtypeStruct((b, t, d), F32), jax.ShapeDtypeStruct((b, t, ROW_SUB, ROW_LANE), F32),
                   jax.ShapeDtypeStruct((8, b * t), jnp.int32), jax.ShapeDtypeStruct((b, t, 128), F32),
                   jax.ShapeDtypeStruct((N_EXPERTS, 128), F32)],
        scratch_shapes=[pltpu.VMEM((tm, tm), BF16), pltpu.VMEM((N_EXPERTS, 1), F32)],
        compiler_params=_cparams(("arbitrary", "arbitrary")),
        name="mix",
    )(x, mod, o_fw, o_bw, proj, proj, proj, o_b, gn, gpm, gpf, wa, wb, wo, rw, rbr)
    return outs


SC_CORES, SC_SUBCORES = 2, 16
SC_ROWS = 32


def _scatter_rows(xn, dest_t, n_rows):
    n_tok = xn.shape[0]
    workers = SC_CORES * SC_SUBCORES
    per_w = n_tok // workers
    n_it = per_w // SC_ROWS
    assert per_w * workers == n_tok and n_it * SC_ROWS == per_w and n_it % 2 == 0
    idx = dest_t.reshape(TOP_K, n_tok // SC_ROWS, SC_ROWS)
    mesh = plsc.VectorSubcoreMesh(core_axis_name="c", subcore_axis_name="s")

    @functools.partial(
        pl.kernel, mesh=mesh,
        out_type=jax.ShapeDtypeStruct((n_rows, ROW_SUB, ROW_LANE), xn.dtype),
        scratch_types=[pltpu.VMEM((2, TOP_K, SC_ROWS), jnp.int32),
                       pltpu.VMEM((2, SC_ROWS, ROW_SUB, ROW_LANE), xn.dtype),
                       pltpu.SemaphoreType.DMA((2,)), pltpu.SemaphoreType.DMA((2,))],
    )
    def scatter_kernel(x_hbm, idx_hbm, out_hbm, idx_v, rows_v, sem_load, sem_store):
        chunk0 = (lax.axis_index("s") * SC_CORES + lax.axis_index("c")) * n_it

        def load(g, slot):
            tok = pl.ds(pl.multiple_of((chunk0 + g) * SC_ROWS, SC_ROWS), SC_ROWS)
            return pltpu.make_async_copy(x_hbm.at[tok], rows_v.at[slot], sem_load.at[slot])

        def start_load(g, slot):
            load(g, slot).start()
            for kk in range(TOP_K):
                pltpu.sync_copy(idx_hbm.at[kk, chunk0 + g], idx_v.at[slot, kk])

        def store(slot, kk):
            return pltpu.make_async_copy(rows_v.at[slot], out_hbm.at[idx_v.at[slot, kk]], sem_store.at[slot])

        def scatter(g, slot):
            load(g, slot).wait()
            for kk in range(TOP_K):
                store(slot, kk).start()
            for kk in range(TOP_K):
                store(slot, kk).wait()

        start_load(0, 0)

        @pl.loop(0, n_it, step=2)
        def _(g):
            start_load(g + 1, 1)
            scatter(g, 0)

            @pl.when(g + 2 < n_it)
            def _():
                start_load(g + 2, 0)

            scatter(g + 1, 1)

    return scatter_kernel(xn, idx)


GU_SLAB = 256


def _gate_up_layout_kernel(w_ref, o_ref):
    half = GU_SLAB // 2
    r = lax.broadcasted_iota(jnp.int32, (GU_SLAB, GU_SLAB), 0)
    c = lax.broadcasted_iota(jnp.int32, (GU_SLAB, GU_SLAB), 1)
    src = jnp.where(c < half, 2 * c, 2 * (c - half) + 1)
    pm = jnp.where(r == src, 1.0, 0.0).astype(BF16)
    for s in range(w_ref.shape[2] // GU_SLAB):
        cols = slice(s * GU_SLAB, (s + 1) * GU_SLAB)
        o_ref[0, :, cols] = _dot(w_ref[0, :, cols].astype(BF16), pm).astype(BF16)


def _gate_up_layout(w_gate_up, b_gate_up):
    e, k, n2 = w_gate_up.shape
    half = GU_SLAB // 2
    tk = 512
    w = pl.pallas_call(
        _gate_up_layout_kernel,
        grid=(e, k // tk),
        in_specs=[pl.BlockSpec((1, tk, n2), lambda i, j: (i, j, 0))],
        out_specs=pl.BlockSpec((1, tk, n2), lambda i, j: (i, j, 0)),
        out_shape=jax.ShapeDtypeStruct((e, k, n2), BF16),
        compiler_params=_cparams(("arbitrary", "arbitrary")),
        name="gate_up_layout",
    )(w_gate_up)
    b = b_gate_up.reshape(e, n2 // GU_SLAB, half, 2).transpose(0, 1, 3, 2).reshape(e, 1, n2)
    return w, b


def _expert_kernel(be_ref, bv_ref, x_ref, wgu_ref, bgu_ref, wd_ref, bd_ref, o_ref, a_ref):
    i = pl.program_id(0)
    valid = bv_ref[i]
    half = GU_SLAB // 2

    @pl.when(valid > 0)
    def _():
        blk = x_ref.shape[0]
        rowi = lax.broadcasted_iota(jnp.int32, (blk, 1), 0)
        x = x_ref[...].reshape(blk, D_MODEL)
        x = jnp.where(rowi < valid, x, 0.0).astype(BF16)
        for s in range(2 * D_FF // GU_SLAB):
            cols = slice(s * GU_SLAB, (s + 1) * GU_SLAB)
            h = _dot(x, wgu_ref[0, :, cols]) + bgu_ref[0, :, cols]
            glu = jnp.minimum(h[:, :half], SWIGLU_LIMIT)
            lin = jnp.clip(h[:, half:], -SWIGLU_LIMIT, SWIGLU_LIMIT)
            a = glu * _sigmoid(SWIGLU_ALPHA * glu) * (lin + 1.0)
            a_ref[:, s * half:(s + 1) * half] = a.astype(BF16)
        o_ref[...] = (_dot(a_ref[...], wd_ref[0].astype(BF16)) + bd_ref[0]).reshape(o_ref.shape)

    @pl.when(valid <= 0)
    def _():
        o_ref[...] = jnp.zeros_like(o_ref)


def _experts(xs, block_e, block_valid, wgu, bgu, wd, bd):
    n_rows = xs.shape[0]
    d = D_MODEL
    n_blocks = block_e.shape[0]
    blk = n_rows // n_blocks
    emap = lambda i, be, bv: (be[i], 0, 0)
    rows = lambda: pl.BlockSpec((blk, ROW_SUB, ROW_LANE), lambda i, be, bv: (i, 0, 0))
    return pl.pallas_call(
        _expert_kernel,
        grid_spec=pltpu.PrefetchScalarGridSpec(
            num_scalar_prefetch=2,
            grid=(n_blocks,),
            in_specs=[rows(),
                      pl.BlockSpec((1, d, 2 * D_FF), emap), pl.BlockSpec((1, 1, 2 * D_FF), emap),
                      pl.BlockSpec((1, D_FF, d), emap), pl.BlockSpec((1, 1, d), emap)],
            out_specs=rows(),
            scratch_shapes=[pltpu.VMEM((blk, D_FF), BF16)],
        ),
        out_shape=jax.ShapeDtypeStruct((n_rows, ROW_SUB, ROW_LANE), F32),
        compiler_params=_cparams(("arbitrary",)),
        name="experts",
    )(block_e, block_valid, xs, wgu, bgu, wd, bd)


def _gather_rows(y, dest_flat):
    n_out = dest_flat.shape[0]
    workers = SC_CORES * SC_SUBCORES
    per_w = n_out // workers
    n_it = per_w // SC_ROWS
    assert per_w * workers == n_out and n_it * SC_ROWS == per_w and n_it % 2 == 0
    mesh = plsc.VectorSubcoreMesh(core_axis_name="c", subcore_axis_name="s")

    @functools.partial(
        pl.kernel, mesh=mesh,
        out_type=jax.ShapeDtypeStruct((n_out, ROW_SUB, ROW_LANE), y.dtype),
        scratch_types=[pltpu.VMEM((2, SC_ROWS), jnp.int32),
                       pltpu.VMEM((2, SC_ROWS, ROW_SUB, ROW_LANE), y.dtype),
                       pltpu.SemaphoreType.DMA((2,))],
    )
    def gather_kernel(y_hbm, idx_hbm, out_hbm, idx_v, rows_v, sem):
        base = (lax.axis_index("s") * SC_CORES + lax.axis_index("c")) * per_w

        def span(g):
            return pl.ds(pl.multiple_of(base + g * SC_ROWS, SC_ROWS), SC_ROWS)

        def row_gather(slot):
            return pltpu.make_async_copy(y_hbm.at[idx_v.at[slot]], rows_v.at[slot], sem.at[slot])

        def start(g, slot):
            pltpu.sync_copy(idx_hbm.at[span(g)], idx_v.at[slot])
            row_gather(slot).start()

        start(0, 0)

        @pl.loop(0, n_it, step=2)
        def _(g):
            start(g + 1, 1)
            row_gather(0).wait()
            pltpu.sync_copy(rows_v.at[0], out_hbm.at[span(g)])

            @pl.when(g + 2 < n_it)
            def _():
                start(g + 2, 0)

            row_gather(1).wait()
            pltpu.sync_copy(rows_v.at[1], out_hbm.at[span(g + 1)])

    return gather_kernel(y, dest_flat)


def _combine_kernel(yg_ref, x1_ref, mod_ref, rg_ref, gpo_ref, o_ref):
    rg = rg_ref[...]
    h = rg[:, 0:1] * yg_ref[0].reshape(COMB_TOK, D_MODEL)
    for kk in range(1, TOP_K):
        h = h + rg[:, kk:kk + 1] * yg_ref[kk].reshape(COMB_TOK, D_MODEL)
    gt2 = mod_ref[0, 5:6, :]
    o_ref[...] = x1_ref[...] + gt2 * _rms(h, gpo_ref[...])


def _combine(yg, x1, mod, rg, g_post_ffn, tokens_per_batch):
    n, d = x1.shape
    per_b = tokens_per_batch // COMB_TOK
    gpo = g_post_ffn.reshape(1, d)
    return pl.pallas_call(
        _combine_kernel,
        grid=(n // COMB_TOK,),
        in_specs=[pl.BlockSpec((TOP_K, COMB_TOK, ROW_SUB, ROW_LANE), lambda i: (0, i, 0, 0)),
                  pl.BlockSpec((COMB_TOK, d), lambda i: (i, 0)),
                  pl.BlockSpec((1, 6, d), lambda i: (i // per_b, 0, 0)),
                  pl.BlockSpec((COMB_TOK, 128), lambda i: (i, 0)),
                  pl.BlockSpec((1, d), lambda i: (0, 0))],
        out_specs=pl.BlockSpec((COMB_TOK, d), lambda i: (i, 0)),
        out_shape=jax.ShapeDtypeStruct((n, d), F32),
        compiler_params=_cparams(("arbitrary",)),
        name="combine",
    )(yg, x1, mod, rg, gpo)


def _moe_plan(ri_t, counts, n_tok):
    idx = ri_t[:TOP_K]
    rank = ri_t[TOP_K:2 * TOP_K]
    cnt = counts.astype(jnp.int32)
    blk = MOE_BLK_LARGE if n_tok * TOP_K >= 8 * MOE_BLK_LARGE * N_EXPERTS else MOE_BLK_SMALL
    padded = (cnt + blk - 1) // blk * blk
    pend = jnp.cumsum(padded)
    pstart = pend - padded
    dest = rank
    for e in range(N_EXPERTS):
        dest = dest + jnp.where(idx == e, pstart[e], 0)
    n_blocks = (n_tok * TOP_K + blk - 1) // blk + N_EXPERTS
    bstart = jnp.arange(n_blocks, dtype=jnp.int32) * blk
    block_e = jnp.minimum(jnp.sum((bstart[:, None] >= pend[None, :]).astype(jnp.int32), axis=1), N_EXPERTS - 1)
    mine = jnp.arange(N_EXPERTS, dtype=jnp.int32)[None, :] == block_e[:, None]
    row_end = jnp.sum(jnp.where(mine, (pstart + cnt)[None, :], 0), axis=1)
    block_valid = jnp.clip(row_end - bstart, 0, blk).astype(jnp.int32)
    return dest.astype(jnp.int32), block_e, block_valid, n_blocks * blk


def _token_mixing(x, mod, p):
    b, t, d = x.shape
    n = b * t
    proj = _in_proj(x, mod, p['g_pre_mix'], p['w_in'])
    o_fw, o_bw, o_b = _mixers(proj, p['lb_logits'], p['na_tabs'])
    x1, xn, ri, rg, counts = _mix(x, mod, proj, o_fw, o_bw, o_b, p['hgrn_norm'], p['g_post_mix'], p['g_pre_ffn'],
                                  p['w_branch_a'], p['w_branch_b'], p['w_out'], p['router_w'], p['router_b'])
    dest, block_e, block_valid, n_rows = _moe_plan(ri, counts[:, 0], n)
    xs = _scatter_rows(xn.reshape(n, ROW_SUB, ROW_LANE), dest, n_rows)
    return dict(x1=x1.reshape(n, d), rg=rg.reshape(n, 128), dest=dest, block_e=block_e, block_valid=block_valid,
                xs=xs, mod=mod, shape=(b, t, d))


def _expert_rows(s, p):
    return _experts(s['xs'], s['block_e'], s['block_valid'], p['w_gu'], p['b_gu'], p['w_d'], p['b_d'])


def _finish(s, y, p):
    b, t, d = s['shape']
    yg = _gather_rows(y, s['dest'].reshape(-1)).reshape(TOP_K, b * t, ROW_SUB, ROW_LANE)
    return _combine(yg, s['x1'], s['mod'], s['rg'], p['g_post_ffn'], t).reshape(b, t, d)


def _after(first, then):
    return lax.optimization_barrier((first, then))


def kernel(x_prompt, x_sample, c_prompt, c_sample, ada_w, ada_b, g_pre_mix, g_post_mix, g_pre_ffn, g_post_ffn,
           w_in, hgrn_lb_logits, hgrn_norm, na_rpb, w_branch_a, w_branch_b, w_out, router_w, router_b,
           w_gate_up, b_gate_up, w_down, b_down):
    d = D_MODEL
    bp, bs = c_prompt.shape[0], c_sample.shape[0]
    pad = (-(bp + bs)) % 8
    c_all = jnp.concatenate([c_prompt, c_sample, jnp.zeros((pad, d), F32)], axis=0)
    mod = _ada_mod(c_all, ada_w[0], ada_b[0]).reshape(-1, 6, d)

    w = w_in[0]
    w_perm = jnp.concatenate([w[:, :5 * d], w[:, 5 * d + 3 * NA_W:], w[:, 5 * d:5 * d + 3 * NA_W]], axis=1).astype(BF16)
    w_gu, b_gu = _gate_up_layout(w_gate_up[0], b_gate_up[0])
    p = dict(
        g_pre_mix=g_pre_mix[0], g_post_mix=g_post_mix[0], g_pre_ffn=g_pre_ffn[0], g_post_ffn=g_post_ffn[0],
        w_in=w_perm, lb_logits=hgrn_lb_logits.astype(F32), hgrn_norm=hgrn_norm[0],
        na_tabs=_na_bias_tables(na_rpb[0]),
        w_branch_a=w_branch_a[0].astype(BF16), w_branch_b=w_branch_b[0].astype(BF16), w_out=w_out[0].astype(BF16),
        router_w=router_w[0].astype(BF16), router_b=router_b[0],
        w_gu=w_gu, b_gu=b_gu,
        w_d=w_down[0], b_d=b_down[0][:, None, :],
    )
    sp = _token_mixing(x_prompt, mod[:bp], p)
    sp['dest'], x_sample = _after(sp['dest'], x_sample)
    ss = _token_mixing(x_sample, mod[bp:bp + bs], p)
    ss['x1'], sp['xs'] = _after(ss['x1'], sp['xs'])
    y_p = _expert_rows(sp, p)
    y_p, ss['xs'] = _after(y_p, ss['xs'])
    y_s = _expert_rows(ss, p)
    out_p = _finish(sp, y_p, p)
    out_s = _finish(ss, y_s, p)
    return (out_p, out_s)
```

```python
import functools

import numpy as np
import jax
import jax.numpy as jnp
from jax import lax
from jax.experimental import pallas as pl
from jax.experimental.pallas import tpu as pltpu
from jax.experimental.pallas import tpu_sc as plsc

F32 = jnp.float32
BF16 = jnp.bfloat16

D_MODEL = 1024
GRID_W = 64
HA_HEADS = 8
HA_DK = 128
NA_HEADS = 8
NA_DH = 64
NA_W = NA_HEADS * NA_DH
NA_KH = 8
NA_KW = 16
N_EXPERTS = 32
TOP_K = 4
D_FF = 1024
SWIGLU_ALPHA = 1.702
SWIGLU_LIMIT = 7.0
RMS_EPS = 1e-6
NEG_BIG = -1e30

COL_A = 0
COL_GATES = 5 * 1024
COL_NA = 7 * 1024
PROJ_COLS = 7 * 1024 + 3 * NA_W

HGRN_CHUNK = 64
HGRN_TILE = 512
PROJ_TM = 2048
PROJ_TN = 2176
MIX_TM = 512
MOE_BLK_SMALL, MOE_BLK_LARGE = 512, 1024
COMB_TOK = 512
SUBLANES = 8
ROW_SUB, ROW_LANE = 8, 128
VMEM_LIMIT = 56 * 1024 * 1024


def _cparams(sem):
    return pltpu.CompilerParams(dimension_semantics=sem, vmem_limit_bytes=VMEM_LIMIT)


def _dot(a, b):
    return jnp.dot(a, b, preferred_element_type=F32)


def _dot_nt(a, b):
    return lax.dot_general(a, b, (((1,), (1,)), ((), ())), preferred_element_type=F32)


def _dot_tn(a, b):
    return lax.dot_general(a, b, (((0,), (0,)), ((), ())), preferred_element_type=F32)


def _split(a):
    hi = a.astype(BF16)
    lo = (a - hi.astype(F32)).astype(BF16)
    return hi, lo


def _dot3(a, w):
    ah, al = _split(a)
    wh, wl = _split(w)
    return _dot(ah, wh) + (_dot(al, wh) + _dot(ah, wl))


def _sigmoid(x):
    return 0.5 * jnp.tanh(0.5 * x) + 0.5


def _rms(x, g):
    return x * lax.rsqrt(jnp.mean(x * x, axis=-1, keepdims=True) + RMS_EPS) * g


def _ada_kernel(c_ref, w_ref, b_ref, o_ref):
    c = c_ref[...]
    o_ref[...] = _dot3(c * _sigmoid(c), w_ref[...]) + b_ref[...]


def _ada_mod(c, ada_w, ada_b):
    bp, d = c.shape
    n = ada_w.shape[1]
    tn = 1536
    return pl.pallas_call(
        _ada_kernel,
        grid=(n // tn,),
        in_specs=[pl.BlockSpec((bp, d), lambda j: (0, 0)),
                  pl.BlockSpec((d, tn), lambda j: (0, j)),
                  pl.BlockSpec((1, tn), lambda j: (0, j))],
        out_specs=pl.BlockSpec((bp, tn), lambda j: (0, j)),
        out_shape=jax.ShapeDtypeStruct((bp, n), F32),
        compiler_params=_cparams(("arbitrary",)),
        name="ada_mod",
    )(c, ada_w, ada_b.reshape(1, n))


def _in_proj_kernel(x_ref, mod_ref, g_ref, w_ref, o_ref, xn_ref):
    @pl.when(pl.program_id(2) == 0)
    def _():
        sh = mod_ref[0, 0:1, :]
        sc = mod_ref[0, 1:2, :]
        xn = _rms(x_ref[0], g_ref[...]) * (1.0 + sc) + sh
        xn_ref[...] = xn.astype(BF16)

    o_ref[0] = _dot(xn_ref[...], w_ref[...]).astype(BF16)


def _in_proj(x, mod, g_pre, w_in_bf):
    b, t, d = x.shape
    tm = min(PROJ_TM, t)
    n = w_in_bf.shape[1]
    return pl.pallas_call(
        _in_proj_kernel,
        grid=(b, t // tm, n // PROJ_TN),
        in_specs=[pl.BlockSpec((1, tm, d), lambda bi, i, j: (bi, i, 0)),
                  pl.BlockSpec((1, 6, d), lambda bi, i, j: (bi, 0, 0)),
                  pl.BlockSpec((1, d), lambda bi, i, j: (0, 0)),
                  pl.BlockSpec((d, PROJ_TN), lambda bi, i, j: (0, j))],
        out_specs=pl.BlockSpec((1, tm, PROJ_TN), lambda bi, i, j: (bi, i, j)),
        out_shape=jax.ShapeDtypeStruct((b, t, n), BF16),
        scratch_shapes=[pltpu.VMEM((tm, d), BF16)],
        compiler_params=_cparams(("arbitrary", "arbitrary", "arbitrary")),
        name="in_proj",
    )(x, mod, g_pre.reshape(1, d), w_in_bf)


def _cumsum_rows(x, reverse):
    groups = x.shape[0] // SUBLANES
    row = lax.broadcasted_iota(jnp.int32, (SUBLANES, x.shape[1]), 0)
    out = [None] * groups
    carry = None
    for g in (range(groups - 1, -1, -1) if reverse else range(groups)):
        blk = x[g * SUBLANES:(g + 1) * SUBLANES]
        step = 1
        while step < SUBLANES:
            if reverse:
                blk = blk + jnp.where(row < SUBLANES - step, pltpu.roll(blk, SUBLANES - step, axis=0), 0.0)
            else:
                blk = blk + jnp.where(row >= step, pltpu.roll(blk, step, axis=0), 0.0)
            step *= 2
        if carry is not None:
            blk = blk + carry
        carry = blk[0:1] if reverse else blk[SUBLANES - 1:SUBLANES]
        out[g] = blk
    return jnp.concatenate(out, axis=0)


def _hgrn_dir(q_ref, f_ref, i_ref, o_ref, s_ref, sn_ref, lbd, row0, reverse):
    c = HGRN_CHUNK
    rows = pl.ds(row0, c)
    q = q_ref[0, rows, :].astype(F32)
    fpre = f_ref[0, rows, :].astype(F32)
    v = i_ref[0, rows, :]
    f = 0.5 * (1.0 + lbd) + (0.5 * (1.0 - lbd)) * jnp.tanh(0.5 * fpre)
    logf = jnp.log(f)
    k = 1.0 - f
    ri = lax.broadcasted_iota(jnp.int32, (c, c), 0)
    ci = lax.broadcasted_iota(jnp.int32, (c, c), 1)
    keep = (ci >= ri) if reverse else (ri >= ci)
    b = _cumsum_rows(logf, reverse)
    if reverse:
        b_mid = b[c // 2:c // 2 + 1, :]
        b_end = b[0:1, :]
    else:
        b_mid = b[c // 2 - 1:c // 2, :]
        b_end = b[c - 1:c, :]
    d = b - b_mid
    qm = q * jnp.exp(d)
    km = k * jnp.exp(-d)
    qi = (qm * jnp.exp(b_mid)).astype(BF16)
    ks = (km * jnp.exp(b_end - b_mid)).astype(BF16)
    qm = qm.astype(BF16)
    km = km.astype(BF16)
    dec = jnp.exp(b_end)
    for h in range(HA_HEADS):
        sl = slice(h * HA_DK, (h + 1) * HA_DK)
        a = _dot_nt(qm[:, sl], km[:, sl])
        a = jnp.where(keep, a, 0.0).astype(BF16)
        lhs = jnp.concatenate([qi[:, sl], a], axis=1)
        rhs = jnp.concatenate([sn_ref[h], v[:, sl]], axis=0)
        o_ref[0, rows, sl] = _dot(lhs, rhs).astype(BF16)
        st = s_ref[h] * dec[:, sl] + _dot_tn(v[:, sl], ks[:, sl])
        s_ref[h] = st
        sn_ref[h] = st.astype(BF16).T


def _hgrn_tile(lbl_ref, qf_ref, ff_ref, if_ref, qb_ref, fb_ref, ib_ref, of_ref, ob_ref,
               sf_ref, sb_ref, snf_ref, snb_ref):
    @pl.when(pl.program_id(1) == 0)
    def _():
        for ref in (sf_ref, sb_ref, snf_ref, snb_ref):
            ref[...] = jnp.zeros_like(ref)

    l0 = lbl_ref[0]
    l1 = lbl_ref[1]
    m = jnp.maximum(l0, l1)
    e0 = jnp.exp(l0 - m)
    e1 = jnp.exp(l1 - m)
    lb = e0 / (e0 + e1)
    n_chunks = HGRN_TILE // HGRN_CHUNK

    def chunk(ci):
        _hgrn_dir(qf_ref, ff_ref, if_ref, of_ref, sf_ref, snf_ref, lb[0:1, :], ci * HGRN_CHUNK, False)
        _hgrn_dir(qb_ref, fb_ref, ib_ref, ob_ref, sb_ref, snb_ref, lb[1:2, :],
                  (n_chunks - 1 - ci) * HGRN_CHUNK, True)

    return chunk


def _mixers_kernel(lbl_ref, qf_ref, ff_ref, if_ref, qb_ref, fb_ref, ib_ref, nq_ref, nk_ref, nv_ref, tab_ref,
                   of_ref, ob_ref, on_ref, sf_ref, sb_ref, snf_ref, snb_ref, *, grid_rows):
    chunk = _hgrn_tile(lbl_ref, qf_ref, ff_ref, if_ref, qb_ref, fb_ref, ib_ref, of_ref, ob_ref,
                       sf_ref, sb_ref, snf_ref, snb_ref)
    na_row = _na_rows(nq_ref, nk_ref, nv_ref, tab_ref, on_ref, grid_rows)
    for ci in range(HGRN_TILE // HGRN_CHUNK):
        chunk(ci)
        na_row(ci)


def _mixers(proj, lb_logits, tabs):
    b, t, _ = proj.shape
    tt = HGRN_TILE
    nt = t // tt
    d = D_MODEL
    cq, ck, cv = (COL_NA // NA_W, COL_NA // NA_W + 1, COL_NA // NA_W + 2)

    def fwd(col):
        return pl.BlockSpec((1, tt, d), lambda bi, ti: (bi, ti, col))

    def bwd(col):
        return pl.BlockSpec((1, tt, d), lambda bi, ti: (bi, nt - 1 - ti, col))

    return pl.pallas_call(
        functools.partial(_mixers_kernel, grid_rows=t // GRID_W),
        grid=(b, nt),
        in_specs=[pl.BlockSpec((2, 2, d), lambda bi, ti: (0, 0, 0)),
                  fwd(0), fwd(1), fwd(3), bwd(0), bwd(2), bwd(3),
                  pl.BlockSpec((1, tt, NA_W), lambda bi, ti: (bi, ti, cq)),
                  pl.BlockSpec((1, t, NA_W), lambda bi, ti: (bi, 0, ck), pipeline_mode=pl.Buffered(1)),
                  pl.BlockSpec((1, t, NA_W), lambda bi, ti: (bi, 0, cv), pipeline_mode=pl.Buffered(1)),
                  pl.BlockSpec(tabs.shape, lambda bi, ti: (0, 0, 0, 0), pipeline_mode=pl.Buffered(1))],
        out_specs=[pl.BlockSpec((1, tt, d), lambda bi, ti: (bi, ti, 0)),
                   pl.BlockSpec((1, tt, d), lambda bi, ti: (bi, nt - 1 - ti, 0)),
                   pl.BlockSpec((1, tt, NA_W), lambda bi, ti: (bi, ti, 0))],
        out_shape=[jax.ShapeDtypeStruct((b, t, d), BF16), jax.ShapeDtypeStruct((b, t, d), BF16),
                   jax.ShapeDtypeStruct((b, t, NA_W), BF16)],
        scratch_shapes=[pltpu.VMEM((HA_HEADS, HA_DK, HA_DK), F32), pltpu.VMEM((HA_HEADS, HA_DK, HA_DK), F32),
                        pltpu.VMEM((HA_HEADS, HA_DK, HA_DK), BF16), pltpu.VMEM((HA_HEADS, HA_DK, HA_DK), BF16)],
        compiler_params=_cparams(("arbitrary", "arbitrary")),
        name="mixers",
    )(lb_logits, proj, proj, proj, proj, proj, proj, proj, proj, proj, tabs)


def _na_bias_tables(rpb):
    q = np.arange(GRID_W)[:, None]
    kc = np.arange(GRID_W)[None, :]
    c0 = np.clip(q - NA_KW // 2, 0, GRID_W - NA_KW)
    ok = (kc >= c0) & (kc < c0 + NA_KW)
    dc = np.clip(kc - q + (NA_KW - 1), 0, 2 * NA_KW - 2)
    rp = rpb.astype(F32)
    base = jnp.full(rp.shape[:2] + ok.shape, NEG_BIG, F32)
    for j in range(rp.shape[2]):
        base = jnp.where((ok & (dc == j))[None, None], rp[:, :, j, None, None], base)
    tabs = []
    for dr0 in range(NA_KH):
        w = base[:, dr0:dr0 + NA_KH]
        w = w.transpose(0, 2, 1, 3).reshape(NA_HEADS // 2, 2 * GRID_W, NA_KH * GRID_W)
        tabs.append(w)
    return jnp.stack(tabs, axis=1)


def _na_rows(q_ref, k_ref, v_ref, tab_ref, o_ref, grid_rows):
    g = pl.program_id(1)
    lane = lax.broadcasted_iota(jnp.int32, (GRID_W, 2 * NA_DH), 1)
    low = lane < NA_DH
    scale = NA_DH ** -0.5
    rows_per_step = HGRN_TILE // GRID_W

    def na_row(rr):
        r = g * rows_per_step + rr
        r0 = jnp.clip(r - NA_KH // 2, 0, grid_rows - NA_KH)
        dr0 = r0 - r + (NA_KH - 1)
        qrow = pl.ds(rr * GRID_W, GRID_W)
        krow = pl.ds(pl.multiple_of(r0 * GRID_W, GRID_W), NA_KH * GRID_W)
        for j in range(NA_HEADS // 2):
            sl = slice(j * 2 * NA_DH, (j + 1) * 2 * NA_DH)
            qp = q_ref[0, qrow, sl]
            zero = jnp.zeros_like(qp)
            ql = jnp.concatenate([jnp.where(low, qp, zero), jnp.where(low, zero, qp)], axis=0)
            s = _dot_nt(ql, k_ref[0, krow, sl]) * scale + tab_ref[j, dr0]
            m = jnp.max(s, axis=-1, keepdims=True)
            p = jnp.exp(s - m)
            l = jnp.sum(p, axis=-1, keepdims=True)
            pv = _dot(p.astype(BF16), v_ref[0, krow, sl]) / l
            o_ref[0, qrow, sl] = jnp.where(low, pv[:GRID_W], pv[GRID_W:]).astype(BF16)

    return na_row


def _mix_kernel(x_ref, mod_ref, of_ref, ob_ref, g_ref, ga_ref, gb_ref, nb_ref,
                gn_ref, gpm_ref, gpf_ref, wa_ref, wb_ref, wo_ref, rw_ref, rb_ref,
                x1_ref, xn_ref, ri_ref, rg_ref, cnt_ref, tri_ref, carry_ref):
    tm = MIX_TM
    first = (pl.program_id(0) == 0) & (pl.program_id(1) == 0)

    @pl.when(first)
    def _():
        r = lax.broadcasted_iota(jnp.int32, (tm, tm), 0)
        c = lax.broadcasted_iota(jnp.int32, (tm, tm), 1)
        tri_ref[...] = jnp.where(r < c, 1.0, 0.0).astype(BF16)
        carry_ref[...] = jnp.zeros_like(carry_ref)

    sc2 = mod_ref[0, 4:5, :]
    sh2 = mod_ref[0, 3:4, :]
    gt1 = mod_ref[0, 2:3, :]

    o = of_ref[0].astype(F32) + ob_ref[0].astype(F32)
    parts = []
    for h in range(HA_HEADS):
        oh = o[:, h * HA_DK:(h + 1) * HA_DK]
        parts.append(oh * lax.rsqrt(jnp.mean(oh * oh, axis=-1, keepdims=True) + RMS_EPS))
    g = g_ref[0].astype(F32)
    oa = jnp.concatenate(parts, axis=-1) * gn_ref[...] * (g * _sigmoid(g))
    ya = _dot(oa.astype(BF16), wa_ref[...])
    yb = _dot(nb_ref[0], wb_ref[...])
    mix = _sigmoid(ga_ref[0].astype(F32)) * ya + _sigmoid(gb_ref[0].astype(F32)) * yb
    mo = _dot(mix.astype(BF16), wo_ref[...])
    x1 = x_ref[0] + gt1 * _rms(mo, gpm_ref[...])
    x1_ref[0] = x1

    xn = _rms(x1, gpf_ref[...]) * (1.0 + sc2) + sh2
    xn_ref[0] = xn.reshape(tm, ROW_SUB, ROW_LANE)
    logits = _dot_nt(rw_ref[...], xn.astype(BF16)) + rb_ref[...]
    sub = lax.broadcasted_iota(jnp.int32, (N_EXPERTS, tm), 0)
    work = logits
    vals, idxs, hots = [], [], []
    for _ in range(TOP_K):
        mv = jnp.max(work, axis=0, keepdims=True)
        mi = jnp.min(jnp.where(work == mv, sub, N_EXPERTS), axis=0, keepdims=True)
        hot = sub == mi
        vals.append(mv)
        idxs.append(mi)
        hots.append(hot)
        work = jnp.where(hot, -jnp.inf, work)
    es = [jnp.exp(v - vals[0]) for v in vals]
    den = es[0] + es[1] + es[2] + es[3]

    cnt = sum(jnp.where(h, 1.0, 0.0) for h in hots)
    before = _dot(cnt.astype(BF16), tri_ref[...]) + carry_ref[...]
    carry_ref[...] = carry_ref[...] + jnp.sum(cnt, axis=1, keepdims=True)
    cnt_ref[...] = jnp.broadcast_to(carry_ref[...], cnt_ref.shape)

    ranks = [jnp.sum(jnp.where(hots[kk], before, 0.0), axis=0, keepdims=True).astype(jnp.int32)
             for kk in range(TOP_K)]
    ri_ref[...] = jnp.concatenate(idxs + ranks, axis=0)
    gates = jnp.concatenate([e / den for e in es] + [jnp.zeros((128 - TOP_K, tm), F32)], axis=0)
    rg_ref[0] = gates.T


def _mix(x, mod, proj, o_fw, o_bw, o_b, hgrn_norm, g_post_mix, g_pre_ffn, wa, wb, wo, rw, rb):
    b, t, d = x.shape
    tm = min(MIX_TM, t)
    assert tm == MIX_TM
    row = lambda a: a.reshape(1, -1)
    tok = lambda w, col: pl.BlockSpec((1, tm, w), lambda bi, i: (bi, i, col))
    full = lambda a: pl.BlockSpec(a.shape, lambda bi, i: (0,) * a.ndim)
    gn, gpm, gpf, rbr = row(hgrn_norm), row(g_post_mix), row(g_pre_ffn), rb.reshape(-1, 1)
    rw = rw.T
    outs = pl.pallas_call(
        _mix_kernel,
        grid=(b, t // tm),
        in_specs=[tok(d, 0),
                  pl.BlockSpec((1, 6, d), lambda bi, i: (bi, 0, 0)),
                  tok(d, 0), tok(d, 0),
                  tok(d, 4),
                  tok(d, COL_GATES // d), tok(d, COL_GATES // d + 1),
                  tok(NA_W, 0),
                  full(gn), full(gpm), full(gpf), full(wa), full(wb), full(wo), full(rw), full(rbr)],
        out_specs=[tok(d, 0), pl.BlockSpec((1, tm, ROW_SUB, ROW_LANE), lambda bi, i: (bi, i, 0, 0)),
                   pl.BlockSpec((8, tm), lambda bi, i: (0, bi * (t // tm) + i)), tok(128, 0),
                   pl.BlockSpec((N_EXPERTS, 128), lambda bi, i: (0, 0))],
        out_shape=[jax.ShapeDtypeStruct((b, t, d), F32), jax.ShapeDtypeStruct((b, t, ROW_SUB, ROW_LANE), F32),
                   jax.ShapeDtypeStruct((8, b * t), jnp.int32), jax.ShapeDtypeStruct((b, t, 128), F32),
                   jax.ShapeDtypeStruct((N_EXPERTS, 128), F32)],
        scratch_shapes=[pltpu.VMEM((tm, tm), BF16), pltpu.VMEM((N_EXPERTS, 1), F32)],
        compiler_params=_cparams(("arbitrary", "arbitrary")),
        name="mix",
    )(x, mod, o_fw, o_bw, proj, proj, proj, o_b, gn, gpm, gpf, wa, wb, wo, rw, rbr)
    return outs


SC_CORES, SC_SUBCORES = 2, 16
SC_ROWS = 32


def _scatter_rows(xn, dest_t, n_rows):
    n_tok = xn.shape[0]
    workers = SC_CORES * SC_SUBCORES
    per_w = n_tok // workers
    n_it = per_w // SC_ROWS
    assert per_w * workers == n_tok and n_it * SC_ROWS == per_w and n_it % 2 == 0
    idx = dest_t.reshape(TOP_K, n_tok // SC_ROWS, SC_ROWS)
    mesh = plsc.VectorSubcoreMesh(core_axis_name="c", subcore_axis_name="s")

    @functools.partial(
        pl.kernel, mesh=mesh,
        out_type=jax.ShapeDtypeStruct((n_rows, ROW_SUB, ROW_LANE), xn.dtype),
        scratch_types=[pltpu.VMEM((2, TOP_K, SC_ROWS), jnp.int32),
                       pltpu.VMEM((2, SC_ROWS, ROW_SUB, ROW_LANE), xn.dtype),
                       pltpu.SemaphoreType.DMA((2,)), pltpu.SemaphoreType.DMA((2,))],
    )
    def scatter_kernel(x_hbm, idx_hbm, out_hbm, idx_v, rows_v, sem_load, sem_store):
        chunk0 = (lax.axis_index("s") * SC_CORES + lax.axis_index("c")) * n_it

        def load(g, slot):
            tok = pl.ds(pl.multiple_of((chunk0 + g) * SC_ROWS, SC_ROWS), SC_ROWS)
            return pltpu.make_async_copy(x_hbm.at[tok], rows_v.at[slot], sem_load.at[slot])

        def start_load(g, slot):
            load(g, slot).start()
            for kk in range(TOP_K):
                pltpu.sync_copy(idx_hbm.at[kk, chunk0 + g], idx_v.at[slot, kk])

        def store(slot, kk):
            return pltpu.make_async_copy(rows_v.at[slot], out_hbm.at[idx_v.at[slot, kk]], sem_store.at[slot])

        def scatter(g, slot):
            load(g, slot).wait()
            for kk in range(TOP_K):
                store(slot, kk).start()
            for kk in range(TOP_K):
                store(slot, kk).wait()

        start_load(0, 0)

        @pl.loop(0, n_it, step=2)
        def _(g):
            start_load(g + 1, 1)
            scatter(g, 0)

            @pl.when(g + 2 < n_it)
            def _():
                start_load(g + 2, 0)

            scatter(g + 1, 1)

    return scatter_kernel(xn, idx)


GU_SLAB = 256


def _gate_up_layout_kernel(w_ref, o_ref):
    half = GU_SLAB // 2
    r = lax.broadcasted_iota(jnp.int32, (GU_SLAB, GU_SLAB), 0)
    c = lax.broadcasted_iota(jnp.int32, (GU_SLAB, GU_SLAB), 1)
    src = jnp.where(c < half, 2 * c, 2 * (c - half) + 1)
    pm = jnp.where(r == src, 1.0, 0.0).astype(BF16)
    for s in range(w_ref.shape[2] // GU_SLAB):
        cols = slice(s * GU_SLAB, (s + 1) * GU_SLAB)
        o_ref[0, :, cols] = _dot(w_ref[0, :, cols].astype(BF16), pm).astype(BF16)


def _gate_up_layout(w_gate_up, b_gate_up):
    e, k, n2 = w_gate_up.shape
    half = GU_SLAB // 2
    tk = 512
    w = pl.pallas_call(
        _gate_up_layout_kernel,
        grid=(e, k // tk),
        in_specs=[pl.BlockSpec((1, tk, n2), lambda i, j: (i, j, 0))],
        out_specs=pl.BlockSpec((1, tk, n2), lambda i, j: (i, j, 0)),
        out_shape=jax.ShapeDtypeStruct((e, k, n2), BF16),
        compiler_params=_cparams(("arbitrary", "arbitrary")),
        name="gate_up_layout",
    )(w_gate_up)
    b = b_gate_up.reshape(e, n2 // GU_SLAB, half, 2).transpose(0, 1, 3, 2).reshape(e, 1, n2)
    return w, b


def _expert_kernel(be_ref, bv_ref, x_ref, wgu_ref, bgu_ref, wd_ref, bd_ref, o_ref, a_ref):
    i = pl.program_id(0)
    valid = bv_ref[i]
    half = GU_SLAB // 2

    @pl.when(valid > 0)
    def _():
        blk = x_ref.shape[0]
        rowi = lax.broadcasted_iota(jnp.int32, (blk, 1), 0)
        x = x_ref[...].reshape(blk, D_MODEL)
        x = jnp.where(rowi < valid, x, 0.0).astype(BF16)
        for s in range(2 * D_FF // GU_SLAB):
            cols = slice(s * GU_SLAB, (s + 1) * GU_SLAB)
            h = _dot(x, wgu_ref[0, :, cols]) + bgu_ref[0, :, cols]
            glu = jnp.minimum(h[:, :half], SWIGLU_LIMIT)
            lin = jnp.clip(h[:, half:], -SWIGLU_LIMIT, SWIGLU_LIMIT)
            a = glu * _sigmoid(SWIGLU_ALPHA * glu) * (lin + 1.0)
            a_ref[:, s * half:(s + 1) * half] = a.astype(BF16)
        o_ref[...] = (_dot(a_ref[...], wd_ref[0].astype(BF16)) + bd_ref[0]).reshape(o_ref.shape)

    @pl.when(valid <= 0)
    def _():
        o_ref[...] = jnp.zeros_like(o_ref)


def _experts(xs, block_e, block_valid, wgu, bgu, wd, bd):
    n_rows = xs.shape[0]
    d = D_MODEL
    n_blocks = block_e.shape[0]
    blk = n_rows // n_blocks
    emap = lambda i, be, bv: (be[i], 0, 0)
    rows = lambda: pl.BlockSpec((blk, ROW_SUB, ROW_LANE), lambda i, be, bv: (i, 0, 0))
    return pl.pallas_call(
        _expert_kernel,
        grid_spec=pltpu.PrefetchScalarGridSpec(
            num_scalar_prefetch=2,
            grid=(n_blocks,),
            in_specs=[rows(),
                      pl.BlockSpec((1, d, 2 * D_FF), emap), pl.BlockSpec((1, 1, 2 * D_FF), emap),
                      pl.BlockSpec((1, D_FF, d), emap), pl.BlockSpec((1, 1, d), emap)],
            out_specs=rows(),
            scratch_shapes=[pltpu.VMEM((blk, D_FF), BF16)],
        ),
        out_shape=jax.ShapeDtypeStruct((n_rows, ROW_SUB, ROW_LANE), F32),
        compiler_params=_cparams(("arbitrary",)),
        name="experts",
    )(block_e, block_valid, xs, wgu, bgu, wd, bd)


def _gather_rows(y, dest_flat):
    n_out = dest_flat.shape[0]
    workers = SC_CORES * SC_SUBCORES
    per_w = n_out // workers
    n_it = per_w // SC_ROWS
    assert per_w * workers == n_out and n_it * SC_ROWS == per_w and n_it % 2 == 0
    mesh = plsc.VectorSubcoreMesh(core_axis_name="c", subcore_axis_name="s")

    @functools.partial(
        pl.kernel, mesh=mesh,
        out_type=jax.ShapeDtypeStruct((n_out, ROW_SUB, ROW_LANE), y.dtype),
        scratch_types=[pltpu.VMEM((2, SC_ROWS), jnp.int32),
                       pltpu.VMEM((2, SC_ROWS, ROW_SUB, ROW_LANE), y.dtype),
                       pltpu.SemaphoreType.DMA((2,))],
    )
    def gather_kernel(y_hbm, idx_hbm, out_hbm, idx_v, rows_v, sem):
        base = (lax.axis_index("s") * SC_CORES + lax.axis_index("c")) * per_w

        def span(g):
            return pl.ds(pl.multiple_of(base + g * SC_ROWS, SC_ROWS), SC_ROWS)

        def row_gather(slot):
            return pltpu.make_async_copy(y_hbm.at[idx_v.at[slot]], rows_v.at[slot], sem.at[slot])

        def start(g, slot):
            pltpu.sync_copy(idx_hbm.at[span(g)], idx_v.at[slot])
            row_gather(slot).start()

        start(0, 0)

        @pl.loop(0, n_it, step=2)
        def _(g):
            start(g + 1, 1)
            row_gather(0).wait()
            pltpu.sync_copy(rows_v.at[0], out_hbm.at[span(g)])

            @pl.when(g + 2 < n_it)
            def _():
                start(g + 2, 0)

            row_gather(1).wait()
            pltpu.sync_copy(rows_v.at[1], out_hbm.at[span(g + 1)])

    return gather_kernel(y, dest_flat)


def _combine_kernel(yg_ref, x1_ref, mod_ref, rg_ref, gpo_ref, o_ref):
    rg = rg_ref[...]
    h = rg[:, 0:1] * yg_ref[0].reshape(COMB_TOK, D_MODEL)
    for kk in range(1, TOP_K):
        h = h + rg[:, kk:kk + 1] * yg_ref[kk].reshape(COMB_TOK, D_MODEL)
    gt2 = mod_ref[0, 5:6, :]
    o_ref[...] = x1_ref[...] + gt2 * _rms(h, gpo_ref[...])


def _combine(yg, x1, mod, rg, g_post_ffn, tokens_per_batch):
    n, d = x1.shape
    per_b = tokens_per_batch // COMB_TOK
    gpo = g_post_ffn.reshape(1, d)
    return pl.pallas_call(
        _combine_kernel,
        grid=(n // COMB_TOK,),
        in_specs=[pl.BlockSpec((TOP_K, COMB_TOK, ROW_SUB, ROW_LANE), lambda i: (0, i, 0, 0)),
                  pl.BlockSpec((COMB_TOK, d), lambda i: (i, 0)),
                  pl.BlockSpec((1, 6, d), lambda i: (i // per_b, 0, 0)),
                  pl.BlockSpec((COMB_TOK, 128), lambda i: (i, 0)),
                  pl.BlockSpec((1, d), lambda i: (0, 0))],
        out_specs=pl.BlockSpec((COMB_TOK, d), lambda i: (i, 0)),
        out_shape=jax.ShapeDtypeStruct((n, d), F32),
        compiler_params=_cparams(("arbitrary",)),
        name="combine",
    )(yg, x1, mod, rg, gpo)


def _moe_plan(ri_t, counts, n_tok):
    idx = ri_t[:TOP_K]
    rank = ri_t[TOP_K:2 * TOP_K]
    cnt = counts.astype(jnp.int32)
    blk = MOE_BLK_LARGE if n_tok * TOP_K >= 8 * MOE_BLK_LARGE * N_EXPERTS else MOE_BLK_SMALL
    padded = (cnt + blk - 1) // blk * blk
    pend = jnp.cumsum(padded)
    pstart = pend - padded
    dest = rank
    for e in range(N_EXPERTS):
        dest = dest + jnp.where(idx == e, pstart[e], 0)
    n_blocks = (n_tok * TOP_K + blk - 1) // blk + N_EXPERTS
    bstart = jnp.arange(n_blocks, dtype=jnp.int32) * blk
    block_e = jnp.minimum(jnp.sum((bstart[:, None] >= pend[None, :]).astype(jnp.int32), axis=1), N_EXPERTS - 1)
    mine = jnp.arange(N_EXPERTS, dtype=jnp.int32)[None, :] == block_e[:, None]
    row_end = jnp.sum(jnp.where(mine, (pstart + cnt)[None, :], 0), axis=1)
    block_valid = jnp.clip(row_end - bstart, 0, blk).astype(jnp.int32)
    return dest.astype(jnp.int32), block_e, block_valid, n_blocks * blk


def _token_mixing(x, mod, p):
    b, t, d = x.shape
    n = b * t
    proj = _in_proj(x, mod, p['g_pre_mix'], p['w_in'])
    o_fw, o_bw, o_b = _mixers(proj, p['lb_logits'], p['na_tabs'])
    x1, xn, ri, rg, counts = _mix(x, mod, proj, o_fw, o_bw, o_b, p['hgrn_norm'], p['g_post_mix'], p['g_pre_ffn'],
                                  p['w_branch_a'], p['w_branch_b'], p['w_out'], p['router_w'], p['router_b'])
    dest, block_e, block_valid, n_rows = _moe_plan(ri, counts[:, 0], n)
    xs = _scatter_rows(xn.reshape(n, ROW_SUB, ROW_LANE), dest, n_rows)
    return dict(x1=x1.reshape(n, d), rg=rg.reshape(n, 128), dest=dest, block_e=block_e, block_valid=block_valid,
                xs=xs, mod=mod, shape=(b, t, d))


def _expert_rows(s, p):
    return _experts(s['xs'], s['block_e'], s['block_valid'], p['w_gu'], p['b_gu'], p['w_d'], p['b_d'])


def _finish(s, y, p):
    b, t, d = s['shape']
    yg = _gather_rows(y, s['dest'].reshape(-1)).reshape(TOP_K, b * t, ROW_SUB, ROW_LANE)
    return _combine(yg, s['x1'], s['mod'], s['rg'], p['g_post_ffn'], t).reshape(b, t, d)


def _after(first, then):
    return lax.optimization_barrier((first, then))


def kernel(x_prompt, x_sample, c_prompt, c_sample, ada_w, ada_b, g_pre_mix, g_post_mix, g_pre_ffn, g_post_ffn,
           w_in, hgrn_lb_logits, hgrn_norm, na_rpb, w_branch_a, w_branch_b, w_out, router_w, router_b,
           w_gate_up, b_gate_up, w_down, b_down):
    d = D_MODEL
    bp, bs = c_prompt.shape[0], c_sample.shape[0]
    pad = (-(bp + bs)) % 8
    c_all = jnp.concatenate([c_prompt, c_sample, jnp.zeros((pad, d), F32)], axis=0)
    mod = _ada_mod(c_all, ada_w[0], ada_b[0]).reshape(-1, 6, d)

    w = w_in[0]
    w_perm = jnp.concatenate([w[:, :5 * d], w[:, 5 * d + 3 * NA_W:], w[:, 5 * d:5 * d + 3 * NA_W]], axis=1).astype(BF16)
    w_gu, b_gu = _gate_up_layout(w_gate_up[0], b_gate_up[0])
    p = dict(
        g_pre_mix=g_pre_mix[0], g_post_mix=g_post_mix[0], g_pre_ffn=g_pre_ffn[0], g_post_ffn=g_post_ffn[0],
        w_in=w_perm, lb_logits=hgrn_lb_logits.astype(F32), hgrn_norm=hgrn_norm[0],
        na_tabs=_na_bias_tables(na_rpb[0]),
        w_branch_a=w_branch_a[0].astype(BF16), w_branch_b=w_branch_b[0].astype(BF16), w_out=w_out[0].astype(BF16),
        router_w=router_w[0].astype(BF16), router_b=router_b[0],
        w_gu=w_gu, b_gu=b_gu,
        w_d=w_down[0], b_d=b_down[0][:, None, :],
    )
    sp = _token_mixing(x_prompt, mod[:bp], p)
    sp['dest'], x_sample = _after(sp['dest'], x_sample)
    ss = _token_mixing(x_sample, mod[bp:bp + bs], p)
    ss['x1'], sp['xs'] = _after(ss['x1'], sp['xs'])
    y_p = _expert_rows(sp, p)
    y_p, ss['xs'] = _after(y_p, ss['xs'])
    y_s = _expert_rows(ss, p)
    out_p = _finish(sp, y_p, p)
    out_s = _finish(ss, y_s, p)
    return (out_p, out_s)
```

```python
import functools

import numpy as np
import jax
import jax.numpy as jnp
from jax import lax
from jax.experimental import pallas as pl
from jax.experimental.pallas import tpu as pltpu
from jax.experimental.pallas import tpu_sc as plsc

F32 = jnp.float32
BF16 = jnp.bfloat16

D_MODEL = 1024
GRID_W = 64
HA_HEADS = 8
HA_DK = 128
NA_HEADS = 8
NA_DH = 64
NA_W = NA_HEADS * NA_DH
NA_KH = 8
NA_KW = 16
N_EXPERTS = 32
TOP_K = 4
D_FF = 1024
SWIGLU_ALPHA = 1.702
SWIGLU_LIMIT = 7.0
RMS_EPS = 1e-6
NEG_BIG = -1e30

COL_A = 0
COL_GATES = 5 * 1024
COL_NA = 7 * 1024
PROJ_COLS = 7 * 1024 + 3 * NA_W

HGRN_CHUNK = 64
HGRN_TILE = 512
PROJ_TM = 2048
PROJ_TN = 2176
MIX_TM = 512
MOE_BLK_SMALL, MOE_BLK_LARGE = 512, 1024
COMB_TOK = 512
SUBLANES = 8
ROW_SUB, ROW_LANE = 8, 128
VMEM_LIMIT = 56 * 1024 * 1024


def _cparams(sem):
    return pltpu.CompilerParams(dimension_semantics=sem, vmem_limit_bytes=VMEM_LIMIT)


def _dot(a, b):
    return jnp.dot(a, b, preferred_element_type=F32)


def _dot_nt(a, b):
    return lax.dot_general(a, b, (((1,), (1,)), ((), ())), preferred_element_type=F32)


def _dot_tn(a, b):
    return lax.dot_general(a, b, (((0,), (0,)), ((), ())), preferred_element_type=F32)


def _split(a):
    hi = a.astype(BF16)
    lo = (a - hi.astype(F32)).astype(BF16)
    return hi, lo


def _dot3(a, w):
    ah, al = _split(a)
    wh, wl = _split(w)
    return _dot(ah, wh) + (_dot(al, wh) + _dot(ah, wl))


def _sigmoid(x):
    return 0.5 * jnp.tanh(0.5 * x) + 0.5


def _rms(x, g):
    return x * lax.rsqrt(jnp.mean(x * x, axis=-1, keepdims=True) + RMS_EPS) * g


def _ada_kernel(c_ref, w_ref, b_ref, o_ref):
    c = c_ref[...]
    o_ref[...] = _dot3(c * _sigmoid(c), w_ref[...]) + b_ref[...]


def _ada_mod(c, ada_w, ada_b):
    bp, d = c.shape
    n = ada_w.shape[1]
    tn = 1536
    return pl.pallas_call(
        _ada_kernel,
        grid=(n // tn,),
        in_specs=[pl.BlockSpec((bp, d), lambda j: (0, 0)),
                  pl.BlockSpec((d, tn), lambda j: (0, j)),
                  pl.BlockSpec((1, tn), lambda j: (0, j))],
        out_specs=pl.BlockSpec((bp, tn), lambda j: (0, j)),
        out_shape=jax.ShapeDtypeStruct((bp, n), F32),
        compiler_params=_cparams(("arbitrary",)),
        name="ada_mod",
    )(c, ada_w, ada_b.reshape(1, n))


def _in_proj_kernel(x_ref, mod_ref, g_ref, w_ref, o_ref, xn_ref):
    @pl.when(pl.program_id(2) == 0)
    def _():
        sh = mod_ref[0, 0:1, :]
        sc = mod_ref[0, 1:2, :]
        xn = _rms(x_ref[0], g_ref[...]) * (1.0 + sc) + sh
        xn_ref[...] = xn.astype(BF16)

    o_ref[0] = _dot(xn_ref[...], w_ref[...]).astype(BF16)


def _in_proj(x, mod, g_pre, w_in_bf):
    b, t, d = x.shape
    tm = min(PROJ_TM, t)
    n = w_in_bf.shape[1]
    return pl.pallas_call(
        _in_proj_kernel,
        grid=(b, t // tm, n // PROJ_TN),
        in_specs=[pl.BlockSpec((1, tm, d), lambda bi, i, j: (bi, i, 0)),
                  pl.BlockSpec((1, 6, d), lambda bi, i, j: (bi, 0, 0)),
                  pl.BlockSpec((1, d), lambda bi, i, j: (0, 0)),
                  pl.BlockSpec((d, PROJ_TN), lambda bi, i, j: (0, j))],
        out_specs=pl.BlockSpec((1, tm, PROJ_TN), lambda bi, i, j: (bi, i, j)),
        out_shape=jax.ShapeDtypeStruct((b, t, n), BF16),
        scratch_shapes=[pltpu.VMEM((tm, d), BF16)],
        compiler_params=_cparams(("arbitrary", "arbitrary", "arbitrary")),
        name="in_proj",
    )(x, mod, g_pre.reshape(1, d), w_in_bf)


def _cumsum_rows(x, reverse):
    groups = x.shape[0] // SUBLANES
    row = lax.broadcasted_iota(jnp.int32, (SUBLANES, x.shape[1]), 0)
    out = [None] * groups
    carry = None
    for g in (range(groups - 1, -1, -1) if reverse else range(groups)):
        blk = x[g * SUBLANES:(g + 1) * SUBLANES]
        step = 1
        while step < SUBLANES:
            if reverse:
                blk = blk + jnp.where(row < SUBLANES - step, pltpu.roll(blk, SUBLANES - step, axis=0), 0.0)
            else:
                blk = blk + jnp.where(row >= step, pltpu.roll(blk, step, axis=0), 0.0)
            step *= 2
        if carry is not None:
            blk = blk + carry
        carry = blk[0:1] if reverse else blk[SUBLANES - 1:SUBLANES]
        out[g] = blk
    return jnp.concatenate(out, axis=0)


def _hgrn_dir(q_ref, f_ref, i_ref, o_ref, s_ref, sn_ref, lbd, row0, reverse):
    c = HGRN_CHUNK
    rows = pl.ds(row0, c)
    q = q_ref[0, rows, :].astype(F32)
    fpre = f_ref[0, rows, :].astype(F32)
    v = i_ref[0, rows, :]
    f = 0.5 * (1.0 + lbd) + (0.5 * (1.0 - lbd)) * jnp.tanh(0.5 * fpre)
    logf = jnp.log(f)
    k = 1.0 - f
    ri = lax.broadcasted_iota(jnp.int32, (c, c), 0)
    ci = lax.broadcasted_iota(jnp.int32, (c, c), 1)
    keep = (ci >= ri) if reverse else (ri >= ci)
    b = _cumsum_rows(logf, reverse)
    if reverse:
        b_mid = b[c // 2:c // 2 + 1, :]
        b_end = b[0:1, :]
    else:
        b_mid = b[c // 2 - 1:c // 2, :]
        b_end = b[c - 1:c, :]
    d = b - b_mid
    qm = q * jnp.exp(d)
    km = k * jnp.exp(-d)
    qi = (qm * jnp.exp(b_mid)).astype(BF16)
    ks = (km * jnp.exp(b_end - b_mid)).astype(BF16)
    qm = qm.astype(BF16)
    km = km.astype(BF16)
    dec = jnp.exp(b_end)
    for h in range(HA_HEADS):
        sl = slice(h * HA_DK, (h + 1) * HA_DK)
        a = _dot_nt(qm[:, sl], km[:, sl])
        a = jnp.where(keep, a, 0.0).astype(BF16)
        lhs = jnp.concatenate([qi[:, sl], a], axis=1)
        rhs = jnp.concatenate([sn_ref[h], v[:, sl]], axis=0)
        o_ref[0, rows, sl] = _dot(lhs, rhs).astype(BF16)
        st = s_ref[h] * dec[:, sl] + _dot_tn(v[:, sl], ks[:, sl])
        s_ref[h] = st
        sn_ref[h] = st.astype(BF16).T


def _hgrn_tile(lbl_ref, qf_ref, ff_ref, if_ref, qb_ref, fb_ref, ib_ref, of_ref, ob_ref,
               sf_ref, sb_ref, snf_ref, snb_ref):
    @pl.when(pl.program_id(1) == 0)
    def _():
        for ref in (sf_ref, sb_ref, snf_ref, snb_ref):
            ref[...] = jnp.zeros_like(ref)

    l0 = lbl_ref[0]
    l1 = lbl_ref[1]
    m = jnp.maximum(l0, l1)
    e0 = jnp.exp(l0 - m)
    e1 = jnp.exp(l1 - m)
    lb = e0 / (e0 + e1)
    n_chunks = HGRN_TILE // HGRN_CHUNK

    def chunk(ci):
        _hgrn_dir(qf_ref, ff_ref, if_ref, of_ref, sf_ref, snf_ref, lb[0:1, :], ci * HGRN_CHUNK, False)
        _hgrn_dir(qb_ref, fb_ref, ib_ref, ob_ref, sb_ref, snb_ref, lb[1:2, :],
                  (n_chunks - 1 - ci) * HGRN_CHUNK, True)

    return chunk


def _mixers_kernel(lbl_ref, qf_ref, ff_ref, if_ref, qb_ref, fb_ref, ib_ref, nq_ref, nk_ref, nv_ref, tab_ref,
                   of_ref, ob_ref, on_ref, sf_ref, sb_ref, snf_ref, snb_ref, *, grid_rows):
    chunk = _hgrn_tile(lbl_ref, qf_ref, ff_ref, if_ref, qb_ref, fb_ref, ib_ref, of_ref, ob_ref,
                       sf_ref, sb_ref, snf_ref, snb_ref)
    na_row = _na_rows(nq_ref, nk_ref, nv_ref, tab_ref, on_ref, grid_rows)
    for ci in range(HGRN_TILE // HGRN_CHUNK):
        chunk(ci)
        na_row(ci)


def _mixers(proj, lb_logits, tabs):
    b, t, _ = proj.shape
    tt = HGRN_TILE
    nt = t // tt
    d = D_MODEL
    cq, ck, cv = (COL_NA // NA_W, COL_NA // NA_W + 1, COL_NA // NA_W + 2)

    def fwd(col):
        return pl.BlockSpec((1, tt, d), lambda bi, ti: (bi, ti, col))

    def bwd(col):
        return pl.BlockSpec((1, tt, d), lambda bi, ti: (bi, nt - 1 - ti, col))

    return pl.pallas_call(
        functools.partial(_mixers_kernel, grid_rows=t // GRID_W),
        grid=(b, nt),
        in_specs=[pl.BlockSpec((2, 2, d), lambda bi, ti: (0, 0, 0)),
                  fwd(0), fwd(1), fwd(3), bwd(0), bwd(2), bwd(3),
                  pl.BlockSpec((1, tt, NA_W), lambda bi, ti: (bi, ti, cq)),
                  pl.BlockSpec((1, t, NA_W), lambda bi, ti: (bi, 0, ck), pipeline_mode=pl.Buffered(1)),
                  pl.BlockSpec((1, t, NA_W), lambda bi, ti: (bi, 0, cv), pipeline_mode=pl.Buffered(1)),
                  pl.BlockSpec(tabs.shape, lambda bi, ti: (0, 0, 0, 0), pipeline_mode=pl.Buffered(1))],
        out_specs=[pl.BlockSpec((1, tt, d), lambda bi, ti: (bi, ti, 0)),
                   pl.BlockSpec((1, tt, d), lambda bi, ti: (bi, nt - 1 - ti, 0)),
                   pl.BlockSpec((1, tt, NA_W), lambda bi, ti: (bi, ti, 0))],
        out_shape=[jax.ShapeDtypeStruct((b, t, d), BF16), jax.ShapeDtypeStruct((b, t, d), BF16),
                   jax.ShapeDtypeStruct((b, t, NA_W), BF16)],
        scratch_shapes=[pltpu.VMEM((HA_HEADS, HA_DK, HA_DK), F32), pltpu.VMEM((HA_HEADS, HA_DK, HA_DK), F32),
                        pltpu.VMEM((HA_HEADS, HA_DK, HA_DK), BF16), pltpu.VMEM((HA_HEADS, HA_DK, HA_DK), BF16)],
        compiler_params=_cparams(("arbitrary", "arbitrary")),
        name="mixers",
    )(lb_logits, proj, proj, proj, proj, proj, proj, proj, proj, proj, tabs)


def _na_bias_tables(rpb):
    q = np.arange(GRID_W)[:, None]
    kc = np.arange(GRID_W)[None, :]
    c0 = np.clip(q - NA_KW // 2, 0, GRID_W - NA_KW)
    ok = (kc >= c0) & (kc < c0 + NA_KW)
    dc = np.clip(kc - q + (NA_KW - 1), 0, 2 * NA_KW - 2)
    rp = rpb.astype(F32)
    base = jnp.full(rp.shape[:2] + ok.shape, NEG_BIG, F32)
    for j in range(rp.shape[2]):
        base = jnp.where((ok & (dc == j))[None, None], rp[:, :, j, None, None], base)
    tabs = []
    for dr0 in range(NA_KH):
        w = base[:, dr0:dr0 + NA_KH]
        w = w.transpose(0, 2, 1, 3).reshape(NA_HEADS // 2, 2 * GRID_W, NA_KH * GRID_W)
        tabs.append(w)
    return jnp.stack(tabs, axis=1)


def _na_rows(q_ref, k_ref, v_ref, tab_ref, o_ref, grid_rows):
    g = pl.program_id(1)
    lane = lax.broadcasted_iota(jnp.int32, (GRID_W, 2 * NA_DH), 1)
    low = lane < NA_DH
    scale = NA_DH ** -0.5
    rows_per_step = HGRN_TILE // GRID_W

    def na_row(rr):
        r = g * rows_per_step + rr
        r0 = jnp.clip(r - NA_KH // 2, 0, grid_rows - NA_KH)
        dr0 = r0 - r + (NA_KH - 1)
        qrow = pl.ds(rr * GRID_W, GRID_W)
        krow = pl.ds(pl.multiple_of(r0 * GRID_W, GRID_W), NA_KH * GRID_W)
        for j in range(NA_HEADS // 2):
            sl = slice(j * 2 * NA_DH, (j + 1) * 2 * NA_DH)
            qp = q_ref[0, qrow, sl]
            zero = jnp.zeros_like(qp)
            ql = jnp.concatenate([jnp.where(low, qp, zero), jnp.where(low, zero, qp)], axis=0)
            s = _dot_nt(ql, k_ref[0, krow, sl]) * scale + tab_ref[j, dr0]
            m = jnp.max(s, axis=-1, keepdims=True)
            p = jnp.exp(s - m)
            l = jnp.sum(p, axis=-1, keepdims=True)
            pv = _dot(p.astype(BF16), v_ref[0, krow, sl]) / l
            o_ref[0, qrow, sl] = jnp.where(low, pv[:GRID_W], pv[GRID_W:]).astype(BF16)

    return na_row


def _mix_kernel(x_ref, mod_ref, of_ref, ob_ref, g_ref, ga_ref, gb_ref, nb_ref,
                gn_ref, gpm_ref, gpf_ref, wa_ref, wb_ref, wo_ref, rw_ref, rb_ref,
                x1_ref, xn_ref, ri_ref, rg_ref, cnt_ref, tri_ref, carry_ref):
    tm = MIX_TM
    first = (pl.program_id(0) == 0) & (pl.program_id(1) == 0)

    @pl.when(first)
    def _():
        r = lax.broadcasted_iota(jnp.int32, (tm, tm), 0)
        c = lax.broadcasted_iota(jnp.int32, (tm, tm), 1)
        tri_ref[...] = jnp.where(r < c, 1.0, 0.0).astype(BF16)
        carry_ref[...] = jnp.zeros_like(carry_ref)

    sc2 = mod_ref[0, 4:5, :]
    sh2 = mod_ref[0, 3:4, :]
    gt1 = mod_ref[0, 2:3, :]

    o = of_ref[0].astype(F32) + ob_ref[0].astype(F32)
    parts = []
    for h in range(HA_HEADS):
        oh = o[:, h * HA_DK:(h + 1) * HA_DK]
        parts.append(oh * lax.rsqrt(jnp.mean(oh * oh, axis=-1, keepdims=True) + RMS_EPS))
    g = g_ref[0].astype(F32)
    oa = jnp.concatenate(parts, axis=-1) * gn_ref[...] * (g * _sigmoid(g))
    ya = _dot(oa.astype(BF16), wa_ref[...])
    yb = _dot(nb_ref[0], wb_ref[...])
    mix = _sigmoid(ga_ref[0].astype(F32)) * ya + _sigmoid(gb_ref[0].astype(F32)) * yb
    mo = _dot(mix.astype(BF16), wo_ref[...])
    x1 = x_ref[0] + gt1 * _rms(mo, gpm_ref[...])
    x1_ref[0] = x1

    xn = _rms(x1, gpf_ref[...]) * (1.0 + sc2) + sh2
    xn_ref[0] = xn.reshape(tm, ROW_SUB, ROW_LANE)
    logits = _dot_nt(rw_ref[...], xn.astype(BF16)) + rb_ref[...]
    sub = lax.broadcasted_iota(jnp.int32, (N_EXPERTS, tm), 0)
    work = logits
    vals, idxs, hots = [], [], []
    for _ in range(TOP_K):
        mv = jnp.max(work, axis=0, keepdims=True)
        mi = jnp.min(jnp.where(work == mv, sub, N_EXPERTS), axis=0, keepdims=True)
        hot = sub == mi
        vals.append(mv)
        idxs.append(mi)
        hots.append(hot)
        work = jnp.where(hot, -jnp.inf, work)
    es = [jnp.exp(v - vals[0]) for v in vals]
    den = es[0] + es[1] + es[2] + es[3]

    cnt = sum(jnp.where(h, 1.0, 0.0) for h in hots)
    before = _dot(cnt.astype(BF16), tri_ref[...]) + carry_ref[...]
    carry_ref[...] = carry_ref[...] + jnp.sum(cnt, axis=1, keepdims=True)
    cnt_ref[...] = jnp.broadcast_to(carry_ref[...], cnt_ref.shape)

    ranks = [jnp.sum(jnp.where(hots[kk], before, 0.0), axis=0, keepdims=True).astype(jnp.int32)
             for kk in range(TOP_K)]
    ri_ref[...] = jnp.concatenate(idxs + ranks, axis=0)
    gates = jnp.concatenate([e / den for e in es] + [jnp.zeros((128 - TOP_K, tm), F32)], axis=0)
    rg_ref[0] = gates.T


def _mix(x, mod, proj, o_fw, o_bw, o_b, hgrn_norm, g_post_mix, g_pre_ffn, wa, wb, wo, rw, rb):
    b, t, d = x.shape
    tm = min(MIX_TM, t)
    assert tm == MIX_TM
    row = lambda a: a.reshape(1, -1)
    tok = lambda w, col: pl.BlockSpec((1, tm, w), lambda bi, i: (bi, i, col))
    full = lambda a: pl.BlockSpec(a.shape, lambda bi, i: (0,) * a.ndim)
    gn, gpm, gpf, rbr = row(hgrn_norm), row(g_post_mix), row(g_pre_ffn), rb.reshape(-1, 1)
    rw = rw.T
    outs = pl.pallas_call(
        _mix_kernel,
        grid=(b, t // tm),
        in_specs=[tok(d, 0),
                  pl.BlockSpec((1, 6, d), lambda bi, i: (bi, 0, 0)),
                  tok(d, 0), tok(d, 0),
                  tok(d, 4),
                  tok(d, COL_GATES // d), tok(d, COL_GATES // d + 1),
                  tok(NA_W, 0),
                  full(gn), full(gpm), full(gpf), full(wa), full(wb), full(wo), full(rw), full(rbr)],
        out_specs=[tok(d, 0), pl.BlockSpec((1, tm, ROW_SUB, ROW_LANE), lambda bi, i: (bi, i, 0, 0)),
                   pl.BlockSpec((8, tm), lambda bi, i: (0, bi * (t // tm) + i)), tok(128, 0),
                   pl.BlockSpec((N_EXPERTS, 128), lambda bi, i: (0, 0))],
        out_shape=[jax.ShapeDtypeStruct((b, t, d), F32), jax.ShapeDtypeStruct((b, t, ROW_SUB, ROW_LANE), F32),
                   jax.ShapeDtypeStruct((8, b * t), jnp.int32), jax.ShapeDtypeStruct((b, t, 128), F32),
                   jax.ShapeDtypeStruct((N_EXPERTS, 128), F32)],
        scratch_shapes=[pltpu.VMEM((tm, tm), BF16), pltpu.VMEM((N_EXPERTS, 1), F32)],
        compiler_params=_cparams(("arbitrary", "arbitrary")),
        name="mix",
    )(x, mod, o_fw, o_bw, proj, proj, proj, o_b, gn, gpm, gpf, wa, wb, wo, rw, rbr)
    return outs


SC_CORES, SC_SUBCORES = 2, 16
SC_ROWS = 32


def _scatter_rows(xn, dest_t, n_rows):
    n_tok = xn.shape[0]
    workers = SC_CORES * SC_SUBCORES
    per_w = n_tok // workers
    n_it = per_w // SC_ROWS
    assert per_w * workers == n_tok and n_it * SC_ROWS == per_w and n_it % 2 == 0
    idx = dest_t.reshape(TOP_K, n_tok // SC_ROWS, SC_ROWS)
    mesh = plsc.VectorSubcoreMesh(core_axis_name="c", subcore_axis_name="s")

    @functools.partial(
        pl.kernel, mesh=mesh,
        out_type=jax.ShapeDtypeStruct((n_rows, ROW_SUB, ROW_LANE), xn.dtype),
        scratch_types=[pltpu.VMEM((2, TOP_K, SC_ROWS), jnp.int32),
                       pltpu.VMEM((2, SC_ROWS, ROW_SUB, ROW_LANE), xn.dtype),
                       pltpu.SemaphoreType.DMA((2,)), pltpu.SemaphoreType.DMA((2,))],
    )
    def scatter_kernel(x_hbm, idx_hbm, out_hbm, idx_v, rows_v, sem_load, sem_store):
        chunk0 = (lax.axis_index("s") * SC_CORES + lax.axis_index("c")) * n_it

        def load(g, slot):
            tok = pl.ds(pl.multiple_of((chunk0 + g) * SC_ROWS, SC_ROWS), SC_ROWS)
            return pltpu.make_async_copy(x_hbm.at[tok], rows_v.at[slot], sem_load.at[slot])

        def start_load(g, slot):
            load(g, slot).start()
            for kk in range(TOP_K):
                pltpu.sync_copy(idx_hbm.at[kk, chunk0 + g], idx_v.at[slot, kk])

        def store(slot, kk):
            return pltpu.make_async_copy(rows_v.at[slot], out_hbm.at[idx_v.at[slot, kk]], sem_store.at[slot])

        def scatter(g, slot):
            load(g, slot).wait()
            for kk in range(TOP_K):
                store(slot, kk).start()
            for kk in range(TOP_K):
                store(slot, kk).wait()

        start_load(0, 0)

        @pl.loop(0, n_it, step=2)
        def _(g):
            start_load(g + 1, 1)
            scatter(g, 0)

            @pl.when(g + 2 < n_it)
            def _():
                start_load(g + 2, 0)

            scatter(g + 1, 1)

    return scatter_kernel(xn, idx)


GU_SLAB = 256


def _gate_up_layout_kernel(w_ref, o_ref):
    half = GU_SLAB // 2
    r = lax.broadcasted_iota(jnp.int32, (GU_SLAB, GU_SLAB), 0)
    c = lax.broadcasted_iota(jnp.int32, (GU_SLAB, GU_SLAB), 1)
    src = jnp.where(c < half, 2 * c, 2 * (c - half) + 1)
    pm = jnp.where(r == src, 1.0, 0.0).astype(BF16)
    for s in range(w_ref.shape[2] // GU_SLAB):
        cols = slice(s * GU_SLAB, (s + 1) * GU_SLAB)
        o_ref[0, :, cols] = _dot(w_ref[0, :, cols].astype(BF16), pm).astype(BF16)


def _gate_up_layout(w_gate_up, b_gate_up):
    e, k, n2 = w_gate_up.shape
    half = GU_SLAB // 2
    tk = 512
    w = pl.pallas_call(
        _gate_up_layout_kernel,
        grid=(e, k // tk),
        in_specs=[pl.BlockSpec((1, tk, n2), lambda i, j: (i, j, 0))],
        out_specs=pl.BlockSpec((1, tk, n2), lambda i, j: (i, j, 0)),
        out_shape=jax.ShapeDtypeStruct((e, k, n2), BF16),
        compiler_params=_cparams(("arbitrary", "arbitrary")),
        name="gate_up_layout",
    )(w_gate_up)
    b = b_gate_up.reshape(e, n2 // GU_SLAB, half, 2).transpose(0, 1, 3, 2).reshape(e, 1, n2)
    return w, b


def _expert_kernel(be_ref, bv_ref, x_ref, wgu_ref, bgu_ref, wd_ref, bd_ref, o_ref, a_ref):
    i = pl.program_id(0)
    valid = bv_ref[i]
    half = GU_SLAB // 2

    @pl.when(valid > 0)
    def _():
        blk = x_ref.shape[0]
        rowi = lax.broadcasted_iota(jnp.int32, (blk, 1), 0)
        x = x_ref[...].reshape(blk, D_MODEL)
        x = jnp.where(rowi < valid, x, 0.0).astype(BF16)
        for s in range(2 * D_FF // GU_SLAB):
            cols = slice(s * GU_SLAB, (s + 1) * GU_SLAB)
            h = _dot(x, wgu_ref[0, :, cols]) + bgu_ref[0, :, cols]
            glu = jnp.minimum(h[:, :half], SWIGLU_LIMIT)
            lin = jnp.clip(h[:, half:], -SWIGLU_LIMIT, SWIGLU_LIMIT)
            a = glu * _sigmoid(SWIGLU_ALPHA * glu) * (lin + 1.0)
            a_ref[:, s * half:(s + 1) * half] = a.astype(BF16)
        o_ref[...] = (_dot(a_ref[...], wd_ref[0].astype(BF16)) + bd_ref[0]).reshape(o_ref.shape)

    @pl.when(valid <= 0)
    def _():
        o_ref[...] = jnp.zeros_like(o_ref)


def _experts(xs, block_e, block_valid, wgu, bgu, wd, bd):
    n_rows = xs.shape[0]
    d = D_MODEL
    n_blocks = block_e.shape[0]
    blk = n_rows // n_blocks
    emap = lambda i, be, bv: (be[i], 0, 0)
    rows = lambda: pl.BlockSpec((blk, ROW_SUB, ROW_LANE), lambda i, be, bv: (i, 0, 0))
    return pl.pallas_call(
        _expert_kernel,
        grid_spec=pltpu.PrefetchScalarGridSpec(
            num_scalar_prefetch=2,
            grid=(n_blocks,),
            in_specs=[rows(),
                      pl.BlockSpec((1, d, 2 * D_FF), emap), pl.BlockSpec((1, 1, 2 * D_FF), emap),
                      pl.BlockSpec((1, D_FF, d), emap), pl.BlockSpec((1, 1, d), emap)],
            out_specs=rows(),
            scratch_shapes=[pltpu.VMEM((blk, D_FF), BF16)],
        ),
        out_shape=jax.ShapeDtypeStruct((n_rows, ROW_SUB, ROW_LANE), F32),
        compiler_params=_cparams(("arbitrary",)),
        name="experts",
    )(block_e, block_valid, xs, wgu, bgu, wd, bd)


def _gather_rows(y, dest_flat):
    n_out = dest_flat.shape[0]
    workers = SC_CORES * SC_SUBCORES
    per_w = n_out // workers
    n_it = per_w // SC_ROWS
    assert per_w * workers == n_out and n_it * SC_ROWS == per_w and n_it % 2 == 0
    mesh = plsc.VectorSubcoreMesh(core_axis_name="c", subcore_axis_name="s")

    @functools.partial(
        pl.kernel, mesh=mesh,
        out_type=jax.ShapeDtypeStruct((n_out, ROW_SUB, ROW_LANE), y.dtype),
        scratch_types=[pltpu.VMEM((2, SC_ROWS), jnp.int32),
                       pltpu.VMEM((2, SC_ROWS, ROW_SUB, ROW_LANE), y.dtype),
                       pltpu.SemaphoreType.DMA((2,))],
    )
    def gather_kernel(y_hbm, idx_hbm, out_hbm, idx_v, rows_v, sem):
        base = (lax.axis_index("s") * SC_CORES + lax.axis_index("c")) * per_w

        def span(g):
            return pl.ds(pl.multiple_of(base + g * SC_ROWS, SC_ROWS), SC_ROWS)

        def row_gather(slot):
            return pltpu.make_async_copy(y_hbm.at[idx_v.at[slot]], rows_v.at[slot], sem.at[slot])

        def start(g, slot):
            pltpu.sync_copy(idx_hbm.at[span(g)], idx_v.at[slot])
            row_gather(slot).start()

        start(0, 0)

        @pl.loop(0, n_it, step=2)
        def _(g):
            start(g + 1, 1)
            row_gather(0).wait()
            pltpu.sync_copy(rows_v.at[0], out_hbm.at[span(g)])

            @pl.when(g + 2 < n_it)
            def _():
                start(g + 2, 0)

            row_gather(1).wait()
            pltpu.sync_copy(rows_v.at[1], out_hbm.at[span(g + 1)])

    return gather_kernel(y, dest_flat)


def _combine_kernel(yg_ref, x1_ref, mod_ref, rg_ref, gpo_ref, o_ref):
    rg = rg_ref[...]
    h = rg[:, 0:1] * yg_ref[0].reshape(COMB_TOK, D_MODEL)
    for kk in range(1, TOP_K):
        h = h + rg[:, kk:kk + 1] * yg_ref[kk].reshape(COMB_TOK, D_MODEL)
    gt2 = mod_ref[0, 5:6, :]
    o_ref[...] = x1_ref[...] + gt2 * _rms(h, gpo_ref[...])


def _combine(yg, x1, mod, rg, g_post_ffn, tokens_per_batch):
    n, d = x1.shape
    per_b = tokens_per_batch // COMB_TOK
    gpo = g_post_ffn.reshape(1, d)
    return pl.pallas_call(
        _combine_kernel,
        grid=(n // COMB_TOK,),
        in_specs=[pl.BlockSpec((TOP_K, COMB_TOK, ROW_SUB, ROW_LANE), lambda i: (0, i, 0, 0)),
                  pl.BlockSpec((COMB_TOK, d), lambda i: (i, 0)),
                  pl.BlockSpec((1, 6, d), lambda i: (i // per_b, 0, 0)),
                  pl.BlockSpec((COMB_TOK, 128), lambda i: (i, 0)),
                  pl.BlockSpec((1, d), lambda i: (0, 0))],
        out_specs=pl.BlockSpec((COMB_TOK, d), lambda i: (i, 0)),
        out_shape=jax.ShapeDtypeStruct((n, d), F32),
        compiler_params=_cparams(("arbitrary",)),
        name="combine",
    )(yg, x1, mod, rg, gpo)


def _moe_plan(ri_t, counts, n_tok):
    idx = ri_t[:TOP_K]
    rank = ri_t[TOP_K:2 * TOP_K]
    cnt = counts.astype(jnp.int32)
    blk = MOE_BLK_LARGE if n_tok * TOP_K >= 8 * MOE_BLK_LARGE * N_EXPERTS else MOE_BLK_SMALL
    padded = (cnt + blk - 1) // blk * blk
    pend = jnp.cumsum(padded)
    pstart = pend - padded
    dest = rank
    for e in range(N_EXPERTS):
        dest = dest + jnp.where(idx == e, pstart[e], 0)
    n_blocks = (n_tok * TOP_K + blk - 1) // blk + N_EXPERTS
    bstart = jnp.arange(n_blocks, dtype=jnp.int32) * blk
    block_e = jnp.minimum(jnp.sum((bstart[:, None] >= pend[None, :]).astype(jnp.int32), axis=1), N_EXPERTS - 1)
    mine = jnp.arange(N_EXPERTS, dtype=jnp.int32)[None, :] == block_e[:, None]
    row_end = jnp.sum(jnp.where(mine, (pstart + cnt)[None, :], 0), axis=1)
    block_valid = jnp.clip(row_end - bstart, 0, blk).astype(jnp.int32)
    return dest.astype(jnp.int32), block_e, block_valid, n_blocks * blk


def _token_mixing(x, mod, p):
    b, t, d = x.shape
    n = b * t
    proj = _in_proj(x, mod, p['g_pre_mix'], p['w_in'])
    o_fw, o_bw, o_b = _mixers(proj, p['lb_logits'], p['na_tabs'])
    x1, xn, ri, rg, counts = _mix(x, mod, proj, o_fw, o_bw, o_b, p['hgrn_norm'], p['g_post_mix'], p['g_pre_ffn'],
                                  p['w_branch_a'], p['w_branch_b'], p['w_out'], p['router_w'], p['router_b'])
    dest, block_e, block_valid, n_rows = _moe_plan(ri, counts[:, 0], n)
    xs = _scatter_rows(xn.reshape(n, ROW_SUB, ROW_LANE), dest, n_rows)
    return dict(x1=x1.reshape(n, d), rg=rg.reshape(n, 128), dest=dest, block_e=block_e, block_valid=block_valid,
                xs=xs, mod=mod, shape=(b, t, d))


def _expert_rows(s, p):
    return _experts(s['xs'], s['block_e'], s['block_valid'], p['w_gu'], p['b_gu'], p['w_d'], p['b_d'])


def _finish(s, y, p):
    b, t, d = s['shape']
    yg = _gather_rows(y, s['dest'].reshape(-1)).reshape(TOP_K, b * t, ROW_SUB, ROW_LANE)
    return _combine(yg, s['x1'], s['mod'], s['rg'], p['g_post_ffn'], t).reshape(b, t, d)


def _after(first, then):
    return lax.optimization_barrier((first, then))


def kernel(x_prompt, x_sample, c_prompt, c_sample, ada_w, ada_b, g_pre_mix, g_post_mix, g_pre_ffn, g_post_ffn,
           w_in, hgrn_lb_logits, hgrn_norm, na_rpb, w_branch_a, w_branch_b, w_out, router_w, router_b,
           w_gate_up, b_gate_up, w_down, b_down):
    d = D_MODEL
    bp, bs = c_prompt.shape[0], c_sample.shape[0]
    assert ada_w.shape[0] == 1 and hgrn_lb_logits.shape[0] == 2, "one layer: two lower-bound slots"
    for xg in (x_prompt, x_sample):
        assert xg.shape[2] == d and xg.shape[1] % HGRN_TILE == 0 and xg.shape[1] // GRID_W >= NA_KH
    pad = (-(bp + bs)) % 8
    c_all = jnp.concatenate([c_prompt, c_sample, jnp.zeros((pad, d), F32)], axis=0)
    mod = _ada_mod(c_all, ada_w[0], ada_b[0]).reshape(-1, 6, d)

    w = w_in[0]
    w_perm = jnp.concatenate([w[:, :5 * d], w[:, 5 * d + 3 * NA_W:], w[:, 5 * d:5 * d + 3 * NA_W]], axis=1).astype(BF16)
    w_gu, b_gu = _gate_up_layout(w_gate_up[0], b_gate_up[0])
    p = dict(
        g_pre_mix=g_pre_mix[0], g_post_mix=g_post_mix[0], g_pre_ffn=g_pre_ffn[0], g_post_ffn=g_post_ffn[0],
        w_in=w_perm, lb_logits=hgrn_lb_logits.astype(F32), hgrn_norm=hgrn_norm[0],
        na_tabs=_na_bias_tables(na_rpb[0]),
        w_branch_a=w_branch_a[0].astype(BF16), w_branch_b=w_branch_b[0].astype(BF16), w_out=w_out[0].astype(BF16),
        router_w=router_w[0].astype(BF16), router_b=router_b[0],
        w_gu=w_gu, b_gu=b_gu,
        w_d=w_down[0], b_d=b_down[0][:, None, :],
    )
    sp = _token_mixing(x_prompt, mod[:bp], p)
    sp['dest'], x_sample = _after(sp['dest'], x_sample)
    ss = _token_mixing(x_sample, mod[bp:bp + bs], p)
    ss['x1'], sp['xs'] = _after(ss['x1'], sp['xs'])
    y_p = _expert_rows(sp, p)
    y_p, ss['xs'] = _after(y_p, ss['xs'])
    y_s = _expert_rows(ss, p)
    out_p = _finish(sp, y_p, p)
    out_s = _finish(ss, y_s, p)
    return (out_p, out_s)
```

```python
import functools

import numpy as np
import jax
import jax.numpy as jnp
from jax import lax
from jax.experimental import pallas as pl
from jax.experimental.pallas import tpu as pltpu
from jax.experimental.pallas import tpu_sc as plsc

F32 = jnp.float32
BF16 = jnp.bfloat16

D_MODEL = 1024
GRID_W = 64
HA_HEADS = 8
HA_DK = 128
NA_HEADS = 8
NA_DH = 64
NA_W = NA_HEADS * NA_DH
NA_KH = 8
NA_KW = 16
N_EXPERTS = 32
TOP_K = 4
D_FF = 1024
SWIGLU_ALPHA = 1.702
SWIGLU_LIMIT = 7.0
RMS_EPS = 1e-6
NEG_BIG = -1e30

COL_A = 0
COL_GATES = 5 * 1024
COL_NA = 7 * 1024
PROJ_COLS = 7 * 1024 + 3 * NA_W

HGRN_CHUNK = 64
HGRN_TILE = 512
PROJ_TM = 2048
PROJ_TN = 2176
MIX_TM = 512
MOE_BLK_SMALL, MOE_BLK_LARGE = 512, 1024
COMB_TOK = 512
SUBLANES = 8
ROW_SUB, ROW_LANE = 8, 128
VMEM_LIMIT = 56 * 1024 * 1024


def _cparams(sem):
    return pltpu.CompilerParams(dimension_semantics=sem, vmem_limit_bytes=VMEM_LIMIT)


def _dot(a, b):
    return jnp.dot(a, b, preferred_element_type=F32)


def _dot_nt(a, b):
    return lax.dot_general(a, b, (((1,), (1,)), ((), ())), preferred_element_type=F32)


def _dot_tn(a, b):
    return lax.dot_general(a, b, (((0,), (0,)), ((), ())), preferred_element_type=F32)


def _split(a):
    hi = a.astype(BF16)
    lo = (a - hi.astype(F32)).astype(BF16)
    return hi, lo


def _dot3(a, w):
    ah, al = _split(a)
    wh, wl = _split(w)
    return _dot(ah, wh) + (_dot(al, wh) + _dot(ah, wl))


def _sigmoid(x):
    return 0.5 * jnp.tanh(0.5 * x) + 0.5


def _rms(x, g):
    return x * lax.rsqrt(jnp.mean(x * x, axis=-1, keepdims=True) + RMS_EPS) * g


def _ada_kernel(c_ref, w_ref, b_ref, o_ref):
    c = c_ref[...]
    o_ref[...] = _dot3(c * _sigmoid(c), w_ref[...]) + b_ref[...]


def _ada_mod(c, ada_w, ada_b):
    bp, d = c.shape
    n = ada_w.shape[1]
    tn = 1536
    return pl.pallas_call(
        _ada_kernel,
        grid=(n // tn,),
        in_specs=[pl.BlockSpec((bp, d), lambda j: (0, 0)),
                  pl.BlockSpec((d, tn), lambda j: (0, j)),
                  pl.BlockSpec((1, tn), lambda j: (0, j))],
        out_specs=pl.BlockSpec((bp, tn), lambda j: (0, j)),
        out_shape=jax.ShapeDtypeStruct((bp, n), F32),
        compiler_params=_cparams(("arbitrary",)),
        name="ada_mod",
    )(c, ada_w, ada_b.reshape(1, n))


def _in_proj_kernel(x_ref, mod_ref, g_ref, w_ref, o_ref, xn_ref):
    @pl.when(pl.program_id(2) == 0)
    def _():
        sh = mod_ref[0, 0:1, :]
        sc = mod_ref[0, 1:2, :]
        xn = _rms(x_ref[0], g_ref[...]) * (1.0 + sc) + sh
        xn_ref[...] = xn.astype(BF16)

    o_ref[0] = _dot(xn_ref[...], w_ref[...]).astype(BF16)


def _in_proj(x, mod, g_pre, w_in_bf):
    b, t, d = x.shape
    tm = min(PROJ_TM, t)
    n = w_in_bf.shape[1]
    return pl.pallas_call(
        _in_proj_kernel,
        grid=(b, t // tm, n // PROJ_TN),
        in_specs=[pl.BlockSpec((1, tm, d), lambda bi, i, j: (bi, i, 0)),
                  pl.BlockSpec((1, 6, d), lambda bi, i, j: (bi, 0, 0)),
                  pl.BlockSpec((1, d), lambda bi, i, j: (0, 0)),
                  pl.BlockSpec((d, PROJ_TN), lambda bi, i, j: (0, j))],
        out_specs=pl.BlockSpec((1, tm, PROJ_TN), lambda bi, i, j: (bi, i, j)),
        out_shape=jax.ShapeDtypeStruct((b, t, n), BF16),
        scratch_shapes=[pltpu.VMEM((tm, d), BF16)],
        compiler_params=_cparams(("arbitrary", "arbitrary", "arbitrary")),
        name="in_proj",
    )(x, mod, g_pre.reshape(1, d), w_in_bf)


def _cumsum_rows(x, reverse):
    groups = x.shape[0] // SUBLANES
    row = lax.broadcasted_iota(jnp.int32, (SUBLANES, x.shape[1]), 0)
    out = [None] * groups
    carry = None
    for g in (range(groups - 1, -1, -1) if reverse else range(groups)):
        blk = x[g * SUBLANES:(g + 1) * SUBLANES]
        step = 1
        while step < SUBLANES:
            if reverse:
                blk = blk + jnp.where(row < SUBLANES - step, pltpu.roll(blk, SUBLANES - step, axis=0), 0.0)
            else:
                blk = blk + jnp.where(row >= step, pltpu.roll(blk, step, axis=0), 0.0)
            step *= 2
        if carry is not None:
            blk = blk + carry
        carry = blk[0:1] if reverse else blk[SUBLANES - 1:SUBLANES]
        out[g] = blk
    return jnp.concatenate(out, axis=0)


def _hgrn_dir(q_ref, f_ref, i_ref, o_ref, s_ref, sn_ref, lbd, row0, reverse):
    c = HGRN_CHUNK
    rows = pl.ds(row0, c)
    q = q_ref[0, rows, :].astype(F32)
    fpre = f_ref[0, rows, :].astype(F32)
    v = i_ref[0, rows, :]
    f = 0.5 * (1.0 + lbd) + (0.5 * (1.0 - lbd)) * jnp.tanh(0.5 * fpre)
    logf = jnp.log(f)
    k = 1.0 - f
    ri = lax.broadcasted_iota(jnp.int32, (c, c), 0)
    ci = lax.broadcasted_iota(jnp.int32, (c, c), 1)
    keep = (ci >= ri) if reverse else (ri >= ci)
    b = _cumsum_rows(logf, reverse)
    if reverse:
        b_mid = b[c // 2:c // 2 + 1, :]
        b_end = b[0:1, :]
    else:
        b_mid = b[c // 2 - 1:c // 2, :]
        b_end = b[c - 1:c, :]
    d = b - b_mid
    qm = q * jnp.exp(d)
    km = k * jnp.exp(-d)
    qi = (qm * jnp.exp(b_mid)).astype(BF16)
    ks = (km * jnp.exp(b_end - b_mid)).astype(BF16)
    qm = qm.astype(BF16)
    km = km.astype(BF16)
    dec = jnp.exp(b_end)
    for h in range(HA_HEADS):
        sl = slice(h * HA_DK, (h + 1) * HA_DK)
        a = _dot_nt(qm[:, sl], km[:, sl])
        a = jnp.where(keep, a, 0.0).astype(BF16)
        lhs = jnp.concatenate([qi[:, sl], a], axis=1)
        rhs = jnp.concatenate([sn_ref[h], v[:, sl]], axis=0)
        o_ref[0, rows, sl] = _dot(lhs, rhs).astype(BF16)
        st = s_ref[h] * dec[:, sl] + _dot_tn(v[:, sl], ks[:, sl])
        s_ref[h] = st
        sn_ref[h] = st.astype(BF16).T


def _hgrn_tile(lbl_ref, qf_ref, ff_ref, if_ref, qb_ref, fb_ref, ib_ref, of_ref, ob_ref,
               sf_ref, sb_ref, snf_ref, snb_ref):
    @pl.when(pl.program_id(1) == 0)
    def _():
        for ref in (sf_ref, sb_ref, snf_ref, snb_ref):
            ref[...] = jnp.zeros_like(ref)

    l0 = lbl_ref[0]
    l1 = lbl_ref[1]
    m = jnp.maximum(l0, l1)
    e0 = jnp.exp(l0 - m)
    e1 = jnp.exp(l1 - m)
    lb = e0 / (e0 + e1)
    n_chunks = HGRN_TILE // HGRN_CHUNK

    def chunk(ci):
        _hgrn_dir(qf_ref, ff_ref, if_ref, of_ref, sf_ref, snf_ref, lb[0:1, :], ci * HGRN_CHUNK, False)
        _hgrn_dir(qb_ref, fb_ref, ib_ref, ob_ref, sb_ref, snb_ref, lb[1:2, :],
                  (n_chunks - 1 - ci) * HGRN_CHUNK, True)

    return chunk


def _mixers_kernel(lbl_ref, qf_ref, ff_ref, if_ref, qb_ref, fb_ref, ib_ref, nq_ref, nk_ref, nv_ref, tab_ref,
                   of_ref, ob_ref, on_ref, sf_ref, sb_ref, snf_ref, snb_ref, *, grid_rows):
    chunk = _hgrn_tile(lbl_ref, qf_ref, ff_ref, if_ref, qb_ref, fb_ref, ib_ref, of_ref, ob_ref,
                       sf_ref, sb_ref, snf_ref, snb_ref)
    na_row = _na_rows(nq_ref, nk_ref, nv_ref, tab_ref, on_ref, grid_rows)
    for ci in range(HGRN_TILE // HGRN_CHUNK):
        chunk(ci)
        na_row(ci)


def _mixers(proj, lb_logits, tabs):
    b, t, _ = proj.shape
    tt = HGRN_TILE
    nt = t // tt
    d = D_MODEL
    cq, ck, cv = (COL_NA // NA_W, COL_NA // NA_W + 1, COL_NA // NA_W + 2)
    seq_bufs = 2 if 4 * t * NA_W * 2 <= VMEM_LIMIT // 3 else 1

    def fwd(col):
        return pl.BlockSpec((1, tt, d), lambda bi, ti: (bi, ti, col))

    def bwd(col):
        return pl.BlockSpec((1, tt, d), lambda bi, ti: (bi, nt - 1 - ti, col))

    return pl.pallas_call(
        functools.partial(_mixers_kernel, grid_rows=t // GRID_W),
        grid=(b, nt),
        in_specs=[pl.BlockSpec((2, 2, d), lambda bi, ti: (0, 0, 0)),
                  fwd(0), fwd(1), fwd(3), bwd(0), bwd(2), bwd(3),
                  pl.BlockSpec((1, tt, NA_W), lambda bi, ti: (bi, ti, cq)),
                  pl.BlockSpec((1, t, NA_W), lambda bi, ti: (bi, 0, ck), pipeline_mode=pl.Buffered(seq_bufs)),
                  pl.BlockSpec((1, t, NA_W), lambda bi, ti: (bi, 0, cv), pipeline_mode=pl.Buffered(seq_bufs)),
                  pl.BlockSpec(tabs.shape, lambda bi, ti: (0, 0, 0, 0), pipeline_mode=pl.Buffered(1))],
        out_specs=[pl.BlockSpec((1, tt, d), lambda bi, ti: (bi, ti, 0)),
                   pl.BlockSpec((1, tt, d), lambda bi, ti: (bi, nt - 1 - ti, 0)),
                   pl.BlockSpec((1, tt, NA_W), lambda bi, ti: (bi, ti, 0))],
        out_shape=[jax.ShapeDtypeStruct((b, t, d), BF16), jax.ShapeDtypeStruct((b, t, d), BF16),
                   jax.ShapeDtypeStruct((b, t, NA_W), BF16)],
        scratch_shapes=[pltpu.VMEM((HA_HEADS, HA_DK, HA_DK), F32), pltpu.VMEM((HA_HEADS, HA_DK, HA_DK), F32),
                        pltpu.VMEM((HA_HEADS, HA_DK, HA_DK), BF16), pltpu.VMEM((HA_HEADS, HA_DK, HA_DK), BF16)],
        compiler_params=_cparams(("arbitrary", "arbitrary")),
        name="mixers",
    )(lb_logits, proj, proj, proj, proj, proj, proj, proj, proj, proj, tabs)


def _na_bias_tables(rpb):
    q = np.arange(GRID_W)[:, None]
    kc = np.arange(GRID_W)[None, :]
    c0 = np.clip(q - NA_KW // 2, 0, GRID_W - NA_KW)
    ok = (kc >= c0) & (kc < c0 + NA_KW)
    dc = np.clip(kc - q + (NA_KW - 1), 0, 2 * NA_KW - 2)
    rp = rpb.astype(F32)
    base = jnp.full(rp.shape[:2] + ok.shape, NEG_BIG, F32)
    for j in range(rp.shape[2]):
        base = jnp.where((ok & (dc == j))[None, None], rp[:, :, j, None, None], base)
    tabs = []
    for dr0 in range(NA_KH):
        w = base[:, dr0:dr0 + NA_KH]
        w = w.transpose(0, 2, 1, 3).reshape(NA_HEADS // 2, 2 * GRID_W, NA_KH * GRID_W)
        tabs.append(w)
    return jnp.stack(tabs, axis=1)


def _na_rows(q_ref, k_ref, v_ref, tab_ref, o_ref, grid_rows):
    g = pl.program_id(1)
    lane = lax.broadcasted_iota(jnp.int32, (GRID_W, 2 * NA_DH), 1)
    low = lane < NA_DH
    scale = NA_DH ** -0.5
    rows_per_step = HGRN_TILE // GRID_W

    def na_row(rr):
        r = g * rows_per_step + rr
        r0 = jnp.clip(r - NA_KH // 2, 0, grid_rows - NA_KH)
        dr0 = r0 - r + (NA_KH - 1)
        qrow = pl.ds(rr * GRID_W, GRID_W)
        krow = pl.ds(pl.multiple_of(r0 * GRID_W, GRID_W), NA_KH * GRID_W)
        for j in range(NA_HEADS // 2):
            sl = slice(j * 2 * NA_DH, (j + 1) * 2 * NA_DH)
            qp = q_ref[0, qrow, sl]
            zero = jnp.zeros_like(qp)
            ql = jnp.concatenate([jnp.where(low, qp, zero), jnp.where(low, zero, qp)], axis=0)
            s = _dot_nt(ql, k_ref[0, krow, sl]) * scale + tab_ref[j, dr0]
            m = jnp.max(s, axis=-1, keepdims=True)
            p = jnp.exp(s - m)
            l = jnp.sum(p, axis=-1, keepdims=True)
            pv = _dot(p.astype(BF16), v_ref[0, krow, sl]) / l
            o_ref[0, qrow, sl] = jnp.where(low, pv[:GRID_W], pv[GRID_W:]).astype(BF16)

    return na_row


def _mix_kernel(x_ref, mod_ref, of_ref, ob_ref, g_ref, ga_ref, gb_ref, nb_ref,
                gn_ref, gpm_ref, gpf_ref, wa_ref, wb_ref, wo_ref, rw_ref, rb_ref,
                x1_ref, xn_ref, ri_ref, rg_ref, cnt_ref, tri_ref, carry_ref):
    tm = MIX_TM
    first = (pl.program_id(0) == 0) & (pl.program_id(1) == 0)

    @pl.when(first)
    def _():
        r = lax.broadcasted_iota(jnp.int32, (tm, tm), 0)
        c = lax.broadcasted_iota(jnp.int32, (tm, tm), 1)
        tri_ref[...] = jnp.where(r < c, 1.0, 0.0).astype(BF16)
        carry_ref[...] = jnp.zeros_like(carry_ref)

    sc2 = mod_ref[0, 4:5, :]
    sh2 = mod_ref[0, 3:4, :]
    gt1 = mod_ref[0, 2:3, :]

    o = of_ref[0].astype(F32) + ob_ref[0].astype(F32)
    parts = []
    for h in range(HA_HEADS):
        oh = o[:, h * HA_DK:(h + 1) * HA_DK]
        parts.append(oh * lax.rsqrt(jnp.mean(oh * oh, axis=-1, keepdims=True) + RMS_EPS))
    g = g_ref[0].astype(F32)
    oa = jnp.concatenate(parts, axis=-1) * gn_ref[...] * (g * _sigmoid(g))
    ya = _dot(oa.astype(BF16), wa_ref[...])
    yb = _dot(nb_ref[0], wb_ref[...])
    mix = _sigmoid(ga_ref[0].astype(F32)) * ya + _sigmoid(gb_ref[0].astype(F32)) * yb
    mo = _dot(mix.astype(BF16), wo_ref[...])
    x1 = x_ref[0] + gt1 * _rms(mo, gpm_ref[...])
    x1_ref[0] = x1

    xn = _rms(x1, gpf_ref[...]) * (1.0 + sc2) + sh2
    xn_ref[0] = xn.reshape(tm, ROW_SUB, ROW_LANE)
    logits = _dot_nt(rw_ref[...], xn.astype(BF16)) + rb_ref[...]
    sub = lax.broadcasted_iota(jnp.int32, (N_EXPERTS, tm), 0)
    work = logits
    vals, idxs, hots = [], [], []
    for _ in range(TOP_K):
        mv = jnp.max(work, axis=0, keepdims=True)
        mi = jnp.min(jnp.where(work == mv, sub, N_EXPERTS), axis=0, keepdims=True)
        hot = sub == mi
        vals.append(mv)
        idxs.append(mi)
        hots.append(hot)
        work = jnp.where(hot, -jnp.inf, work)
    es = [jnp.exp(v - vals[0]) for v in vals]
    den = es[0] + es[1] + es[2] + es[3]

    cnt = sum(jnp.where(h, 1.0, 0.0) for h in hots)
    before = _dot(cnt.astype(BF16), tri_ref[...]) + carry_ref[...]
    carry_ref[...] = carry_ref[...] + jnp.sum(cnt, axis=1, keepdims=True)
    cnt_ref[...] = jnp.broadcast_to(carry_ref[...], cnt_ref.shape)

    ranks = [jnp.sum(jnp.where(hots[kk], before, 0.0), axis=0, keepdims=True).astype(jnp.int32)
             for kk in range(TOP_K)]
    ri_ref[...] = jnp.concatenate(idxs + ranks, axis=0)
    gates = jnp.concatenate([e / den for e in es] + [jnp.zeros((128 - TOP_K, tm), F32)], axis=0)
    rg_ref[0] = gates.T


def _mix(x, mod, proj, o_fw, o_bw, o_b, hgrn_norm, g_post_mix, g_pre_ffn, wa, wb, wo, rw, rb):
    b, t, d = x.shape
    tm = min(MIX_TM, t)
    assert tm == MIX_TM
    row = lambda a: a.reshape(1, -1)
    tok = lambda w, col: pl.BlockSpec((1, tm, w), lambda bi, i: (bi, i, col))
    full = lambda a: pl.BlockSpec(a.shape, lambda bi, i: (0,) * a.ndim)
    gn, gpm, gpf, rbr = row(hgrn_norm), row(g_post_mix), row(g_pre_ffn), rb.reshape(-1, 1)
    rw = rw.T
    outs = pl.pallas_call(
        _mix_kernel,
        grid=(b, t // tm),
        in_specs=[tok(d, 0),
                  pl.BlockSpec((1, 6, d), lambda bi, i: (bi, 0, 0)),
                  tok(d, 0), tok(d, 0),
                  tok(d, 4),
                  tok(d, COL_GATES // d), tok(d, COL_GATES // d + 1),
                  tok(NA_W, 0),
                  full(gn), full(gpm), full(gpf), full(wa), full(wb), full(wo), full(rw), full(rbr)],
        out_specs=[tok(d, 0), pl.BlockSpec((1, tm, ROW_SUB, ROW_LANE), lambda bi, i: (bi, i, 0, 0)),
                   pl.BlockSpec((8, tm), lambda bi, i: (0, bi * (t // tm) + i)), tok(128, 0),
                   pl.BlockSpec((N_EXPERTS, 128), lambda bi, i: (0, 0))],
        out_shape=[jax.ShapeDtypeStruct((b, t, d), F32), jax.ShapeDtypeStruct((b, t, ROW_SUB, ROW_LANE), F32),
                   jax.ShapeDtypeStruct((8, b * t), jnp.int32), jax.ShapeDtypeStruct((b, t, 128), F32),
                   jax.ShapeDtypeStruct((N_EXPERTS, 128), F32)],
        scratch_shapes=[pltpu.VMEM((tm, tm), BF16), pltpu.VMEM((N_EXPERTS, 1), F32)],
        compiler_params=_cparams(("arbitrary", "arbitrary")),
        name="mix",
    )(x, mod, o_fw, o_bw, proj, proj, proj, o_b, gn, gpm, gpf, wa, wb, wo, rw, rbr)
    return outs


SC_CORES, SC_SUBCORES = 2, 16
SC_ROWS = 32


def _scatter_rows(xn, dest_t, n_rows):
    n_tok = xn.shape[0]
    workers = SC_CORES * SC_SUBCORES
    per_w = n_tok // workers
    n_it = per_w // SC_ROWS
    assert per_w * workers == n_tok and n_it * SC_ROWS == per_w and n_it % 2 == 0
    idx = dest_t.reshape(TOP_K, n_tok // SC_ROWS, SC_ROWS)
    mesh = plsc.VectorSubcoreMesh(core_axis_name="c", subcore_axis_name="s")

    @functools.partial(
        pl.kernel, mesh=mesh,
        out_type=jax.ShapeDtypeStruct((n_rows, ROW_SUB, ROW_LANE), xn.dtype),
        scratch_types=[pltpu.VMEM((2, TOP_K, SC_ROWS), jnp.int32),
                       pltpu.VMEM((2, SC_ROWS, ROW_SUB, ROW_LANE), xn.dtype),
                       pltpu.SemaphoreType.DMA((2,)), pltpu.SemaphoreType.DMA((2,))],
    )
    def scatter_kernel(x_hbm, idx_hbm, out_hbm, idx_v, rows_v, sem_load, sem_store):
        chunk0 = (lax.axis_index("s") * SC_CORES + lax.axis_index("c")) * n_it

        def load(g, slot):
            tok = pl.ds(pl.multiple_of((chunk0 + g) * SC_ROWS, SC_ROWS), SC_ROWS)
            return pltpu.make_async_copy(x_hbm.at[tok], rows_v.at[slot], sem_load.at[slot])

        def start_load(g, slot):
            load(g, slot).start()
            for kk in range(TOP_K):
                pltpu.sync_copy(idx_hbm.at[kk, chunk0 + g], idx_v.at[slot, kk])

        def store(slot, kk):
            return pltpu.make_async_copy(rows_v.at[slot], out_hbm.at[idx_v.at[slot, kk]], sem_store.at[slot])

        def scatter(g, slot):
            load(g, slot).wait()
            for kk in range(TOP_K):
                store(slot, kk).start()
            for kk in range(TOP_K):
                store(slot, kk).wait()

        start_load(0, 0)

        @pl.loop(0, n_it, step=2)
        def _(g):
            start_load(g + 1, 1)
            scatter(g, 0)

            @pl.when(g + 2 < n_it)
            def _():
                start_load(g + 2, 0)

            scatter(g + 1, 1)

    return scatter_kernel(xn, idx)


GU_SLAB = 256


def _gate_up_layout_kernel(w_ref, o_ref):
    half = GU_SLAB // 2
    r = lax.broadcasted_iota(jnp.int32, (GU_SLAB, GU_SLAB), 0)
    c = lax.broadcasted_iota(jnp.int32, (GU_SLAB, GU_SLAB), 1)
    src = jnp.where(c < half, 2 * c, 2 * (c - half) + 1)
    pm = jnp.where(r == src, 1.0, 0.0).astype(BF16)
    for s in range(w_ref.shape[2] // GU_SLAB):
        cols = slice(s * GU_SLAB, (s + 1) * GU_SLAB)
        o_ref[0, :, cols] = _dot(w_ref[0, :, cols].astype(BF16), pm).astype(BF16)


def _gate_up_layout(w_gate_up, b_gate_up):
    e, k, n2 = w_gate_up.shape
    half = GU_SLAB // 2
    tk = 512
    w = pl.pallas_call(
        _gate_up_layout_kernel,
        grid=(e, k // tk),
        in_specs=[pl.BlockSpec((1, tk, n2), lambda i, j: (i, j, 0))],
        out_specs=pl.BlockSpec((1, tk, n2), lambda i, j: (i, j, 0)),
        out_shape=jax.ShapeDtypeStruct((e, k, n2), BF16),
        compiler_params=_cparams(("arbitrary", "arbitrary")),
        name="gate_up_layout",
    )(w_gate_up)
    b = b_gate_up.reshape(e, n2 // GU_SLAB, half, 2).transpose(0, 1, 3, 2).reshape(e, 1, n2)
    return w, b


def _expert_kernel(be_ref, bv_ref, x_ref, wgu_ref, bgu_ref, wd_ref, bd_ref, o_ref, a_ref):
    i = pl.program_id(0)
    valid = bv_ref[i]
    half = GU_SLAB // 2

    @pl.when(valid > 0)
    def _():
        blk = x_ref.shape[0]
        rowi = lax.broadcasted_iota(jnp.int32, (blk, 1), 0)
        x = x_ref[...].reshape(blk, D_MODEL)
        x = jnp.where(rowi < valid, x, 0.0).astype(BF16)
        for s in range(2 * D_FF // GU_SLAB):
            cols = slice(s * GU_SLAB, (s + 1) * GU_SLAB)
            h = _dot(x, wgu_ref[0, :, cols]) + bgu_ref[0, :, cols]
            glu = jnp.minimum(h[:, :half], SWIGLU_LIMIT)
            lin = jnp.clip(h[:, half:], -SWIGLU_LIMIT, SWIGLU_LIMIT)
            a = glu * _sigmoid(SWIGLU_ALPHA * glu) * (lin + 1.0)
            a_ref[:, s * half:(s + 1) * half] = a.astype(BF16)
        o_ref[...] = (_dot(a_ref[...], wd_ref[0].astype(BF16)) + bd_ref[0]).reshape(o_ref.shape)

    @pl.when(valid <= 0)
    def _():
        o_ref[...] = jnp.zeros_like(o_ref)


def _experts(xs, block_e, block_valid, wgu, bgu, wd, bd):
    n_rows = xs.shape[0]
    d = D_MODEL
    n_blocks = block_e.shape[0]
    blk = n_rows // n_blocks
    emap = lambda i, be, bv: (be[i], 0, 0)
    rows = lambda: pl.BlockSpec((blk, ROW_SUB, ROW_LANE), lambda i, be, bv: (i, 0, 0))
    return pl.pallas_call(
        _expert_kernel,
        grid_spec=pltpu.PrefetchScalarGridSpec(
            num_scalar_prefetch=2,
            grid=(n_blocks,),
            in_specs=[rows(),
                      pl.BlockSpec((1, d, 2 * D_FF), emap), pl.BlockSpec((1, 1, 2 * D_FF), emap),
                      pl.BlockSpec((1, D_FF, d), emap), pl.BlockSpec((1, 1, d), emap)],
            out_specs=rows(),
            scratch_shapes=[pltpu.VMEM((blk, D_FF), BF16)],
        ),
        out_shape=jax.ShapeDtypeStruct((n_rows, ROW_SUB, ROW_LANE), F32),
        compiler_params=_cparams(("arbitrary",)),
        name="experts",
    )(block_e, block_valid, xs, wgu, bgu, wd, bd)


def _gather_rows(y, dest_flat):
    n_out = dest_flat.shape[0]
    workers = SC_CORES * SC_SUBCORES
    per_w = n_out // workers
    n_it = per_w // SC_ROWS
    assert per_w * workers == n_out and n_it * SC_ROWS == per_w and n_it % 2 == 0
    mesh = plsc.VectorSubcoreMesh(core_axis_name="c", subcore_axis_name="s")

    @functools.partial(
        pl.kernel, mesh=mesh,
        out_type=jax.ShapeDtypeStruct((n_out, ROW_SUB, ROW_LANE), y.dtype),
        scratch_types=[pltpu.VMEM((2, SC_ROWS), jnp.int32),
                       pltpu.VMEM((2, SC_ROWS, ROW_SUB, ROW_LANE), y.dtype),
                       pltpu.SemaphoreType.DMA((2,))],
    )
    def gather_kernel(y_hbm, idx_hbm, out_hbm, idx_v, rows_v, sem):
        base = (lax.axis_index("s") * SC_CORES + lax.axis_index("c")) * per_w

        def span(g):
            return pl.ds(pl.multiple_of(base + g * SC_ROWS, SC_ROWS), SC_ROWS)

        def row_gather(slot):
            return pltpu.make_async_copy(y_hbm.at[idx_v.at[slot]], rows_v.at[slot], sem.at[slot])

        def start(g, slot):
            pltpu.sync_copy(idx_hbm.at[span(g)], idx_v.at[slot])
            row_gather(slot).start()

        start(0, 0)

        @pl.loop(0, n_it, step=2)
        def _(g):
            start(g + 1, 1)
            row_gather(0).wait()
            pltpu.sync_copy(rows_v.at[0], out_hbm.at[span(g)])

            @pl.when(g + 2 < n_it)
            def _():
                start(g + 2, 0)

            row_gather(1).wait()
            pltpu.sync_copy(rows_v.at[1], out_hbm.at[span(g + 1)])

    return gather_kernel(y, dest_flat)


def _combine_kernel(yg_ref, x1_ref, mod_ref, rg_ref, gpo_ref, o_ref):
    rg = rg_ref[...]
    h = rg[:, 0:1] * yg_ref[0].reshape(COMB_TOK, D_MODEL)
    for kk in range(1, TOP_K):
        h = h + rg[:, kk:kk + 1] * yg_ref[kk].reshape(COMB_TOK, D_MODEL)
    gt2 = mod_ref[0, 5:6, :]
    o_ref[...] = x1_ref[...] + gt2 * _rms(h, gpo_ref[...])


def _combine(yg, x1, mod, rg, g_post_ffn, tokens_per_batch):
    n, d = x1.shape
    per_b = tokens_per_batch // COMB_TOK
    gpo = g_post_ffn.reshape(1, d)
    return pl.pallas_call(
        _combine_kernel,
        grid=(n // COMB_TOK,),
        in_specs=[pl.BlockSpec((TOP_K, COMB_TOK, ROW_SUB, ROW_LANE), lambda i: (0, i, 0, 0)),
                  pl.BlockSpec((COMB_TOK, d), lambda i: (i, 0)),
                  pl.BlockSpec((1, 6, d), lambda i: (i // per_b, 0, 0)),
                  pl.BlockSpec((COMB_TOK, 128), lambda i: (i, 0)),
                  pl.BlockSpec((1, d), lambda i: (0, 0))],
        out_specs=pl.BlockSpec((COMB_TOK, d), lambda i: (i, 0)),
        out_shape=jax.ShapeDtypeStruct((n, d), F32),
        compiler_params=_cparams(("arbitrary",)),
        name="combine",
    )(yg, x1, mod, rg, gpo)


def _moe_plan(ri_t, counts, n_tok):
    idx = ri_t[:TOP_K]
    rank = ri_t[TOP_K:2 * TOP_K]
    cnt = counts.astype(jnp.int32)
    blk = MOE_BLK_LARGE if n_tok * TOP_K >= 8 * MOE_BLK_LARGE * N_EXPERTS else MOE_BLK_SMALL
    padded = (cnt + blk - 1) // blk * blk
    pend = jnp.cumsum(padded)
    pstart = pend - padded
    dest = rank
    for e in range(N_EXPERTS):
        dest = dest + jnp.where(idx == e, pstart[e], 0)
    n_blocks = (n_tok * TOP_K + blk - 1) // blk + N_EXPERTS
    bstart = jnp.arange(n_blocks, dtype=jnp.int32) * blk
    block_e = jnp.minimum(jnp.sum((bstart[:, None] >= pend[None, :]).astype(jnp.int32), axis=1), N_EXPERTS - 1)
    mine = jnp.arange(N_EXPERTS, dtype=jnp.int32)[None, :] == block_e[:, None]
    row_end = jnp.sum(jnp.where(mine, (pstart + cnt)[None, :], 0), axis=1)
    block_valid = jnp.clip(row_end - bstart, 0, blk).astype(jnp.int32)
    return dest.astype(jnp.int32), block_e, block_valid, n_blocks * blk


def _token_mixing(x, mod, p):
    b, t, d = x.shape
    n = b * t
    proj = _in_proj(x, mod, p['g_pre_mix'], p['w_in'])
    o_fw, o_bw, o_b = _mixers(proj, p['lb_logits'], p['na_tabs'])
    x1, xn, ri, rg, counts = _mix(x, mod, proj, o_fw, o_bw, o_b, p['hgrn_norm'], p['g_post_mix'], p['g_pre_ffn'],
                                  p['w_branch_a'], p['w_branch_b'], p['w_out'], p['router_w'], p['router_b'])
    dest, block_e, block_valid, n_rows = _moe_plan(ri, counts[:, 0], n)
    xs = _scatter_rows(xn.reshape(n, ROW_SUB, ROW_LANE), dest, n_rows)
    return dict(x1=x1.reshape(n, d), rg=rg.reshape(n, 128), dest=dest, block_e=block_e, block_valid=block_valid,
                xs=xs, mod=mod, shape=(b, t, d))


def _expert_rows(s, p):
    return _experts(s['xs'], s['block_e'], s['block_valid'], p['w_gu'], p['b_gu'], p['w_d'], p['b_d'])


def _finish(s, y, p):
    b, t, d = s['shape']
    yg = _gather_rows(y, s['dest'].reshape(-1)).reshape(TOP_K, b * t, ROW_SUB, ROW_LANE)
    return _combine(yg, s['x1'], s['mod'], s['rg'], p['g_post_ffn'], t).reshape(b, t, d)


def _after(first, then):
    return lax.optimization_barrier((first, then))


def kernel(x_prompt, x_sample, c_prompt, c_sample, ada_w, ada_b, g_pre_mix, g_post_mix, g_pre_ffn, g_post_ffn,
           w_in, hgrn_lb_logits, hgrn_norm, na_rpb, w_branch_a, w_branch_b, w_out, router_w, router_b,
           w_gate_up, b_gate_up, w_down, b_down):
    d = D_MODEL
    bp, bs = c_prompt.shape[0], c_sample.shape[0]
    assert ada_w.shape[0] == 1 and hgrn_lb_logits.shape[0] == 2, "one layer: two lower-bound slots"
    for xg in (x_prompt, x_sample):
        assert xg.shape[2] == d and xg.shape[1] % HGRN_TILE == 0 and xg.shape[1] // GRID_W >= NA_KH
    pad = (-(bp + bs)) % 8
    c_all = jnp.concatenate([c_prompt, c_sample, jnp.zeros((pad, d), F32)], axis=0)
    mod = _ada_mod(c_all, ada_w[0], ada_b[0]).reshape(-1, 6, d)

    w = w_in[0]
    w_perm = jnp.concatenate([w[:, :5 * d], w[:, 5 * d + 3 * NA_W:], w[:, 5 * d:5 * d + 3 * NA_W]], axis=1).astype(BF16)
    w_gu, b_gu = _gate_up_layout(w_gate_up[0], b_gate_up[0])
    p = dict(
        g_pre_mix=g_pre_mix[0], g_post_mix=g_post_mix[0], g_pre_ffn=g_pre_ffn[0], g_post_ffn=g_post_ffn[0],
        w_in=w_perm, lb_logits=hgrn_lb_logits.astype(F32), hgrn_norm=hgrn_norm[0],
        na_tabs=_na_bias_tables(na_rpb[0]),
        w_branch_a=w_branch_a[0].astype(BF16), w_branch_b=w_branch_b[0].astype(BF16), w_out=w_out[0].astype(BF16),
        router_w=router_w[0].astype(BF16), router_b=router_b[0],
        w_gu=w_gu, b_gu=b_gu,
        w_d=w_down[0], b_d=b_down[0][:, None, :],
    )
    sp = _token_mixing(x_prompt, mod[:bp], p)
    sp['dest'], x_sample = _after(sp['dest'], x_sample)
    ss = _token_mixing(x_sample, mod[bp:bp + bs], p)
    ss['x1'], sp['xs'] = _after(ss['x1'], sp['xs'])
    y_p = _expert_rows(sp, p)
    y_p, ss['xs'] = _after(y_p, ss['xs'])
    y_s = _expert_rows(ss, p)
    out_p = _finish(sp, y_p, p)
    out_s = _finish(ss, y_s, p)
    return (out_p, out_s)
```

```python
import functools

import numpy as np
import jax
import jax.numpy as jnp
from jax import lax
from jax.experimental import pallas as pl
from jax.experimental.pallas import tpu as pltpu
from jax.experimental.pallas import tpu_sc as plsc

F32 = jnp.float32
BF16 = jnp.bfloat16

D_MODEL = 1024
GRID_W = 64
HA_HEADS = 8
HA_DK = 128
NA_HEADS = 8
NA_DH = 64
NA_W = NA_HEADS * NA_DH
NA_KH = 8
NA_KW = 16
N_EXPERTS = 32
TOP_K = 4
D_FF = 1024
SWIGLU_ALPHA = 1.702
SWIGLU_LIMIT = 7.0
RMS_EPS = 1e-6
NEG_BIG = -1e30

COL_A = 0
COL_GATES = 5 * 1024
COL_NA = 7 * 1024
PROJ_COLS = 7 * 1024 + 3 * NA_W

HGRN_CHUNK = 64
HGRN_TILE = 512
PROJ_TM = 2048
PROJ_TN = 2176
MIX_TM = 512
MOE_BLK_SMALL, MOE_BLK_LARGE = 512, 1024
COMB_TOK = 512
SUBLANES = 8
ROW_SUB, ROW_LANE = 8, 128
VMEM_LIMIT = 56 * 1024 * 1024


def _cparams(sem):
    return pltpu.CompilerParams(dimension_semantics=sem, vmem_limit_bytes=VMEM_LIMIT)


def _dot(a, b):
    return jnp.dot(a, b, preferred_element_type=F32)


def _dot_nt(a, b):
    return lax.dot_general(a, b, (((1,), (1,)), ((), ())), preferred_element_type=F32)


def _dot_tn(a, b):
    return lax.dot_general(a, b, (((0,), (0,)), ((), ())), preferred_element_type=F32)


def _split(a):
    hi = a.astype(BF16)
    lo = (a - hi.astype(F32)).astype(BF16)
    return hi, lo


def _dot3(a, w):
    ah, al = _split(a)
    wh, wl = _split(w)
    return _dot(ah, wh) + (_dot(al, wh) + _dot(ah, wl))


def _sigmoid(x):
    return 0.5 * jnp.tanh(0.5 * x) + 0.5


def _rms(x, g):
    return x * lax.rsqrt(jnp.mean(x * x, axis=-1, keepdims=True) + RMS_EPS) * g


def _ada_kernel(c_ref, w_ref, b_ref, o_ref):
    c = c_ref[...]
    o_ref[...] = _dot3(c * _sigmoid(c), w_ref[...]) + b_ref[...]


def _ada_mod(c, ada_w, ada_b):
    bp, d = c.shape
    n = ada_w.shape[1]
    tn = 1536
    return pl.pallas_call(
        _ada_kernel,
        grid=(n // tn,),
        in_specs=[pl.BlockSpec((bp, d), lambda j: (0, 0)),
                  pl.BlockSpec((d, tn), lambda j: (0, j)),
                  pl.BlockSpec((1, tn), lambda j: (0, j))],
        out_specs=pl.BlockSpec((bp, tn), lambda j: (0, j)),
        out_shape=jax.ShapeDtypeStruct((bp, n), F32),
        compiler_params=_cparams(("arbitrary",)),
        name="ada_mod",
    )(c, ada_w, ada_b.reshape(1, n))


def _in_proj_kernel(x_ref, mod_ref, g_ref, w_ref, o_ref, xn_ref):
    @pl.when(pl.program_id(2) == 0)
    def _():
        sh = mod_ref[0, 0:1, :]
        sc = mod_ref[0, 1:2, :]
        xn = _rms(x_ref[0], g_ref[...]) * (1.0 + sc) + sh
        xn_ref[...] = xn.astype(BF16)

    o_ref[0] = _dot(xn_ref[...], w_ref[...]).astype(BF16)


def _in_proj(x, mod, g_pre, w_in_bf):
    b, t, d = x.shape
    tm = min(PROJ_TM, t)
    n = w_in_bf.shape[1]
    return pl.pallas_call(
        _in_proj_kernel,
        grid=(b, t // tm, n // PROJ_TN),
        in_specs=[pl.BlockSpec((1, tm, d), lambda bi, i, j: (bi, i, 0)),
                  pl.BlockSpec((1, 6, d), lambda bi, i, j: (bi, 0, 0)),
                  pl.BlockSpec((1, d), lambda bi, i, j: (0, 0)),
                  pl.BlockSpec((d, PROJ_TN), lambda bi, i, j: (0, j))],
        out_specs=pl.BlockSpec((1, tm, PROJ_TN), lambda bi, i, j: (bi, i, j)),
        out_shape=jax.ShapeDtypeStruct((b, t, n), BF16),
        scratch_shapes=[pltpu.VMEM((tm, d), BF16)],
        compiler_params=_cparams(("arbitrary", "arbitrary", "arbitrary")),
        name="in_proj",
    )(x, mod, g_pre.reshape(1, d), w_in_bf)


def _cumsum_rows(x, reverse):
    groups = x.shape[0] // SUBLANES
    row = lax.broadcasted_iota(jnp.int32, (SUBLANES, x.shape[1]), 0)
    out = [None] * groups
    carry = None
    for g in (range(groups - 1, -1, -1) if reverse else range(groups)):
        blk = x[g * SUBLANES:(g + 1) * SUBLANES]
        step = 1
        while step < SUBLANES:
            if reverse:
                blk = blk + jnp.where(row < SUBLANES - step, pltpu.roll(blk, SUBLANES - step, axis=0), 0.0)
            else:
                blk = blk + jnp.where(row >= step, pltpu.roll(blk, step, axis=0), 0.0)
            step *= 2
        if carry is not None:
            blk = blk + carry
        carry = blk[0:1] if reverse else blk[SUBLANES - 1:SUBLANES]
        out[g] = blk
    return jnp.concatenate(out, axis=0)


def _hgrn_dir(q_ref, f_ref, i_ref, o_ref, s_ref, sn_ref, lbd, row0, reverse):
    c = HGRN_CHUNK
    rows = pl.ds(row0, c)
    q = q_ref[0, rows, :].astype(F32)
    fpre = f_ref[0, rows, :].astype(F32)
    v = i_ref[0, rows, :]
    f = 0.5 * (1.0 + lbd) + (0.5 * (1.0 - lbd)) * jnp.tanh(0.5 * fpre)
    logf = jnp.log(f)
    k = 1.0 - f
    ri = lax.broadcasted_iota(jnp.int32, (c, c), 0)
    ci = lax.broadcasted_iota(jnp.int32, (c, c), 1)
    keep = (ci >= ri) if reverse else (ri >= ci)
    b = _cumsum_rows(logf, reverse)
    if reverse:
        b_mid = b[c // 2:c // 2 + 1, :]
        b_end = b[0:1, :]
    else:
        b_mid = b[c // 2 - 1:c // 2, :]
        b_end = b[c - 1:c, :]
    d = b - b_mid
    qm = q * jnp.exp(d)
    km = k * jnp.exp(-d)
    qi = (qm * jnp.exp(b_mid)).astype(BF16)
    ks = (km * jnp.exp(b_end - b_mid)).astype(BF16)
    qm = qm.astype(BF16)
    km = km.astype(BF16)
    dec = jnp.exp(b_end)
    for h in range(HA_HEADS):
        sl = slice(h * HA_DK, (h + 1) * HA_DK)
        a = _dot_nt(qm[:, sl], km[:, sl])
        a = jnp.where(keep, a, 0.0).astype(BF16)
        lhs = jnp.concatenate([qi[:, sl], a], axis=1)
        rhs = jnp.concatenate([sn_ref[h], v[:, sl]], axis=0)
        o_ref[0, rows, sl] = _dot(lhs, rhs).astype(BF16)
        st = s_ref[h] * dec[:, sl] + _dot_tn(v[:, sl], ks[:, sl])
        s_ref[h] = st
        sn_ref[h] = st.astype(BF16).T


def _hgrn_tile(lbl_ref, qf_ref, ff_ref, if_ref, qb_ref, fb_ref, ib_ref, of_ref, ob_ref,
               sf_ref, sb_ref, snf_ref, snb_ref):
    @pl.when(pl.program_id(1) == 0)
    def _():
        for ref in (sf_ref, sb_ref, snf_ref, snb_ref):
            ref[...] = jnp.zeros_like(ref)

    l0 = lbl_ref[0]
    l1 = lbl_ref[1]
    m = jnp.maximum(l0, l1)
    e0 = jnp.exp(l0 - m)
    e1 = jnp.exp(l1 - m)
    lb = e0 / (e0 + e1)
    n_chunks = HGRN_TILE // HGRN_CHUNK

    def chunk(ci):
        _hgrn_dir(qf_ref, ff_ref, if_ref, of_ref, sf_ref, snf_ref, lb[0:1, :], ci * HGRN_CHUNK, False)
        _hgrn_dir(qb_ref, fb_ref, ib_ref, ob_ref, sb_ref, snb_ref, lb[1:2, :],
                  (n_chunks - 1 - ci) * HGRN_CHUNK, True)

    return chunk


def _mixers_kernel(lbl_ref, qf_ref, ff_ref, if_ref, qb_ref, fb_ref, ib_ref, nq_ref, nk_ref, nv_ref, tab_ref,
                   of_ref, ob_ref, on_ref, sf_ref, sb_ref, snf_ref, snb_ref, *, grid_rows):
    chunk = _hgrn_tile(lbl_ref, qf_ref, ff_ref, if_ref, qb_ref, fb_ref, ib_ref, of_ref, ob_ref,
                       sf_ref, sb_ref, snf_ref, snb_ref)
    na_row = _na_rows(nq_ref, nk_ref, nv_ref, tab_ref, on_ref, grid_rows)
    for ci in range(HGRN_TILE // HGRN_CHUNK):
        chunk(ci)
        na_row(ci)


def _mixers(proj, lb_logits, tabs):
    b, t, _ = proj.shape
    tt = HGRN_TILE
    nt = t // tt
    d = D_MODEL
    cq, ck, cv = (COL_NA // NA_W, COL_NA // NA_W + 1, COL_NA // NA_W + 2)
    seq_bufs = 2 if 4 * t * NA_W * 2 <= VMEM_LIMIT // 3 else 1

    def fwd(col):
        return pl.BlockSpec((1, tt, d), lambda bi, ti: (bi, ti, col))

    def bwd(col):
        return pl.BlockSpec((1, tt, d), lambda bi, ti: (bi, nt - 1 - ti, col))

    return pl.pallas_call(
        functools.partial(_mixers_kernel, grid_rows=t // GRID_W),
        grid=(b, nt),
        in_specs=[pl.BlockSpec((2, 2, d), lambda bi, ti: (0, 0, 0)),
                  fwd(0), fwd(1), fwd(3), bwd(0), bwd(2), bwd(3),
                  pl.BlockSpec((1, tt, NA_W), lambda bi, ti: (bi, ti, cq)),
                  pl.BlockSpec((1, t, NA_W), lambda bi, ti: (bi, 0, ck), pipeline_mode=pl.Buffered(seq_bufs)),
                  pl.BlockSpec((1, t, NA_W), lambda bi, ti: (bi, 0, cv), pipeline_mode=pl.Buffered(seq_bufs)),
                  pl.BlockSpec(tabs.shape, lambda bi, ti: (0, 0, 0, 0), pipeline_mode=pl.Buffered(1))],
        out_specs=[pl.BlockSpec((1, tt, d), lambda bi, ti: (bi, ti, 0)),
                   pl.BlockSpec((1, tt, d), lambda bi, ti: (bi, nt - 1 - ti, 0)),
                   pl.BlockSpec((1, tt, NA_W), lambda bi, ti: (bi, ti, 0))],
        out_shape=[jax.ShapeDtypeStruct((b, t, d), BF16), jax.ShapeDtypeStruct((b, t, d), BF16),
                   jax.ShapeDtypeStruct((b, t, NA_W), BF16)],
        scratch_shapes=[pltpu.VMEM((HA_HEADS, HA_DK, HA_DK), F32), pltpu.VMEM((HA_HEADS, HA_DK, HA_DK), F32),
                        pltpu.VMEM((HA_HEADS, HA_DK, HA_DK), BF16), pltpu.VMEM((HA_HEADS, HA_DK, HA_DK), BF16)],
        compiler_params=_cparams(("arbitrary", "arbitrary")),
        name="mixers",
    )(lb_logits, proj, proj, proj, proj, proj, proj, proj, proj, proj, tabs)


def _na_bias_tables(rpb):
    q = np.arange(GRID_W)[:, None]
    kc = np.arange(GRID_W)[None, :]
    c0 = np.clip(q - NA_KW // 2, 0, GRID_W - NA_KW)
    ok = (kc >= c0) & (kc < c0 + NA_KW)
    dc = np.clip(kc - q + (NA_KW - 1), 0, 2 * NA_KW - 2)
    rp = rpb.astype(F32)
    base = jnp.full(rp.shape[:2] + ok.shape, NEG_BIG, F32)
    for j in range(rp.shape[2]):
        base = jnp.where((ok & (dc == j))[None, None], rp[:, :, j, None, None], base)
    tabs = []
    for dr0 in range(NA_KH):
        w = base[:, dr0:dr0 + NA_KH]
        w = w.transpose(0, 2, 1, 3).reshape(NA_HEADS // 2, 2 * GRID_W, NA_KH * GRID_W)
        tabs.append(w)
    return jnp.stack(tabs, axis=1)


def _na_rows(q_ref, k_ref, v_ref, tab_ref, o_ref, grid_rows):
    g = pl.program_id(1)
    lane = lax.broadcasted_iota(jnp.int32, (GRID_W, 2 * NA_DH), 1)
    low = lane < NA_DH
    scale = NA_DH ** -0.5
    rows_per_step = HGRN_TILE // GRID_W

    def na_row(rr):
        r = g * rows_per_step + rr
        r0 = jnp.clip(r - NA_KH // 2, 0, grid_rows - NA_KH)
        dr0 = r0 - r + (NA_KH - 1)
        qrow = pl.ds(rr * GRID_W, GRID_W)
        krow = pl.ds(pl.multiple_of(r0 * GRID_W, GRID_W), NA_KH * GRID_W)
        for j in range(NA_HEADS // 2):
            sl = slice(j * 2 * NA_DH, (j + 1) * 2 * NA_DH)
            qp = q_ref[0, qrow, sl]
            zero = jnp.zeros_like(qp)
            ql = jnp.concatenate([jnp.where(low, qp, zero), jnp.where(low, zero, qp)], axis=0)
            s = _dot_nt(ql, k_ref[0, krow, sl]) * scale + tab_ref[j, dr0]
            m = jnp.max(s, axis=-1, keepdims=True)
            p = jnp.exp(s - m)
            l = jnp.sum(p, axis=-1, keepdims=True)
            pv = _dot(p.astype(BF16), v_ref[0, krow, sl]) / l
            o_ref[0, qrow, sl] = jnp.where(low, pv[:GRID_W], pv[GRID_W:]).astype(BF16)

    return na_row


def _mix_kernel(x_ref, mod_ref, of_ref, ob_ref, g_ref, ga_ref, gb_ref, nb_ref,
                gn_ref, gpm_ref, gpf_ref, wa_ref, wb_ref, wo_ref, rw_ref, rb_ref,
                x1_ref, xn_ref, ri_ref, rg_ref, cnt_ref, tri_ref, carry_ref):
    tm = MIX_TM
    first = (pl.program_id(0) == 0) & (pl.program_id(1) == 0)

    @pl.when(first)
    def _():
        r = lax.broadcasted_iota(jnp.int32, (tm, tm), 0)
        c = lax.broadcasted_iota(jnp.int32, (tm, tm), 1)
        tri_ref[...] = jnp.where(r < c, 1.0, 0.0).astype(BF16)
        carry_ref[...] = jnp.zeros_like(carry_ref)

    sc2 = mod_ref[0, 4:5, :]
    sh2 = mod_ref[0, 3:4, :]
    gt1 = mod_ref[0, 2:3, :]

    o = of_ref[0].astype(F32) + ob_ref[0].astype(F32)
    parts = []
    for h in range(HA_HEADS):
        oh = o[:, h * HA_DK:(h + 1) * HA_DK]
        parts.append(oh * lax.rsqrt(jnp.mean(oh * oh, axis=-1, keepdims=True) + RMS_EPS))
    g = g_ref[0].astype(F32)
    oa = jnp.concatenate(parts, axis=-1) * gn_ref[...] * (g * _sigmoid(g))
    ya = _dot(oa.astype(BF16), wa_ref[...])
    yb = _dot(nb_ref[0], wb_ref[...])
    mix = _sigmoid(ga_ref[0].astype(F32)) * ya + _sigmoid(gb_ref[0].astype(F32)) * yb
    mo = _dot(mix.astype(BF16), wo_ref[...])
    x1 = x_ref[0] + gt1 * _rms(mo, gpm_ref[...])
    x1_ref[0] = x1

    xn = _rms(x1, gpf_ref[...]) * (1.0 + sc2) + sh2
    xn_ref[0] = xn.reshape(tm, ROW_SUB, ROW_LANE)
    logits = _dot_nt(rw_ref[...], xn.astype(BF16)) + rb_ref[...]
    sub = lax.broadcasted_iota(jnp.int32, (N_EXPERTS, tm), 0)
    work = logits
    vals, idxs, hots = [], [], []
    for _ in range(TOP_K):
        mv = jnp.max(work, axis=0, keepdims=True)
        mi = jnp.min(jnp.where(work == mv, sub, N_EXPERTS), axis=0, keepdims=True)
        hot = sub == mi
        vals.append(mv)
        idxs.append(mi)
        hots.append(hot)
        work = jnp.where(hot, -jnp.inf, work)
    es = [jnp.exp(v - vals[0]) for v in vals]
    den = es[0] + es[1] + es[2] + es[3]

    cnt = sum(jnp.where(h, 1.0, 0.0) for h in hots)
    before = _dot(cnt.astype(BF16), tri_ref[...]) + carry_ref[...]
    carry_ref[...] = carry_ref[...] + jnp.sum(cnt, axis=1, keepdims=True)
    cnt_ref[...] = jnp.broadcast_to(carry_ref[...], cnt_ref.shape)

    ranks = [jnp.sum(jnp.where(hots[kk], before, 0.0), axis=0, keepdims=True).astype(jnp.int32)
             for kk in range(TOP_K)]
    ri_ref[...] = jnp.concatenate(idxs + ranks, axis=0)
    gates = jnp.concatenate([e / den for e in es] + [jnp.zeros((128 - TOP_K, tm), F32)], axis=0)
    rg_ref[0] = gates.T


def _mix(x, mod, proj, o_fw, o_bw, o_b, hgrn_norm, g_post_mix, g_pre_ffn, wa, wb, wo, rw, rb):
    b, t, d = x.shape
    tm = min(MIX_TM, t)
    assert tm == MIX_TM
    row = lambda a: a.reshape(1, -1)
    tok = lambda w, col: pl.BlockSpec((1, tm, w), lambda bi, i: (bi, i, col))
    full = lambda a: pl.BlockSpec(a.shape, lambda bi, i: (0,) * a.ndim)
    gn, gpm, gpf, rbr = row(hgrn_norm), row(g_post_mix), row(g_pre_ffn), rb.reshape(-1, 1)
    rw = rw.T
    outs = pl.pallas_call(
        _mix_kernel,
        grid=(b, t // tm),
        in_specs=[tok(d, 0),
                  pl.BlockSpec((1, 6, d), lambda bi, i: (bi, 0, 0)),
                  tok(d, 0), tok(d, 0),
                  tok(d, 4),
                  tok(d, COL_GATES // d), tok(d, COL_GATES // d + 1),
                  tok(NA_W, 0),
                  full(gn), full(gpm), full(gpf), full(wa), full(wb), full(wo), full(rw), full(rbr)],
        out_specs=[tok(d, 0), pl.BlockSpec((1, tm, ROW_SUB, ROW_LANE), lambda bi, i: (bi, i, 0, 0)),
                   pl.BlockSpec((8, tm), lambda bi, i: (0, bi * (t // tm) + i)), tok(128, 0),
                   pl.BlockSpec((N_EXPERTS, 128), lambda bi, i: (0, 0))],
        out_shape=[jax.ShapeDtypeStruct((b, t, d), F32), jax.ShapeDtypeStruct((b, t, ROW_SUB, ROW_LANE), F32),
                   jax.ShapeDtypeStruct((8, b * t), jnp.int32), jax.ShapeDtypeStruct((b, t, 128), F32),
                   jax.ShapeDtypeStruct((N_EXPERTS, 128), F32)],
        scratch_shapes=[pltpu.VMEM((tm, tm), BF16), pltpu.VMEM((N_EXPERTS, 1), F32)],
        compiler_params=_cparams(("arbitrary", "arbitrary")),
        name="mix",
    )(x, mod, o_fw, o_bw, proj, proj, proj, o_b, gn, gpm, gpf, wa, wb, wo, rw, rbr)
    return outs


SC_CORES, SC_SUBCORES = 2, 16
SC_ROWS = 32


def _scatter_rows(xn, dest_t, n_rows):
    n_tok = xn.shape[0]
    workers = SC_CORES * SC_SUBCORES
    per_w = n_tok // workers
    n_it = per_w // SC_ROWS
    assert per_w * workers == n_tok and n_it * SC_ROWS == per_w and n_it % 2 == 0
    idx = dest_t.reshape(TOP_K, n_tok // SC_ROWS, SC_ROWS)
    mesh = plsc.VectorSubcoreMesh(core_axis_name="c", subcore_axis_name="s")

    @functools.partial(
        pl.kernel, mesh=mesh,
        out_type=jax.ShapeDtypeStruct((n_rows, ROW_SUB, ROW_LANE), xn.dtype),
        scratch_types=[pltpu.VMEM((2, TOP_K, SC_ROWS), jnp.int32),
                       pltpu.VMEM((2, SC_ROWS, ROW_SUB, ROW_LANE), xn.dtype),
                       pltpu.SemaphoreType.DMA((2,)), pltpu.SemaphoreType.DMA((2,))],
    )
    def scatter_kernel(x_hbm, idx_hbm, out_hbm, idx_v, rows_v, sem_load, sem_store):
        chunk0 = (lax.axis_index("s") * SC_CORES + lax.axis_index("c")) * n_it

        def load(g, slot):
            tok = pl.ds(pl.multiple_of((chunk0 + g) * SC_ROWS, SC_ROWS), SC_ROWS)
            return pltpu.make_async_copy(x_hbm.at[tok], rows_v.at[slot], sem_load.at[slot])

        def start_load(g, slot):
            load(g, slot).start()
            for kk in range(TOP_K):
                pltpu.sync_copy(idx_hbm.at[kk, chunk0 + g], idx_v.at[slot, kk])

        def store(slot, kk):
            return pltpu.make_async_copy(rows_v.at[slot], out_hbm.at[idx_v.at[slot, kk]], sem_store.at[slot])

        def scatter(g, slot):
            load(g, slot).wait()
            for kk in range(TOP_K):
                store(slot, kk).start()
            for kk in range(TOP_K):
                store(slot, kk).wait()

        start_load(0, 0)

        @pl.loop(0, n_it, step=2)
        def _(g):
            start_load(g + 1, 1)
            scatter(g, 0)

            @pl.when(g + 2 < n_it)
            def _():
                start_load(g + 2, 0)

            scatter(g + 1, 1)

    return scatter_kernel(xn, idx)


GU_SLAB = 256


def _gate_up_layout_kernel(w_ref, o_ref):
    half = GU_SLAB // 2
    r = lax.broadcasted_iota(jnp.int32, (GU_SLAB, GU_SLAB), 0)
    c = lax.broadcasted_iota(jnp.int32, (GU_SLAB, GU_SLAB), 1)
    src = jnp.where(c < half, 2 * c, 2 * (c - half) + 1)
    pm = jnp.where(r == src, 1.0, 0.0).astype(BF16)
    for s in range(w_ref.shape[2] // GU_SLAB):
        cols = slice(s * GU_SLAB, (s + 1) * GU_SLAB)
        o_ref[0, :, cols] = _dot(w_ref[0, :, cols].astype(BF16), pm).astype(BF16)


def _gate_up_layout(w_gate_up, b_gate_up):
    e, k, n2 = w_gate_up.shape
    half = GU_SLAB // 2
    tk = 512
    w = pl.pallas_call(
        _gate_up_layout_kernel,
        grid=(e, k // tk),
        in_specs=[pl.BlockSpec((1, tk, n2), lambda i, j: (i, j, 0))],
        out_specs=pl.BlockSpec((1, tk, n2), lambda i, j: (i, j, 0)),
        out_shape=jax.ShapeDtypeStruct((e, k, n2), BF16),
        compiler_params=_cparams(("arbitrary", "arbitrary")),
        name="gate_up_layout",
    )(w_gate_up)
    b = b_gate_up.reshape(e, n2 // GU_SLAB, half, 2).transpose(0, 1, 3, 2).reshape(e, 1, n2)
    return w, b


def _expert_block(bv_ref, a_ref, idx, x_ref, wgu_ref, bgu_ref, wd_ref, bd_ref, o_ref):
    valid = bv_ref[idx[0]]
    half = GU_SLAB // 2

    @pl.when(valid > 0)
    def _():
        blk = x_ref.shape[0]
        rowi = lax.broadcasted_iota(jnp.int32, (blk, 1), 0)
        x = x_ref[...].reshape(blk, D_MODEL)
        x = jnp.where(rowi < valid, x, 0.0).astype(BF16)
        for s in range(2 * D_FF // GU_SLAB):
            cols = slice(s * GU_SLAB, (s + 1) * GU_SLAB)
            h = _dot(x, wgu_ref[0, :, cols]) + bgu_ref[0, :, cols]
            glu = jnp.minimum(h[:, :half], SWIGLU_LIMIT)
            lin = jnp.clip(h[:, half:], -SWIGLU_LIMIT, SWIGLU_LIMIT)
            a = glu * _sigmoid(SWIGLU_ALPHA * glu) * (lin + 1.0)
            a_ref[:, s * half:(s + 1) * half] = a.astype(BF16)
        o_ref[...] = (_dot(a_ref[...], wd_ref[0].astype(BF16)) + bd_ref[0]).reshape(o_ref.shape)

    @pl.when(valid <= 0)
    def _():
        o_ref[...] = jnp.zeros_like(o_ref)


def _experts(xs, block_e, block_valid, wgu, bgu, wd, bd):
    n_rows = xs.shape[0]
    d = D_MODEL
    n_blocks = block_e.shape[0]
    blk = n_rows // n_blocks

    def outer(be_ref, bv_ref, xs_hbm, wgu_hbm, bgu_hbm, wd_hbm, bd_hbm, y_hbm, a_ref):
        emap = lambda i: (be_ref[i], 0, 0)
        pltpu.emit_pipeline(
            functools.partial(_expert_block, bv_ref, a_ref),
            grid=(n_blocks,),
            in_specs=[pl.BlockSpec((blk, ROW_SUB, ROW_LANE), lambda i: (i, 0, 0), pipeline_mode=pl.Buffered(3)),
                      pl.BlockSpec((1, d, 2 * D_FF), emap), pl.BlockSpec((1, 1, 2 * D_FF), emap),
                      pl.BlockSpec((1, D_FF, d), emap), pl.BlockSpec((1, 1, d), emap)],
            out_specs=[pl.BlockSpec((blk, ROW_SUB, ROW_LANE), lambda i: (i, 0, 0))],
            _explicit_indices=True,
        )(xs_hbm, wgu_hbm, bgu_hbm, wd_hbm, bd_hbm, y_hbm)

    smem = lambda: pl.BlockSpec(memory_space=pltpu.SMEM)
    hbm = lambda: pl.BlockSpec(memory_space=pl.ANY)
    return pl.pallas_call(
        outer,
        in_specs=[smem(), smem(), hbm(), hbm(), hbm(), hbm(), hbm()],
        out_specs=hbm(),
        out_shape=jax.ShapeDtypeStruct((n_rows, ROW_SUB, ROW_LANE), F32),
        scratch_shapes=[pltpu.VMEM((blk, D_FF), BF16)],
        compiler_params=pltpu.CompilerParams(vmem_limit_bytes=VMEM_LIMIT),
        name="experts",
    )(block_e, block_valid, xs, wgu, bgu, wd, bd)


def _gather_rows(y, dest_flat):
    n_out = dest_flat.shape[0]
    workers = SC_CORES * SC_SUBCORES
    per_w = n_out // workers
    n_it = per_w // SC_ROWS
    assert per_w * workers == n_out and n_it * SC_ROWS == per_w and n_it % 2 == 0
    mesh = plsc.VectorSubcoreMesh(core_axis_name="c", subcore_axis_name="s")

    @functools.partial(
        pl.kernel, mesh=mesh,
        out_type=jax.ShapeDtypeStruct((n_out, ROW_SUB, ROW_LANE), y.dtype),
        scratch_types=[pltpu.VMEM((2, SC_ROWS), jnp.int32),
                       pltpu.VMEM((2, SC_ROWS, ROW_SUB, ROW_LANE), y.dtype),
                       pltpu.SemaphoreType.DMA((2,))],
    )
    def gather_kernel(y_hbm, idx_hbm, out_hbm, idx_v, rows_v, sem):
        base = (lax.axis_index("s") * SC_CORES + lax.axis_index("c")) * per_w

        def span(g):
            return pl.ds(pl.multiple_of(base + g * SC_ROWS, SC_ROWS), SC_ROWS)

        def row_gather(slot):
            return pltpu.make_async_copy(y_hbm.at[idx_v.at[slot]], rows_v.at[slot], sem.at[slot])

        def start(g, slot):
            pltpu.sync_copy(idx_hbm.at[span(g)], idx_v.at[slot])
            row_gather(slot).start()

        start(0, 0)

        @pl.loop(0, n_it, step=2)
        def _(g):
            start(g + 1, 1)
            row_gather(0).wait()
            pltpu.sync_copy(rows_v.at[0], out_hbm.at[span(g)])

            @pl.when(g + 2 < n_it)
            def _():
                start(g + 2, 0)

            row_gather(1).wait()
            pltpu.sync_copy(rows_v.at[1], out_hbm.at[span(g + 1)])

    return gather_kernel(y, dest_flat)


def _combine_kernel(yg_ref, x1_ref, mod_ref, rg_ref, gpo_ref, o_ref):
    rg = rg_ref[...]
    h = rg[:, 0:1] * yg_ref[0].reshape(COMB_TOK, D_MODEL)
    for kk in range(1, TOP_K):
        h = h + rg[:, kk:kk + 1] * yg_ref[kk].reshape(COMB_TOK, D_MODEL)
    gt2 = mod_ref[0, 5:6, :]
    o_ref[...] = x1_ref[...] + gt2 * _rms(h, gpo_ref[...])


def _combine(yg, x1, mod, rg, g_post_ffn, tokens_per_batch):
    n, d = x1.shape
    per_b = tokens_per_batch // COMB_TOK
    gpo = g_post_ffn.reshape(1, d)
    return pl.pallas_call(
        _combine_kernel,
        grid=(n // COMB_TOK,),
        in_specs=[pl.BlockSpec((TOP_K, COMB_TOK, ROW_SUB, ROW_LANE), lambda i: (0, i, 0, 0)),
                  pl.BlockSpec((COMB_TOK, d), lambda i: (i, 0)),
                  pl.BlockSpec((1, 6, d), lambda i: (i // per_b, 0, 0)),
                  pl.BlockSpec((COMB_TOK, 128), lambda i: (i, 0)),
                  pl.BlockSpec((1, d), lambda i: (0, 0))],
        out_specs=pl.BlockSpec((COMB_TOK, d), lambda i: (i, 0)),
        out_shape=jax.ShapeDtypeStruct((n, d), F32),
        compiler_params=_cparams(("arbitrary",)),
        name="combine",
    )(yg, x1, mod, rg, gpo)


def _moe_plan(ri_t, counts, n_tok):
    idx = ri_t[:TOP_K]
    rank = ri_t[TOP_K:2 * TOP_K]
    cnt = counts.astype(jnp.int32)
    blk = MOE_BLK_LARGE if n_tok * TOP_K >= 8 * MOE_BLK_LARGE * N_EXPERTS else MOE_BLK_SMALL
    padded = (cnt + blk - 1) // blk * blk
    pend = jnp.cumsum(padded)
    pstart = pend - padded
    dest = rank
    for e in range(N_EXPERTS):
        dest = dest + jnp.where(idx == e, pstart[e], 0)
    n_blocks = (n_tok * TOP_K + blk - 1) // blk + N_EXPERTS
    bstart = jnp.arange(n_blocks, dtype=jnp.int32) * blk
    block_e = jnp.minimum(jnp.sum((bstart[:, None] >= pend[None, :]).astype(jnp.int32), axis=1), N_EXPERTS - 1)
    mine = jnp.arange(N_EXPERTS, dtype=jnp.int32)[None, :] == block_e[:, None]
    row_end = jnp.sum(jnp.where(mine, (pstart + cnt)[None, :], 0), axis=1)
    block_valid = jnp.clip(row_end - bstart, 0, blk).astype(jnp.int32)
    return dest.astype(jnp.int32), block_e, block_valid, n_blocks * blk


def _token_mixing(x, mod, p):
    b, t, d = x.shape
    n = b * t
    proj = _in_proj(x, mod, p['g_pre_mix'], p['w_in'])
    o_fw, o_bw, o_b = _mixers(proj, p['lb_logits'], p['na_tabs'])
    x1, xn, ri, rg, counts = _mix(x, mod, proj, o_fw, o_bw, o_b, p['hgrn_norm'], p['g_post_mix'], p['g_pre_ffn'],
                                  p['w_branch_a'], p['w_branch_b'], p['w_out'], p['router_w'], p['router_b'])
    dest, block_e, block_valid, n_rows = _moe_plan(ri, counts[:, 0], n)
    xs = _scatter_rows(xn.reshape(n, ROW_SUB, ROW_LANE), dest, n_rows)
    return dict(x1=x1.reshape(n, d), rg=rg.reshape(n, 128), dest=dest, block_e=block_e, block_valid=block_valid,
                xs=xs, mod=mod, shape=(b, t, d))


def _expert_rows(s, p):
    return _experts(s['xs'], s['block_e'], s['block_valid'], p['w_gu'], p['b_gu'], p['w_d'], p['b_d'])


def _finish(s, y, p):
    b, t, d = s['shape']
    yg = _gather_rows(y, s['dest'].reshape(-1)).reshape(TOP_K, b * t, ROW_SUB, ROW_LANE)
    return _combine(yg, s['x1'], s['mod'], s['rg'], p['g_post_ffn'], t).reshape(b, t, d)


def _after(first, then):
    return lax.optimization_barrier((first, then))


def kernel(x_prompt, x_sample, c_prompt, c_sample, ada_w, ada_b, g_pre_mix, g_post_mix, g_pre_ffn, g_post_ffn,
           w_in, hgrn_lb_logits, hgrn_norm, na_rpb, w_branch_a, w_branch_b, w_out, router_w, router_b,
           w_gate_up, b_gate_up, w_down, b_down):
    d = D_MODEL
    bp, bs = c_prompt.shape[0], c_sample.shape[0]
    assert ada_w.shape[0] == 1 and hgrn_lb_logits.shape[0] == 2, "one layer: two lower-bound slots"
    for xg in (x_prompt, x_sample):
        assert xg.shape[2] == d and xg.shape[1] % HGRN_TILE == 0 and xg.shape[1] // GRID_W >= NA_KH
    pad = (-(bp + bs)) % 8
    c_all = jnp.concatenate([c_prompt, c_sample, jnp.zeros((pad, d), F32)], axis=0)
    mod = _ada_mod(c_all, ada_w[0], ada_b[0]).reshape(-1, 6, d)

    w = w_in[0]
    w_perm = jnp.concatenate([w[:, :5 * d], w[:, 5 * d + 3 * NA_W:], w[:, 5 * d:5 * d + 3 * NA_W]], axis=1).astype(BF16)
    w_gu, b_gu = _gate_up_layout(w_gate_up[0], b_gate_up[0])
    p = dict(
        g_pre_mix=g_pre_mix[0], g_post_mix=g_post_mix[0], g_pre_ffn=g_pre_ffn[0], g_post_ffn=g_post_ffn[0],
        w_in=w_perm, lb_logits=hgrn_lb_logits.astype(F32), hgrn_norm=hgrn_norm[0],
        na_tabs=_na_bias_tables(na_rpb[0]),
        w_branch_a=w_branch_a[0].astype(BF16), w_branch_b=w_branch_b[0].astype(BF16), w_out=w_out[0].astype(BF16),
        router_w=router_w[0].astype(BF16), router_b=router_b[0],
        w_gu=w_gu, b_gu=b_gu,
        w_d=w_down[0], b_d=b_down[0][:, None, :],
    )
    sp = _token_mixing(x_prompt, mod[:bp], p)
    sp['dest'], x_sample = _after(sp['dest'], x_sample)
    ss = _token_mixing(x_sample, mod[bp:bp + bs], p)
    ss['x1'], sp['xs'] = _after(ss['x1'], sp['xs'])
    y_p = _expert_rows(sp, p)
    y_p, ss['xs'] = _after(y_p, ss['xs'])
    y_s = _expert_rows(ss, p)
    out_p = _finish(sp, y_p, p)
    out_s = _finish(ss, y_s, p)
    return (out_p, out_s)
```

```python
import functools

import numpy as np
import jax
import jax.numpy as jnp
from jax import lax
from jax.experimental import pallas as pl
from jax.experimental.pallas import tpu as pltpu
from jax.experimental.pallas import tpu_sc as plsc

F32 = jnp.float32
BF16 = jnp.bfloat16

D_MODEL = 1024
GRID_W = 64
HA_HEADS = 8
HA_DK = 128
NA_HEADS = 8
NA_DH = 64
NA_W = NA_HEADS * NA_DH
NA_KH = 8
NA_KW = 16
N_EXPERTS = 32
TOP_K = 4
D_FF = 1024
SWIGLU_ALPHA = 1.702
SWIGLU_LIMIT = 7.0
RMS_EPS = 1e-6
NEG_BIG = -1e30

COL_A = 0
COL_GATES = 5 * 1024
COL_NA = 7 * 1024
PROJ_COLS = 7 * 1024 + 3 * NA_W

HGRN_CHUNK = 64
HGRN_TILE = 512
PROJ_TM = 2048
PROJ_TN = 2176
MIX_TM = 512
MOE_BLK_SMALL, MOE_BLK_LARGE = 512, 1024
COMB_TOK = 512
SUBLANES = 8
ROW_SUB, ROW_LANE = 8, 128
VMEM_LIMIT = 56 * 1024 * 1024


def _cparams(sem):
    return pltpu.CompilerParams(dimension_semantics=sem, vmem_limit_bytes=VMEM_LIMIT)


def _dot(a, b):
    return jnp.dot(a, b, preferred_element_type=F32)


def _dot_nt(a, b):
    return lax.dot_general(a, b, (((1,), (1,)), ((), ())), preferred_element_type=F32)


def _dot_tn(a, b):
    return lax.dot_general(a, b, (((0,), (0,)), ((), ())), preferred_element_type=F32)


def _split(a):
    hi = a.astype(BF16)
    lo = (a - hi.astype(F32)).astype(BF16)
    return hi, lo


def _dot3(a, w):
    ah, al = _split(a)
    wh, wl = _split(w)
    return _dot(ah, wh) + (_dot(al, wh) + _dot(ah, wl))


def _sigmoid(x):
    return 0.5 * jnp.tanh(0.5 * x) + 0.5


def _rms(x, g):
    return x * lax.rsqrt(jnp.mean(x * x, axis=-1, keepdims=True) + RMS_EPS) * g


def _ada_kernel(c_ref, w_ref, b_ref, o_ref):
    c = c_ref[...]
    o_ref[...] = _dot3(c * _sigmoid(c), w_ref[...]) + b_ref[...]


def _ada_mod(c, ada_w, ada_b):
    bp, d = c.shape
    n = ada_w.shape[1]
    tn = 1536
    return pl.pallas_call(
        _ada_kernel,
        grid=(n // tn,),
        in_specs=[pl.BlockSpec((bp, d), lambda j: (0, 0)),
                  pl.BlockSpec((d, tn), lambda j: (0, j)),
                  pl.BlockSpec((1, tn), lambda j: (0, j))],
        out_specs=pl.BlockSpec((bp, tn), lambda j: (0, j)),
        out_shape=jax.ShapeDtypeStruct((bp, n), F32),
        compiler_params=_cparams(("arbitrary",)),
        name="ada_mod",
    )(c, ada_w, ada_b.reshape(1, n))


def _in_proj_kernel(x_ref, mod_ref, g_ref, w_ref, o_ref, xn_ref):
    @pl.when(pl.program_id(2) == 0)
    def _():
        sh = mod_ref[0, 0:1, :]
        sc = mod_ref[0, 1:2, :]
        xn = _rms(x_ref[0], g_ref[...]) * (1.0 + sc) + sh
        xn_ref[...] = xn.astype(BF16)

    o_ref[0] = _dot(xn_ref[...], w_ref[...]).astype(BF16)


def _in_proj(x, mod, g_pre, w_in_bf):
    b, t, d = x.shape
    tm = min(PROJ_TM, t)
    n = w_in_bf.shape[1]
    return pl.pallas_call(
        _in_proj_kernel,
        grid=(b, t // tm, n // PROJ_TN),
        in_specs=[pl.BlockSpec((1, tm, d), lambda bi, i, j: (bi, i, 0)),
                  pl.BlockSpec((1, 6, d), lambda bi, i, j: (bi, 0, 0)),
                  pl.BlockSpec((1, d), lambda bi, i, j: (0, 0)),
                  pl.BlockSpec((d, PROJ_TN), lambda bi, i, j: (0, j))],
        out_specs=pl.BlockSpec((1, tm, PROJ_TN), lambda bi, i, j: (bi, i, j)),
        out_shape=jax.ShapeDtypeStruct((b, t, n), BF16),
        scratch_shapes=[pltpu.VMEM((tm, d), BF16)],
        compiler_params=_cparams(("arbitrary", "arbitrary", "arbitrary")),
        name="in_proj",
    )(x, mod, g_pre.reshape(1, d), w_in_bf)


def _cumsum_rows(x, reverse):
    groups = x.shape[0] // SUBLANES
    row = lax.broadcasted_iota(jnp.int32, (SUBLANES, x.shape[1]), 0)
    out = [None] * groups
    carry = None
    for g in (range(groups - 1, -1, -1) if reverse else range(groups)):
        blk = x[g * SUBLANES:(g + 1) * SUBLANES]
        step = 1
        while step < SUBLANES:
            if reverse:
                blk = blk + jnp.where(row < SUBLANES - step, pltpu.roll(blk, SUBLANES - step, axis=0), 0.0)
            else:
                blk = blk + jnp.where(row >= step, pltpu.roll(blk, step, axis=0), 0.0)
            step *= 2
        if carry is not None:
            blk = blk + carry
        carry = blk[0:1] if reverse else blk[SUBLANES - 1:SUBLANES]
        out[g] = blk
    return jnp.concatenate(out, axis=0)


def _hgrn_dir(q_ref, f_ref, i_ref, o_ref, s_ref, sn_ref, lbd, row0, reverse):
    c = HGRN_CHUNK
    rows = pl.ds(row0, c)
    q = q_ref[0, rows, :].astype(F32)
    fpre = f_ref[0, rows, :].astype(F32)
    v = i_ref[0, rows, :]
    f = 0.5 * (1.0 + lbd) + (0.5 * (1.0 - lbd)) * jnp.tanh(0.5 * fpre)
    logf = jnp.log(f)
    k = 1.0 - f
    ri = lax.broadcasted_iota(jnp.int32, (c, c), 0)
    ci = lax.broadcasted_iota(jnp.int32, (c, c), 1)
    keep = (ci >= ri) if reverse else (ri >= ci)
    b = _cumsum_rows(logf, reverse)
    if reverse:
        b_mid = b[c // 2:c // 2 + 1, :]
        b_end = b[0:1, :]
    else:
        b_mid = b[c // 2 - 1:c // 2, :]
        b_end = b[c - 1:c, :]
    d = b - b_mid
    qm = q * jnp.exp(d)
    km = k * jnp.exp(-d)
    qi = (qm * jnp.exp(b_mid)).astype(BF16)
    ks = (km * jnp.exp(b_end - b_mid)).astype(BF16)
    qm = qm.astype(BF16)
    km = km.astype(BF16)
    dec = jnp.exp(b_end)
    for h in range(HA_HEADS):
        sl = slice(h * HA_DK, (h + 1) * HA_DK)
        a = _dot_nt(qm[:, sl], km[:, sl])
        a = jnp.where(keep, a, 0.0).astype(BF16)
        lhs = jnp.concatenate([qi[:, sl], a], axis=1)
        rhs = jnp.concatenate([sn_ref[h], v[:, sl]], axis=0)
        o_ref[0, rows, sl] = _dot(lhs, rhs).astype(BF16)
        st = s_ref[h] * dec[:, sl] + _dot_tn(v[:, sl], ks[:, sl])
        s_ref[h] = st
        sn_ref[h] = st.astype(BF16).T


def _hgrn_tile(lbl_ref, qf_ref, ff_ref, if_ref, qb_ref, fb_ref, ib_ref, of_ref, ob_ref,
               sf_ref, sb_ref, snf_ref, snb_ref):
    @pl.when(pl.program_id(1) == 0)
    def _():
        for ref in (sf_ref, sb_ref, snf_ref, snb_ref):
            ref[...] = jnp.zeros_like(ref)

    l0 = lbl_ref[0]
    l1 = lbl_ref[1]
    m = jnp.maximum(l0, l1)
    e0 = jnp.exp(l0 - m)
    e1 = jnp.exp(l1 - m)
    lb = e0 / (e0 + e1)
    n_chunks = HGRN_TILE // HGRN_CHUNK

    def chunk(ci):
        _hgrn_dir(qf_ref, ff_ref, if_ref, of_ref, sf_ref, snf_ref, lb[0:1, :], ci * HGRN_CHUNK, False)
        _hgrn_dir(qb_ref, fb_ref, ib_ref, ob_ref, sb_ref, snb_ref, lb[1:2, :],
                  (n_chunks - 1 - ci) * HGRN_CHUNK, True)

    return chunk


def _mixers_kernel(lbl_ref, qf_ref, ff_ref, if_ref, qb_ref, fb_ref, ib_ref, nq_ref, nk_ref, nv_ref, tab_ref,
                   of_ref, ob_ref, on_ref, sf_ref, sb_ref, snf_ref, snb_ref, *, grid_rows):
    chunk = _hgrn_tile(lbl_ref, qf_ref, ff_ref, if_ref, qb_ref, fb_ref, ib_ref, of_ref, ob_ref,
                       sf_ref, sb_ref, snf_ref, snb_ref)
    na_row = _na_rows(nq_ref, nk_ref, nv_ref, tab_ref, on_ref, grid_rows)
    for ci in range(HGRN_TILE // HGRN_CHUNK):
        chunk(ci)
        na_row(ci)


def _mixers(proj, lb_logits, tabs):
    b, t, _ = proj.shape
    tt = HGRN_TILE
    nt = t // tt
    d = D_MODEL
    cq, ck, cv = (COL_NA // NA_W, COL_NA // NA_W + 1, COL_NA // NA_W + 2)
    seq_bufs = 2 if 4 * t * NA_W * 2 <= VMEM_LIMIT // 3 else 1

    def fwd(col):
        return pl.BlockSpec((1, tt, d), lambda bi, ti: (bi, ti, col))

    def bwd(col):
        return pl.BlockSpec((1, tt, d), lambda bi, ti: (bi, nt - 1 - ti, col))

    return pl.pallas_call(
        functools.partial(_mixers_kernel, grid_rows=t // GRID_W),
        grid=(b, nt),
        in_specs=[pl.BlockSpec((2, 2, d), lambda bi, ti: (0, 0, 0)),
                  fwd(0), fwd(1), fwd(3), bwd(0), bwd(2), bwd(3),
                  pl.BlockSpec((1, tt, NA_W), lambda bi, ti: (bi, ti, cq)),
                  pl.BlockSpec((1, t, NA_W), lambda bi, ti: (bi, 0, ck), pipeline_mode=pl.Buffered(seq_bufs)),
                  pl.BlockSpec((1, t, NA_W), lambda bi, ti: (bi, 0, cv), pipeline_mode=pl.Buffered(seq_bufs)),
                  pl.BlockSpec(tabs.shape, lambda bi, ti: (0, 0, 0, 0), pipeline_mode=pl.Buffered(1))],
        out_specs=[pl.BlockSpec((1, tt, d), lambda bi, ti: (bi, ti, 0)),
                   pl.BlockSpec((1, tt, d), lambda bi, ti: (bi, nt - 1 - ti, 0)),
                   pl.BlockSpec((1, tt, NA_W), lambda bi, ti: (bi, ti, 0))],
        out_shape=[jax.ShapeDtypeStruct((b, t, d), BF16), jax.ShapeDtypeStruct((b, t, d), BF16),
                   jax.ShapeDtypeStruct((b, t, NA_W), BF16)],
        scratch_shapes=[pltpu.VMEM((HA_HEADS, HA_DK, HA_DK), F32), pltpu.VMEM((HA_HEADS, HA_DK, HA_DK), F32),
                        pltpu.VMEM((HA_HEADS, HA_DK, HA_DK), BF16), pltpu.VMEM((HA_HEADS, HA_DK, HA_DK), BF16)],
        compiler_params=_cparams(("arbitrary", "arbitrary")),
        name="mixers",
    )(lb_logits, proj, proj, proj, proj, proj, proj, proj, proj, proj, tabs)


def _na_bias_tables(rpb):
    q = np.arange(GRID_W)[:, None]
    kc = np.arange(GRID_W)[None, :]
    c0 = np.clip(q - NA_KW // 2, 0, GRID_W - NA_KW)
    ok = (kc >= c0) & (kc < c0 + NA_KW)
    dc = np.clip(kc - q + (NA_KW - 1), 0, 2 * NA_KW - 2)
    rp = rpb.astype(F32)
    base = jnp.full(rp.shape[:2] + ok.shape, NEG_BIG, F32)
    for j in range(rp.shape[2]):
        base = jnp.where((ok & (dc == j))[None, None], rp[:, :, j, None, None], base)
    tabs = []
    for dr0 in range(NA_KH):
        w = base[:, dr0:dr0 + NA_KH]
        w = w.transpose(0, 2, 1, 3).reshape(NA_HEADS // 2, 2 * GRID_W, NA_KH * GRID_W)
        tabs.append(w)
    return jnp.stack(tabs, axis=1)


def _na_rows(q_ref, k_ref, v_ref, tab_ref, o_ref, grid_rows):
    g = pl.program_id(1)
    lane = lax.broadcasted_iota(jnp.int32, (GRID_W, 2 * NA_DH), 1)
    low = lane < NA_DH
    scale = NA_DH ** -0.5
    rows_per_step = HGRN_TILE // GRID_W

    def na_row(rr):
        r = g * rows_per_step + rr
        r0 = jnp.clip(r - NA_KH // 2, 0, grid_rows - NA_KH)
        dr0 = r0 - r + (NA_KH - 1)
        qrow = pl.ds(rr * GRID_W, GRID_W)
        krow = pl.ds(pl.multiple_of(r0 * GRID_W, GRID_W), NA_KH * GRID_W)
        for j in range(NA_HEADS // 2):
            sl = slice(j * 2 * NA_DH, (j + 1) * 2 * NA_DH)
            qp = q_ref[0, qrow, sl]
            zero = jnp.zeros_like(qp)
            ql = jnp.concatenate([jnp.where(low, qp, zero), jnp.where(low, zero, qp)], axis=0)
            s = _dot_nt(ql, k_ref[0, krow, sl]) * scale + tab_ref[j, dr0]
            m = jnp.max(s, axis=-1, keepdims=True)
            p = jnp.exp(s - m)
            l = jnp.sum(p, axis=-1, keepdims=True)
            pv = _dot(p.astype(BF16), v_ref[0, krow, sl]) / l
            o_ref[0, qrow, sl] = jnp.where(low, pv[:GRID_W], pv[GRID_W:]).astype(BF16)

    return na_row


def _mix_kernel(x_ref, mod_ref, of_ref, ob_ref, g_ref, ga_ref, gb_ref, nb_ref,
                gn_ref, gpm_ref, gpf_ref, wa_ref, wb_ref, wo_ref, rw_ref, rb_ref,
                x1_ref, xn_ref, ri_ref, rg_ref, cnt_ref, tri_ref, carry_ref):
    tm = MIX_TM
    first = (pl.program_id(0) == 0) & (pl.program_id(1) == 0)

    @pl.when(first)
    def _():
        r = lax.broadcasted_iota(jnp.int32, (tm, tm), 0)
        c = lax.broadcasted_iota(jnp.int32, (tm, tm), 1)
        tri_ref[...] = jnp.where(r < c, 1.0, 0.0).astype(BF16)
        carry_ref[...] = jnp.zeros_like(carry_ref)

    sc2 = mod_ref[0, 4:5, :]
    sh2 = mod_ref[0, 3:4, :]
    gt1 = mod_ref[0, 2:3, :]

    o = of_ref[0].astype(F32) + ob_ref[0].astype(F32)
    parts = []
    for h in range(HA_HEADS):
        oh = o[:, h * HA_DK:(h + 1) * HA_DK]
        parts.append(oh * lax.rsqrt(jnp.mean(oh * oh, axis=-1, keepdims=True) + RMS_EPS))
    g = g_ref[0].astype(F32)
    oa = jnp.concatenate(parts, axis=-1) * gn_ref[...] * (g * _sigmoid(g))
    ya = _dot(oa.astype(BF16), wa_ref[...])
    yb = _dot(nb_ref[0], wb_ref[...])
    mix = _sigmoid(ga_ref[0].astype(F32)) * ya + _sigmoid(gb_ref[0].astype(F32)) * yb
    mo = _dot(mix.astype(BF16), wo_ref[...])
    x1 = x_ref[0] + gt1 * _rms(mo, gpm_ref[...])
    x1_ref[0] = x1

    xn = _rms(x1, gpf_ref[...]) * (1.0 + sc2) + sh2
    xn_ref[0] = xn.reshape(tm, ROW_SUB, ROW_LANE)
    logits = _dot_nt(rw_ref[...], xn.astype(BF16)) + rb_ref[...]
    sub = lax.broadcasted_iota(jnp.int32, (N_EXPERTS, tm), 0)
    work = logits
    vals, idxs, hots = [], [], []
    for _ in range(TOP_K):
        mv = jnp.max(work, axis=0, keepdims=True)
        mi = jnp.min(jnp.where(work == mv, sub, N_EXPERTS), axis=0, keepdims=True)
        hot = sub == mi
        vals.append(mv)
        idxs.append(mi)
        hots.append(hot)
        work = jnp.where(hot, -jnp.inf, work)
    es = [jnp.exp(v - vals[0]) for v in vals]
    den = es[0] + es[1] + es[2] + es[3]

    cnt = sum(jnp.where(h, 1.0, 0.0) for h in hots)
    before = _dot(cnt.astype(BF16), tri_ref[...]) + carry_ref[...]
    carry_ref[...] = carry_ref[...] + jnp.sum(cnt, axis=1, keepdims=True)
    cnt_ref[...] = jnp.broadcast_to(carry_ref[...], cnt_ref.shape)

    ranks = [jnp.sum(jnp.where(hots[kk], before, 0.0), axis=0, keepdims=True).astype(jnp.int32)
             for kk in range(TOP_K)]
    ri_ref[...] = jnp.concatenate(idxs + ranks, axis=0)
    gates = jnp.concatenate([e / den for e in es] + [jnp.zeros((128 - TOP_K, tm), F32)], axis=0)
    rg_ref[0] = gates.T


def _mix(x, mod, proj, o_fw, o_bw, o_b, hgrn_norm, g_post_mix, g_pre_ffn, wa, wb, wo, rw, rb):
    b, t, d = x.shape
    tm = min(MIX_TM, t)
    assert tm == MIX_TM
    row = lambda a: a.reshape(1, -1)
    tok = lambda w, col: pl.BlockSpec((1, tm, w), lambda bi, i: (bi, i, col))
    full = lambda a: pl.BlockSpec(a.shape, lambda bi, i: (0,) * a.ndim)
    gn, gpm, gpf, rbr = row(hgrn_norm), row(g_post_mix), row(g_pre_ffn), rb.reshape(-1, 1)
    rw = rw.T
    outs = pl.pallas_call(
        _mix_kernel,
        grid=(b, t // tm),
        in_specs=[tok(d, 0),
                  pl.BlockSpec((1, 6, d), lambda bi, i: (bi, 0, 0)),
                  tok(d, 0), tok(d, 0),
                  tok(d, 4),
                  tok(d, COL_GATES // d), tok(d, COL_GATES // d + 1),
                  tok(NA_W, 0),
                  full(gn), full(gpm), full(gpf), full(wa), full(wb), full(wo), full(rw), full(rbr)],
        out_specs=[tok(d, 0), pl.BlockSpec((1, tm, ROW_SUB, ROW_LANE), lambda bi, i: (bi, i, 0, 0)),
                   pl.BlockSpec((8, tm), lambda bi, i: (0, bi * (t // tm) + i)), tok(128, 0),
                   pl.BlockSpec((N_EXPERTS, 128), lambda bi, i: (0, 0))],
        out_shape=[jax.ShapeDtypeStruct((b, t, d), F32), jax.ShapeDtypeStruct((b, t, ROW_SUB, ROW_LANE), F32),
                   jax.ShapeDtypeStruct((8, b * t), jnp.int32), jax.ShapeDtypeStruct((b, t, 128), F32),
                   jax.ShapeDtypeStruct((N_EXPERTS, 128), F32)],
        scratch_shapes=[pltpu.VMEM((tm, tm), BF16), pltpu.VMEM((N_EXPERTS, 1), F32)],
        compiler_params=_cparams(("arbitrary", "arbitrary")),
        name="mix",
    )(x, mod, o_fw, o_bw, proj, proj, proj, o_b, gn, gpm, gpf, wa, wb, wo, rw, rbr)
    return outs


SC_CORES, SC_SUBCORES = 2, 16
SC_ROWS = 32


def _scatter_rows(xn, dest_t, n_rows):
    n_tok = xn.shape[0]
    workers = SC_CORES * SC_SUBCORES
    per_w = n_tok // workers
    n_it = per_w // SC_ROWS
    assert per_w * workers == n_tok and n_it * SC_ROWS == per_w and n_it % 2 == 0
    idx = dest_t.reshape(TOP_K, n_tok // SC_ROWS, SC_ROWS)
    mesh = plsc.VectorSubcoreMesh(core_axis_name="c", subcore_axis_name="s")

    @functools.partial(
        pl.kernel, mesh=mesh,
        out_type=jax.ShapeDtypeStruct((n_rows, ROW_SUB, ROW_LANE), xn.dtype),
        scratch_types=[pltpu.VMEM((2, TOP_K, SC_ROWS), jnp.int32),
                       pltpu.VMEM((2, SC_ROWS, ROW_SUB, ROW_LANE), xn.dtype),
                       pltpu.SemaphoreType.DMA((2,)), pltpu.SemaphoreType.DMA((2,))],
    )
    def scatter_kernel(x_hbm, idx_hbm, out_hbm, idx_v, rows_v, sem_load, sem_store):
        chunk0 = (lax.axis_index("s") * SC_CORES + lax.axis_index("c")) * n_it

        def load(g, slot):
            tok = pl.ds(pl.multiple_of((chunk0 + g) * SC_ROWS, SC_ROWS), SC_ROWS)
            return pltpu.make_async_copy(x_hbm.at[tok], rows_v.at[slot], sem_load.at[slot])

        def start_load(g, slot):
            load(g, slot).start()
            for kk in range(TOP_K):
                pltpu.sync_copy(idx_hbm.at[kk, chunk0 + g], idx_v.at[slot, kk])

        def store(slot, kk):
            return pltpu.make_async_copy(rows_v.at[slot], out_hbm.at[idx_v.at[slot, kk]], sem_store.at[slot])

        def scatter(g, slot):
            load(g, slot).wait()
            for kk in range(TOP_K):
                store(slot, kk).start()
            for kk in range(TOP_K):
                store(slot, kk).wait()

        start_load(0, 0)

        @pl.loop(0, n_it, step=2)
        def _(g):
            start_load(g + 1, 1)
            scatter(g, 0)

            @pl.when(g + 2 < n_it)
            def _():
                start_load(g + 2, 0)

            scatter(g + 1, 1)

    return scatter_kernel(xn, idx)


GU_SLAB = 256


def _gate_up_layout_kernel(w_ref, o_ref):
    half = GU_SLAB // 2
    r = lax.broadcasted_iota(jnp.int32, (GU_SLAB, GU_SLAB), 0)
    c = lax.broadcasted_iota(jnp.int32, (GU_SLAB, GU_SLAB), 1)
    src = jnp.where(c < half, 2 * c, 2 * (c - half) + 1)
    pm = jnp.where(r == src, 1.0, 0.0).astype(BF16)
    for s in range(w_ref.shape[2] // GU_SLAB):
        cols = slice(s * GU_SLAB, (s + 1) * GU_SLAB)
        o_ref[0, :, cols] = _dot(w_ref[0, :, cols].astype(BF16), pm).astype(BF16)


def _gate_up_layout(w_gate_up, b_gate_up):
    e, k, n2 = w_gate_up.shape
    half = GU_SLAB // 2
    tk = 512
    w = pl.pallas_call(
        _gate_up_layout_kernel,
        grid=(e, k // tk),
        in_specs=[pl.BlockSpec((1, tk, n2), lambda i, j: (i, j, 0))],
        out_specs=pl.BlockSpec((1, tk, n2), lambda i, j: (i, j, 0)),
        out_shape=jax.ShapeDtypeStruct((e, k, n2), BF16),
        compiler_params=_cparams(("arbitrary", "arbitrary")),
        name="gate_up_layout",
    )(w_gate_up)
    b = b_gate_up.reshape(e, n2 // GU_SLAB, half, 2).transpose(0, 1, 3, 2).reshape(e, 1, n2)
    return w, b


def _expert_block(bv_ref, a_ref, idx, x_ref, wgu_ref, bgu_ref, wd_ref, bd_ref, o_ref):
    valid = bv_ref[idx[0]]
    half = GU_SLAB // 2

    @pl.when(valid > 0)
    def _():
        blk = x_ref.shape[0]
        rowi = lax.broadcasted_iota(jnp.int32, (blk, 1), 0)
        x = x_ref[...].reshape(blk, D_MODEL)
        x = jnp.where(rowi < valid, x, 0.0).astype(BF16)
        for s in range(2 * D_FF // GU_SLAB):
            cols = slice(s * GU_SLAB, (s + 1) * GU_SLAB)
            h = _dot(x, wgu_ref[0, :, cols]) + bgu_ref[0, :, cols]
            glu = jnp.minimum(h[:, :half], SWIGLU_LIMIT)
            lin = jnp.clip(h[:, half:], -SWIGLU_LIMIT, SWIGLU_LIMIT)
            a = glu * _sigmoid(SWIGLU_ALPHA * glu) * (lin + 1.0)
            a_ref[:, s * half:(s + 1) * half] = a.astype(BF16)
        o_ref[...] = (_dot(a_ref[...], wd_ref[0].astype(BF16)) + bd_ref[0]).reshape(o_ref.shape)

    @pl.when(valid <= 0)
    def _():
        o_ref[...] = jnp.zeros_like(o_ref)


def _experts(xs, block_e, block_valid, wgu, bgu, wd, bd):
    n_rows = xs.shape[0]
    d = D_MODEL
    n_blocks = block_e.shape[0]
    blk = n_rows // n_blocks

    def outer(be_ref, bv_ref, xs_hbm, wgu_hbm, bgu_hbm, wd_hbm, bd_hbm, y_hbm, a_ref):
        emap = lambda i: (be_ref[i], 0, 0)
        ahead = lambda: pl.Buffered(2, use_lookahead=True)
        pltpu.emit_pipeline(
            functools.partial(_expert_block, bv_ref, a_ref),
            grid=(n_blocks,),
            in_specs=[pl.BlockSpec((blk, ROW_SUB, ROW_LANE), lambda i: (i, 0, 0), pipeline_mode=pl.Buffered(3)),
                      pl.BlockSpec((1, d, 2 * D_FF), emap, pipeline_mode=ahead()),
                      pl.BlockSpec((1, 1, 2 * D_FF), emap, pipeline_mode=ahead()),
                      pl.BlockSpec((1, D_FF, d), emap, pipeline_mode=ahead()),
                      pl.BlockSpec((1, 1, d), emap, pipeline_mode=ahead())],
            out_specs=[pl.BlockSpec((blk, ROW_SUB, ROW_LANE), lambda i: (i, 0, 0))],
            _explicit_indices=True,
        )(xs_hbm, wgu_hbm, bgu_hbm, wd_hbm, bd_hbm, y_hbm)

    smem = lambda: pl.BlockSpec(memory_space=pltpu.SMEM)
    hbm = lambda: pl.BlockSpec(memory_space=pl.ANY)
    return pl.pallas_call(
        outer,
        in_specs=[smem(), smem(), hbm(), hbm(), hbm(), hbm(), hbm()],
        out_specs=hbm(),
        out_shape=jax.ShapeDtypeStruct((n_rows, ROW_SUB, ROW_LANE), F32),
        scratch_shapes=[pltpu.VMEM((blk, D_FF), BF16)],
        compiler_params=pltpu.CompilerParams(vmem_limit_bytes=VMEM_LIMIT),
        name="experts",
    )(block_e, block_valid, xs, wgu, bgu, wd, bd)


def _gather_rows(y, dest_flat):
    n_out = dest_flat.shape[0]
    workers = SC_CORES * SC_SUBCORES
    per_w = n_out // workers
    n_it = per_w // SC_ROWS
    assert per_w * workers == n_out and n_it * SC_ROWS == per_w and n_it % 2 == 0
    mesh = plsc.VectorSubcoreMesh(core_axis_name="c", subcore_axis_name="s")

    @functools.partial(
        pl.kernel, mesh=mesh,
        out_type=jax.ShapeDtypeStruct((n_out, ROW_SUB, ROW_LANE), y.dtype),
        scratch_types=[pltpu.VMEM((2, SC_ROWS), jnp.int32),
                       pltpu.VMEM((2, SC_ROWS, ROW_SUB, ROW_LANE), y.dtype),
                       pltpu.SemaphoreType.DMA((2,))],
    )
    def gather_kernel(y_hbm, idx_hbm, out_hbm, idx_v, rows_v, sem):
        base = (lax.axis_index("s") * SC_CORES + lax.axis_index("c")) * per_w

        def span(g):
            return pl.ds(pl.multiple_of(base + g * SC_ROWS, SC_ROWS), SC_ROWS)

        def row_gather(slot):
            return pltpu.make_async_copy(y_hbm.at[idx_v.at[slot]], rows_v.at[slot], sem.at[slot])

        def start(g, slot):
            pltpu.sync_copy(idx_hbm.at[span(g)], idx_v.at[slot])
            row_gather(slot).start()

        start(0, 0)

        @pl.loop(0, n_it, step=2)
        def _(g):
            start(g + 1, 1)
            row_gather(0).wait()
            pltpu.sync_copy(rows_v.at[0], out_hbm.at[span(g)])

            @pl.when(g + 2 < n_it)
            def _():
                start(g + 2, 0)

            row_gather(1).wait()
            pltpu.sync_copy(rows_v.at[1], out_hbm.at[span(g + 1)])

    return gather_kernel(y, dest_flat)


def _combine_kernel(yg_ref, x1_ref, mod_ref, rg_ref, gpo_ref, o_ref):
    rg = rg_ref[...]
    h = rg[:, 0:1] * yg_ref[0].reshape(COMB_TOK, D_MODEL)
    for kk in range(1, TOP_K):
        h = h + rg[:, kk:kk + 1] * yg_ref[kk].reshape(COMB_TOK, D_MODEL)
    gt2 = mod_ref[0, 5:6, :]
    o_ref[...] = x1_ref[...] + gt2 * _rms(h, gpo_ref[...])


def _combine(yg, x1, mod, rg, g_post_ffn, tokens_per_batch):
    n, d = x1.shape
    per_b = tokens_per_batch // COMB_TOK
    gpo = g_post_ffn.reshape(1, d)
    return pl.pallas_call(
        _combine_kernel,
        grid=(n // COMB_TOK,),
        in_specs=[pl.BlockSpec((TOP_K, COMB_TOK, ROW_SUB, ROW_LANE), lambda i: (0, i, 0, 0)),
                  pl.BlockSpec((COMB_TOK, d), lambda i: (i, 0)),
                  pl.BlockSpec((1, 6, d), lambda i: (i // per_b, 0, 0)),
                  pl.BlockSpec((COMB_TOK, 128), lambda i: (i, 0)),
                  pl.BlockSpec((1, d), lambda i: (0, 0))],
        out_specs=pl.BlockSpec((COMB_TOK, d), lambda i: (i, 0)),
        out_shape=jax.ShapeDtypeStruct((n, d), F32),
        compiler_params=_cparams(("arbitrary",)),
        name="combine",
    )(yg, x1, mod, rg, gpo)


def _moe_plan(ri_t, counts, n_tok):
    idx = ri_t[:TOP_K]
    rank = ri_t[TOP_K:2 * TOP_K]
    cnt = counts.astype(jnp.int32)
    blk = MOE_BLK_LARGE if n_tok * TOP_K >= 8 * MOE_BLK_LARGE * N_EXPERTS else MOE_BLK_SMALL
    padded = (cnt + blk - 1) // blk * blk
    pend = jnp.cumsum(padded)
    pstart = pend - padded
    dest = rank
    for e in range(N_EXPERTS):
        dest = dest + jnp.where(idx == e, pstart[e], 0)
    n_blocks = (n_tok * TOP_K + blk - 1) // blk + N_EXPERTS
    bstart = jnp.arange(n_blocks, dtype=jnp.int32) * blk
    block_e = jnp.minimum(jnp.sum((bstart[:, None] >= pend[None, :]).astype(jnp.int32), axis=1), N_EXPERTS - 1)
    mine = jnp.arange(N_EXPERTS, dtype=jnp.int32)[None, :] == block_e[:, None]
    row_end = jnp.sum(jnp.where(mine, (pstart + cnt)[None, :], 0), axis=1)
    block_valid = jnp.clip(row_end - bstart, 0, blk).astype(jnp.int32)
    return dest.astype(jnp.int32), block_e, block_valid, n_blocks * blk


def _token_mixing(x, mod, p):
    b, t, d = x.shape
    n = b * t
    proj = _in_proj(x, mod, p['g_pre_mix'], p['w_in'])
    o_fw, o_bw, o_b = _mixers(proj, p['lb_logits'], p['na_tabs'])
    x1, xn, ri, rg, counts = _mix(x, mod, proj, o_fw, o_bw, o_b, p['hgrn_norm'], p['g_post_mix'], p['g_pre_ffn'],
                                  p['w_branch_a'], p['w_branch_b'], p['w_out'], p['router_w'], p['router_b'])
    dest, block_e, block_valid, n_rows = _moe_plan(ri, counts[:, 0], n)
    xs = _scatter_rows(xn.reshape(n, ROW_SUB, ROW_LANE), dest, n_rows)
    return dict(x1=x1.reshape(n, d), rg=rg.reshape(n, 128), dest=dest, block_e=block_e, block_valid=block_valid,
                xs=xs, mod=mod, shape=(b, t, d))


def _expert_rows(s, p):
    return _experts(s['xs'], s['block_e'], s['block_valid'], p['w_gu'], p['b_gu'], p['w_d'], p['b_d'])


def _finish(s, y, p):
    b, t, d = s['shape']
    yg = _gather_rows(y, s['dest'].reshape(-1)).reshape(TOP_K, b * t, ROW_SUB, ROW_LANE)
    return _combine(yg, s['x1'], s['mod'], s['rg'], p['g_post_ffn'], t).reshape(b, t, d)


def _after(first, then):
    return lax.optimization_barrier((first, then))


def kernel(x_prompt, x_sample, c_prompt, c_sample, ada_w, ada_b, g_pre_mix, g_post_mix, g_pre_ffn, g_post_ffn,
           w_in, hgrn_lb_logits, hgrn_norm, na_rpb, w_branch_a, w_branch_b, w_out, router_w, router_b,
           w_gate_up, b_gate_up, w_down, b_down):
    d = D_MODEL
    bp, bs = c_prompt.shape[0], c_sample.shape[0]
    assert ada_w.shape[0] == 1 and hgrn_lb_logits.shape[0] == 2, "one layer: two lower-bound slots"
    for xg in (x_prompt, x_sample):
        assert xg.shape[2] == d and xg.shape[1] % HGRN_TILE == 0 and xg.shape[1] // GRID_W >= NA_KH
    pad = (-(bp + bs)) % 8
    c_all = jnp.concatenate([c_prompt, c_sample, jnp.zeros((pad, d), F32)], axis=0)
    mod = _ada_mod(c_all, ada_w[0], ada_b[0]).reshape(-1, 6, d)

    w = w_in[0]
    w_perm = jnp.concatenate([w[:, :5 * d], w[:, 5 * d + 3 * NA_W:], w[:, 5 * d:5 * d + 3 * NA_W]], axis=1).astype(BF16)
    w_gu, b_gu = _gate_up_layout(w_gate_up[0], b_gate_up[0])
    p = dict(
        g_pre_mix=g_pre_mix[0], g_post_mix=g_post_mix[0], g_pre_ffn=g_pre_ffn[0], g_post_ffn=g_post_ffn[0],
        w_in=w_perm, lb_logits=hgrn_lb_logits.astype(F32), hgrn_norm=hgrn_norm[0],
        na_tabs=_na_bias_tables(na_rpb[0]),
        w_branch_a=w_branch_a[0].astype(BF16), w_branch_b=w_branch_b[0].astype(BF16), w_out=w_out[0].astype(BF16),
        router_w=router_w[0].astype(BF16), router_b=router_b[0],
        w_gu=w_gu, b_gu=b_gu,
        w_d=w_down[0], b_d=b_down[0][:, None, :],
    )
    sp = _token_mixing(x_prompt, mod[:bp], p)
    sp['dest'], x_sample = _after(sp['dest'], x_sample)
    ss = _token_mixing(x_sample, mod[bp:bp + bs], p)
    ss['x1'], sp['xs'] = _after(ss['x1'], sp['xs'])
    y_p = _expert_rows(sp, p)
    y_p, ss['xs'] = _after(y_p, ss['xs'])
    y_s = _expert_rows(ss, p)
    out_p = _finish(sp, y_p, p)
    out_s = _finish(ss, y_s, p)
    return (out_p, out_s)
```

```python
import functools

import numpy as np
import jax
import jax.numpy as jnp
from jax import lax
from jax.experimental import pallas as pl
from jax.experimental.pallas import tpu as pltpu
from jax.experimental.pallas import tpu_sc as plsc

F32 = jnp.float32
BF16 = jnp.bfloat16

D_MODEL = 1024
GRID_W = 64
HA_HEADS = 8
HA_DK = 128
NA_HEADS = 8
NA_DH = 64
NA_W = NA_HEADS * NA_DH
NA_KH = 8
NA_KW = 16
N_EXPERTS = 32
TOP_K = 4
D_FF = 1024
SWIGLU_ALPHA = 1.702
SWIGLU_LIMIT = 7.0
RMS_EPS = 1e-6
NEG_BIG = -1e30

COL_A = 0
COL_GATES = 5 * 1024
COL_NA = 7 * 1024
PROJ_COLS = 7 * 1024 + 3 * NA_W

HGRN_CHUNK = 64
HGRN_TILE = 512
PROJ_TM = 2048
PROJ_TN = 2176
MIX_TM = 512
MOE_BLK_SMALL, MOE_BLK_LARGE = 512, 1024
COMB_TOK = 512
SUBLANES = 8
ROW_SUB, ROW_LANE = 8, 128
VMEM_LIMIT = 56 * 1024 * 1024


def _cparams(sem):
    return pltpu.CompilerParams(dimension_semantics=sem, vmem_limit_bytes=VMEM_LIMIT)


def _dot(a, b):
    return jnp.dot(a, b, preferred_element_type=F32)


def _dot_nt(a, b):
    return lax.dot_general(a, b, (((1,), (1,)), ((), ())), preferred_element_type=F32)


def _dot_tn(a, b):
    return lax.dot_general(a, b, (((0,), (0,)), ((), ())), preferred_element_type=F32)


def _split(a):
    hi = a.astype(BF16)
    lo = (a - hi.astype(F32)).astype(BF16)
    return hi, lo


def _dot3(a, w):
    ah, al = _split(a)
    wh, wl = _split(w)
    return _dot(ah, wh) + (_dot(al, wh) + _dot(ah, wl))


def _sigmoid(x):
    return 0.5 * jnp.tanh(0.5 * x) + 0.5


def _rms(x, g):
    return x * lax.rsqrt(jnp.mean(x * x, axis=-1, keepdims=True) + RMS_EPS) * g


def _ada_kernel(c_ref, w_ref, b_ref, o_ref):
    c = c_ref[...]
    o_ref[...] = _dot3(c * _sigmoid(c), w_ref[...]) + b_ref[...]


def _ada_mod(c, ada_w, ada_b):
    bp, d = c.shape
    n = ada_w.shape[1]
    tn = 1536
    return pl.pallas_call(
        _ada_kernel,
        grid=(n // tn,),
        in_specs=[pl.BlockSpec((bp, d), lambda j: (0, 0)),
                  pl.BlockSpec((d, tn), lambda j: (0, j)),
                  pl.BlockSpec((1, tn), lambda j: (0, j))],
        out_specs=pl.BlockSpec((bp, tn), lambda j: (0, j)),
        out_shape=jax.ShapeDtypeStruct((bp, n), F32),
        compiler_params=_cparams(("arbitrary",)),
        name="ada_mod",
    )(c, ada_w, ada_b.reshape(1, n))


def _in_proj_kernel(x_ref, mod_ref, g_ref, w_ref, o_ref, xn_ref):
    @pl.when(pl.program_id(2) == 0)
    def _():
        sh = mod_ref[0, 0:1, :]
        sc = mod_ref[0, 1:2, :]
        xn = _rms(x_ref[0], g_ref[...]) * (1.0 + sc) + sh
        xn_ref[...] = xn.astype(BF16)

    o_ref[0] = _dot(xn_ref[...], w_ref[...]).astype(BF16)


def _in_proj(x, mod, g_pre, w_in_bf):
    b, t, d = x.shape
    tm = min(PROJ_TM, t)
    n = w_in_bf.shape[1]
    return pl.pallas_call(
        _in_proj_kernel,
        grid=(b, t // tm, n // PROJ_TN),
        in_specs=[pl.BlockSpec((1, tm, d), lambda bi, i, j: (bi, i, 0)),
                  pl.BlockSpec((1, 6, d), lambda bi, i, j: (bi, 0, 0)),
                  pl.BlockSpec((1, d), lambda bi, i, j: (0, 0)),
                  pl.BlockSpec((d, PROJ_TN), lambda bi, i, j: (0, j))],
        out_specs=pl.BlockSpec((1, tm, PROJ_TN), lambda bi, i, j: (bi, i, j)),
        out_shape=jax.ShapeDtypeStruct((b, t, n), BF16),
        scratch_shapes=[pltpu.VMEM((tm, d), BF16)],
        compiler_params=_cparams(("arbitrary", "arbitrary", "arbitrary")),
        name="in_proj",
    )(x, mod, g_pre.reshape(1, d), w_in_bf)


def _cumsum_rows(x, reverse):
    groups = x.shape[0] // SUBLANES
    row = lax.broadcasted_iota(jnp.int32, (SUBLANES, x.shape[1]), 0)
    out = [None] * groups
    carry = None
    for g in (range(groups - 1, -1, -1) if reverse else range(groups)):
        blk = x[g * SUBLANES:(g + 1) * SUBLANES]
        step = 1
        while step < SUBLANES:
            if reverse:
                blk = blk + jnp.where(row < SUBLANES - step, pltpu.roll(blk, SUBLANES - step, axis=0), 0.0)
            else:
                blk = blk + jnp.where(row >= step, pltpu.roll(blk, step, axis=0), 0.0)
            step *= 2
        if carry is not None:
            blk = blk + carry
        carry = blk[0:1] if reverse else blk[SUBLANES - 1:SUBLANES]
        out[g] = blk
    return jnp.concatenate(out, axis=0)


def _hgrn_dir(q_ref, f_ref, i_ref, o_ref, s_ref, sn_ref, lbd, row0, reverse):
    c = HGRN_CHUNK
    rows = pl.ds(row0, c)
    q = q_ref[0, rows, :].astype(F32)
    fpre = f_ref[0, rows, :].astype(F32)
    v = i_ref[0, rows, :]
    f = 0.5 * (1.0 + lbd) + (0.5 * (1.0 - lbd)) * jnp.tanh(0.5 * fpre)
    logf = jnp.log(f)
    k = 1.0 - f
    ri = lax.broadcasted_iota(jnp.int32, (c, c), 0)
    ci = lax.broadcasted_iota(jnp.int32, (c, c), 1)
    keep = (ci >= ri) if reverse else (ri >= ci)
    b = _cumsum_rows(logf, reverse)
    if reverse:
        b_mid = b[c // 2:c // 2 + 1, :]
        b_end = b[0:1, :]
    else:
        b_mid = b[c // 2 - 1:c // 2, :]
        b_end = b[c - 1:c, :]
    d = b - b_mid
    qm = q * jnp.exp(d)
    km = k * jnp.exp(-d)
    qi = (qm * jnp.exp(b_mid)).astype(BF16)
    ks = (km * jnp.exp(b_end - b_mid)).astype(BF16)
    qm = qm.astype(BF16)
    km = km.astype(BF16)
    dec = jnp.exp(b_end)
    for h in range(HA_HEADS):
        sl = slice(h * HA_DK, (h + 1) * HA_DK)
        a = _dot_nt(qm[:, sl], km[:, sl])
        a = jnp.where(keep, a, 0.0).astype(BF16)
        lhs = jnp.concatenate([qi[:, sl], a], axis=1)
        rhs = jnp.concatenate([sn_ref[h], v[:, sl]], axis=0)
        o_ref[0, rows, sl] = _dot(lhs, rhs).astype(BF16)
        st = s_ref[h] * dec[:, sl] + _dot_tn(v[:, sl], ks[:, sl])
        s_ref[h] = st
        sn_ref[h] = st.astype(BF16).T


def _hgrn_tile(lbl_ref, qf_ref, ff_ref, if_ref, qb_ref, fb_ref, ib_ref, of_ref, ob_ref,
               sf_ref, sb_ref, snf_ref, snb_ref):
    @pl.when(pl.program_id(1) == 0)
    def _():
        for ref in (sf_ref, sb_ref, snf_ref, snb_ref):
            ref[...] = jnp.zeros_like(ref)

    l0 = lbl_ref[0]
    l1 = lbl_ref[1]
    m = jnp.maximum(l0, l1)
    e0 = jnp.exp(l0 - m)
    e1 = jnp.exp(l1 - m)
    lb = e0 / (e0 + e1)
    n_chunks = HGRN_TILE // HGRN_CHUNK

    def chunk(ci):
        _hgrn_dir(qf_ref, ff_ref, if_ref, of_ref, sf_ref, snf_ref, lb[0:1, :], ci * HGRN_CHUNK, False)
        _hgrn_dir(qb_ref, fb_ref, ib_ref, ob_ref, sb_ref, snb_ref, lb[1:2, :],
                  (n_chunks - 1 - ci) * HGRN_CHUNK, True)

    return chunk


def _mixers_kernel(lbl_ref, qf_ref, ff_ref, if_ref, qb_ref, fb_ref, ib_ref, nq_ref, nk_ref, nv_ref, tab_ref,
                   of_ref, ob_ref, on_ref, sf_ref, sb_ref, snf_ref, snb_ref, *, grid_rows):
    chunk = _hgrn_tile(lbl_ref, qf_ref, ff_ref, if_ref, qb_ref, fb_ref, ib_ref, of_ref, ob_ref,
                       sf_ref, sb_ref, snf_ref, snb_ref)
    na_row = _na_rows(nq_ref, nk_ref, nv_ref, tab_ref, on_ref, grid_rows)
    for ci in range(HGRN_TILE // HGRN_CHUNK):
        chunk(ci)
        na_row(ci)


def _mixers(proj, lb_logits, tabs):
    b, t, _ = proj.shape
    tt = HGRN_TILE
    nt = t // tt
    d = D_MODEL
    cq, ck, cv = (COL_NA // NA_W, COL_NA // NA_W + 1, COL_NA // NA_W + 2)
    seq_bufs = 2 if 4 * t * NA_W * 2 <= VMEM_LIMIT // 3 else 1

    def fwd(col):
        return pl.BlockSpec((1, tt, d), lambda bi, ti: (bi, ti, col))

    def bwd(col):
        return pl.BlockSpec((1, tt, d), lambda bi, ti: (bi, nt - 1 - ti, col))

    return pl.pallas_call(
        functools.partial(_mixers_kernel, grid_rows=t // GRID_W),
        grid=(b, nt),
        in_specs=[pl.BlockSpec((2, 2, d), lambda bi, ti: (0, 0, 0)),
                  fwd(0), fwd(1), fwd(3), bwd(0), bwd(2), bwd(3),
                  pl.BlockSpec((1, tt, NA_W), lambda bi, ti: (bi, ti, cq)),
                  pl.BlockSpec((1, t, NA_W), lambda bi, ti: (bi, 0, ck), pipeline_mode=pl.Buffered(seq_bufs)),
                  pl.BlockSpec((1, t, NA_W), lambda bi, ti: (bi, 0, cv), pipeline_mode=pl.Buffered(seq_bufs)),
                  pl.BlockSpec(tabs.shape, lambda bi, ti: (0, 0, 0, 0), pipeline_mode=pl.Buffered(1))],
        out_specs=[pl.BlockSpec((1, tt, d), lambda bi, ti: (bi, ti, 0)),
                   pl.BlockSpec((1, tt, d), lambda bi, ti: (bi, nt - 1 - ti, 0)),
                   pl.BlockSpec((1, tt, NA_W), lambda bi, ti: (bi, ti, 0))],
        out_shape=[jax.ShapeDtypeStruct((b, t, d), BF16), jax.ShapeDtypeStruct((b, t, d), BF16),
                   jax.ShapeDtypeStruct((b, t, NA_W), BF16)],
        scratch_shapes=[pltpu.VMEM((HA_HEADS, HA_DK, HA_DK), F32), pltpu.VMEM((HA_HEADS, HA_DK, HA_DK), F32),
                        pltpu.VMEM((HA_HEADS, HA_DK, HA_DK), BF16), pltpu.VMEM((HA_HEADS, HA_DK, HA_DK), BF16)],
        compiler_params=_cparams(("arbitrary", "arbitrary")),
        name="mixers",
    )(lb_logits, proj, proj, proj, proj, proj, proj, proj, proj, proj, tabs)


def _na_bias_tables(rpb):
    q = np.arange(GRID_W)[:, None]
    kc = np.arange(GRID_W)[None, :]
    c0 = np.clip(q - NA_KW // 2, 0, GRID_W - NA_KW)
    ok = (kc >= c0) & (kc < c0 + NA_KW)
    dc = np.clip(kc - q + (NA_KW - 1), 0, 2 * NA_KW - 2)
    rp = rpb.astype(F32)
    base = jnp.full(rp.shape[:2] + ok.shape, NEG_BIG, F32)
    for j in range(rp.shape[2]):
        base = jnp.where((ok & (dc == j))[None, None], rp[:, :, j, None, None], base)
    tabs = []
    for dr0 in range(NA_KH):
        w = base[:, dr0:dr0 + NA_KH]
        w = w.transpose(0, 2, 1, 3).reshape(NA_HEADS // 2, 2 * GRID_W, NA_KH * GRID_W)
        tabs.append(w)
    return jnp.stack(tabs, axis=1)


def _na_rows(q_ref, k_ref, v_ref, tab_ref, o_ref, grid_rows):
    g = pl.program_id(1)
    lane = lax.broadcasted_iota(jnp.int32, (GRID_W, 2 * NA_DH), 1)
    low = lane < NA_DH
    scale = NA_DH ** -0.5
    rows_per_step = HGRN_TILE // GRID_W

    def na_row(rr):
        r = g * rows_per_step + rr
        r0 = jnp.clip(r - NA_KH // 2, 0, grid_rows - NA_KH)
        dr0 = r0 - r + (NA_KH - 1)
        qrow = pl.ds(rr * GRID_W, GRID_W)
        krow = pl.ds(pl.multiple_of(r0 * GRID_W, GRID_W), NA_KH * GRID_W)
        for j in range(NA_HEADS // 2):
            sl = slice(j * 2 * NA_DH, (j + 1) * 2 * NA_DH)
            qp = q_ref[0, qrow, sl]
            zero = jnp.zeros_like(qp)
            ql = jnp.concatenate([jnp.where(low, qp, zero), jnp.where(low, zero, qp)], axis=0)
            s = _dot_nt(ql, k_ref[0, krow, sl]) * scale + tab_ref[j, dr0]
            m = jnp.max(s, axis=-1, keepdims=True)
            p = jnp.exp(s - m)
            l = jnp.sum(p, axis=-1, keepdims=True)
            pv = _dot(p.astype(BF16), v_ref[0, krow, sl]) / l
            o_ref[0, qrow, sl] = jnp.where(low, pv[:GRID_W], pv[GRID_W:]).astype(BF16)

    return na_row


def _mix_kernel(x_ref, mod_ref, of_ref, ob_ref, g_ref, ga_ref, gb_ref, nb_ref,
                gn_ref, gpm_ref, gpf_ref, wa_ref, wb_ref, wo_ref, rw_ref, rb_ref,
                x1_ref, xn_ref, ri_ref, rg_ref, cnt_ref, tri_ref, carry_ref):
    tm = MIX_TM
    first = (pl.program_id(0) == 0) & (pl.program_id(1) == 0)

    @pl.when(first)
    def _():
        r = lax.broadcasted_iota(jnp.int32, (tm, tm), 0)
        c = lax.broadcasted_iota(jnp.int32, (tm, tm), 1)
        tri_ref[...] = jnp.where(r < c, 1.0, 0.0).astype(BF16)
        carry_ref[...] = jnp.zeros_like(carry_ref)

    sc2 = mod_ref[0, 4:5, :]
    sh2 = mod_ref[0, 3:4, :]
    gt1 = mod_ref[0, 2:3, :]

    o = of_ref[0].astype(F32) + ob_ref[0].astype(F32)
    parts = []
    for h in range(HA_HEADS):
        oh = o[:, h * HA_DK:(h + 1) * HA_DK]
        parts.append(oh * lax.rsqrt(jnp.mean(oh * oh, axis=-1, keepdims=True) + RMS_EPS))
    g = g_ref[0].astype(F32)
    oa = jnp.concatenate(parts, axis=-1) * gn_ref[...] * (g * _sigmoid(g))
    ya = _dot(oa.astype(BF16), wa_ref[...])
    yb = _dot(nb_ref[0], wb_ref[...])
    mix = _sigmoid(ga_ref[0].astype(F32)) * ya + _sigmoid(gb_ref[0].astype(F32)) * yb
    mo = _dot(mix.astype(BF16), wo_ref[...])
    x1 = x_ref[0] + gt1 * _rms(mo, gpm_ref[...])
    x1_ref[0] = x1

    xn = _rms(x1, gpf_ref[...]) * (1.0 + sc2) + sh2
    xn_ref[0] = xn.reshape(tm, ROW_SUB, ROW_LANE)
    logits = _dot_nt(rw_ref[...], xn.astype(BF16)) + rb_ref[...]
    sub = lax.broadcasted_iota(jnp.int32, (N_EXPERTS, tm), 0)
    work = logits
    vals, idxs, hots = [], [], []
    for _ in range(TOP_K):
        mv = jnp.max(work, axis=0, keepdims=True)
        mi = jnp.min(jnp.where(work == mv, sub, N_EXPERTS), axis=0, keepdims=True)
        hot = sub == mi
        vals.append(mv)
        idxs.append(mi)
        hots.append(hot)
        work = jnp.where(hot, -jnp.inf, work)
    es = [jnp.exp(v - vals[0]) for v in vals]
    den = es[0] + es[1] + es[2] + es[3]

    cnt = sum(jnp.where(h, 1.0, 0.0) for h in hots)
    before = _dot(cnt.astype(BF16), tri_ref[...]) + carry_ref[...]
    carry_ref[...] = carry_ref[...] + jnp.sum(cnt, axis=1, keepdims=True)
    cnt_ref[...] = jnp.broadcast_to(carry_ref[...], cnt_ref.shape)

    ranks = [jnp.sum(jnp.where(hots[kk], before, 0.0), axis=0, keepdims=True).astype(jnp.int32)
             for kk in range(TOP_K)]
    ri_ref[...] = jnp.concatenate(idxs + ranks, axis=0)
    gates = jnp.concatenate([e / den for e in es] + [jnp.zeros((128 - TOP_K, tm), F32)], axis=0)
    rg_ref[0] = gates.T


def _mix(x, mod, proj, o_fw, o_bw, o_b, hgrn_norm, g_post_mix, g_pre_ffn, wa, wb, wo, rw, rb):
    b, t, d = x.shape
    tm = min(MIX_TM, t)
    assert tm == MIX_TM
    row = lambda a: a.reshape(1, -1)
    tok = lambda w, col: pl.BlockSpec((1, tm, w), lambda bi, i: (bi, i, col))
    full = lambda a: pl.BlockSpec(a.shape, lambda bi, i: (0,) * a.ndim)
    gn, gpm, gpf, rbr = row(hgrn_norm), row(g_post_mix), row(g_pre_ffn), rb.reshape(-1, 1)
    rw = rw.T
    outs = pl.pallas_call(
        _mix_kernel,
        grid=(b, t // tm),
        in_specs=[tok(d, 0),
                  pl.BlockSpec((1, 6, d), lambda bi, i: (bi, 0, 0)),
                  tok(d, 0), tok(d, 0),
                  tok(d, 4),
                  tok(d, COL_GATES // d), tok(d, COL_GATES // d + 1),
                  tok(NA_W, 0),
                  full(gn), full(gpm), full(gpf), full(wa), full(wb), full(wo), full(rw), full(rbr)],
        out_specs=[tok(d, 0), pl.BlockSpec((1, tm, ROW_SUB, ROW_LANE), lambda bi, i: (bi, i, 0, 0)),
                   pl.BlockSpec((8, tm), lambda bi, i: (0, bi * (t // tm) + i)), tok(128, 0),
                   pl.BlockSpec((N_EXPERTS, 128), lambda bi, i: (0, 0))],
        out_shape=[jax.ShapeDtypeStruct((b, t, d), F32), jax.ShapeDtypeStruct((b, t, ROW_SUB, ROW_LANE), F32),
                   jax.ShapeDtypeStruct((8, b * t), jnp.int32), jax.ShapeDtypeStruct((b, t, 128), F32),
                   jax.ShapeDtypeStruct((N_EXPERTS, 128), F32)],
        scratch_shapes=[pltpu.VMEM((tm, tm), BF16), pltpu.VMEM((N_EXPERTS, 1), F32)],
        compiler_params=_cparams(("arbitrary", "arbitrary")),
        name="mix",
    )(x, mod, o_fw, o_bw, proj, proj, proj, o_b, gn, gpm, gpf, wa, wb, wo, rw, rbr)
    return outs


SC_CORES, SC_SUBCORES = 2, 16
SC_ROWS = 32


def _scatter_rows(xn, dest_t, n_rows):
    n_tok = xn.shape[0]
    workers = SC_CORES * SC_SUBCORES
    per_w = n_tok // workers
    n_it = per_w // SC_ROWS
    assert per_w * workers == n_tok and n_it * SC_ROWS == per_w and n_it % 2 == 0
    idx = dest_t.reshape(TOP_K, n_tok // SC_ROWS, SC_ROWS)
    mesh = plsc.VectorSubcoreMesh(core_axis_name="c", subcore_axis_name="s")

    @functools.partial(
        pl.kernel, mesh=mesh,
        out_type=jax.ShapeDtypeStruct((n_rows, ROW_SUB, ROW_LANE), xn.dtype),
        scratch_types=[pltpu.VMEM((2, TOP_K, SC_ROWS), jnp.int32),
                       pltpu.VMEM((2, SC_ROWS, ROW_SUB, ROW_LANE), xn.dtype),
                       pltpu.SemaphoreType.DMA((2,)), pltpu.SemaphoreType.DMA((2,))],
    )
    def scatter_kernel(x_hbm, idx_hbm, out_hbm, idx_v, rows_v, sem_load, sem_store):
        chunk0 = (lax.axis_index("s") * SC_CORES + lax.axis_index("c")) * n_it

        def load(g, slot):
            tok = pl.ds(pl.multiple_of((chunk0 + g) * SC_ROWS, SC_ROWS), SC_ROWS)
            return pltpu.make_async_copy(x_hbm.at[tok], rows_v.at[slot], sem_load.at[slot])

        def start_load(g, slot):
            load(g, slot).start()
            for kk in range(TOP_K):
                pltpu.sync_copy(idx_hbm.at[kk, chunk0 + g], idx_v.at[slot, kk])

        def store(slot, kk):
            return pltpu.make_async_copy(rows_v.at[slot], out_hbm.at[idx_v.at[slot, kk]], sem_store.at[slot])

        def scatter(g, slot):
            load(g, slot).wait()
            for kk in range(TOP_K):
                store(slot, kk).start()
            for kk in range(TOP_K):
                store(slot, kk).wait()

        start_load(0, 0)

        @pl.loop(0, n_it, step=2)
        def _(g):
            start_load(g + 1, 1)
            scatter(g, 0)

            @pl.when(g + 2 < n_it)
            def _():
                start_load(g + 2, 0)

            scatter(g + 1, 1)

    return scatter_kernel(xn, idx)


GU_SLAB = 256


def _gate_up_layout_kernel(w_ref, o_ref):
    half = GU_SLAB // 2
    r = lax.broadcasted_iota(jnp.int32, (GU_SLAB, GU_SLAB), 0)
    c = lax.broadcasted_iota(jnp.int32, (GU_SLAB, GU_SLAB), 1)
    src = jnp.where(c < half, 2 * c, 2 * (c - half) + 1)
    pm = jnp.where(r == src, 1.0, 0.0).astype(BF16)
    for s in range(w_ref.shape[2] // GU_SLAB):
        cols = slice(s * GU_SLAB, (s + 1) * GU_SLAB)
        o_ref[0, :, cols] = _dot(w_ref[0, :, cols].astype(BF16), pm).astype(BF16)


def _gate_up_layout(w_gate_up, b_gate_up):
    e, k, n2 = w_gate_up.shape
    half = GU_SLAB // 2
    tk = 512
    w = pl.pallas_call(
        _gate_up_layout_kernel,
        grid=(e, k // tk),
        in_specs=[pl.BlockSpec((1, tk, n2), lambda i, j: (i, j, 0))],
        out_specs=pl.BlockSpec((1, tk, n2), lambda i, j: (i, j, 0)),
        out_shape=jax.ShapeDtypeStruct((e, k, n2), BF16),
        compiler_params=_cparams(("arbitrary", "arbitrary")),
        name="gate_up_layout",
    )(w_gate_up)
    b = b_gate_up.reshape(e, n2 // GU_SLAB, half, 2).transpose(0, 1, 3, 2).reshape(e, 1, n2)
    return w, b


def _expert_block(bv_ref, a_ref, idx, x_ref, wgu_ref, bgu_ref, wd_ref, bd_ref, o_ref):
    valid = bv_ref[idx[0]]
    half = GU_SLAB // 2

    @pl.when(valid > 0)
    def _():
        blk = x_ref.shape[0]
        rowi = lax.broadcasted_iota(jnp.int32, (blk, 1), 0)
        x = x_ref[...].reshape(blk, D_MODEL)
        x = jnp.where(rowi < valid, x, 0.0).astype(BF16)
        for s in range(2 * D_FF // GU_SLAB):
            cols = slice(s * GU_SLAB, (s + 1) * GU_SLAB)
            h = _dot(x, wgu_ref[0, :, cols]) + bgu_ref[0, :, cols]
            glu = jnp.minimum(h[:, :half], SWIGLU_LIMIT)
            lin = jnp.clip(h[:, half:], -SWIGLU_LIMIT, SWIGLU_LIMIT)
            a = glu * _sigmoid(SWIGLU_ALPHA * glu) * (lin + 1.0)
            a_ref[:, s * half:(s + 1) * half] = a.astype(BF16)
        o_ref[...] = (_dot(a_ref[...], wd_ref[0].astype(BF16)) + bd_ref[0]).reshape(o_ref.shape)

    @pl.when(valid <= 0)
    def _():
        o_ref[...] = jnp.zeros_like(o_ref)


def _experts(xs, block_e, block_valid, wgu, bgu, wd, bd):
    n_rows = xs.shape[0]
    d = D_MODEL
    n_blocks = block_e.shape[0]
    blk = n_rows // n_blocks

    def outer(be_ref, bv_ref, xs_hbm, wgu_hbm, bgu_hbm, wd_hbm, bd_hbm, y_hbm, a_ref):
        emap = lambda i: (be_ref[i], 0, 0)
        ahead = lambda: pl.Buffered(2, use_lookahead=True)
        pltpu.emit_pipeline(
            functools.partial(_expert_block, bv_ref, a_ref),
            grid=(n_blocks,),
            in_specs=[pl.BlockSpec((blk, ROW_SUB, ROW_LANE), lambda i: (i, 0, 0), pipeline_mode=pl.Buffered(3)),
                      pl.BlockSpec((1, d, 2 * D_FF), emap, pipeline_mode=ahead()),
                      pl.BlockSpec((1, 1, 2 * D_FF), emap, pipeline_mode=ahead()),
                      pl.BlockSpec((1, D_FF, d), emap, pipeline_mode=ahead()),
                      pl.BlockSpec((1, 1, d), emap, pipeline_mode=ahead())],
            out_specs=[pl.BlockSpec((blk, ROW_SUB, ROW_LANE), lambda i: (i, 0, 0))],
            _explicit_indices=True,
        )(xs_hbm, wgu_hbm, bgu_hbm, wd_hbm, bd_hbm, y_hbm)

    smem = lambda: pl.BlockSpec(memory_space=pltpu.SMEM)
    hbm = lambda: pl.BlockSpec(memory_space=pl.ANY)
    return pl.pallas_call(
        outer,
        in_specs=[smem(), smem(), hbm(), hbm(), hbm(), hbm(), hbm()],
        out_specs=hbm(),
        out_shape=jax.ShapeDtypeStruct((n_rows, ROW_SUB, ROW_LANE), F32),
        scratch_shapes=[pltpu.VMEM((blk, D_FF), BF16)],
        compiler_params=pltpu.CompilerParams(vmem_limit_bytes=VMEM_LIMIT),
        name="experts",
    )(block_e, block_valid, xs, wgu, bgu, wd, bd)


def _gather_rows(y, dest_flat):
    n_out = dest_flat.shape[0]
    workers = SC_CORES * SC_SUBCORES
    per_w = n_out // workers
    n_it = per_w // SC_ROWS
    assert per_w * workers == n_out and n_it * SC_ROWS == per_w and n_it % 2 == 0
    mesh = plsc.VectorSubcoreMesh(core_axis_name="c", subcore_axis_name="s")

    @functools.partial(
        pl.kernel, mesh=mesh,
        out_type=jax.ShapeDtypeStruct((n_out, ROW_SUB, ROW_LANE), y.dtype),
        scratch_types=[pltpu.VMEM((2, SC_ROWS), jnp.int32),
                       pltpu.VMEM((2, SC_ROWS, ROW_SUB, ROW_LANE), y.dtype),
                       pltpu.SemaphoreType.DMA((2,))],
    )
    def gather_kernel(y_hbm, idx_hbm, out_hbm, idx_v, rows_v, sem):
        base = (lax.axis_index("s") * SC_CORES + lax.axis_index("c")) * per_w

        def span(g):
            return pl.ds(pl.multiple_of(base + g * SC_ROWS, SC_ROWS), SC_ROWS)

        def row_gather(slot):
            return pltpu.make_async_copy(y_hbm.at[idx_v.at[slot]], rows_v.at[slot], sem.at[slot])

        def start(g, slot):
            pltpu.sync_copy(idx_hbm.at[span(g)], idx_v.at[slot])
            row_gather(slot).start()

        start(0, 0)

        @pl.loop(0, n_it, step=2)
        def _(g):
            start(g + 1, 1)
            row_gather(0).wait()
            pltpu.sync_copy(rows_v.at[0], out_hbm.at[span(g)])

            @pl.when(g + 2 < n_it)
            def _():
                start(g + 2, 0)

            row_gather(1).wait()
            pltpu.sync_copy(rows_v.at[1], out_hbm.at[span(g + 1)])

    return gather_kernel(y, dest_flat)


def _combine_kernel(yg_ref, x1_ref, mod_ref, rg_ref, gpo_ref, o_ref):
    rg = rg_ref[...]
    h = rg[:, 0:1] * yg_ref[0].reshape(COMB_TOK, D_MODEL)
    for kk in range(1, TOP_K):
        h = h + rg[:, kk:kk + 1] * yg_ref[kk].reshape(COMB_TOK, D_MODEL)
    gt2 = mod_ref[0, 5:6, :]
    o_ref[...] = x1_ref[...] + gt2 * _rms(h, gpo_ref[...])


def _combine(yg, x1, mod, rg, g_post_ffn, tokens_per_batch):
    n, d = x1.shape
    per_b = tokens_per_batch // COMB_TOK
    gpo = g_post_ffn.reshape(1, d)

    def outer(yg_hbm, x1_hbm, mod_hbm, rg_hbm, gpo_hbm, out_hbm):
        deep = lambda: pl.Buffered(3)
        pltpu.emit_pipeline(
            _combine_kernel,
            grid=(n // COMB_TOK,),
            in_specs=[pl.BlockSpec((TOP_K, COMB_TOK, ROW_SUB, ROW_LANE), lambda i: (0, i, 0, 0), pipeline_mode=deep()),
                      pl.BlockSpec((COMB_TOK, d), lambda i: (i, 0), pipeline_mode=deep()),
                      pl.BlockSpec((1, 6, d), lambda i: (i // per_b, 0, 0)),
                      pl.BlockSpec((COMB_TOK, 128), lambda i: (i, 0)),
                      pl.BlockSpec((1, d), lambda i: (0, 0))],
            out_specs=[pl.BlockSpec((COMB_TOK, d), lambda i: (i, 0))],
        )(yg_hbm, x1_hbm, mod_hbm, rg_hbm, gpo_hbm, out_hbm)

    hbm = lambda: pl.BlockSpec(memory_space=pl.ANY)
    return pl.pallas_call(
        outer,
        in_specs=[hbm(), hbm(), hbm(), hbm(), hbm()],
        out_specs=hbm(),
        out_shape=jax.ShapeDtypeStruct((n, d), F32),
        compiler_params=pltpu.CompilerParams(vmem_limit_bytes=VMEM_LIMIT),
        name="combine",
    )(yg, x1, mod, rg, gpo)


def _moe_plan(ri_t, counts, n_tok):
    idx = ri_t[:TOP_K]
    rank = ri_t[TOP_K:2 * TOP_K]
    cnt = counts.astype(jnp.int32)
    blk = MOE_BLK_LARGE if n_tok * TOP_K >= 8 * MOE_BLK_LARGE * N_EXPERTS else MOE_BLK_SMALL
    padded = (cnt + blk - 1) // blk * blk
    pend = jnp.cumsum(padded)
    pstart = pend - padded
    dest = rank
    for e in range(N_EXPERTS):
        dest = dest + jnp.where(idx == e, pstart[e], 0)
    n_blocks = (n_tok * TOP_K + blk - 1) // blk + N_EXPERTS
    bstart = jnp.arange(n_blocks, dtype=jnp.int32) * blk
    block_e = jnp.minimum(jnp.sum((bstart[:, None] >= pend[None, :]).astype(jnp.int32), axis=1), N_EXPERTS - 1)
    mine = jnp.arange(N_EXPERTS, dtype=jnp.int32)[None, :] == block_e[:, None]
    row_end = jnp.sum(jnp.where(mine, (pstart + cnt)[None, :], 0), axis=1)
    block_valid = jnp.clip(row_end - bstart, 0, blk).astype(jnp.int32)
    return dest.astype(jnp.int32), block_e, block_valid, n_blocks * blk


def _token_mixing(x, mod, p):
    b, t, d = x.shape
    n = b * t
    proj = _in_proj(x, mod, p['g_pre_mix'], p['w_in'])
    o_fw, o_bw, o_b = _mixers(proj, p['lb_logits'], p['na_tabs'])
    x1, xn, ri, rg, counts = _mix(x, mod, proj, o_fw, o_bw, o_b, p['hgrn_norm'], p['g_post_mix'], p['g_pre_ffn'],
                                  p['w_branch_a'], p['w_branch_b'], p['w_out'], p['router_w'], p['router_b'])
    dest, block_e, block_valid, n_rows = _moe_plan(ri, counts[:, 0], n)
    xs = _scatter_rows(xn.reshape(n, ROW_SUB, ROW_LANE), dest, n_rows)
    return dict(x1=x1.reshape(n, d), rg=rg.reshape(n, 128), dest=dest, block_e=block_e, block_valid=block_valid,
                xs=xs, mod=mod, shape=(b, t, d))


def _expert_rows(s, p):
    return _experts(s['xs'], s['block_e'], s['block_valid'], p['w_gu'], p['b_gu'], p['w_d'], p['b_d'])


def _finish(s, y, p):
    b, t, d = s['shape']
    yg = _gather_rows(y, s['dest'].reshape(-1)).reshape(TOP_K, b * t, ROW_SUB, ROW_LANE)
    return _combine(yg, s['x1'], s['mod'], s['rg'], p['g_post_ffn'], t).reshape(b, t, d)


def _after(first, then):
    return lax.optimization_barrier((first, then))


def kernel(x_prompt, x_sample, c_prompt, c_sample, ada_w, ada_b, g_pre_mix, g_post_mix, g_pre_ffn, g_post_ffn,
           w_in, hgrn_lb_logits, hgrn_norm, na_rpb, w_branch_a, w_branch_b, w_out, router_w, router_b,
           w_gate_up, b_gate_up, w_down, b_down):
    d = D_MODEL
    bp, bs = c_prompt.shape[0], c_sample.shape[0]
    assert ada_w.shape[0] == 1 and hgrn_lb_logits.shape[0] == 2, "one layer: two lower-bound slots"
    for xg in (x_prompt, x_sample):
        assert xg.shape[2] == d and xg.shape[1] % HGRN_TILE == 0 and xg.shape[1] // GRID_W >= NA_KH
    pad = (-(bp + bs)) % 8
    c_all = jnp.concatenate([c_prompt, c_sample, jnp.zeros((pad, d), F32)], axis=0)
    mod = _ada_mod(c_all, ada_w[0], ada_b[0]).reshape(-1, 6, d)

    w = w_in[0]
    w_perm = jnp.concatenate([w[:, :5 * d], w[:, 5 * d + 3 * NA_W:], w[:, 5 * d:5 * d + 3 * NA_W]], axis=1).astype(BF16)
    w_gu, b_gu = _gate_up_layout(w_gate_up[0], b_gate_up[0])
    p = dict(
        g_pre_mix=g_pre_mix[0], g_post_mix=g_post_mix[0], g_pre_ffn=g_pre_ffn[0], g_post_ffn=g_post_ffn[0],
        w_in=w_perm, lb_logits=hgrn_lb_logits.astype(F32), hgrn_norm=hgrn_norm[0],
        na_tabs=_na_bias_tables(na_rpb[0]),
        w_branch_a=w_branch_a[0].astype(BF16), w_branch_b=w_branch_b[0].astype(BF16), w_out=w_out[0].astype(BF16),
        router_w=router_w[0].astype(BF16), router_b=router_b[0],
        w_gu=w_gu, b_gu=b_gu,
        w_d=w_down[0], b_d=b_down[0][:, None, :],
    )
    sp = _token_mixing(x_prompt, mod[:bp], p)
    sp['dest'], x_sample = _after(sp['dest'], x_sample)
    ss = _token_mixing(x_sample, mod[bp:bp + bs], p)
    ss['x1'], sp['xs'] = _after(ss['x1'], sp['xs'])
    y_p = _expert_rows(sp, p)
    y_p, ss['xs'] = _after(y_p, ss['xs'])
    y_s = _expert_rows(ss, p)
    out_p = _finish(sp, y_p, p)
    out_s = _finish(ss, y_s, p)
    return (out_p, out_s)
```
